```python
import jax
import jax.numpy as jnp
from jax import lax
import numpy as np

D_MODEL = 1024
BATCH = 32
SEQ = 256
DEPTH = 2
DEC_BATCH = 8
DEC_SEQ = 4096
PAST_LEN = 256

GRID_W = 64
N_EVEN = (DEPTH + 1) // 2
N_ODD = DEPTH // 2
CHUNK = 128
Q_BLOCK = 128
EPS = 1e-6
RET_HEADS = 8
RET_DK = 64
RET_DV = 128
RET_Q = RET_HEADS * RET_DK
RET_V = RET_HEADS * RET_DV
SSM_D_INNER = 1024
SSM_HEADDIM = 64
SSM_HEADS = SSM_D_INNER // SSM_HEADDIM
SSM_GROUPS = 4
SSM_RATIO = SSM_HEADS // SSM_GROUPS
SSM_STATE = 128
SSM_XBC = SSM_D_INNER + 2 * SSM_GROUPS * SSM_STATE
HYB_IN = 2 * RET_Q + 2 * RET_V + SSM_D_INNER + SSM_XBC + SSM_HEADS
HYB_MIX = RET_V + SSM_D_INNER
ATT_HEADS = 16
ATT_KV_HEADS = 4
ATT_RATIO = ATT_HEADS // ATT_KV_HEADS
ATT_HD = 64
ATT_QKV = (ATT_HEADS + 2 * ATT_KV_HEADS) * ATT_HD
ROT_FREQS = ATT_HD // 4
ROPE_BASE = 10000.0
FFN_HIDDEN = 2816

kernel_name = 'hybrid_retention_ssd_gqa_prefix_dit_step'


def rms_norm(x):
    xf = x.astype(jnp.float32)
    return (xf * lax.rsqrt(jnp.mean(xf * xf, axis=-1, keepdims=True) + EPS)).astype(x.dtype)


def head_layer_norm(x):
    xf = x.astype(jnp.float32)
    mu = jnp.mean(xf, axis=-1, keepdims=True)
    var = jnp.mean(jnp.square(xf - mu), axis=-1, keepdims=True)
    return ((xf - mu) * lax.rsqrt(var + EPS)).astype(x.dtype)


def dwconv3(x, w, b):
    xp = jnp.pad(x, ((0, 0), (1, 1), (0, 0)))
    return xp[:, :-2] * w[0] + xp[:, 1:-1] * w[1] + xp[:, 2:] * w[2] + b


def grid_rope_angles(L):
    rows = L // GRID_W
    row = jnp.repeat(jnp.arange(rows, dtype=jnp.float32), GRID_W)
    col = jnp.tile(jnp.arange(GRID_W, dtype=jnp.float32), rows)
    inv = ROPE_BASE ** (-jnp.arange(ROT_FREQS, dtype=jnp.float32) / ROT_FREQS)
    ang = jnp.stack([row[:, None] * inv, col[:, None] * inv], axis=1)
    return jnp.cos(ang), jnp.sin(ang)


def apply_rope(x, cos, sin):
    b, L, h, d = x.shape
    xr = x.astype(jnp.float32).reshape(b, L, h, 2, 2, ROT_FREQS)
    x1, x2 = xr[..., 0, :], xr[..., 1, :]
    c, s = cos[None, :, None], sin[None, :, None]
    out = jnp.stack([x1 * c - x2 * s, x1 * s + x2 * c], axis=-2)
    return out.reshape(b, L, h, d).astype(x.dtype)


def chunked_decay_scan(q, k, v, log_a, s0):
    f32 = jnp.float32
    b, L, g, dk = q.shape
    r, dv = v.shape[3], v.shape[4]
    n = L // CHUNK

    def chunks(t):
        return jnp.moveaxis(t.astype(f32).reshape((b, n, CHUNK) + t.shape[2:]), 1, 0)

    lower = jnp.tril(jnp.ones((CHUNK, CHUNK), dtype=bool))

    def step(s, inp):
        qc, kc, vc, ac = inp
        cum = jnp.cumsum(ac, axis=1)
        cum_t = jnp.moveaxis(cum, 1, -1)
        decay = jnp.exp(jnp.where(lower, cum_t[..., :, None] - cum_t[..., None, :], -jnp.inf))
        scores = jnp.einsum('bigd,bjgd->bgij', qc, kc)
        y = jnp.einsum('bgij,bgrij,bjgrv->bigrv', scores, decay, vc)
        y = y + jnp.einsum('bigd,bgrdv->bigrv', qc, s) * jnp.exp(cum)[..., None]
        last = cum_t[..., -1:]
        w = jnp.exp(last - cum_t)
        s = s * jnp.exp(last)[..., None] + jnp.einsum('bjgd,bjgrv,bgrj->bgrdv', kc, vc, w)
        return s, y

    s_fin, ys = lax.scan(step, s0.astype(f32), (chunks(q), chunks(k), chunks(v), chunks(log_a)))
    y = jnp.moveaxis(ys, 0, 1).reshape(b, L, g, r, dv)
    return y.astype(q.dtype), s_fin.astype(s0.dtype)


def hybrid_mixer(h, s_ret0, s_ssm0, w_in, w_out, log_decay, gn_gain, conv_w, conv_b,
                 a_log, dt_bias, d_skip, norm_gain):
    b, L, _ = h.shape
    cuts = np.cumsum([RET_Q, RET_Q, RET_V, RET_V, SSM_D_INNER, SSM_XBC]).tolist()
    q, k, v, g, z, xbc, dt_raw = jnp.split(h @ w_in, cuts, axis=-1)
    q = q.reshape(b, L, RET_HEADS, RET_DK)
    k = k.reshape(b, L, RET_HEADS, RET_DK) * (RET_DK ** -0.5)
    v = v.reshape(b, L, RET_HEADS, 1, RET_DV)
    xbc = jax.nn.silu(dwconv3(xbc, conv_w, conv_b))
    xs, bm, cm = jnp.split(xbc, [SSM_D_INNER, SSM_D_INNER + SSM_GROUPS * SSM_STATE], axis=-1)
    xs = xs.reshape(b, L, SSM_HEADS, SSM_HEADDIM)
    bm = bm.reshape(b, L, SSM_GROUPS, SSM_STATE)
    cm = cm.reshape(b, L, SSM_GROUPS, SSM_STATE)
    ret_ys, ret_fin, ssm_ys, ssm_fin = [], [], [], []
    for d in range(2):
        fl = (lambda t: t) if d == 0 else (lambda t: jnp.flip(t, axis=1))
        la_ret = jnp.broadcast_to(log_decay[d].astype(jnp.float32)[:, None], (b, L, RET_HEADS, 1))
        y, s = chunked_decay_scan(fl(q), fl(k), fl(v), la_ret, s_ret0[:, d][:, :, None])
        ret_ys.append(fl(y))
        ret_fin.append(s.reshape(b, RET_HEADS, RET_DK, RET_DV))
        dt = jax.nn.softplus((dt_raw + dt_bias[d]).astype(jnp.float32))
        la_ssm = (dt * -jnp.exp(a_log[d].astype(jnp.float32))).reshape(b, L, SSM_GROUPS, SSM_RATIO)
        v_ssm = (xs * dt[..., None]).reshape(b, L, SSM_GROUPS, SSM_RATIO, SSM_HEADDIM)
        s0 = s_ssm0[:, d].reshape(b, SSM_GROUPS, SSM_RATIO, SSM_STATE, SSM_HEADDIM)
        y, s = chunked_decay_scan(fl(cm), fl(bm), fl(v_ssm), fl(la_ssm), s0)
        ssm_ys.append(fl(y).reshape(b, L, SSM_HEADS, SSM_HEADDIM) + d_skip[d][:, None] * xs)
        ssm_fin.append(s.reshape(b, SSM_HEADS, SSM_STATE, SSM_HEADDIM))
    y_ret = head_layer_norm((ret_ys[0] + ret_ys[1]).reshape(b, L, RET_HEADS, RET_DV)).reshape(b, L, RET_V)
    y_ret = jax.nn.silu(g) * (y_ret * gn_gain)
    y_ssm = (ssm_ys[0] + ssm_ys[1]).reshape(b, L, SSM_D_INNER)
    y_ssm = rms_norm(y_ssm * jax.nn.silu(z)) * norm_gain
    out = jnp.concatenate([y_ret, y_ssm], axis=-1) @ w_out
    return out, jnp.stack(ret_fin, axis=1), jnp.stack(ssm_fin, axis=1)


def blocked_attention(q, k, v):
    b, L = q.shape[0], q.shape[1]
    nb = L // Q_BLOCK
    qb = jnp.moveaxis(q.reshape(b, nb, Q_BLOCK, ATT_KV_HEADS, ATT_RATIO, ATT_HD), 1, 0)
    scale = ATT_HD ** -0.5

    def one_block(qblk):
        s = jnp.einsum('bqgrd,bkgd->bgrqk', qblk, k, preferred_element_type=jnp.float32) * scale
        p = jax.nn.softmax(s, axis=-1)
        return jnp.einsum('bgrqk,bkgd->bqgrd', p.astype(v.dtype), v)

    o = lax.map(one_block, qb)
    return jnp.moveaxis(o, 0, 1).reshape(b, L, ATT_HEADS * ATT_HD)


def attention_mixer(h, kv_ctx, rope, w_qkv, q_gain, k_gain, w_o):
    b, L, _ = h.shape
    q, k, v = jnp.split(h @ w_qkv, [ATT_HEADS * ATT_HD, (ATT_HEADS + ATT_KV_HEADS) * ATT_HD], axis=-1)
    q = rms_norm(q.reshape(b, L, ATT_HEADS, ATT_HD)) * q_gain
    k = rms_norm(k.reshape(b, L, ATT_KV_HEADS, ATT_HD)) * k_gain
    v = v.reshape(b, L, ATT_KV_HEADS, ATT_HD)
    if rope is None:
        keys, vals = k, v
    else:
        q = apply_rope(q, rope[0], rope[1])
        keys = jnp.concatenate([kv_ctx[0], apply_rope(k, rope[0], rope[1])], axis=1)
        vals = jnp.concatenate([kv_ctx[1], v], axis=1)
    return blocked_attention(q, keys, vals) @ w_o, k, v


def conv_ffn(h, w_up, conv_w, conv_b, w_down):
    u = dwconv3(h @ w_up, conv_w, conv_b)
    a, val = jnp.split(u, 2, axis=-1)
    return (jax.nn.silu(a) * val) @ w_down


def trunk(x, cond, caches, P):
    b, L, _ = x.shape
    rope = None if caches is None else grid_rope_angles(L)
    ret_states, ssm_states, ks, vs = [], [], [], []
    for layer in range(DEPTH):
        mod = (jax.nn.silu(cond) @ P['w_mod'][layer] + P['b_mod'][layer])[:, None, :]
        sh1, sc1, g1, sh2, sc2, g2 = jnp.split(mod, 6, axis=-1)
        h = rms_norm(x) * P['norm_mix'][layer] * (1 + sc1) + sh1
        i = layer // 2
        if layer % 2 == 0:
            if caches is None:
                s_ret0 = jnp.zeros((b, 2, RET_HEADS, RET_DK, RET_DV), x.dtype)
                s_ssm0 = jnp.zeros((b, 2, SSM_HEADS, SSM_STATE, SSM_HEADDIM), x.dtype)
            else:
                s_ret0, s_ssm0 = caches[0][:, i], caches[1][:, i]
            y, s_ret, s_ssm = hybrid_mixer(
                h, s_ret0, s_ssm0, P['hyb_w_in'][i], P['hyb_w_out'][i], P['ret_log_decay'][i],
                P['ret_gn_gain'][i], P['ssm_conv_w'][i], P['ssm_conv_b'][i], P['ssm_a_log'][i],
                P['ssm_dt_bias'][i], P['ssm_d'][i], P['ssm_norm_gain'][i])
            ret_states.append(s_ret)
            ssm_states.append(s_ssm)
        else:
            kv_ctx = None if caches is None else (caches[2][:, i], caches[3][:, i])
            y, k_new, v_new = attention_mixer(h, kv_ctx, rope, P['attn_w_qkv'][i], P['attn_q_gain'][i],
                                              P['attn_k_gain'][i], P['attn_w_o'][i])
            ks.append(k_new)
            vs.append(v_new)
        x = x + g1 * y
        h = rms_norm(x) * P['norm_ffn'][layer] * (1 + sc2) + sh2
        x = x + g2 * conv_ffn(h, P['ffn_w_up'][layer], P['ffn_conv_w'][layer], P['ffn_conv_b'][layer],
                              P['ffn_w_down'][layer])
    return x, ret_states, ssm_states, ks, vs


def setup_inputs(seed: int = 0) -> dict:
    key = jax.random.key(seed)
    ks = iter(jax.random.split(key, 40))
    f32 = jnp.float32

    def nrm(shape, scale):
        return scale * jax.random.normal(next(ks), shape, f32)

    base_decay = jnp.log(1.0 - 2.0 ** (-5.0 - jnp.arange(RET_HEADS, dtype=f32)))
    ret_log_decay = base_decay * jnp.exp(nrm((N_EVEN, 2, RET_HEADS), 0.1))
    ssm_a_log = jnp.log(jax.random.uniform(next(ks), (N_EVEN, 2, SSM_HEADS), f32, 1.0, 16.0))
    dt0 = jnp.exp(jax.random.uniform(next(ks), (N_EVEN, 2, SSM_HEADS), f32, np.log(1e-3), np.log(1e-1)))
    ssm_dt_bias = dt0 + jnp.log(-jnp.expm1(-dt0))
    return {
        'x_prompt': nrm((BATCH, SEQ, D_MODEL), 1.0),
        'x_sample': nrm((DEC_BATCH, DEC_SEQ, D_MODEL), 1.0),
        'state_ret': nrm((DEC_BATCH, N_EVEN, 2, RET_HEADS, RET_DK, RET_DV), 0.5),
        'state_ssm': nrm((DEC_BATCH, N_EVEN, 2, SSM_HEADS, SSM_STATE, SSM_HEADDIM), 0.1),
        'cache_attn_k': nrm((DEC_BATCH, N_ODD, PAST_LEN, ATT_KV_HEADS, ATT_HD), 1.0),
        'cache_attn_v': nrm((DEC_BATCH, N_ODD, PAST_LEN, ATT_KV_HEADS, ATT_HD), 1.0),
        'c': nrm((DEC_BATCH, D_MODEL), 1.0),
        'c_ctx': nrm((D_MODEL,), 1.0),
        'w_mod': nrm((DEPTH, D_MODEL, 6 * D_MODEL), 0.5 * D_MODEL ** -0.5),
        'b_mod': nrm((DEPTH, 6 * D_MODEL), 0.02),
        'norm_mix': 1.0 + nrm((DEPTH, D_MODEL), 0.1),
        'norm_ffn': 1.0 + nrm((DEPTH, D_MODEL), 0.1),
        'ffn_w_up': nrm((DEPTH, D_MODEL, 2 * FFN_HIDDEN), D_MODEL ** -0.5),
        'ffn_conv_w': nrm((DEPTH, 3, 2 * FFN_HIDDEN), 3 ** -0.5),
        'ffn_conv_b': nrm((DEPTH, 2 * FFN_HIDDEN), 0.02),
        'ffn_w_down': nrm((DEPTH, FFN_HIDDEN, D_MODEL), FFN_HIDDEN ** -0.5),
        'hyb_w_in': nrm((N_EVEN, D_MODEL, HYB_IN), D_MODEL ** -0.5),
        'hyb_w_out': nrm((N_EVEN, HYB_MIX, D_MODEL), HYB_MIX ** -0.5),
        'ret_log_decay': ret_log_decay,
        'ret_gn_gain': 1.0 + nrm((N_EVEN, RET_V), 0.1),
        'ssm_conv_w': nrm((N_EVEN, 3, SSM_XBC), 3 ** -0.5),
        'ssm_conv_b': nrm((N_EVEN, SSM_XBC), 0.02),
        'ssm_a_log': ssm_a_log,
        'ssm_dt_bias': ssm_dt_bias,
        'ssm_d': 1.0 + nrm((N_EVEN, 2, SSM_HEADS), 0.1),
        'ssm_norm_gain': 1.0 + nrm((N_EVEN, SSM_D_INNER), 0.1),
        'attn_w_qkv': nrm((N_ODD, D_MODEL, ATT_QKV), D_MODEL ** -0.5),
        'attn_q_gain': 1.0 + nrm((N_ODD, ATT_HD), 0.1),
        'attn_k_gain': 1.0 + nrm((N_ODD, ATT_HD), 0.1),
        'attn_w_o': nrm((N_ODD, ATT_HEADS * ATT_HD, D_MODEL), (ATT_HEADS * ATT_HD) ** -0.5),
    }


def reference(x_prompt, x_sample, state_ret, state_ssm, cache_attn_k, cache_attn_v, c, c_ctx,
              w_mod, b_mod, norm_mix, norm_ffn, ffn_w_up, ffn_conv_w, ffn_conv_b, ffn_w_down,
              hyb_w_in, hyb_w_out, ret_log_decay, ret_gn_gain, ssm_conv_w, ssm_conv_b, ssm_a_log,
              ssm_dt_bias, ssm_d, ssm_norm_gain, attn_w_qkv, attn_q_gain, attn_k_gain, attn_w_o):
    P = {
        'w_mod': w_mod, 'b_mod': b_mod, 'norm_mix': norm_mix, 'norm_ffn': norm_ffn,
        'ffn_w_up': ffn_w_up, 'ffn_conv_w': ffn_conv_w, 'ffn_conv_b': ffn_conv_b, 'ffn_w_down': ffn_w_down,
        'hyb_w_in': hyb_w_in, 'hyb_w_out': hyb_w_out, 'ret_log_decay': ret_log_decay,
        'ret_gn_gain': ret_gn_gain, 'ssm_conv_w': ssm_conv_w, 'ssm_conv_b': ssm_conv_b,
        'ssm_a_log': ssm_a_log, 'ssm_dt_bias': ssm_dt_bias, 'ssm_d': ssm_d, 'ssm_norm_gain': ssm_norm_gain,
        'attn_w_qkv': attn_w_qkv, 'attn_q_gain': attn_q_gain, 'attn_k_gain': attn_k_gain, 'attn_w_o': attn_w_o,
    }
    y_prompt, rs, ss, ks, vs = trunk(x_prompt, c_ctx[None, :], None, P)
    new_state_ret = jnp.stack(rs, axis=1)
    new_state_ssm = jnp.stack(ss, axis=1)
    new_cache_attn_k = jnp.stack(ks, axis=1)
    new_cache_attn_v = jnp.stack(vs, axis=1)
    y_sample = trunk(x_sample, c, (state_ret, state_ssm, cache_attn_k, cache_attn_v), P)[0]
    return (y_prompt, y_sample, new_state_ret, new_state_ssm, new_cache_attn_k, new_cache_attn_v)
```

```python
import functools

import jax
import jax.numpy as jnp
from jax import lax
from jax.experimental import pallas as pl
from jax.experimental.pallas import tpu as pltpu

F32 = jnp.float32
BF16 = jnp.bfloat16
HIGHEST = lax.Precision.HIGHEST

D_MODEL = 1024
EPS = 1e-6
GRID_W = 64
CHUNK = 128
RET_HEADS = 8
RET_DK = 64
RET_DV = 128
RET_Q = RET_HEADS * RET_DK
RET_V = RET_HEADS * RET_DV
SSM_D_INNER = 1024
SSM_HEADDIM = 64
SSM_HEADS = 16
SSM_GROUPS = 4
SSM_RATIO = 4
SSM_STATE = 128
SSM_BC = SSM_GROUPS * SSM_STATE
SSM_XBC = SSM_D_INNER + 2 * SSM_BC
HYB_QKV = 2 * RET_Q + RET_V
HYB_REST = 2 * RET_V + SSM_XBC + 128
HYB_MIX = RET_V + SSM_D_INNER
ATT_HEADS = 16
ATT_KV_HEADS = 4
ATT_RATIO = 4
ATT_HD = 64
ATT_Q = ATT_HEADS * ATT_HD
ATT_KV = ATT_KV_HEADS * ATT_HD
ROT_FREQS = ATT_HD // 4
ROPE_BASE = 10000.0
FFN_HIDDEN = 2816

LANES = 128
SUBLANES = 8
VMEM_LIMIT = 56 * 1024 * 1024
NEG_BIG = -1e30


def _cparams(sem):
    return pltpu.CompilerParams(dimension_semantics=sem, vmem_limit_bytes=VMEM_LIMIT)


def _norm_mod(x, gain, sc, sh):
    ms = jnp.mean(x * x, axis=-1, keepdims=True)
    return x * lax.rsqrt(ms + EPS) * gain * (1.0 + sc) + sh


def _vec_spec(width, tiles_per_batch, per_batch, grid_rank):
    if grid_rank == 1:
        return pl.BlockSpec((1, 1, width), lambda i: ((i // tiles_per_batch) if per_batch else 0, 0, 0))
    return pl.BlockSpec((1, 1, width), lambda i, j: ((i // tiles_per_batch) if per_batch else 0, 0, 0))


def _mod_body(c_ref, w_ref, b_ref, o_ref):
    cs = jax.nn.silu(c_ref[...])
    o_ref[0] = jnp.dot(cs, w_ref[0], precision=HIGHEST, preferred_element_type=F32) + b_ref[0]


def _modulation(cond, w_mod, b_mod):
    depth, _, n = w_mod.shape
    rows = cond.shape[0]
    tn = n // 4
    return pl.pallas_call(
        _mod_body,
        grid=(depth, n // tn),
        in_specs=[
            pl.BlockSpec((rows, D_MODEL), lambda l, j: (0, 0)),
            pl.BlockSpec((1, D_MODEL, tn), lambda l, j: (l, 0, j)),
            pl.BlockSpec((1, 1, tn), lambda l, j: (l, 0, j)),
        ],
        out_specs=pl.BlockSpec((1, rows, tn), lambda l, j: (l, 0, j)),
        out_shape=jax.ShapeDtypeStruct((depth, rows, n), F32),
        compiler_params=_cparams(("arbitrary", "arbitrary")),
        name="modulation",
    )(cond, w_mod, b_mod.reshape(depth, 1, n))


def _nm_matmul_body(x_ref, gain_ref, sc_ref, sh_ref, w_ref, o_ref, h_scr):
    @pl.when(pl.program_id(1) == 0)
    def _():
        h_scr[...] = _norm_mod(x_ref[...], gain_ref[...], sc_ref[0], sh_ref[0]).astype(BF16)

    o_ref[...] = jnp.dot(h_scr[...], w_ref[...], preferred_element_type=F32).astype(o_ref.dtype)


def _nm_matmul(x, gain, sc, sh, w, out_dtype, seq_len, tm, tn, name):
    m = x.shape[0]
    n = w.shape[1]
    per_batch = sc.shape[0] > 1
    tpb = seq_len // tm
    return pl.pallas_call(
        _nm_matmul_body,
        grid=(m // tm, n // tn),
        in_specs=[
            pl.BlockSpec((tm, D_MODEL), lambda i, j: (i, 0)),
            pl.BlockSpec((1, D_MODEL), lambda i, j: (0, 0)),
            _vec_spec(D_MODEL, tpb, per_batch, 2),
            _vec_spec(D_MODEL, tpb, per_batch, 2),
            pl.BlockSpec((D_MODEL, tn), lambda i, j: (0, j)),
        ],
        out_specs=pl.BlockSpec((tm, tn), lambda i, j: (i, j)),
        out_shape=jax.ShapeDtypeStruct((m, n), out_dtype),
        scratch_shapes=[pltpu.VMEM((tm, D_MODEL), BF16)],
        compiler_params=_cparams(("arbitrary", "arbitrary")),
        name=name,
    )(x, gain, sc, sh, w)


def _ffn_body(tiles_per_seq, tm, x_ref, xn_ref, xp_ref, gain_ref, sc_ref, sh_ref, gate_ref,
              wa_ref, wv_ref, cwa_ref, cwv_ref, cba_ref, cbv_ref, wd_ref, o_ref, h_scr, acc_scr):
    i = pl.program_id(0)
    j = pl.program_id(1)

    @pl.when(j == 0)
    def _():
        gain, sc, sh = gain_ref[...], sc_ref[0], sh_ref[0]
        h_scr[0:tm] = _norm_mod(x_ref[...], gain, sc, sh).astype(BF16)
        pos = i % tiles_per_seq
        hn = jnp.where(pos == tiles_per_seq - 1, 0.0, _norm_mod(xn_ref[0], gain, sc, sh))
        hp = jnp.where(pos == 0, 0.0, _norm_mod(xp_ref[0], gain, sc, sh))
        h_scr[tm:tm + 2 * SUBLANES] = jnp.concatenate([hn, hp], axis=0).astype(BF16)
        acc_scr[...] = jnp.zeros_like(acc_scr)

    h = h_scr[...]
    rows = tm + 2 * SUBLANES

    def conv(u, cw_ref, cb_ref):
        up = pltpu.roll(u, 1, 0)
        un = pltpu.roll(u, rows - 1, 0)
        c = up * cw_ref[0:1] + u * cw_ref[1:2] + un * cw_ref[2:3] + cb_ref[...]
        return c[0:tm]

    ca = conv(jnp.dot(h, wa_ref[...], preferred_element_type=F32), cwa_ref, cba_ref)
    cv = conv(jnp.dot(h, wv_ref[...], preferred_element_type=F32), cwv_ref, cbv_ref)
    act = (jax.nn.silu(ca) * cv).astype(BF16)
    acc_scr[...] += jnp.dot(act, wd_ref[...], preferred_element_type=F32)

    @pl.when(j == pl.num_programs(1) - 1)
    def _():
        o_ref[...] = x_ref[...] + gate_ref[0] * acc_scr[...]


def _conv_ffn(x, gain, sc, sh, gate, w_up, conv_w, conv_b, w_down, seq_len, tm, th):
    m = x.shape[0]
    per_batch = sc.shape[0] > 1
    tpb = seq_len // tm
    nh = FFN_HIDDEN // th
    m8 = m // SUBLANES
    r8 = tm // SUBLANES
    x8 = x.reshape(m8, SUBLANES, D_MODEL)
    conv_b = conv_b.reshape(1, 2 * FFN_HIDDEN)
    vec = _vec_spec(D_MODEL, tpb, per_batch, 2)
    return pl.pallas_call(
        functools.partial(_ffn_body, tpb, tm),
        grid=(m // tm, nh),
        in_specs=[
            pl.BlockSpec((tm, D_MODEL), lambda i, j: (i, 0)),
            pl.BlockSpec((1, SUBLANES, D_MODEL), lambda i, j: (jnp.minimum((i + 1) * r8, m8 - 1), 0, 0)),
            pl.BlockSpec((1, SUBLANES, D_MODEL), lambda i, j: (jnp.maximum(i * r8 - 1, 0), 0, 0)),
            pl.BlockSpec((1, D_MODEL), lambda i, j: (0, 0)),
            vec, vec, vec,
            pl.BlockSpec((D_MODEL, th), lambda i, j: (0, j)),
            pl.BlockSpec((D_MODEL, th), lambda i, j: (0, j + nh)),
            pl.BlockSpec((3, th), lambda i, j: (0, j)),
            pl.BlockSpec((3, th), lambda i, j: (0, j + nh)),
            pl.BlockSpec((1, th), lambda i, j: (0, j)),
            pl.BlockSpec((1, th), lambda i, j: (0, j + nh)),
            pl.BlockSpec((th, D_MODEL), lambda i, j: (j, 0)),
        ],
        out_specs=pl.BlockSpec((tm, D_MODEL), lambda i, j: (i, 0)),
        out_shape=jax.ShapeDtypeStruct((m, D_MODEL), F32),
        scratch_shapes=[pltpu.VMEM((tm + 2 * SUBLANES, D_MODEL), BF16), pltpu.VMEM((tm, D_MODEL), F32)],
        compiler_params=_cparams(("arbitrary", "arbitrary")),
        name="conv_ffn",
    )(x, x8, x8, gain, sc, sh, gate, w_up, w_up, conv_w, conv_w, conv_b, conv_b, w_down)


def _scan_body(d, has_init, qkv_ref, xbc_ref, xn_ref, xp_ref, dt_ref, cw_ref, cb_ref, dtb_ref, alog_ref,
               logdec_ref, dskip_ref, *rest):
    if has_init:
        sret0_ref, sssm0_ref = rest[0], rest[1]
        rest = rest[2:]
    y_ref, sret_ref, sssm_ref, s_ret, s_ssm = rest
    i = pl.program_id(1)
    n = pl.num_programs(1)
    c = i if d == 0 else n - 1 - i

    @pl.when(i == 0)
    def _():
        if has_init:
            s_ret[...] = sret0_ref[0]
            s_ssm[...] = sssm0_ref[0]
        else:
            s_ret[...] = jnp.zeros_like(s_ret)
            s_ssm[...] = jnp.zeros_like(s_ssm)

    ii = lax.broadcasted_iota(jnp.int32, (CHUNK, CHUNK), 0)
    jj = lax.broadcasted_iota(jnp.int32, (CHUNK, CHUNK), 1)
    rel = ((ii - jj) if d == 0 else (jj - ii)).astype(F32)
    mask = rel >= 0.0
    row = lax.broadcasted_iota(jnp.int32, (CHUNK, 1), 0).astype(F32)
    steps_in = (row + 1.0) if d == 0 else (CHUNK - row)
    steps_out = (CHUNK - 1.0 - row) if d == 0 else row
    full = jnp.full((1, LANES), float(CHUNK), F32)

    for h in range(RET_HEADS):
        la = logdec_ref[d, h]
        qh = qkv_ref[:, h * RET_DK:(h + 1) * RET_DK]
        kh = qkv_ref[:, RET_Q + h * RET_DK:RET_Q + (h + 1) * RET_DK]
        vh = qkv_ref[:, 2 * RET_Q + h * RET_DV:2 * RET_Q + (h + 1) * RET_DV]
        decay = jnp.exp(jnp.where(mask, rel * la, -jnp.inf))
        s = lax.dot_general(qh, kh, (((1,), (1,)), ((), ())), preferred_element_type=F32)
        p = (s * decay).astype(BF16)
        y = jnp.dot(p, vh, preferred_element_type=F32)
        y = y + jnp.dot(qh, s_ret[h].astype(BF16), preferred_element_type=F32) * jnp.exp(steps_in * la)
        y_ref[:, h * RET_DV:(h + 1) * RET_DV] = y
        vw = (vh.astype(F32) * jnp.exp(steps_out * la)).astype(BF16)
        upd = lax.dot_general(kh, vw, (((0,), (0,)), ((), ())), preferred_element_type=F32)
        s_ret[h] = s_ret[h] * jnp.exp(full * la) + upd

    xn = jnp.where(c == n - 1, 0.0, xn_ref[0])
    xp = jnp.where(c == 0, 0.0, xp_ref[0])
    xcat = jnp.concatenate([xbc_ref[...], xn, xp], axis=0)
    rows = CHUNK + 2 * SUBLANES
    xc = (pltpu.roll(xcat, 1, 0) * cw_ref[0:1] + xcat * cw_ref[1:2]
          + pltpu.roll(xcat, rows - 1, 0) * cw_ref[2:3] + cb_ref[...])
    xc = jax.nn.silu(xc[0:CHUNK])
    xs = xc[:, 0:SSM_D_INNER]
    bm = xc[:, SSM_D_INNER:SSM_D_INNER + SSM_BC]
    cm = xc[:, SSM_D_INNER + SSM_BC:SSM_XBC]

    dt = jax.nn.softplus(dt_ref[...] + dtb_ref[...])
    la = dt * (-jnp.exp(alog_ref[...]))
    cum = jnp.dot(mask.astype(F32), la, precision=HIGHEST, preferred_element_type=F32)
    cum_t = cum.T
    last = cum[CHUNK - 1:CHUNK, :] if d == 0 else cum[0:1, :]
    e_in = jnp.exp(cum)
    e_out = jnp.exp(last - cum)
    e_full = jnp.exp(last)

    for g in range(SSM_GROUPS):
        cg = cm[:, g * SSM_STATE:(g + 1) * SSM_STATE].astype(BF16)
        bg = bm[:, g * SSM_STATE:(g + 1) * SSM_STATE].astype(BF16)
        s = lax.dot_general(cg, bg, (((1,), (1,)), ((), ())), preferred_element_type=F32)
        ys = []
        for r in range(SSM_RATIO):
            h = g * SSM_RATIO + r
            decay = jnp.exp(jnp.where(mask, cum[:, h:h + 1] - cum_t[h:h + 1, :], -jnp.inf))
            p = (s * decay).astype(BF16)
            xh = xs[:, h * SSM_HEADDIM:(h + 1) * SSM_HEADDIM]
            vh = xh * dt[:, h:h + 1]
            y = jnp.dot(p, vh.astype(BF16), preferred_element_type=F32)
            y = y + jnp.dot(cg, s_ssm[h].astype(BF16), preferred_element_type=F32) * e_in[:, h:h + 1]
            ys.append(y + dskip_ref[d, h] * xh)
            vw = (vh * e_out[:, h:h + 1]).astype(BF16)
            upd = lax.dot_general(bg, vw, (((0,), (0,)), ((), ())), preferred_element_type=F32)
            s_ssm[h] = s_ssm[h] * e_full[:, h:h + 1] + upd
        for r in range(0, SSM_RATIO, 2):
            col = RET_V + (g * SSM_RATIO + r) * SSM_HEADDIM
            y_ref[:, col:col + 2 * SSM_HEADDIM] = jnp.concatenate([ys[r], ys[r + 1]], axis=1)

    @pl.when(i == n - 1)
    def _():
        sret_ref[0] = s_ret[...]
        sssm_ref[0] = s_ssm[...]


def _scan(d, qkv, rest, conv_w, conv_b, dt_bias, a_log, log_decay, d_skip, s_ret0, s_ssm0, batch, seq_len):
    m = qkv.shape[0]
    n = seq_len // CHUNK
    m8 = m // SUBLANES
    r8 = CHUNK // SUBLANES
    has_init = s_ret0 is not None
    rest8 = rest.reshape(m8, SUBLANES, HYB_REST)
    xbc_blk = 2 * RET_V // SSM_XBC
    dt_blk = (2 * RET_V + SSM_XBC) // LANES

    def chunk(b, i):
        return b * n + (i if d == 0 else n - 1 - i)

    pad = lambda v: jnp.pad(v.astype(F32), (0, LANES - v.shape[0])).reshape(1, LANES)
    smem = pl.BlockSpec(memory_space=pltpu.SMEM)
    in_specs = [
        pl.BlockSpec((CHUNK, HYB_QKV), lambda b, i: (chunk(b, i), 0)),
        pl.BlockSpec((CHUNK, SSM_XBC), lambda b, i: (chunk(b, i), xbc_blk)),
        pl.BlockSpec((1, SUBLANES, SSM_XBC), lambda b, i: (jnp.minimum((chunk(b, i) + 1) * r8, m8 - 1), 0, xbc_blk)),
        pl.BlockSpec((1, SUBLANES, SSM_XBC), lambda b, i: (jnp.maximum(chunk(b, i) * r8 - 1, 0), 0, xbc_blk)),
        pl.BlockSpec((CHUNK, LANES), lambda b, i: (chunk(b, i), dt_blk)),
        pl.BlockSpec((3, SSM_XBC), lambda b, i: (0, 0)),
        pl.BlockSpec((1, SSM_XBC), lambda b, i: (0, 0)),
        pl.BlockSpec((1, LANES), lambda b, i: (0, 0)),
        pl.BlockSpec((1, LANES), lambda b, i: (0, 0)),
        smem, smem,
    ]
    args = [qkv, rest, rest8, rest8, rest, conv_w, conv_b.reshape(1, SSM_XBC), pad(dt_bias[d]), pad(a_log[d]),
            log_decay.astype(F32), d_skip.astype(F32)]
    ret_spec = pl.BlockSpec((1, RET_HEADS, RET_DK, RET_DV), lambda b, i: (b, 0, 0, 0))
    ssm_spec = pl.BlockSpec((1, SSM_HEADS, SSM_STATE, SSM_HEADDIM), lambda b, i: (b, 0, 0, 0))
    if has_init:
        in_specs += [ret_spec, ssm_spec]
        args += [s_ret0, s_ssm0]
    return pl.pallas_call(
        functools.partial(_scan_body, d, has_init),
        grid=(batch, n),
        in_specs=in_specs,
        out_specs=[pl.BlockSpec((CHUNK, HYB_MIX), lambda b, i: (chunk(b, i), 0)), ret_spec, ssm_spec],
        out_shape=[
            jax.ShapeDtypeStruct((m, HYB_MIX), F32),
            jax.ShapeDtypeStruct((batch, RET_HEADS, RET_DK, RET_DV), F32),
            jax.ShapeDtypeStruct((batch, SSM_HEADS, SSM_STATE, SSM_HEADDIM), F32),
        ],
        scratch_shapes=[pltpu.VMEM((RET_HEADS, RET_DK, RET_DV), F32),
                        pltpu.VMEM((SSM_HEADS, SSM_STATE, SSM_HEADDIM), F32)],
        compiler_params=_cparams(("arbitrary", "arbitrary")),
        name=f"hybrid_scan_dir{d}",
    )(*args)


def _hyb_out_body(y0_ref, y1_ref, g_ref, z_ref, x_ref, gate_ref, gn_ref, ng_ref, w_ref, o_ref):
    y = y0_ref[...] + y1_ref[...]
    parts = []
    for h in range(RET_HEADS):
        yh = y[:, h * RET_DV:(h + 1) * RET_DV]
        mu = jnp.mean(yh, axis=-1, keepdims=True)
        var = jnp.mean(jnp.square(yh - mu), axis=-1, keepdims=True)
        parts.append((yh - mu) * lax.rsqrt(var + EPS))
    y_ret = jax.nn.silu(g_ref[...]) * (jnp.concatenate(parts, axis=1) * gn_ref[...])
    yz = y[:, RET_V:] * jax.nn.silu(z_ref[...])
    y_ssm = yz * lax.rsqrt(jnp.mean(yz * yz, axis=-1, keepdims=True) + EPS) * ng_ref[...]
    out = jnp.dot(y_ret.astype(BF16), w_ref[0:RET_V], preferred_element_type=F32)
    out = out + jnp.dot(y_ssm.astype(BF16), w_ref[RET_V:HYB_MIX], preferred_element_type=F32)
    o_ref[...] = x_ref[...] + gate_ref[0] * out


def _hyb_out(y0, y1, rest, x, gate, gn_gain, norm_gain, w_out, seq_len, tm):
    m = x.shape[0]
    per_batch = gate.shape[0] > 1
    tpb = seq_len // tm
    return pl.pallas_call(
        _hyb_out_body,
        grid=(m // tm,),
        in_specs=[
            pl.BlockSpec((tm, HYB_MIX), lambda i: (i, 0)),
            pl.BlockSpec((tm, HYB_MIX), lambda i: (i, 0)),
            pl.BlockSpec((tm, RET_V), lambda i: (i, 0)),
            pl.BlockSpec((tm, SSM_D_INNER), lambda i: (i, 1)),
            pl.BlockSpec((tm, D_MODEL), lambda i: (i, 0)),
            _vec_spec(D_MODEL, tpb, per_batch, 1),
            pl.BlockSpec((1, RET_V), lambda i: (0, 0)),
            pl.BlockSpec((1, SSM_D_INNER), lambda i: (0, 0)),
            pl.BlockSpec((HYB_MIX, D_MODEL), lambda i: (0, 0)),
        ],
        out_specs=pl.BlockSpec((tm, D_MODEL), lambda i: (i, 0)),
        out_shape=jax.ShapeDtypeStruct((m, D_MODEL), F32),
        compiler_params=_cparams(("arbitrary",)),
        name="hybrid_out_proj",
    )(y0, y1, rest, rest, x, gate, gn_gain, norm_gain, w_out)


def _head_rms(t, seg_ref):
    outs = []
    for c in range(t.shape[1] // LANES):
        tc = t[:, c * LANES:(c + 1) * LANES]
        sq = tc * tc
        hi = sq.astype(BF16)
        lo = (sq - hi.astype(F32)).astype(BF16)
        ssum = (jnp.dot(hi, seg_ref[...], preferred_element_type=F32)
                + jnp.dot(lo, seg_ref[...], preferred_element_type=F32))
        outs.append(tc * lax.rsqrt(ssum * (1.0 / ATT_HD) + EPS))
    return outs


def _rope(tc, cos, sin, lane):
    fwd = pltpu.roll(tc, LANES - ROT_FREQS, 1)
    bwd = pltpu.roll(tc, ROT_FREQS, 1)
    return tc * cos + jnp.where(lane % (2 * ROT_FREQS) < ROT_FREQS, fwd, bwd) * sin


def _qkv_body(has_rope, x_ref, gain_ref, sc_ref, sh_ref, w_ref, seg_ref, qg_ref, kg_ref, *rest):
    if has_rope:
        cos_ref, sin_ref, q_ref, k_ref, v_ref = rest
    else:
        q_ref, k_ref, v_ref, kf_ref, vf_ref = rest
    h = _norm_mod(x_ref[...], gain_ref[...], sc_ref[0], sh_ref[0]).astype(BF16)
    qkv = jnp.dot(h, w_ref[...], preferred_element_type=F32)
    q = _head_rms(qkv[:, 0:ATT_Q], seg_ref)
    k = _head_rms(qkv[:, ATT_Q:ATT_Q + ATT_KV], seg_ref)
    v = qkv[:, ATT_Q + ATT_KV:ATT_Q + 2 * ATT_KV]
    q = [t * qg_ref[...] for t in q]
    k = [t * kg_ref[...] for t in k]
    if has_rope:
        lane = lax.broadcasted_iota(jnp.int32, (1, LANES), 1)
        cos, sin = cos_ref[...], sin_ref[...]
        q = [_rope(t, cos, sin, lane) for t in q]
        kr = [_rope(t, cos, sin, lane) for t in k]
    else:
        kr = k
        kf_ref[...] = jnp.concatenate(k, axis=1)
        vf_ref[...] = v
    q_ref[...] = (jnp.concatenate(q, axis=1) * (ATT_HD ** -0.5)).astype(BF16)
    k_ref[...] = jnp.concatenate(kr, axis=1).astype(BF16)
    v_ref[...] = v.astype(BF16)


def _qkv_proj(x, gain, sc, sh, w_qkv, q_gain, k_gain, rope, seq_len, tm):
    m = x.shape[0]
    per_batch = sc.shape[0] > 1
    tpb = seq_len // tm
    has_rope = rope is not None
    seg = (lax.broadcasted_iota(jnp.int32, (LANES, LANES), 0) // ATT_HD
           == lax.broadcasted_iota(jnp.int32, (LANES, LANES), 1) // ATT_HD).astype(BF16)
    tile2 = lambda v: jnp.tile(v.astype(F32), LANES // ATT_HD).reshape(1, LANES)
    in_specs = [
        pl.BlockSpec((tm, D_MODEL), lambda i: (i, 0)),
        pl.BlockSpec((1, D_MODEL), lambda i: (0, 0)),
        _vec_spec(D_MODEL, tpb, per_batch, 1),
        _vec_spec(D_MODEL, tpb, per_batch, 1),
        pl.BlockSpec((D_MODEL, ATT_Q + 2 * ATT_KV), lambda i: (0, 0)),
        pl.BlockSpec((LANES, LANES), lambda i: (0, 0)),
        pl.BlockSpec((1, LANES), lambda i: (0, 0)),
        pl.BlockSpec((1, LANES), lambda i: (0, 0)),
    ]
    args = [x, gain, sc, sh, w_qkv, seg, tile2(q_gain), tile2(k_gain)]
    out_specs = [pl.BlockSpec((tm, ATT_Q), lambda i: (i, 0)),
                 pl.BlockSpec((tm, ATT_KV), lambda i: (i, 0)),
                 pl.BlockSpec((tm, ATT_KV), lambda i: (i, 0))]
    out_shape = [jax.ShapeDtypeStruct((m, ATT_Q), BF16), jax.ShapeDtypeStruct((m, ATT_KV), BF16),
                 jax.ShapeDtypeStruct((m, ATT_KV), BF16)]
    if has_rope:
        in_specs += [pl.BlockSpec((tm, LANES), lambda i: (i % tpb, 0))] * 2
        args += list(rope)
    else:
        out_specs += [pl.BlockSpec((tm, ATT_KV), lambda i: (i, 0))] * 2
        out_shape += [jax.ShapeDtypeStruct((m, ATT_KV), F32)] * 2
    return pl.pallas_call(
        functools.partial(_qkv_body, has_rope),
        grid=(m // tm,),
        in_specs=in_specs,
        out_specs=out_specs,
        out_shape=out_shape,
        compiler_params=_cparams(("arbitrary",)),
        name="attn_qkv_proj",
    )(*args)


def _rope_tables(seq_len):
    rows = seq_len // GRID_W
    row = jnp.repeat(jnp.arange(rows, dtype=F32), GRID_W)
    col = jnp.tile(jnp.arange(GRID_W, dtype=F32), rows)
    inv = ROPE_BASE ** (-jnp.arange(ROT_FREQS, dtype=F32) / ROT_FREQS)
    ar, ac = row[:, None] * inv, col[:, None] * inv
    cos = jnp.concatenate([jnp.cos(ar), jnp.cos(ar), jnp.cos(ac), jnp.cos(ac)], axis=1)
    sin = jnp.concatenate([-jnp.sin(ar), jnp.sin(ar), -jnp.sin(ac), jnp.sin(ac)], axis=1)
    return jnp.tile(cos, (1, LANES // ATT_HD)), jnp.tile(sin, (1, LANES // ATT_HD))


def _attn_body(tq, tk, q_ref, kt_ref, v_ref, x_ref, gate_ref, wo_ref, o_ref, o_scr):
    nkv = kt_ref.shape[2] // tk
    rows = ATT_RATIO * tq
    for g in range(ATT_KV_HEADS):
        qg = jnp.concatenate(
            [q_ref[0, :, (g * ATT_RATIO + r) * ATT_HD:(g * ATT_RATIO + r + 1) * ATT_HD] for r in range(ATT_RATIO)],
            axis=0)

        def kv_step(t, carry):
            m_prev, acc = carry
            start = pl.multiple_of(t * tk, tk)
            kt = kt_ref[0, g * ATT_HD:(g + 1) * ATT_HD, pl.ds(start, tk)]
            va = v_ref[0, pl.ds(start, tk), g * LANES:(g + 1) * LANES]
            s = jnp.dot(qg, kt, preferred_element_type=F32)
            m_new = jnp.maximum(m_prev, jnp.max(s, axis=-1, keepdims=True))
            p = jnp.exp(s - m_new).astype(BF16)
            acc = jnp.exp(m_prev - m_new) * acc + jnp.dot(p, va, preferred_element_type=F32)
            return m_new, acc

        init = (jnp.full((rows, 1), NEG_BIG, F32), jnp.zeros((rows, LANES), F32))
        _, acc = lax.fori_loop(0, nkv, kv_step, init)
        og = acc[:, 0:ATT_HD] / acc[:, ATT_HD:ATT_HD + 1]
        for r in range(ATT_RATIO):
            hd = g * ATT_RATIO + r
            o_scr[:, hd * ATT_HD:(hd + 1) * ATT_HD] = og[r * tq:(r + 1) * tq]
    out = jnp.dot(o_scr[...].astype(BF16), wo_ref[...], preferred_element_type=F32)
    o_ref[0] = x_ref[0] + gate_ref[0] * out


def _attention(q, kt, va, x, gate, w_o, batch, seq_len, tq, tk):
    per_batch = gate.shape[0] > 1
    nkeys = kt.shape[2]
    q3 = q.reshape(batch, seq_len, ATT_Q)
    x3 = x.reshape(batch, seq_len, D_MODEL)
    out = pl.pallas_call(
        functools.partial(_attn_body, tq, tk),
        grid=(batch, seq_len // tq),
        in_specs=[
            pl.BlockSpec((1, tq, ATT_Q), lambda b, i: (b, i, 0)),
            pl.BlockSpec((1, ATT_KV, nkeys), lambda b, i: (b, 0, 0)),
            pl.BlockSpec((1, nkeys, ATT_KV_HEADS * LANES), lambda b, i: (b, 0, 0)),
            pl.BlockSpec((1, tq, D_MODEL), lambda b, i: (b, i, 0)),
            pl.BlockSpec((1, 1, D_MODEL), lambda b, i: (b if per_batch else 0, 0, 0)),
            pl.BlockSpec((ATT_Q, D_MODEL), lambda b, i: (0, 0)),
        ],
        out_specs=pl.BlockSpec((1, tq, D_MODEL), lambda b, i: (b, i, 0)),
        out_shape=jax.ShapeDtypeStruct((batch, seq_len, D_MODEL), F32),
        scratch_shapes=[pltpu.VMEM((tq, ATT_Q), F32)],
        compiler_params=_cparams(("arbitrary", "arbitrary")),
        name="attention",
    )(q3, kt, va, x3, gate, w_o)
    return out.reshape(batch * seq_len, D_MODEL)


def _keys_values(k, v, batch):
    n = k.shape[0] // batch
    kt = jnp.swapaxes(k.reshape(batch, n, ATT_KV), 1, 2)
    v4 = v.reshape(batch, n, ATT_KV_HEADS, ATT_HD)
    va = jnp.concatenate([v4, jnp.ones_like(v4)], axis=-1).reshape(batch, n, ATT_KV_HEADS * LANES)
    return kt, va


def _trunk(x, mods, caches, P, batch, seq_len, tm, tq, tk):
    rope = None if caches is None else _rope_tables(seq_len)
    outs = {}
    sh1, sc1, g1, sh2, sc2, g2 = mods[0]
    qkv = _nm_matmul(x, P['norm_mix'][0:1], sc1, sh1, P['hyb_w_qkv'], BF16, seq_len, tm, 512, "hybrid_in_proj_qkv")
    rest = _nm_matmul(x, P['norm_mix'][0:1], sc1, sh1, P['hyb_w_rest'], F32, seq_len, tm, HYB_REST // 11,
                      "hybrid_in_proj_rest")
    ys, rets, ssms = [], [], []
    for d in range(2):
        s_ret0 = None if caches is None else caches[0][:, 0, d]
        s_ssm0 = None if caches is None else caches[1][:, 0, d]
        y, s_ret, s_ssm = _scan(d, qkv, rest, P['ssm_conv_w'][0], P['ssm_conv_b'][0], P['ssm_dt_bias'][0],
                                P['ssm_a_log'][0], P['ret_log_decay'][0], P['ssm_d'][0], s_ret0, s_ssm0,
                                batch, seq_len)
        ys.append(y)
        rets.append(s_ret)
        ssms.append(s_ssm)
    outs['ret'] = jnp.stack(rets, axis=1)[:, None]
    outs['ssm'] = jnp.stack(ssms, axis=1)[:, None]
    x = _hyb_out(ys[0], ys[1], rest, x, g1, P['ret_gn_gain'][0:1], P['ssm_norm_gain'][0:1], P['hyb_w_out'],
                 seq_len, tm)
    x = _conv_ffn(x, P['norm_ffn'][0:1], sc2, sh2, g2, P['ffn_w_up'][0], P['ffn_conv_w'][0], P['ffn_conv_b'][0],
                  P['ffn_w_down'][0], seq_len, tm, 256)
    sh1, sc1, g1, sh2, sc2, g2 = mods[1]
    res = _qkv_proj(x, P['norm_mix'][1:2], sc1, sh1, P['attn_w_qkv'], P['attn_q_gain'][0], P['attn_k_gain'][0],
                    rope, seq_len, tm)
    if caches is None:
        q, k, v, kf, vf = res
        outs['k'] = kf.reshape(batch, 1, seq_len, ATT_KV_HEADS, ATT_HD)
        outs['v'] = vf.reshape(batch, 1, seq_len, ATT_KV_HEADS, ATT_HD)
        kt, va = _keys_values(k, v, batch)
    else:
        q, k, v = res
        past = caches[2].shape[2]
        ck = caches[2][:, 0].reshape(batch, past, ATT_KV).astype(BF16)
        cv = caches[3][:, 0].reshape(batch, past, ATT_KV).astype(BF16)
        k_all = jnp.concatenate([ck, k.reshape(batch, seq_len, ATT_KV)], axis=1)
        v_all = jnp.concatenate([cv, v.reshape(batch, seq_len, ATT_KV)], axis=1)
        kt, va = _keys_values(k_all.reshape(-1, ATT_KV), v_all.reshape(-1, ATT_KV), batch)
    x = _attention(q, kt, va, x, g1, P['attn_w_o'], batch, seq_len, tq, tk)
    x = _conv_ffn(x, P['norm_ffn'][1:2], sc2, sh2, g2, P['ffn_w_up'][1], P['ffn_conv_w'][1], P['ffn_conv_b'][1],
                  P['ffn_w_down'][1], seq_len, tm, 256)
    return x, outs


def kernel(x_prompt, x_sample, state_ret, state_ssm, cache_attn_k, cache_attn_v, c, c_ctx, w_mod, b_mod, norm_mix,
           norm_ffn, ffn_w_up, ffn_conv_w, ffn_conv_b, ffn_w_down, hyb_w_in, hyb_w_out, ret_log_decay, ret_gn_gain,
           ssm_conv_w, ssm_conv_b, ssm_a_log, ssm_dt_bias, ssm_d, ssm_norm_gain, attn_w_qkv, attn_q_gain,
           attn_k_gain, attn_w_o):
    batch, seq, _ = x_prompt.shape
    dec_batch, dec_seq, _ = x_sample.shape
    depth = w_mod.shape[0]

    rows = -(-(dec_batch + 1) // SUBLANES) * SUBLANES
    cond = jnp.concatenate([c, c_ctx[None, :], jnp.zeros((rows - dec_batch - 1, D_MODEL), F32)], axis=0)
    mod = _modulation(cond, w_mod, b_mod).reshape(depth, rows, 6, 1, D_MODEL)
    mods_sample = [[mod[l, 0:dec_batch, t] for t in range(6)] for l in range(depth)]
    mods_prompt = [[mod[l, dec_batch:dec_batch + 1, t] for t in range(6)] for l in range(depth)]

    w_in = hyb_w_in[0]
    kscale = jnp.concatenate([jnp.ones((RET_Q,), F32), jnp.full((RET_Q,), RET_DK ** -0.5, F32),
                              jnp.ones((RET_V,), F32)])
    n_main = HYB_QKV + 2 * RET_V + SSM_XBC
    P = {
        'norm_mix': norm_mix, 'norm_ffn': norm_ffn,
        'hyb_w_qkv': (w_in[:, 0:HYB_QKV] * kscale).astype(BF16),
        'hyb_w_rest': jnp.pad(w_in[:, HYB_QKV:], ((0, 0), (0, HYB_QKV + HYB_REST - w_in.shape[1]))).astype(BF16),
        'hyb_w_out': hyb_w_out[0].astype(BF16),
        'ret_log_decay': ret_log_decay, 'ret_gn_gain': ret_gn_gain,
        'ssm_conv_w': ssm_conv_w, 'ssm_conv_b': ssm_conv_b, 'ssm_a_log': ssm_a_log, 'ssm_dt_bias': ssm_dt_bias,
        'ssm_d': ssm_d, 'ssm_norm_gain': ssm_norm_gain,
        'attn_w_qkv': attn_w_qkv[0].astype(BF16), 'attn_q_gain': attn_q_gain, 'attn_k_gain': attn_k_gain,
        'attn_w_o': attn_w_o[0].astype(BF16),
        'ffn_w_up': ffn_w_up.astype(BF16), 'ffn_conv_w': ffn_conv_w, 'ffn_conv_b': ffn_conv_b,
        'ffn_w_down': ffn_w_down.astype(BF16),
    }
    del n_main

    y_prompt, outs = _trunk(x_prompt.reshape(batch * seq, D_MODEL), mods_prompt, None, P, batch, seq,
                            tm=min(256, seq), tq=min(256, seq), tk=min(256, seq))
    caches = (state_ret, state_ssm, cache_attn_k, cache_attn_v)
    y_sample, _ = _trunk(x_sample.reshape(dec_batch * dec_seq, D_MODEL), mods_sample, caches, P, dec_batch, dec_seq,
                         tm=512, tq=128, tk=256)
    return (y_prompt.reshape(batch, seq, D_MODEL), y_sample.reshape(dec_batch, dec_seq, D_MODEL),
            outs['ret'], outs['ssm'], outs['k'], outs['v'])
```

```python
import functools

import jax
import jax.numpy as jnp
from jax import lax
from jax.experimental import pallas as pl
from jax.experimental.pallas import tpu as pltpu

F32 = jnp.float32
BF16 = jnp.bfloat16
HIGHEST = lax.Precision.HIGHEST

D_MODEL = 1024
EPS = 1e-6
GRID_W = 64
CHUNK = 128
RET_HEADS = 8
RET_DK = 64
RET_DV = 128
RET_Q = RET_HEADS * RET_DK
RET_V = RET_HEADS * RET_DV
SSM_D_INNER = 1024
SSM_HEADDIM = 64
SSM_HEADS = 16
SSM_GROUPS = 4
SSM_RATIO = 4
SSM_STATE = 128
SSM_BC = SSM_GROUPS * SSM_STATE
SSM_XBC = SSM_D_INNER + 2 * SSM_BC
HYB_QKV = 2 * RET_Q + RET_V
HYB_REST = 2 * RET_V + SSM_XBC + 128
HYB_MIX = RET_V + SSM_D_INNER
ATT_HEADS = 16
ATT_KV_HEADS = 4
ATT_RATIO = 4
ATT_HD = 64
ATT_Q = ATT_HEADS * ATT_HD
ATT_KV = ATT_KV_HEADS * ATT_HD
ROT_FREQS = ATT_HD // 4
ROPE_BASE = 10000.0
FFN_HIDDEN = 2816

LANES = 128
SUBLANES = 8
VMEM_LIMIT = 56 * 1024 * 1024
NEG_BIG = -1e30
Q_SCALE = ATT_HD ** -0.5 * 1.4426950408889634


def _cparams(sem):
    return pltpu.CompilerParams(dimension_semantics=sem, vmem_limit_bytes=VMEM_LIMIT)


def _norm_mod(x, gain, sc, sh):
    ms = jnp.mean(x * x, axis=-1, keepdims=True)
    return x * lax.rsqrt(ms + EPS) * gain * (1.0 + sc) + sh


def _vec_spec(width, tiles_per_batch, per_batch, grid_rank):
    if grid_rank == 1:
        return pl.BlockSpec((1, 1, width), lambda i: ((i // tiles_per_batch) if per_batch else 0, 0, 0))
    return pl.BlockSpec((1, 1, width), lambda i, j: ((i // tiles_per_batch) if per_batch else 0, 0, 0))


def _mod_body(c_ref, w_ref, b_ref, o_ref):
    cs = jax.nn.silu(c_ref[...])
    o_ref[0] = jnp.dot(cs, w_ref[0], precision=HIGHEST, preferred_element_type=F32) + b_ref[0]


def _modulation(cond, w_mod, b_mod):
    depth, _, n = w_mod.shape
    rows = cond.shape[0]
    tn = n // 4
    return pl.pallas_call(
        _mod_body,
        grid=(depth, n // tn),
        in_specs=[
            pl.BlockSpec((rows, D_MODEL), lambda l, j: (0, 0)),
            pl.BlockSpec((1, D_MODEL, tn), lambda l, j: (l, 0, j)),
            pl.BlockSpec((1, 1, tn), lambda l, j: (l, 0, j)),
        ],
        out_specs=pl.BlockSpec((1, rows, tn), lambda l, j: (l, 0, j)),
        out_shape=jax.ShapeDtypeStruct((depth, rows, n), F32),
        compiler_params=_cparams(("arbitrary", "arbitrary")),
        name="modulation",
    )(cond, w_mod, b_mod.reshape(depth, 1, n))


def _chunks(total, width):
    return [(s, min(width, total - s)) for s in range(0, total, width)]


def _resident(shape):
    return pl.BlockSpec(shape, lambda *_: (0,) * len(shape), pipeline_mode=pl.Buffered(1))


def _hyb_in_body(tn, x_ref, gain_ref, sc_ref, sh_ref, w_ref, qkv_ref, rest_ref):
    h = _norm_mod(x_ref[...], gain_ref[...], sc_ref[0], sh_ref[0]).astype(BF16)
    for col, width in _chunks(HYB_QKV, tn):
        qkv_ref[:, col:col + width] = jnp.dot(h, w_ref[:, col:col + width],
                                              preferred_element_type=F32).astype(BF16)
    for col, width in _chunks(HYB_REST, tn):
        rest_ref[:, col:col + width] = jnp.dot(h, w_ref[:, HYB_QKV + col:HYB_QKV + col + width],
                                               preferred_element_type=F32)


def _hyb_in_proj(x, gain, sc, sh, w, seq_len, tm, tn):
    m = x.shape[0]
    per_batch = sc.shape[0] > 1
    tpb = seq_len // tm
    return pl.pallas_call(
        functools.partial(_hyb_in_body, tn),
        grid=(m // tm,),
        in_specs=[
            pl.BlockSpec((tm, D_MODEL), lambda i: (i, 0)),
            pl.BlockSpec((1, D_MODEL), lambda i: (0, 0)),
            _vec_spec(D_MODEL, tpb, per_batch, 1),
            _vec_spec(D_MODEL, tpb, per_batch, 1),
            _resident((D_MODEL, HYB_QKV + HYB_REST)),
        ],
        out_specs=[pl.BlockSpec((tm, HYB_QKV), lambda i: (i, 0)), pl.BlockSpec((tm, HYB_REST), lambda i: (i, 0))],
        out_shape=[jax.ShapeDtypeStruct((m, HYB_QKV), BF16), jax.ShapeDtypeStruct((m, HYB_REST), F32)],
        compiler_params=_cparams(("arbitrary",)),
        name="hybrid_in_proj",
    )(x, gain, sc, sh, w)


def _ffn_body(tiles_per_seq, tm, th, x_ref, xn_ref, xp_ref, gain_ref, sc_ref, sh_ref, gate_ref,
              wu_ref, cw_ref, cb_ref, wd_ref, o_ref, h_scr, act_scr):
    i = pl.program_id(0)
    gain, sc, sh = gain_ref[...], sc_ref[0], sh_ref[0]
    h_scr[0:tm] = _norm_mod(x_ref[...], gain, sc, sh).astype(BF16)
    pos = i % tiles_per_seq
    hn = jnp.where(pos == tiles_per_seq - 1, 0.0, _norm_mod(xn_ref[0], gain, sc, sh))
    hp = jnp.where(pos == 0, 0.0, _norm_mod(xp_ref[0], gain, sc, sh))
    h_scr[tm:tm + 2 * SUBLANES] = jnp.concatenate([hn, hp], axis=0).astype(BF16)
    h = h_scr[...]
    rows = tm + 2 * SUBLANES

    def conv_up(col, width):
        u = jnp.dot(h, wu_ref[:, col:col + width], preferred_element_type=F32)
        c = (pltpu.roll(u, 1, 0) * cw_ref[0:1, col:col + width] + u * cw_ref[1:2, col:col + width]
             + pltpu.roll(u, rows - 1, 0) * cw_ref[2:3, col:col + width] + cb_ref[:, col:col + width])
        return c[0:tm]

    for col, width in _chunks(FFN_HIDDEN, th):
        act = jax.nn.silu(conv_up(col, width)) * conv_up(FFN_HIDDEN + col, width)
        act_scr[:, col:col + width] = act.astype(BF16)
    out = jnp.dot(act_scr[...], wd_ref[...], preferred_element_type=F32)
    o_ref[...] = x_ref[...] + gate_ref[0] * out


def _conv_ffn(x, gain, sc, sh, gate, w_up, conv_w, conv_b, w_down, seq_len, tm, th):
    m = x.shape[0]
    per_batch = sc.shape[0] > 1
    tpb = seq_len // tm
    m8 = m // SUBLANES
    r8 = tm // SUBLANES
    x8 = x.reshape(m8, SUBLANES, D_MODEL)
    conv_b = conv_b.reshape(1, 2 * FFN_HIDDEN)
    vec = _vec_spec(D_MODEL, tpb, per_batch, 1)
    return pl.pallas_call(
        functools.partial(_ffn_body, tpb, tm, th),
        grid=(m // tm,),
        in_specs=[
            pl.BlockSpec((tm, D_MODEL), lambda i: (i, 0)),
            pl.BlockSpec((1, SUBLANES, D_MODEL), lambda i: (jnp.minimum((i + 1) * r8, m8 - 1), 0, 0)),
            pl.BlockSpec((1, SUBLANES, D_MODEL), lambda i: (jnp.maximum(i * r8 - 1, 0), 0, 0)),
            pl.BlockSpec((1, D_MODEL), lambda i: (0, 0)),
            vec, vec, vec,
            _resident((D_MODEL, 2 * FFN_HIDDEN)),
            _resident((3, 2 * FFN_HIDDEN)),
            _resident((1, 2 * FFN_HIDDEN)),
            _resident((FFN_HIDDEN, D_MODEL)),
        ],
        out_specs=pl.BlockSpec((tm, D_MODEL), lambda i: (i, 0)),
        out_shape=jax.ShapeDtypeStruct((m, D_MODEL), F32),
        scratch_shapes=[pltpu.VMEM((tm + 2 * SUBLANES, D_MODEL), BF16), pltpu.VMEM((tm, FFN_HIDDEN), BF16)],
        compiler_params=_cparams(("arbitrary",)),
        name="conv_ffn",
    )(x, x8, x8, gain, sc, sh, gate, w_up, conv_w, conv_b, w_down)


def _scan_body(d, has_init, qkv_ref, xbc_ref, xn_ref, xp_ref, dt_ref, cw_ref, cb_ref, dtb_ref, alog_ref,
               logdec_ref, dskip_ref, *rest):
    if has_init:
        sret0_ref, sssm0_ref = rest[0], rest[1]
        rest = rest[2:]
    y_ref, sret_ref, sssm_ref, s_ret, s_ssm = rest
    i = pl.program_id(1)
    n = pl.num_programs(1)
    c = i if d == 0 else n - 1 - i

    @pl.when(i == 0)
    def _():
        if has_init:
            s_ret[...] = sret0_ref[0]
            s_ssm[...] = sssm0_ref[0]
        else:
            s_ret[...] = jnp.zeros_like(s_ret)
            s_ssm[...] = jnp.zeros_like(s_ssm)

    ii = lax.broadcasted_iota(jnp.int32, (CHUNK, CHUNK), 0)
    jj = lax.broadcasted_iota(jnp.int32, (CHUNK, CHUNK), 1)
    rel = ((ii - jj) if d == 0 else (jj - ii)).astype(F32)
    mask = rel >= 0.0
    row = lax.broadcasted_iota(jnp.int32, (CHUNK, 1), 0).astype(F32)
    steps_in = (row + 1.0) if d == 0 else (CHUNK - row)
    steps_out = (CHUNK - 1.0 - row) if d == 0 else row
    full = jnp.full((1, LANES), float(CHUNK), F32)

    for h in range(RET_HEADS):
        la = logdec_ref[d, h]
        qh = qkv_ref[:, h * RET_DK:(h + 1) * RET_DK]
        kh = qkv_ref[:, RET_Q + h * RET_DK:RET_Q + (h + 1) * RET_DK]
        vh = qkv_ref[:, 2 * RET_Q + h * RET_DV:2 * RET_Q + (h + 1) * RET_DV]
        decay = jnp.exp(jnp.where(mask, rel * la, -jnp.inf))
        s = lax.dot_general(qh, kh, (((1,), (1,)), ((), ())), preferred_element_type=F32)
        p = (s * decay).astype(BF16)
        y = jnp.dot(p, vh, preferred_element_type=F32)
        y = y + jnp.dot(qh, s_ret[h].astype(BF16), preferred_element_type=F32) * jnp.exp(steps_in * la)
        y_ref[:, h * RET_DV:(h + 1) * RET_DV] = y
        vw = (vh.astype(F32) * jnp.exp(steps_out * la)).astype(BF16)
        upd = lax.dot_general(kh, vw, (((0,), (0,)), ((), ())), preferred_element_type=F32)
        s_ret[h] = s_ret[h] * jnp.exp(full * la) + upd

    xn = jnp.where(c == n - 1, 0.0, xn_ref[0])
    xp = jnp.where(c == 0, 0.0, xp_ref[0])
    xcat = jnp.concatenate([xbc_ref[...], xn, xp], axis=0)
    rows = CHUNK + 2 * SUBLANES
    xc = (pltpu.roll(xcat, 1, 0) * cw_ref[0:1] + xcat * cw_ref[1:2]
          + pltpu.roll(xcat, rows - 1, 0) * cw_ref[2:3] + cb_ref[...])
    xc = jax.nn.silu(xc[0:CHUNK])
    xs = xc[:, 0:SSM_D_INNER]
    bm = xc[:, SSM_D_INNER:SSM_D_INNER + SSM_BC]
    cm = xc[:, SSM_D_INNER + SSM_BC:SSM_XBC]

    dt = jax.nn.softplus(dt_ref[...] + dtb_ref[...])
    la = dt * (-jnp.exp(alog_ref[...]))
    cum = jnp.dot(mask.astype(F32), la, precision=HIGHEST, preferred_element_type=F32)
    cum_t = cum.T
    last = cum[CHUNK - 1:CHUNK, :] if d == 0 else cum[0:1, :]
    e_in = jnp.exp(cum)
    e_out = jnp.exp(last - cum)
    e_full = jnp.exp(last)

    for g in range(SSM_GROUPS):
        cg = cm[:, g * SSM_STATE:(g + 1) * SSM_STATE].astype(BF16)
        bg = bm[:, g * SSM_STATE:(g + 1) * SSM_STATE].astype(BF16)
        s = lax.dot_general(cg, bg, (((1,), (1,)), ((), ())), preferred_element_type=F32)
        ys = []
        for r in range(SSM_RATIO):
            h = g * SSM_RATIO + r
            decay = jnp.exp(jnp.where(mask, cum[:, h:h + 1] - cum_t[h:h + 1, :], -jnp.inf))
            p = (s * decay).astype(BF16)
            xh = xs[:, h * SSM_HEADDIM:(h + 1) * SSM_HEADDIM]
            vh = xh * dt[:, h:h + 1]
            y = jnp.dot(p, vh.astype(BF16), preferred_element_type=F32)
            y = y + jnp.dot(cg, s_ssm[h].astype(BF16), preferred_element_type=F32) * e_in[:, h:h + 1]
            ys.append(y + dskip_ref[d, h] * xh)
            vw = (vh * e_out[:, h:h + 1]).astype(BF16)
            upd = lax.dot_general(bg, vw, (((0,), (0,)), ((), ())), preferred_element_type=F32)
            s_ssm[h] = s_ssm[h] * e_full[:, h:h + 1] + upd
        for r in range(0, SSM_RATIO, 2):
            col = RET_V + (g * SSM_RATIO + r) * SSM_HEADDIM
            y_ref[:, col:col + 2 * SSM_HEADDIM] = jnp.concatenate([ys[r], ys[r + 1]], axis=1)

    @pl.when(i == n - 1)
    def _():
        sret_ref[0] = s_ret[...]
        sssm_ref[0] = s_ssm[...]


def _scan(d, qkv, rest, conv_w, conv_b, dt_bias, a_log, log_decay, d_skip, s_ret0, s_ssm0, batch, seq_len):
    m = qkv.shape[0]
    n = seq_len // CHUNK
    m8 = m // SUBLANES
    r8 = CHUNK // SUBLANES
    has_init = s_ret0 is not None
    rest8 = rest.reshape(m8, SUBLANES, HYB_REST)
    xbc_blk = 2 * RET_V // SSM_XBC
    dt_blk = (2 * RET_V + SSM_XBC) // LANES

    def chunk(b, i):
        return b * n + (i if d == 0 else n - 1 - i)

    pad = lambda v: jnp.pad(v.astype(F32), (0, LANES - v.shape[0])).reshape(1, LANES)
    smem = pl.BlockSpec(memory_space=pltpu.SMEM)
    in_specs = [
        pl.BlockSpec((CHUNK, HYB_QKV), lambda b, i: (chunk(b, i), 0)),
        pl.BlockSpec((CHUNK, SSM_XBC), lambda b, i: (chunk(b, i), xbc_blk)),
        pl.BlockSpec((1, SUBLANES, SSM_XBC), lambda b, i: (jnp.minimum((chunk(b, i) + 1) * r8, m8 - 1), 0, xbc_blk)),
        pl.BlockSpec((1, SUBLANES, SSM_XBC), lambda b, i: (jnp.maximum(chunk(b, i) * r8 - 1, 0), 0, xbc_blk)),
        pl.BlockSpec((CHUNK, LANES), lambda b, i: (chunk(b, i), dt_blk)),
        pl.BlockSpec((3, SSM_XBC), lambda b, i: (0, 0)),
        pl.BlockSpec((1, SSM_XBC), lambda b, i: (0, 0)),
        pl.BlockSpec((1, LANES), lambda b, i: (0, 0)),
        pl.BlockSpec((1, LANES), lambda b, i: (0, 0)),
        smem, smem,
    ]
    args = [qkv, rest, rest8, rest8, rest, conv_w, conv_b.reshape(1, SSM_XBC), pad(dt_bias[d]), pad(a_log[d]),
            log_decay.astype(F32), d_skip.astype(F32)]
    ret_spec = pl.BlockSpec((1, RET_HEADS, RET_DK, RET_DV), lambda b, i: (b, 0, 0, 0))
    ssm_spec = pl.BlockSpec((1, SSM_HEADS, SSM_STATE, SSM_HEADDIM), lambda b, i: (b, 0, 0, 0))
    if has_init:
        in_specs += [ret_spec, ssm_spec]
        args += [s_ret0, s_ssm0]
    return pl.pallas_call(
        functools.partial(_scan_body, d, has_init),
        grid=(batch, n),
        in_specs=in_specs,
        out_specs=[pl.BlockSpec((CHUNK, HYB_MIX), lambda b, i: (chunk(b, i), 0)), ret_spec, ssm_spec],
        out_shape=[
            jax.ShapeDtypeStruct((m, HYB_MIX), F32),
            jax.ShapeDtypeStruct((batch, RET_HEADS, RET_DK, RET_DV), F32),
            jax.ShapeDtypeStruct((batch, SSM_HEADS, SSM_STATE, SSM_HEADDIM), F32),
        ],
        scratch_shapes=[pltpu.VMEM((RET_HEADS, RET_DK, RET_DV), F32),
                        pltpu.VMEM((SSM_HEADS, SSM_STATE, SSM_HEADDIM), F32)],
        compiler_params=_cparams(("arbitrary", "arbitrary")),
        name=f"hybrid_scan_dir{d}",
    )(*args)


def _hyb_out_body(y0_ref, y1_ref, g_ref, z_ref, x_ref, gate_ref, gn_ref, ng_ref, w_ref, o_ref):
    y = y0_ref[...] + y1_ref[...]
    parts = []
    for h in range(RET_HEADS):
        yh = y[:, h * RET_DV:(h + 1) * RET_DV]
        mu = jnp.mean(yh, axis=-1, keepdims=True)
        var = jnp.mean(jnp.square(yh - mu), axis=-1, keepdims=True)
        parts.append((yh - mu) * lax.rsqrt(var + EPS))
    y_ret = jax.nn.silu(g_ref[...]) * (jnp.concatenate(parts, axis=1) * gn_ref[...])
    yz = y[:, RET_V:] * jax.nn.silu(z_ref[...])
    y_ssm = yz * lax.rsqrt(jnp.mean(yz * yz, axis=-1, keepdims=True) + EPS) * ng_ref[...]
    out = jnp.dot(y_ret.astype(BF16), w_ref[0:RET_V], preferred_element_type=F32)
    out = out + jnp.dot(y_ssm.astype(BF16), w_ref[RET_V:HYB_MIX], preferred_element_type=F32)
    o_ref[...] = x_ref[...] + gate_ref[0] * out


def _hyb_out(y0, y1, rest, x, gate, gn_gain, norm_gain, w_out, seq_len, tm):
    m = x.shape[0]
    per_batch = gate.shape[0] > 1
    tpb = seq_len // tm
    return pl.pallas_call(
        _hyb_out_body,
        grid=(m // tm,),
        in_specs=[
            pl.BlockSpec((tm, HYB_MIX), lambda i: (i, 0)),
            pl.BlockSpec((tm, HYB_MIX), lambda i: (i, 0)),
            pl.BlockSpec((tm, RET_V), lambda i: (i, 0)),
            pl.BlockSpec((tm, SSM_D_INNER), lambda i: (i, 1)),
            pl.BlockSpec((tm, D_MODEL), lambda i: (i, 0)),
            _vec_spec(D_MODEL, tpb, per_batch, 1),
            pl.BlockSpec((1, RET_V), lambda i: (0, 0)),
            pl.BlockSpec((1, SSM_D_INNER), lambda i: (0, 0)),
            pl.BlockSpec((HYB_MIX, D_MODEL), lambda i: (0, 0)),
        ],
        out_specs=pl.BlockSpec((tm, D_MODEL), lambda i: (i, 0)),
        out_shape=jax.ShapeDtypeStruct((m, D_MODEL), F32),
        compiler_params=_cparams(("arbitrary",)),
        name="hybrid_out_proj",
    )(y0, y1, rest, rest, x, gate, gn_gain, norm_gain, w_out)


def _head_rms(t, seg_ref):
    outs = []
    for c in range(t.shape[1] // LANES):
        tc = t[:, c * LANES:(c + 1) * LANES]
        sq = tc * tc
        hi = sq.astype(BF16)
        lo = (sq - hi.astype(F32)).astype(BF16)
        ssum = (jnp.dot(hi, seg_ref[...], preferred_element_type=F32)
                + jnp.dot(lo, seg_ref[...], preferred_element_type=F32))
        outs.append(tc * lax.rsqrt(ssum * (1.0 / ATT_HD) + EPS))
    return outs


def _rope(tc, cos, sin, lane):
    fwd = pltpu.roll(tc, LANES - ROT_FREQS, 1)
    bwd = pltpu.roll(tc, ROT_FREQS, 1)
    return tc * cos + jnp.where(lane % (2 * ROT_FREQS) < ROT_FREQS, fwd, bwd) * sin


def _qkv_body(has_rope, x_ref, gain_ref, sc_ref, sh_ref, w_ref, seg_ref, qg_ref, kg_ref, *rest):
    if has_rope:
        cos_ref, sin_ref, q_ref, k_ref, v_ref = rest
    else:
        q_ref, k_ref, v_ref, kf_ref, vf_ref = rest
    h = _norm_mod(x_ref[...], gain_ref[...], sc_ref[0], sh_ref[0]).astype(BF16)
    qkv = jnp.dot(h, w_ref[...], preferred_element_type=F32)
    q = _head_rms(qkv[:, 0:ATT_Q], seg_ref)
    k = _head_rms(qkv[:, ATT_Q:ATT_Q + ATT_KV], seg_ref)
    v = qkv[:, ATT_Q + ATT_KV:ATT_Q + 2 * ATT_KV]
    q = [t * qg_ref[...] for t in q]
    k = [t * kg_ref[...] for t in k]
    if has_rope:
        lane = lax.broadcasted_iota(jnp.int32, (1, LANES), 1)
        cos, sin = cos_ref[...], sin_ref[...]
        q = [_rope(t, cos, sin, lane) for t in q]
        kr = [_rope(t, cos, sin, lane) for t in k]
    else:
        kr = k
        kf_ref[...] = jnp.concatenate(k, axis=1)
        vf_ref[...] = v
    q_ref[...] = (jnp.concatenate(q, axis=1) * Q_SCALE).astype(BF16)
    k_ref[...] = jnp.concatenate(kr, axis=1).astype(BF16)
    v_ref[...] = v.astype(BF16)


def _qkv_proj(x, gain, sc, sh, w_qkv, q_gain, k_gain, rope, seq_len, tm):
    m = x.shape[0]
    per_batch = sc.shape[0] > 1
    tpb = seq_len // tm
    has_rope = rope is not None
    seg = (lax.broadcasted_iota(jnp.int32, (LANES, LANES), 0) // ATT_HD
           == lax.broadcasted_iota(jnp.int32, (LANES, LANES), 1) // ATT_HD).astype(BF16)
    tile2 = lambda v: jnp.tile(v.astype(F32), LANES // ATT_HD).reshape(1, LANES)
    in_specs = [
        pl.BlockSpec((tm, D_MODEL), lambda i: (i, 0)),
        pl.BlockSpec((1, D_MODEL), lambda i: (0, 0)),
        _vec_spec(D_MODEL, tpb, per_batch, 1),
        _vec_spec(D_MODEL, tpb, per_batch, 1),
        pl.BlockSpec((D_MODEL, ATT_Q + 2 * ATT_KV), lambda i: (0, 0)),
        pl.BlockSpec((LANES, LANES), lambda i: (0, 0)),
        pl.BlockSpec((1, LANES), lambda i: (0, 0)),
        pl.BlockSpec((1, LANES), lambda i: (0, 0)),
    ]
    args = [x, gain, sc, sh, w_qkv, seg, tile2(q_gain), tile2(k_gain)]
    out_specs = [pl.BlockSpec((tm, ATT_Q), lambda i: (i, 0)),
                 pl.BlockSpec((tm, ATT_KV), lambda i: (i, 0)),
                 pl.BlockSpec((tm, ATT_KV), lambda i: (i, 0))]
    out_shape = [jax.ShapeDtypeStruct((m, ATT_Q), BF16), jax.ShapeDtypeStruct((m, ATT_KV), BF16),
                 jax.ShapeDtypeStruct((m, ATT_KV), BF16)]
    if has_rope:
        in_specs += [pl.BlockSpec((tm, LANES), lambda i: (i % tpb, 0))] * 2
        args += list(rope)
    else:
        out_specs += [pl.BlockSpec((tm, ATT_KV), lambda i: (i, 0))] * 2
        out_shape += [jax.ShapeDtypeStruct((m, ATT_KV), F32)] * 2
    return pl.pallas_call(
        functools.partial(_qkv_body, has_rope),
        grid=(m // tm,),
        in_specs=in_specs,
        out_specs=out_specs,
        out_shape=out_shape,
        compiler_params=_cparams(("arbitrary",)),
        name="attn_qkv_proj",
    )(*args)


def _rope_tables(seq_len):
    rows = seq_len // GRID_W
    row = jnp.repeat(jnp.arange(rows, dtype=F32), GRID_W)
    col = jnp.tile(jnp.arange(GRID_W, dtype=F32), rows)
    inv = ROPE_BASE ** (-jnp.arange(ROT_FREQS, dtype=F32) / ROT_FREQS)
    ar, ac = row[:, None] * inv, col[:, None] * inv
    cos = jnp.concatenate([jnp.cos(ar), jnp.cos(ar), jnp.cos(ac), jnp.cos(ac)], axis=1)
    sin = jnp.concatenate([-jnp.sin(ar), jnp.sin(ar), -jnp.sin(ac), jnp.sin(ac)], axis=1)
    return jnp.tile(cos, (1, LANES // ATT_HD)), jnp.tile(sin, (1, LANES // ATT_HD))


V_ROWS = ATT_HD + 16


def _attn_body(tq, tk, qt_ref, k_ref, vt_ref, x_ref, gate_ref, wo_ref, o_ref, acc_scr, s_scr, ot_scr):
    nkv = k_ref.shape[2] // tk
    for g in range(ATT_KV_HEADS):
        qts = [qt_ref[0, (g * ATT_RATIO + r) * ATT_HD:(g * ATT_RATIO + r + 1) * ATT_HD, :]
               for r in range(ATT_RATIO)]
        acc_scr[...] = jnp.zeros_like(acc_scr)

        def scores(t, slot):
            kt = k_ref[0, g, pl.ds(pl.multiple_of(t * tk, tk), tk), :]
            for r in range(ATT_RATIO):
                s_scr[slot, r] = jnp.dot(kt, qts[r], preferred_element_type=F32)

        def softmax_pv(t, slot, ms):
            vt = vt_ref[0, g, :, pl.ds(pl.multiple_of(t * tk, tk), tk)]
            new_ms = [jnp.maximum(ms[r], jnp.max(s_scr[slot, r], axis=0, keepdims=True)) for r in range(ATT_RATIO)]
            ps = [jnp.exp2(s_scr[slot, r] - new_ms[r]).astype(BF16) for r in range(ATT_RATIO)]
            for r in range(ATT_RATIO):
                acc_scr[r] = (jnp.exp2(ms[r] - new_ms[r]) * acc_scr[r]
                              + jnp.dot(vt, ps[r], preferred_element_type=F32))
            return tuple(new_ms)

        def pair_step(i, ms):
            scores(2 * i + 1, 1)
            ms = softmax_pv(2 * i, 0, ms)
            scores(2 * i + 2, 0)
            return softmax_pv(2 * i + 1, 1, ms)

        scores(0, 0)
        ms = tuple(jnp.full((1, tq), NEG_BIG, F32) for _ in range(ATT_RATIO))
        ms = lax.fori_loop(0, (nkv - 1) // 2, pair_step, ms)
        if (nkv - 1) % 2:
            scores(nkv - 1, 1)
            ms = softmax_pv(nkv - 2, 0, ms)
        softmax_pv(nkv - 1, (nkv - 1) % 2, ms)
        for r in range(ATT_RATIO):
            hd = g * ATT_RATIO + r
            a = acc_scr[r]
            ot_scr[hd * ATT_HD:(hd + 1) * ATT_HD, :] = (a[0:ATT_HD] / a[ATT_HD:ATT_HD + 1]).astype(BF16)
    out = lax.dot_general(ot_scr[...], wo_ref[...], (((0,), (0,)), ((), ())), preferred_element_type=F32)
    o_ref[0] = x_ref[0] + gate_ref[0] * out


def _attention(qt, k4, vt, x, gate, w_o, batch, seq_len, tq, tk):
    per_batch = gate.shape[0] > 1
    nkeys = k4.shape[2]
    x3 = x.reshape(batch, seq_len, D_MODEL)
    out = pl.pallas_call(
        functools.partial(_attn_body, tq, tk),
        grid=(batch, seq_len // tq),
        in_specs=[
            pl.BlockSpec((1, ATT_Q, tq), lambda b, i: (b, 0, i)),
            pl.BlockSpec((1, ATT_KV_HEADS, nkeys, ATT_HD), lambda b, i: (b, 0, 0, 0)),
            pl.BlockSpec((1, ATT_KV_HEADS, V_ROWS, nkeys), lambda b, i: (b, 0, 0, 0)),
            pl.BlockSpec((1, tq, D_MODEL), lambda b, i: (b, i, 0)),
            pl.BlockSpec((1, 1, D_MODEL), lambda b, i: (b if per_batch else 0, 0, 0)),
            pl.BlockSpec((ATT_Q, D_MODEL), lambda b, i: (0, 0)),
        ],
        out_specs=pl.BlockSpec((1, tq, D_MODEL), lambda b, i: (b, i, 0)),
        out_shape=jax.ShapeDtypeStruct((batch, seq_len, D_MODEL), F32),
        scratch_shapes=[pltpu.VMEM((ATT_RATIO, V_ROWS, tq), F32), pltpu.VMEM((2, ATT_RATIO, tk, tq), F32),
                        pltpu.VMEM((ATT_Q, tq), BF16)],
        compiler_params=_cparams(("arbitrary", "arbitrary")),
        name="attention",
    )(qt, k4, vt, x3, gate, w_o)
    return out.reshape(batch * seq_len, D_MODEL)


def _attn_layouts(q, k, v, batch):
    n = k.shape[0] // batch
    qt = jnp.swapaxes(q.reshape(batch, -1, ATT_Q), 1, 2)
    k4 = jnp.swapaxes(k.reshape(batch, n, ATT_KV_HEADS, ATT_HD), 1, 2)
    v4 = jnp.transpose(v.reshape(batch, n, ATT_KV_HEADS, ATT_HD), (0, 2, 3, 1))
    vt = jnp.concatenate([v4, jnp.ones((batch, ATT_KV_HEADS, V_ROWS - ATT_HD, n), v.dtype)], axis=2)
    return qt, k4, vt


def _trunk(x, mods, caches, P, batch, seq_len, tm, tq, tk):
    rope = None if caches is None else _rope_tables(seq_len)
    outs = {}
    sh1, sc1, g1, sh2, sc2, g2 = mods[0]
    qkv, rest = _hyb_in_proj(x, P['norm_mix'][0:1], sc1, sh1, P['hyb_w_in'], seq_len, tm, 512)
    ys, rets, ssms = [], [], []
    for d in range(2):
        s_ret0 = None if caches is None else caches[0][:, 0, d]
        s_ssm0 = None if caches is None else caches[1][:, 0, d]
        y, s_ret, s_ssm = _scan(d, qkv, rest, P['ssm_conv_w'][0], P['ssm_conv_b'][0], P['ssm_dt_bias'][0],
                                P['ssm_a_log'][0], P['ret_log_decay'][0], P['ssm_d'][0], s_ret0, s_ssm0,
                                batch, seq_len)
        ys.append(y)
        rets.append(s_ret)
        ssms.append(s_ssm)
    outs['ret'] = jnp.stack(rets, axis=1)[:, None]
    outs['ssm'] = jnp.stack(ssms, axis=1)[:, None]
    x = _hyb_out(ys[0], ys[1], rest, x, g1, P['ret_gn_gain'][0:1], P['ssm_norm_gain'][0:1], P['hyb_w_out'],
                 seq_len, tm)
    x = _conv_ffn(x, P['norm_ffn'][0:1], sc2, sh2, g2, P['ffn_w_up'][0], P['ffn_conv_w'][0], P['ffn_conv_b'][0],
                  P['ffn_w_down'][0], seq_len, tm, 512)
    sh1, sc1, g1, sh2, sc2, g2 = mods[1]
    res = _qkv_proj(x, P['norm_mix'][1:2], sc1, sh1, P['attn_w_qkv'], P['attn_q_gain'][0], P['attn_k_gain'][0],
                    rope, seq_len, tm)
    if caches is None:
        q, k, v, kf, vf = res
        outs['k'] = kf.reshape(batch, 1, seq_len, ATT_KV_HEADS, ATT_HD)
        outs['v'] = vf.reshape(batch, 1, seq_len, ATT_KV_HEADS, ATT_HD)
        qt, k4, vt = _attn_layouts(q, k, v, batch)
    else:
        q, k, v = res
        past = caches[2].shape[2]
        ck = caches[2][:, 0].reshape(batch, past, ATT_KV).astype(BF16)
        cv = caches[3][:, 0].reshape(batch, past, ATT_KV).astype(BF16)
        k_all = jnp.concatenate([ck, k.reshape(batch, seq_len, ATT_KV)], axis=1)
        v_all = jnp.concatenate([cv, v.reshape(batch, seq_len, ATT_KV)], axis=1)
        qt, k4, vt = _attn_layouts(q, k_all.reshape(-1, ATT_KV), v_all.reshape(-1, ATT_KV), batch)
    x = _attention(qt, k4, vt, x, g1, P['attn_w_o'], batch, seq_len, tq, tk)
    x = _conv_ffn(x, P['norm_ffn'][1:2], sc2, sh2, g2, P['ffn_w_up'][1], P['ffn_conv_w'][1], P['ffn_conv_b'][1],
                  P['ffn_w_down'][1], seq_len, tm, 512)
    return x, outs


def kernel(x_prompt, x_sample, state_ret, state_ssm, cache_attn_k, cache_attn_v, c, c_ctx, w_mod, b_mod, norm_mix,
           norm_ffn, ffn_w_up, ffn_conv_w, ffn_conv_b, ffn_w_down, hyb_w_in, hyb_w_out, ret_log_decay, ret_gn_gain,
           ssm_conv_w, ssm_conv_b, ssm_a_log, ssm_dt_bias, ssm_d, ssm_norm_gain, attn_w_qkv, attn_q_gain,
           attn_k_gain, attn_w_o):
    batch, seq, _ = x_prompt.shape
    dec_batch, dec_seq, _ = x_sample.shape
    depth = w_mod.shape[0]

    rows = -(-(dec_batch + 1) // SUBLANES) * SUBLANES
    cond = jnp.concatenate([c, c_ctx[None, :], jnp.zeros((rows - dec_batch - 1, D_MODEL), F32)], axis=0)
    mod = _modulation(cond, w_mod, b_mod).reshape(depth, rows, 6, 1, D_MODEL)
    mods_sample = [[mod[l, 0:dec_batch, t] for t in range(6)] for l in range(depth)]
    mods_prompt = [[mod[l, dec_batch:dec_batch + 1, t] for t in range(6)] for l in range(depth)]

    w_in = hyb_w_in[0]
    kscale = jnp.concatenate([jnp.ones((RET_Q,), F32), jnp.full((RET_Q,), RET_DK ** -0.5, F32),
                              jnp.ones((w_in.shape[1] - 2 * RET_Q,), F32)])
    P = {
        'norm_mix': norm_mix, 'norm_ffn': norm_ffn,
        'hyb_w_in': jnp.pad(w_in * kscale, ((0, 0), (0, HYB_QKV + HYB_REST - w_in.shape[1]))).astype(BF16),
        'hyb_w_out': hyb_w_out[0].astype(BF16),
        'ret_log_decay': ret_log_decay, 'ret_gn_gain': ret_gn_gain,
        'ssm_conv_w': ssm_conv_w, 'ssm_conv_b': ssm_conv_b, 'ssm_a_log': ssm_a_log, 'ssm_dt_bias': ssm_dt_bias,
        'ssm_d': ssm_d, 'ssm_norm_gain': ssm_norm_gain,
        'attn_w_qkv': attn_w_qkv[0].astype(BF16), 'attn_q_gain': attn_q_gain, 'attn_k_gain': attn_k_gain,
        'attn_w_o': attn_w_o[0].astype(BF16),
        'ffn_w_up': ffn_w_up.astype(BF16), 'ffn_conv_w': ffn_conv_w, 'ffn_conv_b': ffn_conv_b,
        'ffn_w_down': ffn_w_down.astype(BF16),
    }

    y_prompt, outs = _trunk(x_prompt.reshape(batch * seq, D_MODEL), mods_prompt, None, P, batch, seq,
                            tm=min(256, seq), tq=min(256, seq), tk=min(256, seq))
    caches = (state_ret, state_ssm, cache_attn_k, cache_attn_v)
    y_sample, _ = _trunk(x_sample.reshape(dec_batch * dec_seq, D_MODEL), mods_sample, caches, P, dec_batch, dec_seq,
                         tm=512, tq=256, tk=256)
    return (y_prompt.reshape(batch, seq, D_MODEL), y_sample.reshape(dec_batch, dec_seq, D_MODEL),
            outs['ret'], outs['ssm'], outs['k'], outs['v'])
```

```python
import functools

import jax
import jax.numpy as jnp
from jax import lax
from jax.experimental import pallas as pl
from jax.experimental.pallas import tpu as pltpu

F32 = jnp.float32
BF16 = jnp.bfloat16
HIGHEST = lax.Precision.HIGHEST

D_MODEL = 1024
EPS = 1e-6
GRID_W = 64
CHUNK = 128
RET_HEADS = 8
RET_DK = 64
RET_DV = 128
RET_Q = RET_HEADS * RET_DK
RET_V = RET_HEADS * RET_DV
SSM_D_INNER = 1024
SSM_HEADDIM = 64
SSM_HEADS = 16
SSM_GROUPS = 4
SSM_RATIO = 4
SSM_STATE = 128
SSM_BC = SSM_GROUPS * SSM_STATE
SSM_XBC = SSM_D_INNER + 2 * SSM_BC
HYB_QKV = 2 * RET_Q + RET_V
HYB_REST = 2 * RET_V + SSM_XBC + 128
HYB_MIX = RET_V + SSM_D_INNER
ATT_HEADS = 16
ATT_KV_HEADS = 4
ATT_RATIO = 4
ATT_HD = 64
ATT_Q = ATT_HEADS * ATT_HD
ATT_KV = ATT_KV_HEADS * ATT_HD
ROT_FREQS = ATT_HD // 4
ROPE_BASE = 10000.0
FFN_HIDDEN = 2816

LANES = 128
SUBLANES = 8
VMEM_LIMIT = 56 * 1024 * 1024
NEG_BIG = -1e30
Q_SCALE = ATT_HD ** -0.5 * 1.4426950408889634


def _cparams(sem):
    return pltpu.CompilerParams(dimension_semantics=sem, vmem_limit_bytes=VMEM_LIMIT)


def _norm_mod(x, gain, sc, sh):
    ms = jnp.mean(x * x, axis=-1, keepdims=True)
    return x * lax.rsqrt(ms + EPS) * gain * (1.0 + sc) + sh


def _vec_spec(width, tiles_per_batch, per_batch, grid_rank):
    if grid_rank == 1:
        return pl.BlockSpec((1, 1, width), lambda i: ((i // tiles_per_batch) if per_batch else 0, 0, 0))
    return pl.BlockSpec((1, 1, width), lambda i, j: ((i // tiles_per_batch) if per_batch else 0, 0, 0))


def _mod_body(c_ref, w_ref, b_ref, o_ref):
    cs = jax.nn.silu(c_ref[...])
    o_ref[0] = jnp.dot(cs, w_ref[0], precision=HIGHEST, preferred_element_type=F32) + b_ref[0]


def _modulation(cond, w_mod, b_mod):
    depth, _, n = w_mod.shape
    rows = cond.shape[0]
    tn = n // 4
    return pl.pallas_call(
        _mod_body,
        grid=(depth, n // tn),
        in_specs=[
            pl.BlockSpec((rows, D_MODEL), lambda l, j: (0, 0)),
            pl.BlockSpec((1, D_MODEL, tn), lambda l, j: (l, 0, j)),
            pl.BlockSpec((1, 1, tn), lambda l, j: (l, 0, j)),
        ],
        out_specs=pl.BlockSpec((1, rows, tn), lambda l, j: (l, 0, j)),
        out_shape=jax.ShapeDtypeStruct((depth, rows, n), F32),
        compiler_params=_cparams(("arbitrary", "arbitrary")),
        name="modulation",
    )(cond, w_mod, b_mod.reshape(depth, 1, n))


def _chunks(total, width):
    return [(s, min(width, total - s)) for s in range(0, total, width)]


def _resident(shape):
    return pl.BlockSpec(shape, lambda *_: (0,) * len(shape), pipeline_mode=pl.Buffered(1))


def _norm_mod_halo(tiles_per_seq, tm, x_ref, xn_ref, xp_ref, gain_ref, sc_ref, sh_ref, h_scr):
    gain, sc, sh = gain_ref[...], sc_ref[0], sh_ref[0]
    h_scr[0:tm] = _norm_mod(x_ref[...], gain, sc, sh).astype(BF16)
    pos = pl.program_id(0) % tiles_per_seq
    hn = jnp.where(pos == tiles_per_seq - 1, 0.0, _norm_mod(xn_ref[0], gain, sc, sh))
    hp = jnp.where(pos == 0, 0.0, _norm_mod(xp_ref[0], gain, sc, sh))
    h_scr[tm:tm + 2 * SUBLANES] = jnp.concatenate([hn, hp], axis=0).astype(BF16)


def _dwconv3_rows(u, cw, cb, tm):
    rows = tm + 2 * SUBLANES
    c = pltpu.roll(u, 1, 0) * cw[0:1] + u * cw[1:2] + pltpu.roll(u, rows - 1, 0) * cw[2:3] + cb
    return c[0:tm]


def _halo_specs(m, tm):
    m8 = m // SUBLANES
    r8 = tm // SUBLANES
    return [pl.BlockSpec((1, SUBLANES, D_MODEL), lambda i: (jnp.minimum((i + 1) * r8, m8 - 1), 0, 0)),
            pl.BlockSpec((1, SUBLANES, D_MODEL), lambda i: (jnp.maximum(i * r8 - 1, 0), 0, 0))]


XBC_COL = 2 * RET_V


def _hyb_in_body(tiles_per_seq, tm, tn, x_ref, xn_ref, xp_ref, gain_ref, sc_ref, sh_ref, w_ref, cw_ref, cb_ref,
                 qkv_ref, rest_ref, h_scr):
    _norm_mod_halo(tiles_per_seq, tm, x_ref, xn_ref, xp_ref, gain_ref, sc_ref, sh_ref, h_scr)
    h_all = h_scr[...]
    h = h_scr[0:tm]
    for col, width in _chunks(HYB_QKV, tn):
        qkv_ref[:, col:col + width] = jnp.dot(h, w_ref[:, col:col + width],
                                              preferred_element_type=F32).astype(BF16)
    for col, width in _chunks(HYB_REST, tn):
        wcols = w_ref[:, HYB_QKV + col:HYB_QKV + col + width]
        if XBC_COL <= col < XBC_COL + SSM_XBC:
            u = jnp.dot(h_all, wcols, preferred_element_type=F32)
            cc = col - XBC_COL
            rest_ref[:, col:col + width] = jax.nn.silu(
                _dwconv3_rows(u, cw_ref[:, cc:cc + width], cb_ref[:, cc:cc + width], tm))
        else:
            rest_ref[:, col:col + width] = jnp.dot(h, wcols, preferred_element_type=F32)


def _hyb_in_proj(x, gain, sc, sh, w, conv_w, conv_b, seq_len, tm, tn):
    m = x.shape[0]
    per_batch = sc.shape[0] > 1
    tpb = seq_len // tm
    assert XBC_COL % tn == 0 and SSM_XBC % tn == 0
    x8 = x.reshape(m // SUBLANES, SUBLANES, D_MODEL)
    return pl.pallas_call(
        functools.partial(_hyb_in_body, tpb, tm, tn),
        grid=(m // tm,),
        in_specs=[
            pl.BlockSpec((tm, D_MODEL), lambda i: (i, 0)),
            *_halo_specs(m, tm),
            pl.BlockSpec((1, D_MODEL), lambda i: (0, 0)),
            _vec_spec(D_MODEL, tpb, per_batch, 1),
            _vec_spec(D_MODEL, tpb, per_batch, 1),
            _resident((D_MODEL, HYB_QKV + HYB_REST)),
            _resident((3, SSM_XBC)),
            _resident((1, SSM_XBC)),
        ],
        out_specs=[pl.BlockSpec((tm, HYB_QKV), lambda i: (i, 0)), pl.BlockSpec((tm, HYB_REST), lambda i: (i, 0))],
        out_shape=[jax.ShapeDtypeStruct((m, HYB_QKV), BF16), jax.ShapeDtypeStruct((m, HYB_REST), F32)],
        scratch_shapes=[pltpu.VMEM((tm + 2 * SUBLANES, D_MODEL), BF16)],
        compiler_params=_cparams(("arbitrary",)),
        name="hybrid_in_proj",
    )(x, x8, x8, gain, sc, sh, w, conv_w, conv_b.reshape(1, SSM_XBC))


def _ffn_body(tiles_per_seq, tm, th, x_ref, xn_ref, xp_ref, gain_ref, sc_ref, sh_ref, gate_ref,
              wu_ref, cw_ref, cb_ref, wd_ref, o_ref, h_scr, act_scr):
    _norm_mod_halo(tiles_per_seq, tm, x_ref, xn_ref, xp_ref, gain_ref, sc_ref, sh_ref, h_scr)
    h = h_scr[...]

    def conv_up(col, width):
        u = jnp.dot(h, wu_ref[:, col:col + width], preferred_element_type=F32)
        return _dwconv3_rows(u, cw_ref[:, col:col + width], cb_ref[:, col:col + width], tm)

    for col, width in _chunks(FFN_HIDDEN, th):
        act = jax.nn.silu(conv_up(col, width)) * conv_up(FFN_HIDDEN + col, width)
        act_scr[:, col:col + width] = act.astype(BF16)
    out = jnp.dot(act_scr[...], wd_ref[...], preferred_element_type=F32)
    o_ref[...] = x_ref[...] + gate_ref[0] * out


def _conv_ffn(x, gain, sc, sh, gate, w_up, conv_w, conv_b, w_down, seq_len, tm, th):
    m = x.shape[0]
    per_batch = sc.shape[0] > 1
    tpb = seq_len // tm
    x8 = x.reshape(m // SUBLANES, SUBLANES, D_MODEL)
    conv_b = conv_b.reshape(1, 2 * FFN_HIDDEN)
    vec = _vec_spec(D_MODEL, tpb, per_batch, 1)
    return pl.pallas_call(
        functools.partial(_ffn_body, tpb, tm, th),
        grid=(m // tm,),
        in_specs=[
            pl.BlockSpec((tm, D_MODEL), lambda i: (i, 0)),
            *_halo_specs(m, tm),
            pl.BlockSpec((1, D_MODEL), lambda i: (0, 0)),
            vec, vec, vec,
            _resident((D_MODEL, 2 * FFN_HIDDEN)),
            _resident((3, 2 * FFN_HIDDEN)),
            _resident((1, 2 * FFN_HIDDEN)),
            _resident((FFN_HIDDEN, D_MODEL)),
        ],
        out_specs=pl.BlockSpec((tm, D_MODEL), lambda i: (i, 0)),
        out_shape=jax.ShapeDtypeStruct((m, D_MODEL), F32),
        scratch_shapes=[pltpu.VMEM((tm + 2 * SUBLANES, D_MODEL), BF16), pltpu.VMEM((tm, FFN_HIDDEN), BF16)],
        compiler_params=_cparams(("arbitrary",)),
        name="conv_ffn",
    )(x, x8, x8, gain, sc, sh, gate, w_up, conv_w, conv_b, w_down)


SSM_GW = SSM_RATIO * SSM_HEADDIM


def _bcol(x, h):
    return jnp.broadcast_to(x[:, h:h + 1], (x.shape[0], LANES))


def _pair_sel(left, a, h):
    return jnp.where(left, _bcol(a, h), _bcol(a, h + 1))


def _ssm_decay_row(left, efull, g):
    return jnp.concatenate([_pair_sel(left, efull, g * SSM_RATIO + 2 * m) for m in range(SSM_RATIO // 2)], axis=1)


def _load_states(s_ret, s_ssm, sret0_ref, sssm0_ref):
    if sret0_ref is None:
        s_ret[...] = jnp.zeros_like(s_ret)
        s_ssm[...] = jnp.zeros_like(s_ssm)
    else:
        s_ret[...] = sret0_ref[0]
        for g in range(SSM_GROUPS):
            s_ssm[g] = jnp.concatenate([sssm0_ref[0, g * SSM_RATIO + r] for r in range(SSM_RATIO)], axis=1)


def _store_states(s_ret, s_ssm, sret_ref, sssm_ref):
    sret_ref[0] = s_ret[...]
    for g in range(SSM_GROUPS):
        for r in range(SSM_RATIO):
            sssm_ref[0, g * SSM_RATIO + r] = s_ssm[g][:, r * SSM_HEADDIM:(r + 1) * SSM_HEADDIM]


def _ssm_log_decay(dt_raw, dtb, alog, lane):
    dt = jax.nn.softplus(dt_raw + dtb)
    return dt, jnp.where(lane < SSM_HEADS, dt * (-jnp.exp(alog)), 0.0)


def _bwd_state_body(has_init, k_ref, v_ref, xs_ref, bm_ref, dt_ref, dtb_ref, alog_ref, logdec_ref, *rest):
    sret0_ref = sssm0_ref = None
    if has_init:
        sret0_ref, sssm0_ref = rest[0], rest[1]
        rest = rest[2:]
    sret_in_ref, sssm_in_ref, sret_fin_ref, sssm_fin_ref, s_ret, s_ssm, wcol = rest
    i = pl.program_id(1)
    n = pl.num_programs(1)
    ii = lax.broadcasted_iota(jnp.int32, (CHUNK, CHUNK), 0)
    jj = lax.broadcasted_iota(jnp.int32, (CHUNK, CHUNK), 1)
    lane = lax.broadcasted_iota(jnp.int32, (1, LANES), 1)
    left = lane < SSM_HEADDIM

    @pl.when(i == 0)
    def _():
        _load_states(s_ret, s_ssm, sret0_ref, sssm0_ref)
        for h in range(RET_HEADS):
            wcol[h] = jnp.exp(ii.astype(F32) * logdec_ref[1, h])

    sret_in_ref[0, 0] = s_ret[...]
    sssm_in_ref[0, 0] = s_ssm[...]

    full = jnp.full((1, LANES), float(CHUNK), F32)
    for h in range(RET_HEADS):
        kh = k_ref[:, h * RET_DK:(h + 1) * RET_DK]
        vw = (v_ref[:, h * RET_DV:(h + 1) * RET_DV].astype(F32) * wcol[h]).astype(BF16)
        upd = lax.dot_general(kh, vw, (((0,), (0,)), ((), ())), preferred_element_type=F32)
        s_ret[h] = s_ret[h] * jnp.exp(full * logdec_ref[1, h]) + upd

    dt, la = _ssm_log_decay(dt_ref[...], dtb_ref[...], alog_ref[...], lane)
    rc = jnp.dot((jj >= ii).astype(F32), la, precision=HIGHEST, preferred_element_type=F32)
    last = rc[0:1]
    w = jnp.exp(last - rc) * dt
    efull = jnp.exp(last)
    for g in range(SSM_GROUPS):
        vw = jnp.concatenate(
            [(xs_ref[:, (g * 2 + m) * LANES:(g * 2 + m + 1) * LANES]
              * _pair_sel(left, w, g * SSM_RATIO + 2 * m)).astype(BF16) for m in range(SSM_RATIO // 2)], axis=1)
        bg = bm_ref[:, g * SSM_STATE:(g + 1) * SSM_STATE].astype(BF16)
        upd = lax.dot_general(bg, vw, (((0,), (0,)), ((), ())), preferred_element_type=F32)
        s_ssm[g] = s_ssm[g] * _ssm_decay_row(left, efull, g) + upd

    @pl.when(i == n - 1)
    def _():
        _store_states(s_ret, s_ssm, sret_fin_ref, sssm_fin_ref)


def _hyb_fwd_body(has_init, qkv_ref, xs_ref, bc_ref, dt_ref, sret_in_ref, sssm_in_ref, dtb_ref, alog_ref, dsk_ref,
                  logdec_ref, *rest):
    sret0_ref = sssm0_ref = None
    if has_init:
        sret0_ref, sssm0_ref = rest[0], rest[1]
        rest = rest[2:]
    y_ref, sret_fin_ref, sssm_fin_ref, s_ret, s_ssm, dcomb, ein0, ein1, wcol = rest
    i = pl.program_id(1)
    n = pl.num_programs(1)
    ii = lax.broadcasted_iota(jnp.int32, (CHUNK, CHUNK), 0)
    jj = lax.broadcasted_iota(jnp.int32, (CHUNK, CHUNK), 1)
    lane = lax.broadcasted_iota(jnp.int32, (1, LANES), 1)
    left = lane < SSM_HEADDIM
    lower = jj <= ii

    @pl.when(i == 0)
    def _():
        _load_states(s_ret, s_ssm, sret0_ref, sssm0_ref)
        fi, fj = ii.astype(F32), jj.astype(F32)
        for h in range(RET_HEADS):
            la0, la1 = logdec_ref[0, h], logdec_ref[1, h]
            dec = jnp.exp(jnp.where(lower, (fi - fj) * la0, (fj - fi) * la1))
            dcomb[h] = jnp.where(ii == jj, 2.0, dec)
            ein0[h] = jnp.exp((fi + 1.0) * la0)
            ein1[h] = jnp.exp((CHUNK - fi) * la1)
            wcol[h] = jnp.exp((CHUNK - 1.0 - fi) * la0)

    full = jnp.full((1, LANES), float(CHUNK), F32)
    for h in range(RET_HEADS):
        qh = qkv_ref[:, h * RET_DK:(h + 1) * RET_DK]
        kh = qkv_ref[:, RET_Q + h * RET_DK:RET_Q + (h + 1) * RET_DK]
        vh = qkv_ref[:, 2 * RET_Q + h * RET_DV:2 * RET_Q + (h + 1) * RET_DV]
        s = lax.dot_general(qh, kh, (((1,), (1,)), ((), ())), preferred_element_type=F32)
        y = jnp.dot((s * dcomb[h]).astype(BF16), vh, preferred_element_type=F32)
        states = jnp.concatenate([s_ret[h], sret_in_ref[0, 0, h]], axis=1).astype(BF16)
        inter = jnp.dot(qh, states, preferred_element_type=F32)
        y_ref[:, h * RET_DV:(h + 1) * RET_DV] = y + inter[:, 0:RET_DV] * ein0[h] + inter[:, RET_DV:] * ein1[h]
        vw = (vh.astype(F32) * wcol[h]).astype(BF16)
        upd = lax.dot_general(kh, vw, (((0,), (0,)), ((), ())), preferred_element_type=F32)
        s_ret[h] = s_ret[h] * jnp.exp(full * logdec_ref[0, h]) + upd

    dt_raw = dt_ref[...]
    dt0, la0 = _ssm_log_decay(dt_raw, dtb_ref[0:1], alog_ref[0:1], lane)
    dt1, la1 = _ssm_log_decay(dt_raw, dtb_ref[1:2], alog_ref[1:2], lane)
    c0 = jnp.dot(lower.astype(F32), la0, precision=HIGHEST, preferred_element_type=F32)
    c1 = jnp.dot((jj >= ii).astype(F32), la1, precision=HIGHEST, preferred_element_type=F32)
    c0t, c1t, dt0t, dt1t = c0.T, c1.T, dt0.T, dt1.T
    dst = dt0t + dt1t
    last0 = c0[CHUNK - 1:CHUNK]
    w0 = jnp.exp(last0 - c0) * dt0
    efull0 = jnp.exp(last0)
    strict = jj < ii
    diag = jj == ii
    for g in range(SSM_GROUPS):
        bg = bc_ref[:, g * SSM_STATE:(g + 1) * SSM_STATE].astype(BF16)
        cg = bc_ref[:, SSM_BC + g * SSM_STATE:SSM_BC + (g + 1) * SSM_STATE].astype(BF16)
        s = lax.dot_general(cg, bg, (((1,), (1,)), ((), ())), preferred_element_type=F32)
        states = jnp.concatenate([s_ssm[g], sssm_in_ref[0, 0, g]], axis=1).astype(BF16)
        inter = jnp.dot(cg, states, preferred_element_type=F32)
        vws = []
        for m in range(SSM_RATIO // 2):
            ha = g * SSM_RATIO + 2 * m
            ps, c0bs, c1bs = [], [], []
            for h in (ha, ha + 1):
                c0b, c1b = _bcol(c0, h), _bcol(c1, h)
                arg = jnp.where(lower, c0b - c0t[h:h + 1], c1b - c1t[h:h + 1])
                wgt = jnp.where(strict, dt0t[h:h + 1], jnp.where(diag, dst[h:h + 1], dt1t[h:h + 1]))
                ps.append((s * jnp.exp(arg) * wgt).astype(BF16))
                c0bs.append(c0b)
                c1bs.append(c1b)
            col = (g * 2 + m) * LANES
            xs_pair = xs_ref[:, col:col + LANES]
            vals = jnp.concatenate([jnp.where(left, xs_pair, 0.0), jnp.where(left, 0.0, xs_pair)], axis=0)
            y = jnp.dot(jnp.concatenate(ps, axis=1), vals.astype(BF16), preferred_element_type=F32)
            y = y + inter[:, m * LANES:(m + 1) * LANES] * jnp.exp(jnp.where(left, c0bs[0], c0bs[1]))
            y = y + (inter[:, SSM_GW + m * LANES:SSM_GW + (m + 1) * LANES]
                     * jnp.exp(jnp.where(left, c1bs[0], c1bs[1])))
            y_ref[:, RET_V + col:RET_V + col + LANES] = y + dsk_ref[:, col:col + LANES] * xs_pair
            vws.append((xs_pair * _pair_sel(left, w0, ha)).astype(BF16))
        upd = lax.dot_general(bg, jnp.concatenate(vws, axis=1), (((0,), (0,)), ((), ())),
                              preferred_element_type=F32)
        s_ssm[g] = s_ssm[g] * _ssm_decay_row(left, efull0, g) + upd

    @pl.when(i == n - 1)
    def _():
        _store_states(s_ret, s_ssm, sret_fin_ref, sssm_fin_ref)


def _hybrid_mixer(qkv, rest, dt_bias, a_log, log_decay, d_skip, s_ret0, s_ssm0, batch, seq_len):
    m = qkv.shape[0]
    n = seq_len // CHUNK
    has_init = s_ret0 is not None
    dt_blk = (XBC_COL + SSM_XBC) // LANES
    pad = lambda v: jnp.pad(v.astype(F32), ((0, 0), (0, LANES - v.shape[1])))
    dtb, alog = pad(dt_bias), pad(a_log)
    logdec = log_decay.astype(F32)
    dsk = jnp.repeat((d_skip[0] + d_skip[1]).astype(F32), SSM_HEADDIM).reshape(1, SSM_D_INNER)
    smem = pl.BlockSpec(memory_space=pltpu.SMEM)
    ret_spec = pl.BlockSpec((1, RET_HEADS, RET_DK, RET_DV), lambda b, i: (b, 0, 0, 0))
    ssm_spec = pl.BlockSpec((1, SSM_HEADS, SSM_STATE, SSM_HEADDIM), lambda b, i: (b, 0, 0, 0))
    ret_shape = jax.ShapeDtypeStruct((batch, RET_HEADS, RET_DK, RET_DV), F32)
    ssm_shape = jax.ShapeDtypeStruct((batch, SSM_HEADS, SSM_STATE, SSM_HEADDIM), F32)
    ret_scr = pltpu.VMEM((RET_HEADS, RET_DK, RET_DV), F32)
    ssm_scr = pltpu.VMEM((SSM_GROUPS, SSM_STATE, SSM_GW), F32)
    const_scr = pltpu.VMEM((RET_HEADS, CHUNK, CHUNK), F32)

    def rev(b, i):
        return b * n + n - 1 - i

    in_specs = [
        pl.BlockSpec((CHUNK, RET_Q), lambda b, i: (rev(b, i), 1)),
        pl.BlockSpec((CHUNK, RET_V), lambda b, i: (rev(b, i), 1)),
        pl.BlockSpec((CHUNK, SSM_D_INNER), lambda b, i: (rev(b, i), XBC_COL // SSM_D_INNER)),
        pl.BlockSpec((CHUNK, SSM_BC), lambda b, i: (rev(b, i), (XBC_COL + SSM_D_INNER) // SSM_BC)),
        pl.BlockSpec((CHUNK, LANES), lambda b, i: (rev(b, i), dt_blk)),
        pl.BlockSpec((1, LANES), lambda b, i: (0, 0)),
        pl.BlockSpec((1, LANES), lambda b, i: (0, 0)),
        smem,
    ]
    args = [qkv, qkv, rest, rest, rest, dtb[1:2], alog[1:2], logdec]
    if has_init:
        in_specs += [ret_spec, ssm_spec]
        args += [s_ret0[:, 1], s_ssm0[:, 1]]
    sret_in, sssm_in, sret1, sssm1 = pl.pallas_call(
        functools.partial(_bwd_state_body, has_init),
        grid=(batch, n),
        in_specs=in_specs,
        out_specs=[
            pl.BlockSpec((1, 1, RET_HEADS, RET_DK, RET_DV), lambda b, i: (b, n - 1 - i, 0, 0, 0)),
            pl.BlockSpec((1, 1, SSM_GROUPS, SSM_STATE, SSM_GW), lambda b, i: (b, n - 1 - i, 0, 0, 0)),
            ret_spec, ssm_spec,
        ],
        out_shape=[
            jax.ShapeDtypeStruct((batch, n, RET_HEADS, RET_DK, RET_DV), F32),
            jax.ShapeDtypeStruct((batch, n, SSM_GROUPS, SSM_STATE, SSM_GW), F32),
            ret_shape, ssm_shape,
        ],
        scratch_shapes=[ret_scr, ssm_scr, const_scr],
        compiler_params=_cparams(("arbitrary", "arbitrary")),
        name="hybrid_reverse_states",
    )(*args)

    def fwd(b, i):
        return b * n + i

    in_specs = [
        pl.BlockSpec((CHUNK, HYB_QKV), lambda b, i: (fwd(b, i), 0)),
        pl.BlockSpec((CHUNK, SSM_D_INNER), lambda b, i: (fwd(b, i), XBC_COL // SSM_D_INNER)),
        pl.BlockSpec((CHUNK, 2 * SSM_BC), lambda b, i: (fwd(b, i), (XBC_COL + SSM_D_INNER) // (2 * SSM_BC))),
        pl.BlockSpec((CHUNK, LANES), lambda b, i: (fwd(b, i), dt_blk)),
        pl.BlockSpec((1, 1, RET_HEADS, RET_DK, RET_DV), lambda b, i: (b, i, 0, 0, 0)),
        pl.BlockSpec((1, 1, SSM_GROUPS, SSM_STATE, SSM_GW), lambda b, i: (b, i, 0, 0, 0)),
        pl.BlockSpec((2, LANES), lambda b, i: (0, 0)),
        pl.BlockSpec((2, LANES), lambda b, i: (0, 0)),
        pl.BlockSpec((1, SSM_D_INNER), lambda b, i: (0, 0)),
        smem,
    ]
    args = [qkv, rest, rest, rest, sret_in, sssm_in, dtb, alog, dsk, logdec]
    if has_init:
        in_specs += [ret_spec, ssm_spec]
        args += [s_ret0[:, 0], s_ssm0[:, 0]]
    y, sret0, sssm0 = pl.pallas_call(
        functools.partial(_hyb_fwd_body, has_init),
        grid=(batch, n),
        in_specs=in_specs,
        out_specs=[pl.BlockSpec((CHUNK, HYB_MIX), lambda b, i: (fwd(b, i), 0)), ret_spec, ssm_spec],
        out_shape=[jax.ShapeDtypeStruct((m, HYB_MIX), F32), ret_shape, ssm_shape],
        scratch_shapes=[ret_scr, ssm_scr, const_scr, const_scr, const_scr, const_scr],
        compiler_params=_cparams(("arbitrary", "arbitrary")),
        name="hybrid_forward_mix",
    )(*args)
    return y, jnp.stack([sret0, sret1], axis=1), jnp.stack([sssm0, sssm1], axis=1)


def _hyb_out_body(y_ref, g_ref, z_ref, x_ref, gate_ref, gn_ref, ng_ref, w_ref, o_ref):
    y = y_ref[...]
    parts = []
    for h in range(RET_HEADS):
        yh = y[:, h * RET_DV:(h + 1) * RET_DV]
        mu = jnp.mean(yh, axis=-1, keepdims=True)
        var = jnp.mean(jnp.square(yh - mu), axis=-1, keepdims=True)
        parts.append((yh - mu) * lax.rsqrt(var + EPS))
    y_ret = jax.nn.silu(g_ref[...]) * (jnp.concatenate(parts, axis=1) * gn_ref[...])
    yz = y[:, RET_V:] * jax.nn.silu(z_ref[...])
    y_ssm = yz * lax.rsqrt(jnp.mean(yz * yz, axis=-1, keepdims=True) + EPS) * ng_ref[...]
    out = jnp.dot(y_ret.astype(BF16), w_ref[0:RET_V], preferred_element_type=F32)
    out = out + jnp.dot(y_ssm.astype(BF16), w_ref[RET_V:HYB_MIX], preferred_element_type=F32)
    o_ref[...] = x_ref[...] + gate_ref[0] * out


def _hyb_out(y, rest, x, gate, gn_gain, norm_gain, w_out, seq_len, tm):
    m = x.shape[0]
    per_batch = gate.shape[0] > 1
    tpb = seq_len // tm
    return pl.pallas_call(
        _hyb_out_body,
        grid=(m // tm,),
        in_specs=[
            pl.BlockSpec((tm, HYB_MIX), lambda i: (i, 0)),
            pl.BlockSpec((tm, RET_V), lambda i: (i, 0)),
            pl.BlockSpec((tm, SSM_D_INNER), lambda i: (i, 1)),
            pl.BlockSpec((tm, D_MODEL), lambda i: (i, 0)),
            _vec_spec(D_MODEL, tpb, per_batch, 1),
            pl.BlockSpec((1, RET_V), lambda i: (0, 0)),
            pl.BlockSpec((1, SSM_D_INNER), lambda i: (0, 0)),
            pl.BlockSpec((HYB_MIX, D_MODEL), lambda i: (0, 0)),
        ],
        out_specs=pl.BlockSpec((tm, D_MODEL), lambda i: (i, 0)),
        out_shape=jax.ShapeDtypeStruct((m, D_MODEL), F32),
        compiler_params=_cparams(("arbitrary",)),
        name="hybrid_out_proj",
    )(y, rest, rest, x, gate, gn_gain, norm_gain, w_out)


def _head_rms(t, seg_ref):
    outs = []
    for c in range(t.shape[1] // LANES):
        tc = t[:, c * LANES:(c + 1) * LANES]
        sq = tc * tc
        hi = sq.astype(BF16)
        lo = (sq - hi.astype(F32)).astype(BF16)
        ssum = (jnp.dot(hi, seg_ref[...], preferred_element_type=F32)
                + jnp.dot(lo, seg_ref[...], preferred_element_type=F32))
        outs.append(tc * lax.rsqrt(ssum * (1.0 / ATT_HD) + EPS))
    return outs


def _rope(tc, cos, sin, lane):
    fwd = pltpu.roll(tc, LANES - ROT_FREQS, 1)
    bwd = pltpu.roll(tc, ROT_FREQS, 1)
    return tc * cos + jnp.where(lane % (2 * ROT_FREQS) < ROT_FREQS, fwd, bwd) * sin


def _qkv_body(has_rope, x_ref, gain_ref, sc_ref, sh_ref, w_ref, seg_ref, qg_ref, kg_ref, *rest):
    if has_rope:
        cos_ref, sin_ref, q_ref, k_ref, v_ref = rest
    else:
        q_ref, k_ref, v_ref, kf_ref, vf_ref = rest
    h = _norm_mod(x_ref[...], gain_ref[...], sc_ref[0], sh_ref[0]).astype(BF16)
    qkv = jnp.dot(h, w_ref[...], preferred_element_type=F32)
    q = _head_rms(qkv[:, 0:ATT_Q], seg_ref)
    k = _head_rms(qkv[:, ATT_Q:ATT_Q + ATT_KV], seg_ref)
    v = qkv[:, ATT_Q + ATT_KV:ATT_Q + 2 * ATT_KV]
    q = [t * qg_ref[...] for t in q]
    k = [t * kg_ref[...] for t in k]
    if has_rope:
        lane = lax.broadcasted_iota(jnp.int32, (1, LANES), 1)
        cos, sin = cos_ref[...], sin_ref[...]
        q = [_rope(t, cos, sin, lane) for t in q]
        kr = [_rope(t, cos, sin, lane) for t in k]
    else:
        kr = k
        kf_ref[...] = jnp.concatenate(k, axis=1)
        vf_ref[...] = v
    q_ref[...] = (jnp.concatenate(q, axis=1) * Q_SCALE).astype(BF16)
    k_ref[...] = jnp.concatenate(kr, axis=1).astype(BF16)
    v_ref[...] = v.astype(BF16)


def _qkv_proj(x, gain, sc, sh, w_qkv, q_gain, k_gain, rope, seq_len, tm):
    m = x.shape[0]
    per_batch = sc.shape[0] > 1
    tpb = seq_len // tm
    has_rope = rope is not None
    seg = (lax.broadcasted_iota(jnp.int32, (LANES, LANES), 0) // ATT_HD
           == lax.broadcasted_iota(jnp.int32, (LANES, LANES), 1) // ATT_HD).astype(BF16)
    tile2 = lambda v: jnp.tile(v.astype(F32), LANES // ATT_HD).reshape(1, LANES)
    in_specs = [
        pl.BlockSpec((tm, D_MODEL), lambda i: (i, 0)),
        pl.BlockSpec((1, D_MODEL), lambda i: (0, 0)),
        _vec_spec(D_MODEL, tpb, per_batch, 1),
        _vec_spec(D_MODEL, tpb, per_batch, 1),
        pl.BlockSpec((D_MODEL, ATT_Q + 2 * ATT_KV), lambda i: (0, 0)),
        pl.BlockSpec((LANES, LANES), lambda i: (0, 0)),
        pl.BlockSpec((1, LANES), lambda i: (0, 0)),
        pl.BlockSpec((1, LANES), lambda i: (0, 0)),
    ]
    args = [x, gain, sc, sh, w_qkv, seg, tile2(q_gain), tile2(k_gain)]
    out_specs = [pl.BlockSpec((tm, ATT_Q), lambda i: (i, 0)),
                 pl.BlockSpec((tm, ATT_KV), lambda i: (i, 0)),
                 pl.BlockSpec((tm, ATT_KV), lambda i: (i, 0))]
    out_shape = [jax.ShapeDtypeStruct((m, ATT_Q), BF16), jax.ShapeDtypeStruct((m, ATT_KV), BF16),
                 jax.ShapeDtypeStruct((m, ATT_KV), BF16)]
    if has_rope:
        in_specs += [pl.BlockSpec((tm, LANES), lambda i: (i % tpb, 0))] * 2
        args += list(rope)
    else:
        out_specs += [pl.BlockSpec((tm, ATT_KV), lambda i: (i, 0))] * 2
        out_shape += [jax.ShapeDtypeStruct((m, ATT_KV), F32)] * 2
    return pl.pallas_call(
        functools.partial(_qkv_body, has_rope),
        grid=(m // tm,),
        in_specs=in_specs,
        out_specs=out_specs,
        out_shape=out_shape,
        compiler_params=_cparams(("arbitrary",)),
        name="attn_qkv_proj",
    )(*args)


def _rope_tables(seq_len):
    rows = seq_len // GRID_W
    row = jnp.repeat(jnp.arange(rows, dtype=F32), GRID_W)
    col = jnp.tile(jnp.arange(GRID_W, dtype=F32), rows)
    inv = ROPE_BASE ** (-jnp.arange(ROT_FREQS, dtype=F32) / ROT_FREQS)
    ar, ac = row[:, None] * inv, col[:, None] * inv
    cos = jnp.concatenate([jnp.cos(ar), jnp.cos(ar), jnp.cos(ac), jnp.cos(ac)], axis=1)
    sin = jnp.concatenate([-jnp.sin(ar), jnp.sin(ar), -jnp.sin(ac), jnp.sin(ac)], axis=1)
    return jnp.tile(cos, (1, LANES // ATT_HD)), jnp.tile(sin, (1, LANES // ATT_HD))


V_ROWS = ATT_HD + 16


def _attn_body(tq, tk, qt_ref, k_ref, vt_ref, x_ref, gate_ref, wo_ref, o_ref, acc_scr, s_scr, ot_scr):
    nkv = k_ref.shape[2] // tk
    for g in range(ATT_KV_HEADS):
        qts = [qt_ref[0, (g * ATT_RATIO + r) * ATT_HD:(g * ATT_RATIO + r + 1) * ATT_HD, :]
               for r in range(ATT_RATIO)]
        acc_scr[...] = jnp.zeros_like(acc_scr)

        def scores(t, slot):
            kt = k_ref[0, g, pl.ds(pl.multiple_of(t * tk, tk), tk), :]
            for r in range(ATT_RATIO):
                s_scr[slot, r] = jnp.dot(kt, qts[r], preferred_element_type=F32)

        def softmax_pv(t, slot, ms):
            vt = vt_ref[0, g, :, pl.ds(pl.multiple_of(t * tk, tk), tk)]
            new_ms = [jnp.maximum(ms[r], jnp.max(s_scr[slot, r], axis=0, keepdims=True)) for r in range(ATT_RATIO)]
            ps = [jnp.exp2(s_scr[slot, r] - new_ms[r]).astype(BF16) for r in range(ATT_RATIO)]
            for r in range(ATT_RATIO):
                acc_scr[r] = (jnp.exp2(ms[r] - new_ms[r]) * acc_scr[r]
                              + jnp.dot(vt, ps[r], preferred_element_type=F32))
            return tuple(new_ms)

        def pair_step(i, ms):
            scores(2 * i + 1, 1)
            ms = softmax_pv(2 * i, 0, ms)
            scores(2 * i + 2, 0)
            return softmax_pv(2 * i + 1, 1, ms)

        scores(0, 0)
        ms = tuple(jnp.full((1, tq), NEG_BIG, F32) for _ in range(ATT_RATIO))
        ms = lax.fori_loop(0, (nkv - 1) // 2, pair_step, ms)
        if (nkv - 1) % 2:
            scores(nkv - 1, 1)
            ms = softmax_pv(nkv - 2, 0, ms)
        softmax_pv(nkv - 1, (nkv - 1) % 2, ms)
        for r in range(ATT_RATIO):
            hd = g * ATT_RATIO + r
            a = acc_scr[r]
            ot_scr[hd * ATT_HD:(hd + 1) * ATT_HD, :] = (a[0:ATT_HD] / a[ATT_HD:ATT_HD + 1]).astype(BF16)
    out = lax.dot_general(ot_scr[...], wo_ref[...], (((0,), (0,)), ((), ())), preferred_element_type=F32)
    o_ref[0] = x_ref[0] + gate_ref[0] * out


def _attention(qt, k4, vt, x, gate, w_o, batch, seq_len, tq, tk):
    per_batch = gate.shape[0] > 1
    nkeys = k4.shape[2]
    x3 = x.reshape(batch, seq_len, D_MODEL)
    out = pl.pallas_call(
        functools.partial(_attn_body, tq, tk),
        grid=(batch, seq_len // tq),
        in_specs=[
            pl.BlockSpec((1, ATT_Q, tq), lambda b, i: (b, 0, i)),
            pl.BlockSpec((1, ATT_KV_HEADS, nkeys, ATT_HD), lambda b, i: (b, 0, 0, 0)),
            pl.BlockSpec((1, ATT_KV_HEADS, V_ROWS, nkeys), lambda b, i: (b, 0, 0, 0)),
            pl.BlockSpec((1, tq, D_MODEL), lambda b, i: (b, i, 0)),
            pl.BlockSpec((1, 1, D_MODEL), lambda b, i: (b if per_batch else 0, 0, 0)),
            pl.BlockSpec((ATT_Q, D_MODEL), lambda b, i: (0, 0)),
        ],
        out_specs=pl.BlockSpec((1, tq, D_MODEL), lambda b, i: (b, i, 0)),
        out_shape=jax.ShapeDtypeStruct((batch, seq_len, D_MODEL), F32),
        scratch_shapes=[pltpu.VMEM((ATT_RATIO, V_ROWS, tq), F32), pltpu.VMEM((2, ATT_RATIO, tk, tq), F32),
                        pltpu.VMEM((ATT_Q, tq), BF16)],
        compiler_params=_cparams(("arbitrary", "arbitrary")),
        name="attention",
    )(qt, k4, vt, x3, gate, w_o)
    return out.reshape(batch * seq_len, D_MODEL)


def _attn_layouts(q, k, v, batch):
    n = k.shape[0] // batch
    qt = jnp.swapaxes(q.reshape(batch, -1, ATT_Q), 1, 2)
    k4 = jnp.swapaxes(k.reshape(batch, n, ATT_KV_HEADS, ATT_HD), 1, 2)
    v4 = jnp.transpose(v.reshape(batch, n, ATT_KV_HEADS, ATT_HD), (0, 2, 3, 1))
    vt = jnp.concatenate([v4, jnp.ones((batch, ATT_KV_HEADS, V_ROWS - ATT_HD, n), v.dtype)], axis=2)
    return qt, k4, vt


def _trunk(x, mods, caches, P, batch, seq_len, tm, tq, tk):
    rope = None if caches is None else _rope_tables(seq_len)
    outs = {}
    sh1, sc1, g1, sh2, sc2, g2 = mods[0]
    qkv, rest = _hyb_in_proj(x, P['norm_mix'][0:1], sc1, sh1, P['hyb_w_in'], P['ssm_conv_w'][0],
                             P['ssm_conv_b'][0], seq_len, tm, 512)
    s_ret0 = None if caches is None else caches[0][:, 0]
    s_ssm0 = None if caches is None else caches[1][:, 0]
    y, s_ret, s_ssm = _hybrid_mixer(qkv, rest, P['ssm_dt_bias'][0], P['ssm_a_log'][0], P['ret_log_decay'][0],
                                    P['ssm_d'][0], s_ret0, s_ssm0, batch, seq_len)
    outs['ret'] = s_ret[:, None]
    outs['ssm'] = s_ssm[:, None]
    x = _hyb_out(y, rest, x, g1, P['ret_gn_gain'][0:1], P['ssm_norm_gain'][0:1], P['hyb_w_out'], seq_len, tm)
    x = _conv_ffn(x, P['norm_ffn'][0:1], sc2, sh2, g2, P['ffn_w_up'][0], P['ffn_conv_w'][0], P['ffn_conv_b'][0],
                  P['ffn_w_down'][0], seq_len, tm, 512)
    sh1, sc1, g1, sh2, sc2, g2 = mods[1]
    res = _qkv_proj(x, P['norm_mix'][1:2], sc1, sh1, P['attn_w_qkv'], P['attn_q_gain'][0], P['attn_k_gain'][0],
                    rope, seq_len, tm)
    if caches is None:
        q, k, v, kf, vf = res
        outs['k'] = kf.reshape(batch, 1, seq_len, ATT_KV_HEADS, ATT_HD)
        outs['v'] = vf.reshape(batch, 1, seq_len, ATT_KV_HEADS, ATT_HD)
        qt, k4, vt = _attn_layouts(q, k, v, batch)
    else:
        q, k, v = res
        past = caches[2].shape[2]
        ck = caches[2][:, 0].reshape(batch, past, ATT_KV).astype(BF16)
        cv = caches[3][:, 0].reshape(batch, past, ATT_KV).astype(BF16)
        k_all = jnp.concatenate([ck, k.reshape(batch, seq_len, ATT_KV)], axis=1)
        v_all = jnp.concatenate([cv, v.reshape(batch, seq_len, ATT_KV)], axis=1)
        qt, k4, vt = _attn_layouts(q, k_all.reshape(-1, ATT_KV), v_all.reshape(-1, ATT_KV), batch)
    x = _attention(qt, k4, vt, x, g1, P['attn_w_o'], batch, seq_len, tq, tk)
    x = _conv_ffn(x, P['norm_ffn'][1:2], sc2, sh2, g2, P['ffn_w_up'][1], P['ffn_conv_w'][1], P['ffn_conv_b'][1],
                  P['ffn_w_down'][1], seq_len, tm, 512)
    return x, outs


def kernel(x_prompt, x_sample, state_ret, state_ssm, cache_attn_k, cache_attn_v, c, c_ctx, w_mod, b_mod, norm_mix,
           norm_ffn, ffn_w_up, ffn_conv_w, ffn_conv_b, ffn_w_down, hyb_w_in, hyb_w_out, ret_log_decay, ret_gn_gain,
           ssm_conv_w, ssm_conv_b, ssm_a_log, ssm_dt_bias, ssm_d, ssm_norm_gain, attn_w_qkv, attn_q_gain,
           attn_k_gain, attn_w_o):
    batch, seq, _ = x_prompt.shape
    dec_batch, dec_seq, _ = x_sample.shape
    depth = w_mod.shape[0]

    rows = -(-(dec_batch + 1) // SUBLANES) * SUBLANES
    cond = jnp.concatenate([c, c_ctx[None, :], jnp.zeros((rows - dec_batch - 1, D_MODEL), F32)], axis=0)
    mod = _modulation(cond, w_mod, b_mod).reshape(depth, rows, 6, 1, D_MODEL)
    mods_sample = [[mod[l, 0:dec_batch, t] for t in range(6)] for l in range(depth)]
    mods_prompt = [[mod[l, dec_batch:dec_batch + 1, t] for t in range(6)] for l in range(depth)]

    w_in = hyb_w_in[0]
    kscale = jnp.concatenate([jnp.ones((RET_Q,), F32), jnp.full((RET_Q,), RET_DK ** -0.5, F32),
                              jnp.ones((w_in.shape[1] - 2 * RET_Q,), F32)])
    P = {
        'norm_mix': norm_mix, 'norm_ffn': norm_ffn,
        'hyb_w_in': jnp.pad(w_in * kscale, ((0, 0), (0, HYB_QKV + HYB_REST - w_in.shape[1]))).astype(BF16),
        'hyb_w_out': hyb_w_out[0].astype(BF16),
        'ret_log_decay': ret_log_decay, 'ret_gn_gain': ret_gn_gain,
        'ssm_conv_w': ssm_conv_w, 'ssm_conv_b': ssm_conv_b, 'ssm_a_log': ssm_a_log, 'ssm_dt_bias': ssm_dt_bias,
        'ssm_d': ssm_d, 'ssm_norm_gain': ssm_norm_gain,
        'attn_w_qkv': attn_w_qkv[0].astype(BF16), 'attn_q_gain': attn_q_gain, 'attn_k_gain': attn_k_gain,
        'attn_w_o': attn_w_o[0].astype(BF16),
        'ffn_w_up': ffn_w_up.astype(BF16), 'ffn_conv_w': ffn_conv_w, 'ffn_conv_b': ffn_conv_b,
        'ffn_w_down': ffn_w_down.astype(BF16),
    }

    y_prompt, outs = _trunk(x_prompt.reshape(batch * seq, D_MODEL), mods_prompt, None, P, batch, seq,
                            tm=min(256, seq), tq=min(256, seq), tk=min(256, seq))
    caches = (state_ret, state_ssm, cache_attn_k, cache_attn_v)
    y_sample, _ = _trunk(x_sample.reshape(dec_batch * dec_seq, D_MODEL), mods_sample, caches, P, dec_batch, dec_seq,
                         tm=512, tq=256, tk=256)
    return (y_prompt.reshape(batch, seq, D_MODEL), y_sample.reshape(dec_batch, dec_seq, D_MODEL),
            outs['ret'], outs['ssm'], outs['k'], outs['v'])
```

```python
import functools

import jax
import jax.numpy as jnp
from jax import lax
from jax.experimental import pallas as pl
from jax.experimental.pallas import tpu as pltpu

F32 = jnp.float32
BF16 = jnp.bfloat16
HIGHEST = lax.Precision.HIGHEST

D_MODEL = 1024
EPS = 1e-6
GRID_W = 64
CHUNK = 128
RET_HEADS = 8
RET_DK = 64
RET_DV = 128
RET_Q = RET_HEADS * RET_DK
RET_V = RET_HEADS * RET_DV
SSM_D_INNER = 1024
SSM_HEADDIM = 64
SSM_HEADS = 16
SSM_GROUPS = 4
SSM_RATIO = 4
SSM_STATE = 128
SSM_BC = SSM_GROUPS * SSM_STATE
SSM_XBC = SSM_D_INNER + 2 * SSM_BC
HYB_QKV = 2 * RET_Q + RET_V
HYB_REST = 2 * RET_V + SSM_XBC + 128
HYB_MIX = RET_V + SSM_D_INNER
ATT_HEADS = 16
ATT_KV_HEADS = 4
ATT_RATIO = 4
ATT_HD = 64
ATT_Q = ATT_HEADS * ATT_HD
ATT_KV = ATT_KV_HEADS * ATT_HD
ROT_FREQS = ATT_HD // 4
ROPE_BASE = 10000.0
FFN_HIDDEN = 2816

LANES = 128
SUBLANES = 8
VMEM_LIMIT = 56 * 1024 * 1024
NEG_BIG = -1e30
Q_SCALE = ATT_HD ** -0.5 * 1.4426950408889634


def _cparams(sem):
    return pltpu.CompilerParams(dimension_semantics=sem, vmem_limit_bytes=VMEM_LIMIT)


def _norm_mod(x, gain, sc, sh):
    ms = jnp.mean(x * x, axis=-1, keepdims=True)
    return x * lax.rsqrt(ms + EPS) * gain * (1.0 + sc) + sh


def _vec_spec(width, tiles_per_batch, per_batch, grid_rank):
    if grid_rank == 1:
        return pl.BlockSpec((1, 1, width), lambda i: ((i // tiles_per_batch) if per_batch else 0, 0, 0))
    return pl.BlockSpec((1, 1, width), lambda i, j: ((i // tiles_per_batch) if per_batch else 0, 0, 0))


def _mod_body(c_ref, w_ref, b_ref, o_ref):
    cs = jax.nn.silu(c_ref[...])
    o_ref[0] = jnp.dot(cs, w_ref[0], precision=HIGHEST, preferred_element_type=F32) + b_ref[0]


def _modulation(cond, w_mod, b_mod):
    depth, _, n = w_mod.shape
    rows = cond.shape[0]
    tn = n // 4
    return pl.pallas_call(
        _mod_body,
        grid=(depth, n // tn),
        in_specs=[
            pl.BlockSpec((rows, D_MODEL), lambda l, j: (0, 0)),
            pl.BlockSpec((1, D_MODEL, tn), lambda l, j: (l, 0, j)),
            pl.BlockSpec((1, 1, tn), lambda l, j: (l, 0, j)),
        ],
        out_specs=pl.BlockSpec((1, rows, tn), lambda l, j: (l, 0, j)),
        out_shape=jax.ShapeDtypeStruct((depth, rows, n), F32),
        compiler_params=_cparams(("arbitrary", "arbitrary")),
        name="modulation",
    )(cond, w_mod, b_mod.reshape(depth, 1, n))


def _chunks(total, width):
    return [(s, min(width, total - s)) for s in range(0, total, width)]


def _resident(shape):
    return pl.BlockSpec(shape, lambda *_: (0,) * len(shape), pipeline_mode=pl.Buffered(1))


def _norm_mod_halo(tiles_per_seq, tm, x_ref, xn_ref, xp_ref, gain_ref, sc_ref, sh_ref, h_scr):
    gain, sc, sh = gain_ref[...], sc_ref[0], sh_ref[0]
    h_scr[0:tm] = _norm_mod(x_ref[...], gain, sc, sh).astype(BF16)
    pos = pl.program_id(0) % tiles_per_seq
    hn = jnp.where(pos == tiles_per_seq - 1, 0.0, _norm_mod(xn_ref[0], gain, sc, sh))
    hp = jnp.where(pos == 0, 0.0, _norm_mod(xp_ref[0], gain, sc, sh))
    h_scr[tm:tm + 2 * SUBLANES] = jnp.concatenate([hn, hp], axis=0).astype(BF16)


def _dwconv3_rows(u, cw, cb, tm):
    rows = tm + 2 * SUBLANES
    c = pltpu.roll(u, 1, 0) * cw[0:1] + u * cw[1:2] + pltpu.roll(u, rows - 1, 0) * cw[2:3] + cb
    return c[0:tm]


def _halo_specs(m, tm):
    m8 = m // SUBLANES
    r8 = tm // SUBLANES
    return [pl.BlockSpec((1, SUBLANES, D_MODEL), lambda i: (jnp.minimum((i + 1) * r8, m8 - 1), 0, 0)),
            pl.BlockSpec((1, SUBLANES, D_MODEL), lambda i: (jnp.maximum(i * r8 - 1, 0), 0, 0))]


XBC_COL = 2 * RET_V


def _hyb_in_body(tiles_per_seq, tm, tn, x_ref, xn_ref, xp_ref, gain_ref, sc_ref, sh_ref, w_ref, cw_ref, cb_ref,
                 qkv_ref, rest_ref, h_scr):
    _norm_mod_halo(tiles_per_seq, tm, x_ref, xn_ref, xp_ref, gain_ref, sc_ref, sh_ref, h_scr)
    h_all = h_scr[...]
    h = h_scr[0:tm]
    for col, width in _chunks(HYB_QKV, tn):
        qkv_ref[:, col:col + width] = jnp.dot(h, w_ref[:, col:col + width],
                                              preferred_element_type=F32).astype(BF16)
    for col, width in _chunks(HYB_REST, tn):
        wcols = w_ref[:, HYB_QKV + col:HYB_QKV + col + width]
        if XBC_COL <= col < XBC_COL + SSM_XBC:
            u = jnp.dot(h_all, wcols, preferred_element_type=F32)
            cc = col - XBC_COL
            rest_ref[:, col:col + width] = jax.nn.silu(
                _dwconv3_rows(u, cw_ref[:, cc:cc + width], cb_ref[:, cc:cc + width], tm))
        else:
            rest_ref[:, col:col + width] = jnp.dot(h, wcols, preferred_element_type=F32)


def _hyb_in_proj(x, gain, sc, sh, w, conv_w, conv_b, seq_len, tm, tn):
    m = x.shape[0]
    per_batch = sc.shape[0] > 1
    tpb = seq_len // tm
    assert XBC_COL % tn == 0 and SSM_XBC % tn == 0
    x8 = x.reshape(m // SUBLANES, SUBLANES, D_MODEL)
    return pl.pallas_call(
        functools.partial(_hyb_in_body, tpb, tm, tn),
        grid=(m // tm,),
        in_specs=[
            pl.BlockSpec((tm, D_MODEL), lambda i: (i, 0)),
            *_halo_specs(m, tm),
            pl.BlockSpec((1, D_MODEL), lambda i: (0, 0)),
            _vec_spec(D_MODEL, tpb, per_batch, 1),
            _vec_spec(D_MODEL, tpb, per_batch, 1),
            _resident((D_MODEL, HYB_QKV + HYB_REST)),
            _resident((3, SSM_XBC)),
            _resident((1, SSM_XBC)),
        ],
        out_specs=[pl.BlockSpec((tm, HYB_QKV), lambda i: (i, 0)), pl.BlockSpec((tm, HYB_REST), lambda i: (i, 0))],
        out_shape=[jax.ShapeDtypeStruct((m, HYB_QKV), BF16), jax.ShapeDtypeStruct((m, HYB_REST), F32)],
        scratch_shapes=[pltpu.VMEM((tm + 2 * SUBLANES, D_MODEL), BF16)],
        compiler_params=_cparams(("arbitrary",)),
        name="hybrid_in_proj",
    )(x, x8, x8, gain, sc, sh, w, conv_w, conv_b.reshape(1, SSM_XBC))


def _ffn_body(tiles_per_seq, tm, th, x_ref, xn_ref, xp_ref, gain_ref, sc_ref, sh_ref, gate_ref,
              wu_ref, cw_ref, cb_ref, wd_ref, o_ref, h_scr, act_scr):
    _norm_mod_halo(tiles_per_seq, tm, x_ref, xn_ref, xp_ref, gain_ref, sc_ref, sh_ref, h_scr)
    h = h_scr[...]

    def conv_up(col, width):
        u = jnp.dot(h, wu_ref[:, col:col + width], preferred_element_type=F32)
        return _dwconv3_rows(u, cw_ref[:, col:col + width], cb_ref[:, col:col + width], tm)

    for col, width in _chunks(FFN_HIDDEN, th):
        act = jax.nn.silu(conv_up(col, width)) * conv_up(FFN_HIDDEN + col, width)
        act_scr[:, col:col + width] = act.astype(BF16)
    out = jnp.dot(act_scr[...], wd_ref[...], preferred_element_type=F32)
    o_ref[...] = x_ref[...] + gate_ref[0] * out


def _conv_ffn(x, gain, sc, sh, gate, w_up, conv_w, conv_b, w_down, seq_len, tm, th):
    m = x.shape[0]
    per_batch = sc.shape[0] > 1
    tpb = seq_len // tm
    x8 = x.reshape(m // SUBLANES, SUBLANES, D_MODEL)
    conv_b = conv_b.reshape(1, 2 * FFN_HIDDEN)
    vec = _vec_spec(D_MODEL, tpb, per_batch, 1)
    return pl.pallas_call(
        functools.partial(_ffn_body, tpb, tm, th),
        grid=(m // tm,),
        in_specs=[
            pl.BlockSpec((tm, D_MODEL), lambda i: (i, 0)),
            *_halo_specs(m, tm),
            pl.BlockSpec((1, D_MODEL), lambda i: (0, 0)),
            vec, vec, vec,
            _resident((D_MODEL, 2 * FFN_HIDDEN)),
            _resident((3, 2 * FFN_HIDDEN)),
            _resident((1, 2 * FFN_HIDDEN)),
            _resident((FFN_HIDDEN, D_MODEL)),
        ],
        out_specs=pl.BlockSpec((tm, D_MODEL), lambda i: (i, 0)),
        out_shape=jax.ShapeDtypeStruct((m, D_MODEL), F32),
        scratch_shapes=[pltpu.VMEM((tm + 2 * SUBLANES, D_MODEL), BF16), pltpu.VMEM((tm, FFN_HIDDEN), BF16)],
        compiler_params=_cparams(("arbitrary",)),
        name="conv_ffn",
    )(x, x8, x8, gain, sc, sh, gate, w_up, conv_w, conv_b, w_down)


SSM_GW = SSM_RATIO * SSM_HEADDIM


def _bcol(x, h):
    return jnp.broadcast_to(x[:, h:h + 1], (x.shape[0], LANES))


def _pair_sel(left, a, h):
    return jnp.where(left, _bcol(a, h), _bcol(a, h + 1))


def _ssm_decay_row(left, efull, g):
    return jnp.concatenate([_pair_sel(left, efull, g * SSM_RATIO + 2 * m) for m in range(SSM_RATIO // 2)], axis=1)


def _load_states(s_ret, s_ssm, sret0_ref, sssm0_ref):
    if sret0_ref is None:
        s_ret[...] = jnp.zeros_like(s_ret)
        s_ssm[...] = jnp.zeros_like(s_ssm)
    else:
        s_ret[...] = sret0_ref[0]
        for g in range(SSM_GROUPS):
            s_ssm[g] = jnp.concatenate([sssm0_ref[0, g * SSM_RATIO + r] for r in range(SSM_RATIO)], axis=1)


def _store_states(s_ret, s_ssm, sret_ref, sssm_ref):
    sret_ref[0] = s_ret[...]
    for g in range(SSM_GROUPS):
        for r in range(SSM_RATIO):
            sssm_ref[0, g * SSM_RATIO + r] = s_ssm[g][:, r * SSM_HEADDIM:(r + 1) * SSM_HEADDIM]


def _ssm_log_decay(dt_raw, dtb, alog, lane):
    dt = jax.nn.softplus(dt_raw + dtb)
    return dt, jnp.where(lane < SSM_HEADS, dt * (-jnp.exp(alog)), 0.0)


def _bwd_state_body(has_init, k_ref, v_ref, xs_ref, bm_ref, dt_ref, dtb_ref, alog_ref, logdec_ref, *rest):
    sret0_ref = sssm0_ref = None
    if has_init:
        sret0_ref, sssm0_ref = rest[0], rest[1]
        rest = rest[2:]
    sret_in_ref, sssm_in_ref, sret_fin_ref, sssm_fin_ref, s_ret, s_ssm, wcol = rest
    i = pl.program_id(1)
    n = pl.num_programs(1)
    ii = lax.broadcasted_iota(jnp.int32, (CHUNK, CHUNK), 0)
    jj = lax.broadcasted_iota(jnp.int32, (CHUNK, CHUNK), 1)
    lane = lax.broadcasted_iota(jnp.int32, (1, LANES), 1)
    left = lane < SSM_HEADDIM

    @pl.when(i == 0)
    def _():
        _load_states(s_ret, s_ssm, sret0_ref, sssm0_ref)
        for h in range(RET_HEADS):
            wcol[h] = jnp.exp(ii.astype(F32) * logdec_ref[1, h])

    sret_in_ref[0, 0] = s_ret[...]
    sssm_in_ref[0, 0] = s_ssm[...]

    full = jnp.full((1, LANES), float(CHUNK), F32)
    for h in range(RET_HEADS):
        kh = k_ref[:, h * RET_DK:(h + 1) * RET_DK]
        vw = (v_ref[:, h * RET_DV:(h + 1) * RET_DV].astype(F32) * wcol[h]).astype(BF16)
        upd = lax.dot_general(kh, vw, (((0,), (0,)), ((), ())), preferred_element_type=F32)
        s_ret[h] = s_ret[h] * jnp.exp(full * logdec_ref[1, h]) + upd

    dt, la = _ssm_log_decay(dt_ref[...], dtb_ref[...], alog_ref[...], lane)
    rc = jnp.dot((jj >= ii).astype(F32), la, precision=HIGHEST, preferred_element_type=F32)
    last = rc[0:1]
    w = jnp.exp(last - rc) * dt
    efull = jnp.exp(last)
    for g in range(SSM_GROUPS):
        vw = jnp.concatenate(
            [(xs_ref[:, (g * 2 + m) * LANES:(g * 2 + m + 1) * LANES]
              * _pair_sel(left, w, g * SSM_RATIO + 2 * m)).astype(BF16) for m in range(SSM_RATIO // 2)], axis=1)
        bg = bm_ref[:, g * SSM_STATE:(g + 1) * SSM_STATE].astype(BF16)
        upd = lax.dot_general(bg, vw, (((0,), (0,)), ((), ())), preferred_element_type=F32)
        s_ssm[g] = s_ssm[g] * _ssm_decay_row(left, efull, g) + upd

    @pl.when(i == n - 1)
    def _():
        _store_states(s_ret, s_ssm, sret_fin_ref, sssm_fin_ref)


def _hyb_fwd_body(has_init, qkv_ref, xs_ref, bc_ref, dt_ref, sret_in_ref, sssm_in_ref, dtb_ref, alog_ref, dsk_ref,
                  logdec_ref, *rest):
    sret0_ref = sssm0_ref = None
    if has_init:
        sret0_ref, sssm0_ref = rest[0], rest[1]
        rest = rest[2:]
    y_ref, sret_fin_ref, sssm_fin_ref, s_ret, s_ssm, dcomb, ein0, ein1, wcol = rest
    i = pl.program_id(1)
    n = pl.num_programs(1)
    ii = lax.broadcasted_iota(jnp.int32, (CHUNK, CHUNK), 0)
    jj = lax.broadcasted_iota(jnp.int32, (CHUNK, CHUNK), 1)
    lane = lax.broadcasted_iota(jnp.int32, (1, LANES), 1)
    left = lane < SSM_HEADDIM
    lower = jj <= ii

    @pl.when(i == 0)
    def _():
        _load_states(s_ret, s_ssm, sret0_ref, sssm0_ref)
        fi, fj = ii.astype(F32), jj.astype(F32)
        for h in range(RET_HEADS):
            la0, la1 = logdec_ref[0, h], logdec_ref[1, h]
            dec = jnp.exp(jnp.where(lower, (fi - fj) * la0, (fj - fi) * la1))
            dcomb[h] = jnp.where(ii == jj, 2.0, dec)
            ein0[h] = jnp.exp((fi + 1.0) * la0)
            ein1[h] = jnp.exp((CHUNK - fi) * la1)
            wcol[h] = jnp.exp((CHUNK - 1.0 - fi) * la0)

    full = jnp.full((1, LANES), float(CHUNK), F32)
    for h in range(RET_HEADS):
        qh = qkv_ref[:, h * RET_DK:(h + 1) * RET_DK]
        kh = qkv_ref[:, RET_Q + h * RET_DK:RET_Q + (h + 1) * RET_DK]
        vh = qkv_ref[:, 2 * RET_Q + h * RET_DV:2 * RET_Q + (h + 1) * RET_DV]
        s = lax.dot_general(qh, kh, (((1,), (1,)), ((), ())), preferred_element_type=F32)
        y = jnp.dot((s * dcomb[h]).astype(BF16), vh, preferred_element_type=F32)
        states = jnp.concatenate([s_ret[h], sret_in_ref[0, 0, h]], axis=1).astype(BF16)
        inter = jnp.dot(qh, states, preferred_element_type=F32)
        y_ref[:, h * RET_DV:(h + 1) * RET_DV] = y + inter[:, 0:RET_DV] * ein0[h] + inter[:, RET_DV:] * ein1[h]
        vw = (vh.astype(F32) * wcol[h]).astype(BF16)
        upd = lax.dot_general(kh, vw, (((0,), (0,)), ((), ())), preferred_element_type=F32)
        s_ret[h] = s_ret[h] * jnp.exp(full * logdec_ref[0, h]) + upd

    dt_raw = dt_ref[...]
    dt0, la0 = _ssm_log_decay(dt_raw, dtb_ref[0:1], alog_ref[0:1], lane)
    dt1, la1 = _ssm_log_decay(dt_raw, dtb_ref[1:2], alog_ref[1:2], lane)
    c0 = jnp.dot(lower.astype(F32), la0, precision=HIGHEST, preferred_element_type=F32)
    c1 = jnp.dot((jj >= ii).astype(F32), la1, precision=HIGHEST, preferred_element_type=F32)
    c0t, c1t, dt0t, dt1t = c0.T, c1.T, dt0.T, dt1.T
    dst = dt0t + dt1t
    last0 = c0[CHUNK - 1:CHUNK]
    w0 = jnp.exp(last0 - c0) * dt0
    efull0 = jnp.exp(last0)
    strict = jj < ii
    diag = jj == ii
    for g in range(SSM_GROUPS):
        bg = bc_ref[:, g * SSM_STATE:(g + 1) * SSM_STATE].astype(BF16)
        cg = bc_ref[:, SSM_BC + g * SSM_STATE:SSM_BC + (g + 1) * SSM_STATE].astype(BF16)
        s = lax.dot_general(cg, bg, (((1,), (1,)), ((), ())), preferred_element_type=F32)
        states = jnp.concatenate([s_ssm[g], sssm_in_ref[0, 0, g]], axis=1).astype(BF16)
        inter = jnp.dot(cg, states, preferred_element_type=F32)
        vws = []
        for m in range(SSM_RATIO // 2):
            ha = g * SSM_RATIO + 2 * m
            ps, c0bs, c1bs = [], [], []
            for h in (ha, ha + 1):
                c0b, c1b = _bcol(c0, h), _bcol(c1, h)
                arg = jnp.where(lower, c0b - c0t[h:h + 1], c1b - c1t[h:h + 1])
                wgt = jnp.where(strict, dt0t[h:h + 1], jnp.where(diag, dst[h:h + 1], dt1t[h:h + 1]))
                ps.append((s * jnp.exp(arg) * wgt).astype(BF16))
                c0bs.append(c0b)
                c1bs.append(c1b)
            col = (g * 2 + m) * LANES
            xs_pair = xs_ref[:, col:col + LANES]
            vals = jnp.concatenate([jnp.where(left, xs_pair, 0.0), jnp.where(left, 0.0, xs_pair)], axis=0)
            y = jnp.dot(jnp.concatenate(ps, axis=1), vals.astype(BF16), preferred_element_type=F32)
            y = y + inter[:, m * LANES:(m + 1) * LANES] * jnp.exp(jnp.where(left, c0bs[0], c0bs[1]))
            y = y + (inter[:, SSM_GW + m * LANES:SSM_GW + (m + 1) * LANES]
                     * jnp.exp(jnp.where(left, c1bs[0], c1bs[1])))
            y_ref[:, RET_V + col:RET_V + col + LANES] = y + dsk_ref[:, col:col + LANES] * xs_pair
            vws.append((xs_pair * _pair_sel(left, w0, ha)).astype(BF16))
        upd = lax.dot_general(bg, jnp.concatenate(vws, axis=1), (((0,), (0,)), ((), ())),
                              preferred_element_type=F32)
        s_ssm[g] = s_ssm[g] * _ssm_decay_row(left, efull0, g) + upd

    @pl.when(i == n - 1)
    def _():
        _store_states(s_ret, s_ssm, sret_fin_ref, sssm_fin_ref)


def _hybrid_mixer(qkv, rest, dt_bias, a_log, log_decay, d_skip, s_ret0, s_ssm0, batch, seq_len):
    m = qkv.shape[0]
    n = seq_len // CHUNK
    has_init = s_ret0 is not None
    dt_blk = (XBC_COL + SSM_XBC) // LANES
    pad = lambda v: jnp.pad(v.astype(F32), ((0, 0), (0, LANES - v.shape[1])))
    dtb, alog = pad(dt_bias), pad(a_log)
    logdec = log_decay.astype(F32)
    dsk = jnp.repeat((d_skip[0] + d_skip[1]).astype(F32), SSM_HEADDIM).reshape(1, SSM_D_INNER)
    smem = pl.BlockSpec(memory_space=pltpu.SMEM)
    ret_spec = pl.BlockSpec((1, RET_HEADS, RET_DK, RET_DV), lambda b, i: (b, 0, 0, 0))
    ssm_spec = pl.BlockSpec((1, SSM_HEADS, SSM_STATE, SSM_HEADDIM), lambda b, i: (b, 0, 0, 0))
    ret_shape = jax.ShapeDtypeStruct((batch, RET_HEADS, RET_DK, RET_DV), F32)
    ssm_shape = jax.ShapeDtypeStruct((batch, SSM_HEADS, SSM_STATE, SSM_HEADDIM), F32)
    ret_scr = pltpu.VMEM((RET_HEADS, RET_DK, RET_DV), F32)
    ssm_scr = pltpu.VMEM((SSM_GROUPS, SSM_STATE, SSM_GW), F32)
    const_scr = pltpu.VMEM((RET_HEADS, CHUNK, CHUNK), F32)

    def rev(b, i):
        return b * n + n - 1 - i

    in_specs = [
        pl.BlockSpec((CHUNK, RET_Q), lambda b, i: (rev(b, i), 1)),
        pl.BlockSpec((CHUNK, RET_V), lambda b, i: (rev(b, i), 1)),
        pl.BlockSpec((CHUNK, SSM_D_INNER), lambda b, i: (rev(b, i), XBC_COL // SSM_D_INNER)),
        pl.BlockSpec((CHUNK, SSM_BC), lambda b, i: (rev(b, i), (XBC_COL + SSM_D_INNER) // SSM_BC)),
        pl.BlockSpec((CHUNK, LANES), lambda b, i: (rev(b, i), dt_blk)),
        pl.BlockSpec((1, LANES), lambda b, i: (0, 0)),
        pl.BlockSpec((1, LANES), lambda b, i: (0, 0)),
        smem,
    ]
    args = [qkv, qkv, rest, rest, rest, dtb[1:2], alog[1:2], logdec]
    if has_init:
        in_specs += [ret_spec, ssm_spec]
        args += [s_ret0[:, 1], s_ssm0[:, 1]]
    sret_in, sssm_in, sret1, sssm1 = pl.pallas_call(
        functools.partial(_bwd_state_body, has_init),
        grid=(batch, n),
        in_specs=in_specs,
        out_specs=[
            pl.BlockSpec((1, 1, RET_HEADS, RET_DK, RET_DV), lambda b, i: (b, n - 1 - i, 0, 0, 0)),
            pl.BlockSpec((1, 1, SSM_GROUPS, SSM_STATE, SSM_GW), lambda b, i: (b, n - 1 - i, 0, 0, 0)),
            ret_spec, ssm_spec,
        ],
        out_shape=[
            jax.ShapeDtypeStruct((batch, n, RET_HEADS, RET_DK, RET_DV), F32),
            jax.ShapeDtypeStruct((batch, n, SSM_GROUPS, SSM_STATE, SSM_GW), F32),
            ret_shape, ssm_shape,
        ],
        scratch_shapes=[ret_scr, ssm_scr, const_scr],
        compiler_params=_cparams(("arbitrary", "arbitrary")),
        name="hybrid_reverse_states",
    )(*args)

    def fwd(b, i):
        return b * n + i

    in_specs = [
        pl.BlockSpec((CHUNK, HYB_QKV), lambda b, i: (fwd(b, i), 0)),
        pl.BlockSpec((CHUNK, SSM_D_INNER), lambda b, i: (fwd(b, i), XBC_COL // SSM_D_INNER)),
        pl.BlockSpec((CHUNK, 2 * SSM_BC), lambda b, i: (fwd(b, i), (XBC_COL + SSM_D_INNER) // (2 * SSM_BC))),
        pl.BlockSpec((CHUNK, LANES), lambda b, i: (fwd(b, i), dt_blk)),
        pl.BlockSpec((1, 1, RET_HEADS, RET_DK, RET_DV), lambda b, i: (b, i, 0, 0, 0)),
        pl.BlockSpec((1, 1, SSM_GROUPS, SSM_STATE, SSM_GW), lambda b, i: (b, i, 0, 0, 0)),
        pl.BlockSpec((2, LANES), lambda b, i: (0, 0)),
        pl.BlockSpec((2, LANES), lambda b, i: (0, 0)),
        pl.BlockSpec((1, SSM_D_INNER), lambda b, i: (0, 0)),
        smem,
    ]
    args = [qkv, rest, rest, rest, sret_in, sssm_in, dtb, alog, dsk, logdec]
    if has_init:
        in_specs += [ret_spec, ssm_spec]
        args += [s_ret0[:, 0], s_ssm0[:, 0]]
    y, sret0, sssm0 = pl.pallas_call(
        functools.partial(_hyb_fwd_body, has_init),
        grid=(batch, n),
        in_specs=in_specs,
        out_specs=[pl.BlockSpec((CHUNK, HYB_MIX), lambda b, i: (fwd(b, i), 0)), ret_spec, ssm_spec],
        out_shape=[jax.ShapeDtypeStruct((m, HYB_MIX), F32), ret_shape, ssm_shape],
        scratch_shapes=[ret_scr, ssm_scr, const_scr, const_scr, const_scr, const_scr],
        compiler_params=_cparams(("arbitrary", "arbitrary")),
        name="hybrid_forward_mix",
    )(*args)
    return y, jnp.stack([sret0, sret1], axis=1), jnp.stack([sssm0, sssm1], axis=1)


def _hyb_out_body(y_ref, g_ref, z_ref, x_ref, gate_ref, gn_ref, ng_ref, w_ref, o_ref):
    y = y_ref[...]
    parts = []
    for h in range(RET_HEADS):
        yh = y[:, h * RET_DV:(h + 1) * RET_DV]
        mu = jnp.mean(yh, axis=-1, keepdims=True)
        var = jnp.mean(jnp.square(yh - mu), axis=-1, keepdims=True)
        parts.append((yh - mu) * lax.rsqrt(var + EPS))
    y_ret = jax.nn.silu(g_ref[...]) * (jnp.concatenate(parts, axis=1) * gn_ref[...])
    yz = y[:, RET_V:] * jax.nn.silu(z_ref[...])
    y_ssm = yz * lax.rsqrt(jnp.mean(yz * yz, axis=-1, keepdims=True) + EPS) * ng_ref[...]
    out = jnp.dot(y_ret.astype(BF16), w_ref[0:RET_V], preferred_element_type=F32)
    out = out + jnp.dot(y_ssm.astype(BF16), w_ref[RET_V:HYB_MIX], preferred_element_type=F32)
    o_ref[...] = x_ref[...] + gate_ref[0] * out


def _hyb_out(y, rest, x, gate, gn_gain, norm_gain, w_out, seq_len, tm):
    m = x.shape[0]
    per_batch = gate.shape[0] > 1
    tpb = seq_len // tm
    return pl.pallas_call(
        _hyb_out_body,
        grid=(m // tm,),
        in_specs=[
            pl.BlockSpec((tm, HYB_MIX), lambda i: (i, 0)),
            pl.BlockSpec((tm, RET_V), lambda i: (i, 0)),
            pl.BlockSpec((tm, SSM_D_INNER), lambda i: (i, 1)),
            pl.BlockSpec((tm, D_MODEL), lambda i: (i, 0)),
            _vec_spec(D_MODEL, tpb, per_batch, 1),
            pl.BlockSpec((1, RET_V), lambda i: (0, 0)),
            pl.BlockSpec((1, SSM_D_INNER), lambda i: (0, 0)),
            pl.BlockSpec((HYB_MIX, D_MODEL), lambda i: (0, 0)),
        ],
        out_specs=pl.BlockSpec((tm, D_MODEL), lambda i: (i, 0)),
        out_shape=jax.ShapeDtypeStruct((m, D_MODEL), F32),
        compiler_params=_cparams(("arbitrary",)),
        name="hybrid_out_proj",
    )(y, rest, rest, x, gate, gn_gain, norm_gain, w_out)


def _head_rms(t, seg_ref):
    outs = []
    for c in range(t.shape[1] // LANES):
        tc = t[:, c * LANES:(c + 1) * LANES]
        sq = tc * tc
        hi = sq.astype(BF16)
        lo = (sq - hi.astype(F32)).astype(BF16)
        ssum = (jnp.dot(hi, seg_ref[...], preferred_element_type=F32)
                + jnp.dot(lo, seg_ref[...], preferred_element_type=F32))
        outs.append(tc * lax.rsqrt(ssum * (1.0 / ATT_HD) + EPS))
    return outs


def _rope(tc, cos, sin, lane):
    fwd = pltpu.roll(tc, LANES - ROT_FREQS, 1)
    bwd = pltpu.roll(tc, ROT_FREQS, 1)
    return tc * cos + jnp.where(lane % (2 * ROT_FREQS) < ROT_FREQS, fwd, bwd) * sin


def _qkv_body(has_rope, x_ref, gain_ref, sc_ref, sh_ref, w_ref, seg_ref, qg_ref, kg_ref, *rest):
    if has_rope:
        cos_ref, sin_ref, q_ref, k_ref, v_ref = rest
    else:
        q_ref, k_ref, v_ref, kf_ref, vf_ref = rest
    h = _norm_mod(x_ref[...], gain_ref[...], sc_ref[0], sh_ref[0]).astype(BF16)
    qkv = jnp.dot(h, w_ref[...], preferred_element_type=F32)
    q = _head_rms(qkv[:, 0:ATT_Q], seg_ref)
    k = _head_rms(qkv[:, ATT_Q:ATT_Q + ATT_KV], seg_ref)
    v = qkv[:, ATT_Q + ATT_KV:ATT_Q + 2 * ATT_KV]
    q = [t * qg_ref[...] for t in q]
    k = [t * kg_ref[...] for t in k]
    if has_rope:
        lane = lax.broadcasted_iota(jnp.int32, (1, LANES), 1)
        cos, sin = cos_ref[...], sin_ref[...]
        q = [_rope(t, cos, sin, lane) for t in q]
        kr = [_rope(t, cos, sin, lane) for t in k]
    else:
        kr = k
        kf_ref[...] = jnp.concatenate(k, axis=1)
        vf_ref[...] = v
    for c, t in enumerate(q):
        q_ref[0, c * LANES:(c + 1) * LANES, :] = (t * Q_SCALE).T.astype(BF16)
    for c in range(ATT_KV // LANES):
        kc = kr[c].astype(BF16)
        vc = v[:, c * LANES:(c + 1) * LANES].T.astype(BF16)
        for a in range(LANES // ATT_HD):
            k_ref[0, 2 * c + a] = kc[:, a * ATT_HD:(a + 1) * ATT_HD]
            v_ref[0, 2 * c + a] = vc[a * ATT_HD:(a + 1) * ATT_HD]


def _qkv_proj(x, gain, sc, sh, w_qkv, q_gain, k_gain, rope, seq_len, tm):
    m = x.shape[0]
    per_batch = sc.shape[0] > 1
    tpb = seq_len // tm
    has_rope = rope is not None
    seg = (lax.broadcasted_iota(jnp.int32, (LANES, LANES), 0) // ATT_HD
           == lax.broadcasted_iota(jnp.int32, (LANES, LANES), 1) // ATT_HD).astype(BF16)
    tile2 = lambda v: jnp.tile(v.astype(F32), LANES // ATT_HD).reshape(1, LANES)
    in_specs = [
        pl.BlockSpec((tm, D_MODEL), lambda i: (i, 0)),
        pl.BlockSpec((1, D_MODEL), lambda i: (0, 0)),
        _vec_spec(D_MODEL, tpb, per_batch, 1),
        _vec_spec(D_MODEL, tpb, per_batch, 1),
        pl.BlockSpec((D_MODEL, ATT_Q + 2 * ATT_KV), lambda i: (0, 0)),
        pl.BlockSpec((LANES, LANES), lambda i: (0, 0)),
        pl.BlockSpec((1, LANES), lambda i: (0, 0)),
        pl.BlockSpec((1, LANES), lambda i: (0, 0)),
    ]
    args = [x, gain, sc, sh, w_qkv, seg, tile2(q_gain), tile2(k_gain)]
    batch = m // seq_len
    out_specs = [pl.BlockSpec((1, ATT_Q, tm), lambda i: (i // tpb, 0, i % tpb)),
                 pl.BlockSpec((1, ATT_KV_HEADS, tm, ATT_HD), lambda i: (i // tpb, 0, i % tpb, 0)),
                 pl.BlockSpec((1, ATT_KV_HEADS, ATT_HD, tm), lambda i: (i // tpb, 0, 0, i % tpb))]
    out_shape = [jax.ShapeDtypeStruct((batch, ATT_Q, seq_len), BF16),
                 jax.ShapeDtypeStruct((batch, ATT_KV_HEADS, seq_len, ATT_HD), BF16),
                 jax.ShapeDtypeStruct((batch, ATT_KV_HEADS, ATT_HD, seq_len), BF16)]
    if has_rope:
        in_specs += [pl.BlockSpec((tm, LANES), lambda i: (i % tpb, 0))] * 2
        args += list(rope)
    else:
        out_specs += [pl.BlockSpec((tm, ATT_KV), lambda i: (i, 0))] * 2
        out_shape += [jax.ShapeDtypeStruct((m, ATT_KV), F32)] * 2
    return pl.pallas_call(
        functools.partial(_qkv_body, has_rope),
        grid=(m // tm,),
        in_specs=in_specs,
        out_specs=out_specs,
        out_shape=out_shape,
        compiler_params=_cparams(("arbitrary",)),
        name="attn_qkv_proj",
    )(*args)


def _rope_tables(seq_len):
    rows = seq_len // GRID_W
    row = jnp.repeat(jnp.arange(rows, dtype=F32), GRID_W)
    col = jnp.tile(jnp.arange(GRID_W, dtype=F32), rows)
    inv = ROPE_BASE ** (-jnp.arange(ROT_FREQS, dtype=F32) / ROT_FREQS)
    ar, ac = row[:, None] * inv, col[:, None] * inv
    cos = jnp.concatenate([jnp.cos(ar), jnp.cos(ar), jnp.cos(ac), jnp.cos(ac)], axis=1)
    sin = jnp.concatenate([-jnp.sin(ar), jnp.sin(ar), -jnp.sin(ac), jnp.sin(ac)], axis=1)
    return jnp.tile(cos, (1, LANES // ATT_HD)), jnp.tile(sin, (1, LANES // ATT_HD))


V_ROWS = ATT_HD + 16
ATT_SLOTS = 4


def _attn_body(tq, tk, qt_ref, k_ref, vt_ref, x_ref, gate_ref, wo_ref, o_ref, acc_scr, s_scr, ot_scr):
    nkv = k_ref.shape[2] // tk
    for g in range(ATT_KV_HEADS):
        qts = [qt_ref[0, (g * ATT_RATIO + r) * ATT_HD:(g * ATT_RATIO + r + 1) * ATT_HD, :]
               for r in range(ATT_RATIO)]
        acc_scr[...] = jnp.zeros_like(acc_scr)

        def scores(t, slot):
            kt = k_ref[0, g, pl.ds(pl.multiple_of(t * tk, tk), tk), :]
            for r in range(ATT_RATIO):
                s_scr[slot, r] = jnp.dot(kt, qts[r], preferred_element_type=F32)

        def softmax_pv(t, slot, ms):
            vt = vt_ref[0, g, :, pl.ds(pl.multiple_of(t * tk, tk), tk)]
            new_ms = [jnp.maximum(ms[r], jnp.max(s_scr[slot, r], axis=0, keepdims=True)) for r in range(ATT_RATIO)]
            ps = [jnp.exp2(s_scr[slot, r] - new_ms[r]).astype(BF16) for r in range(ATT_RATIO)]
            for r in range(ATT_RATIO):
                acc_scr[r] = (jnp.exp2(ms[r] - new_ms[r]) * acc_scr[r]
                              + jnp.dot(vt, ps[r], preferred_element_type=F32))
            return tuple(new_ms)

        def two_tiles(t, ms, last):
            slot = t % ATT_SLOTS if isinstance(t, int) else None
            for a in range(2):
                if last is None or t + 2 + a <= last:
                    scores(t + 2 + a, (slot + 2 + a) % ATT_SLOTS)
            for a in range(2):
                if last is None or t + a <= last:
                    ms = softmax_pv(t + a, slot + a, ms)
            return ms

        def quad_step(i, ms):
            t = i * ATT_SLOTS
            scores(t + 2, 2)
            scores(t + 3, 3)
            ms = softmax_pv(t, 0, ms)
            ms = softmax_pv(t + 1, 1, ms)
            scores(t + 4, 0)
            scores(t + 5, 1)
            ms = softmax_pv(t + 2, 2, ms)
            return softmax_pv(t + 3, 3, ms)

        for t in range(min(2, nkv)):
            scores(t, t)
        ms = tuple(jnp.full((1, tq), NEG_BIG, F32) for _ in range(ATT_RATIO))
        quads = max(nkv - 2, 0) // ATT_SLOTS
        ms = lax.fori_loop(0, quads, quad_step, ms)
        for t in range(quads * ATT_SLOTS, nkv, 2):
            ms = two_tiles(t, ms, nkv - 1)
        for r in range(ATT_RATIO):
            hd = g * ATT_RATIO + r
            a = acc_scr[r]
            ot_scr[hd * ATT_HD:(hd + 1) * ATT_HD, :] = (a[0:ATT_HD] / a[ATT_HD:ATT_HD + 1]).astype(BF16)
    out = lax.dot_general(ot_scr[...], wo_ref[...], (((0,), (0,)), ((), ())), preferred_element_type=F32)
    o_ref[0] = x_ref[0] + gate_ref[0] * out


def _attention(qt, k4, vt, x, gate, w_o, batch, seq_len, tq, tk):
    per_batch = gate.shape[0] > 1
    nkeys = k4.shape[2]
    x3 = x.reshape(batch, seq_len, D_MODEL)
    out = pl.pallas_call(
        functools.partial(_attn_body, tq, tk),
        grid=(batch, seq_len // tq),
        in_specs=[
            pl.BlockSpec((1, ATT_Q, tq), lambda b, i: (b, 0, i)),
            pl.BlockSpec((1, ATT_KV_HEADS, nkeys, ATT_HD), lambda b, i: (b, 0, 0, 0)),
            pl.BlockSpec((1, ATT_KV_HEADS, V_ROWS, nkeys), lambda b, i: (b, 0, 0, 0)),
            pl.BlockSpec((1, tq, D_MODEL), lambda b, i: (b, i, 0)),
            pl.BlockSpec((1, 1, D_MODEL), lambda b, i: (b if per_batch else 0, 0, 0)),
            pl.BlockSpec((ATT_Q, D_MODEL), lambda b, i: (0, 0)),
        ],
        out_specs=pl.BlockSpec((1, tq, D_MODEL), lambda b, i: (b, i, 0)),
        out_shape=jax.ShapeDtypeStruct((batch, seq_len, D_MODEL), F32),
        scratch_shapes=[pltpu.VMEM((ATT_RATIO, V_ROWS, tq), F32), pltpu.VMEM((ATT_SLOTS, ATT_RATIO, tk, tq), F32),
                        pltpu.VMEM((ATT_Q, tq), BF16)],
        compiler_params=_cparams(("arbitrary", "arbitrary")),
        name="attention",
    )(qt, k4, vt, x3, gate, w_o)
    return out.reshape(batch * seq_len, D_MODEL)


def _keys_values(k4, vt, cache_k, cache_v):
    if cache_k is not None:
        k4 = jnp.concatenate([jnp.swapaxes(cache_k, 1, 2).astype(BF16), k4], axis=2)
        vt = jnp.concatenate([jnp.transpose(cache_v, (0, 2, 3, 1)).astype(BF16), vt], axis=3)
    ones = jnp.ones(vt.shape[:2] + (V_ROWS - ATT_HD, vt.shape[3]), vt.dtype)
    return k4, jnp.concatenate([vt, ones], axis=2)


def _trunk(x, mods, caches, P, batch, seq_len, tm, tq, tk):
    rope = None if caches is None else _rope_tables(seq_len)
    outs = {}
    sh1, sc1, g1, sh2, sc2, g2 = mods[0]
    qkv, rest = _hyb_in_proj(x, P['norm_mix'][0:1], sc1, sh1, P['hyb_w_in'], P['ssm_conv_w'][0],
                             P['ssm_conv_b'][0], seq_len, tm, 512)
    s_ret0 = None if caches is None else caches[0][:, 0]
    s_ssm0 = None if caches is None else caches[1][:, 0]
    y, s_ret, s_ssm = _hybrid_mixer(qkv, rest, P['ssm_dt_bias'][0], P['ssm_a_log'][0], P['ret_log_decay'][0],
                                    P['ssm_d'][0], s_ret0, s_ssm0, batch, seq_len)
    outs['ret'] = s_ret[:, None]
    outs['ssm'] = s_ssm[:, None]
    x = _hyb_out(y, rest, x, g1, P['ret_gn_gain'][0:1], P['ssm_norm_gain'][0:1], P['hyb_w_out'], seq_len, tm)
    x = _conv_ffn(x, P['norm_ffn'][0:1], sc2, sh2, g2, P['ffn_w_up'][0], P['ffn_conv_w'][0], P['ffn_conv_b'][0],
                  P['ffn_w_down'][0], seq_len, tm, 512)
    sh1, sc1, g1, sh2, sc2, g2 = mods[1]
    res = _qkv_proj(x, P['norm_mix'][1:2], sc1, sh1, P['attn_w_qkv'], P['attn_q_gain'][0], P['attn_k_gain'][0],
                    rope, seq_len, tm)
    if caches is None:
        qt, k4, vt, kf, vf = res
        outs['k'] = kf.reshape(batch, 1, seq_len, ATT_KV_HEADS, ATT_HD)
        outs['v'] = vf.reshape(batch, 1, seq_len, ATT_KV_HEADS, ATT_HD)
        k4, vt = _keys_values(k4, vt, None, None)
    else:
        qt, k4, vt = res
        k4, vt = _keys_values(k4, vt, caches[2][:, 0], caches[3][:, 0])
    x = _attention(qt, k4, vt, x, g1, P['attn_w_o'], batch, seq_len, tq, tk)
    x = _conv_ffn(x, P['norm_ffn'][1:2], sc2, sh2, g2, P['ffn_w_up'][1], P['ffn_conv_w'][1], P['ffn_conv_b'][1],
                  P['ffn_w_down'][1], seq_len, tm, 512)
    return x, outs


def kernel(x_prompt, x_sample, state_ret, state_ssm, cache_attn_k, cache_attn_v, c, c_ctx, w_mod, b_mod, norm_mix,
           norm_ffn, ffn_w_up, ffn_conv_w, ffn_conv_b, ffn_w_down, hyb_w_in, hyb_w_out, ret_log_decay, ret_gn_gain,
           ssm_conv_w, ssm_conv_b, ssm_a_log, ssm_dt_bias, ssm_d, ssm_norm_gain, attn_w_qkv, attn_q_gain,
           attn_k_gain, attn_w_o):
    batch, seq, _ = x_prompt.shape
    dec_batch, dec_seq, _ = x_sample.shape
    depth = w_mod.shape[0]

    rows = -(-(dec_batch + 1) // SUBLANES) * SUBLANES
    cond = jnp.concatenate([c, c_ctx[None, :], jnp.zeros((rows - dec_batch - 1, D_MODEL), F32)], axis=0)
    mod = _modulation(cond, w_mod, b_mod).reshape(depth, rows, 6, 1, D_MODEL)
    mods_sample = [[mod[l, 0:dec_batch, t] for t in range(6)] for l in range(depth)]
    mods_prompt = [[mod[l, dec_batch:dec_batch + 1, t] for t in range(6)] for l in range(depth)]

    w_in = hyb_w_in[0]
    kscale = jnp.concatenate([jnp.ones((RET_Q,), F32), jnp.full((RET_Q,), RET_DK ** -0.5, F32),
                              jnp.ones((w_in.shape[1] - 2 * RET_Q,), F32)])
    P = {
        'norm_mix': norm_mix, 'norm_ffn': norm_ffn,
        'hyb_w_in': jnp.pad(w_in * kscale, ((0, 0), (0, HYB_QKV + HYB_REST - w_in.shape[1]))).astype(BF16),
        'hyb_w_out': hyb_w_out[0].astype(BF16),
        'ret_log_decay': ret_log_decay, 'ret_gn_gain': ret_gn_gain,
        'ssm_conv_w': ssm_conv_w, 'ssm_conv_b': ssm_conv_b, 'ssm_a_log': ssm_a_log, 'ssm_dt_bias': ssm_dt_bias,
        'ssm_d': ssm_d, 'ssm_norm_gain': ssm_norm_gain,
        'attn_w_qkv': attn_w_qkv[0].astype(BF16), 'attn_q_gain': attn_q_gain, 'attn_k_gain': attn_k_gain,
        'attn_w_o': attn_w_o[0].astype(BF16),
        'ffn_w_up': ffn_w_up.astype(BF16), 'ffn_conv_w': ffn_conv_w, 'ffn_conv_b': ffn_conv_b,
        'ffn_w_down': ffn_w_down.astype(BF16),
    }

    y_prompt, outs = _trunk(x_prompt.reshape(batch * seq, D_MODEL), mods_prompt, None, P, batch, seq,
                            tm=min(256, seq), tq=min(256, seq), tk=min(256, seq))
    caches = (state_ret, state_ssm, cache_attn_k, cache_attn_v)
    y_sample, _ = _trunk(x_sample.reshape(dec_batch * dec_seq, D_MODEL), mods_sample, caches, P, dec_batch, dec_seq,
                         tm=512, tq=256, tk=256)
    return (y_prompt.reshape(batch, seq, D_MODEL), y_sample.reshape(dec_batch, dec_seq, D_MODEL),
            outs['ret'], outs['ssm'], outs['k'], outs['v'])
```

```python
import functools

import jax
import jax.numpy as jnp
from jax import lax
from jax.experimental import pallas as pl
from jax.experimental.pallas import tpu as pltpu

F32 = jnp.float32
BF16 = jnp.bfloat16
HIGHEST = lax.Precision.HIGHEST

D_MODEL = 1024
EPS = 1e-6
GRID_W = 64
CHUNK = 128
RET_HEADS = 8
RET_DK = 64
RET_DV = 128
RET_Q = RET_HEADS * RET_DK
RET_V = RET_HEADS * RET_DV
SSM_D_INNER = 1024
SSM_HEADDIM = 64
SSM_HEADS = 16
SSM_GROUPS = 4
SSM_RATIO = 4
SSM_STATE = 128
SSM_BC = SSM_GROUPS * SSM_STATE
SSM_XBC = SSM_D_INNER + 2 * SSM_BC
HYB_QKV = 2 * RET_Q + RET_V
HYB_REST = 2 * RET_V + SSM_XBC + 128
HYB_MIX = RET_V + SSM_D_INNER
ATT_HEADS = 16
ATT_KV_HEADS = 4
ATT_RATIO = 4
ATT_HD = 64
ATT_Q = ATT_HEADS * ATT_HD
ATT_KV = ATT_KV_HEADS * ATT_HD
ROT_FREQS = ATT_HD // 4
ROPE_BASE = 10000.0
FFN_HIDDEN = 2816

LANES = 128
SUBLANES = 8
VMEM_LIMIT = 56 * 1024 * 1024
NEG_BIG = -1e30
Q_SCALE = ATT_HD ** -0.5 * 1.4426950408889634


def _cparams(sem):
    return pltpu.CompilerParams(dimension_semantics=sem, vmem_limit_bytes=VMEM_LIMIT)


def _norm_mod(x, gain, sc, sh):
    ms = jnp.mean(x * x, axis=-1, keepdims=True)
    return x * lax.rsqrt(ms + EPS) * gain * (1.0 + sc) + sh


def _vec_spec(width, tiles_per_batch, per_batch, grid_rank):
    if grid_rank == 1:
        return pl.BlockSpec((1, 1, width), lambda i: ((i // tiles_per_batch) if per_batch else 0, 0, 0))
    return pl.BlockSpec((1, 1, width), lambda i, j: ((i // tiles_per_batch) if per_batch else 0, 0, 0))


def _mod_body(c_ref, w_ref, b_ref, o_ref):
    cs = jax.nn.silu(c_ref[...])
    o_ref[0] = jnp.dot(cs, w_ref[0], precision=HIGHEST, preferred_element_type=F32) + b_ref[0]


def _modulation(cond, w_mod, b_mod):
    depth, _, n = w_mod.shape
    rows = cond.shape[0]
    tn = n // 4
    return pl.pallas_call(
        _mod_body,
        grid=(depth, n // tn),
        in_specs=[
            pl.BlockSpec((rows, D_MODEL), lambda l, j: (0, 0)),
            pl.BlockSpec((1, D_MODEL, tn), lambda l, j: (l, 0, j)),
            pl.BlockSpec((1, 1, tn), lambda l, j: (l, 0, j)),
        ],
        out_specs=pl.BlockSpec((1, rows, tn), lambda l, j: (l, 0, j)),
        out_shape=jax.ShapeDtypeStruct((depth, rows, n), F32),
        compiler_params=_cparams(("arbitrary", "arbitrary")),
        name="modulation",
    )(cond, w_mod, b_mod.reshape(depth, 1, n))


def _chunks(total, width):
    return [(s, min(width, total - s)) for s in range(0, total, width)]


def _resident(shape):
    return pl.BlockSpec(shape, lambda *_: (0,) * len(shape), pipeline_mode=pl.Buffered(1))


def _norm_mod_halo(tiles_per_seq, tm, x_ref, xn_ref, xp_ref, gain_ref, sc_ref, sh_ref, h_scr):
    gain, sc, sh = gain_ref[...], sc_ref[0], sh_ref[0]
    h_scr[0:tm] = _norm_mod(x_ref[...], gain, sc, sh).astype(BF16)
    pos = pl.program_id(0) % tiles_per_seq
    hn = jnp.where(pos == tiles_per_seq - 1, 0.0, _norm_mod(xn_ref[0], gain, sc, sh))
    hp = jnp.where(pos == 0, 0.0, _norm_mod(xp_ref[0], gain, sc, sh))
    h_scr[tm:tm + 2 * SUBLANES] = jnp.concatenate([hn, hp], axis=0).astype(BF16)


def _dwconv3_rows(u, cw, cb, tm):
    rows = tm + 2 * SUBLANES
    c = pltpu.roll(u, 1, 0) * cw[0:1] + u * cw[1:2] + pltpu.roll(u, rows - 1, 0) * cw[2:3] + cb
    return c[0:tm]


def _halo_specs(m, tm):
    m8 = m // SUBLANES
    r8 = tm // SUBLANES
    return [pl.BlockSpec((1, SUBLANES, D_MODEL), lambda i: (jnp.minimum((i + 1) * r8, m8 - 1), 0, 0)),
            pl.BlockSpec((1, SUBLANES, D_MODEL), lambda i: (jnp.maximum(i * r8 - 1, 0), 0, 0))]


XBC_COL = 2 * RET_V


def _hyb_in_body(tiles_per_seq, tm, tn, x_ref, xn_ref, xp_ref, gain_ref, sc_ref, sh_ref, w_ref, cw_ref, cb_ref,
                 qkv_ref, rest_ref, h_scr):
    _norm_mod_halo(tiles_per_seq, tm, x_ref, xn_ref, xp_ref, gain_ref, sc_ref, sh_ref, h_scr)
    h_all = h_scr[...]
    h = h_scr[0:tm]
    for col, width in _chunks(HYB_QKV, tn):
        qkv_ref[:, col:col + width] = jnp.dot(h, w_ref[:, col:col + width],
                                              preferred_element_type=F32).astype(BF16)
    for col, width in _chunks(HYB_REST, tn):
        wcols = w_ref[:, HYB_QKV + col:HYB_QKV + col + width]
        if XBC_COL <= col < XBC_COL + SSM_XBC:
            u = jnp.dot(h_all, wcols, preferred_element_type=F32)
            cc = col - XBC_COL
            rest_ref[:, col:col + width] = jax.nn.silu(
                _dwconv3_rows(u, cw_ref[:, cc:cc + width], cb_ref[:, cc:cc + width], tm))
        else:
            rest_ref[:, col:col + width] = jnp.dot(h, wcols, preferred_element_type=F32)


def _hyb_in_proj(x, gain, sc, sh, w, conv_w, conv_b, seq_len, tm, tn):
    m = x.shape[0]
    per_batch = sc.shape[0] > 1
    tpb = seq_len // tm
    assert XBC_COL % tn == 0 and SSM_XBC % tn == 0
    x8 = x.reshape(m // SUBLANES, SUBLANES, D_MODEL)
    return pl.pallas_call(
        functools.partial(_hyb_in_body, tpb, tm, tn),
        grid=(m // tm,),
        in_specs=[
            pl.BlockSpec((tm, D_MODEL), lambda i: (i, 0)),
            *_halo_specs(m, tm),
            pl.BlockSpec((1, D_MODEL), lambda i: (0, 0)),
            _vec_spec(D_MODEL, tpb, per_batch, 1),
            _vec_spec(D_MODEL, tpb, per_batch, 1),
            _resident((D_MODEL, HYB_QKV + HYB_REST)),
            _resident((3, SSM_XBC)),
            _resident((1, SSM_XBC)),
        ],
        out_specs=[pl.BlockSpec((tm, HYB_QKV), lambda i: (i, 0)), pl.BlockSpec((tm, HYB_REST), lambda i: (i, 0))],
        out_shape=[jax.ShapeDtypeStruct((m, HYB_QKV), BF16), jax.ShapeDtypeStruct((m, HYB_REST), F32)],
        scratch_shapes=[pltpu.VMEM((tm + 2 * SUBLANES, D_MODEL), BF16)],
        compiler_params=_cparams(("arbitrary",)),
        name="hybrid_in_proj",
    )(x, x8, x8, gain, sc, sh, w, conv_w, conv_b.reshape(1, SSM_XBC))


def _ffn_body(tiles_per_seq, tm, th, x_ref, xn_ref, xp_ref, gain_ref, sc_ref, sh_ref, gate_ref,
              wu_ref, cw_ref, cb_ref, wd_ref, o_ref, h_scr, act_scr):
    _norm_mod_halo(tiles_per_seq, tm, x_ref, xn_ref, xp_ref, gain_ref, sc_ref, sh_ref, h_scr)
    h = h_scr[...]

    def conv_up(col, width):
        u = jnp.dot(h, wu_ref[:, col:col + width], preferred_element_type=F32)
        return _dwconv3_rows(u, cw_ref[:, col:col + width], cb_ref[:, col:col + width], tm)

    for col, width in _chunks(FFN_HIDDEN, th):
        act = jax.nn.silu(conv_up(col, width)) * conv_up(FFN_HIDDEN + col, width)
        act_scr[:, col:col + width] = act.astype(BF16)
    out = jnp.dot(act_scr[...], wd_ref[...], preferred_element_type=F32)
    o_ref[...] = x_ref[...] + gate_ref[0] * out


def _conv_ffn(x, gain, sc, sh, gate, w_up, conv_w, conv_b, w_down, seq_len, tm, th):
    m = x.shape[0]
    per_batch = sc.shape[0] > 1
    tpb = seq_len // tm
    x8 = x.reshape(m // SUBLANES, SUBLANES, D_MODEL)
    conv_b = conv_b.reshape(1, 2 * FFN_HIDDEN)
    vec = _vec_spec(D_MODEL, tpb, per_batch, 1)
    return pl.pallas_call(
        functools.partial(_ffn_body, tpb, tm, th),
        grid=(m // tm,),
        in_specs=[
            pl.BlockSpec((tm, D_MODEL), lambda i: (i, 0)),
            *_halo_specs(m, tm),
            pl.BlockSpec((1, D_MODEL), lambda i: (0, 0)),
            vec, vec, vec,
            _resident((D_MODEL, 2 * FFN_HIDDEN)),
            _resident((3, 2 * FFN_HIDDEN)),
            _resident((1, 2 * FFN_HIDDEN)),
            _resident((FFN_HIDDEN, D_MODEL)),
        ],
        out_specs=pl.BlockSpec((tm, D_MODEL), lambda i: (i, 0)),
        out_shape=jax.ShapeDtypeStruct((m, D_MODEL), F32),
        scratch_shapes=[pltpu.VMEM((tm + 2 * SUBLANES, D_MODEL), BF16), pltpu.VMEM((tm, FFN_HIDDEN), BF16)],
        compiler_params=_cparams(("arbitrary",)),
        name="conv_ffn",
    )(x, x8, x8, gain, sc, sh, gate, w_up, conv_w, conv_b, w_down)


SSM_GW = SSM_RATIO * SSM_HEADDIM


def _bcol(x, h):
    return jnp.broadcast_to(x[:, h:h + 1], (x.shape[0], LANES))


def _pair_sel(left, a, h):
    return jnp.where(left, _bcol(a, h), _bcol(a, h + 1))


def _expand_heads(a, spread_ref):
    hi = a.astype(BF16)
    lo = (a - hi.astype(F32)).astype(BF16)
    return (jnp.dot(hi, spread_ref[...], preferred_element_type=F32)
            + jnp.dot(lo, spread_ref[...], preferred_element_type=F32))


def _ssm_decay_row(left, efull, g):
    return jnp.concatenate([_pair_sel(left, efull, g * SSM_RATIO + 2 * m) for m in range(SSM_RATIO // 2)], axis=1)


def _load_states(s_ret, s_ssm, sret0_ref, sssm0_ref):
    if sret0_ref is None:
        s_ret[...] = jnp.zeros_like(s_ret)
        s_ssm[...] = jnp.zeros_like(s_ssm)
    else:
        s_ret[...] = sret0_ref[0]
        for g in range(SSM_GROUPS):
            s_ssm[g] = jnp.concatenate([sssm0_ref[0, g * SSM_RATIO + r] for r in range(SSM_RATIO)], axis=1)


def _store_states(s_ret, s_ssm, sret_ref, sssm_ref):
    sret_ref[0] = s_ret[...]
    for g in range(SSM_GROUPS):
        for r in range(SSM_RATIO):
            sssm_ref[0, g * SSM_RATIO + r] = s_ssm[g][:, r * SSM_HEADDIM:(r + 1) * SSM_HEADDIM]


def _ssm_log_decay(dt_raw, dtb, alog, lane):
    dt = jax.nn.softplus(dt_raw + dtb)
    return dt, jnp.where(lane < SSM_HEADS, dt * (-jnp.exp(alog)), 0.0)


def _bwd_state_body(has_init, k_ref, v_ref, xs_ref, bm_ref, dt_ref, dtb_ref, alog_ref, spread_ref, logdec_ref,
                    *rest):
    sret0_ref = sssm0_ref = None
    if has_init:
        sret0_ref, sssm0_ref = rest[0], rest[1]
        rest = rest[2:]
    sret_in_ref, sssm_in_ref, sret_fin_ref, sssm_fin_ref, s_ret, s_ssm, wcol = rest
    i = pl.program_id(1)
    n = pl.num_programs(1)
    ii = lax.broadcasted_iota(jnp.int32, (CHUNK, CHUNK), 0)
    jj = lax.broadcasted_iota(jnp.int32, (CHUNK, CHUNK), 1)
    lane = lax.broadcasted_iota(jnp.int32, (1, LANES), 1)
    left = lane < SSM_HEADDIM

    @pl.when(i == 0)
    def _():
        _load_states(s_ret, s_ssm, sret0_ref, sssm0_ref)
        for h in range(RET_HEADS):
            wcol[h] = jnp.exp(ii.astype(F32) * logdec_ref[1, h])

    sret_in_ref[0, 0] = s_ret[...]
    sssm_in_ref[0, 0] = s_ssm[...]

    dt, la = _ssm_log_decay(dt_ref[...], dtb_ref[...], alog_ref[...], lane)
    rc = jnp.dot((jj >= ii).astype(F32), la, precision=HIGHEST, preferred_element_type=F32)

    full = jnp.full((1, LANES), float(CHUNK), F32)
    for h in range(RET_HEADS):
        kh = k_ref[:, h * RET_DK:(h + 1) * RET_DK]
        vw = (v_ref[:, h * RET_DV:(h + 1) * RET_DV].astype(F32) * wcol[h]).astype(BF16)
        upd = lax.dot_general(kh, vw, (((0,), (0,)), ((), ())), preferred_element_type=F32)
        s_ret[h] = s_ret[h] * jnp.exp(full * logdec_ref[1, h]) + upd

    last = rc[0:1]
    wx = _expand_heads(jnp.exp(last - rc) * dt, spread_ref)
    efull = jnp.exp(last)
    for g in range(SSM_GROUPS):
        vw = (xs_ref[:, g * SSM_GW:(g + 1) * SSM_GW] * wx[:, g * SSM_GW:(g + 1) * SSM_GW]).astype(BF16)
        bg = bm_ref[:, g * SSM_STATE:(g + 1) * SSM_STATE].astype(BF16)
        upd = lax.dot_general(bg, vw, (((0,), (0,)), ((), ())), preferred_element_type=F32)
        s_ssm[g] = s_ssm[g] * _ssm_decay_row(left, efull, g) + upd

    @pl.when(i == n - 1)
    def _():
        _store_states(s_ret, s_ssm, sret_fin_ref, sssm_fin_ref)


def _hyb_fwd_body(has_init, qkv_ref, xs_ref, bc_ref, dt_ref, sret_in_ref, sssm_in_ref, dtb_ref, alog_ref, dsk_ref,
                  spread_ref, logdec_ref, *rest):
    sret0_ref = sssm0_ref = None
    if has_init:
        sret0_ref, sssm0_ref = rest[0], rest[1]
        rest = rest[2:]
    y_ref, sret_fin_ref, sssm_fin_ref, s_ret, s_ssm, dcomb, ein0, ein1, wcol = rest
    i = pl.program_id(1)
    n = pl.num_programs(1)
    ii = lax.broadcasted_iota(jnp.int32, (CHUNK, CHUNK), 0)
    jj = lax.broadcasted_iota(jnp.int32, (CHUNK, CHUNK), 1)
    lane = lax.broadcasted_iota(jnp.int32, (1, LANES), 1)
    left = lane < SSM_HEADDIM
    lower = jj <= ii

    @pl.when(i == 0)
    def _():
        _load_states(s_ret, s_ssm, sret0_ref, sssm0_ref)
        fi, fj = ii.astype(F32), jj.astype(F32)
        for h in range(RET_HEADS):
            la0, la1 = logdec_ref[0, h], logdec_ref[1, h]
            dec = jnp.exp(jnp.where(lower, (fi - fj) * la0, (fj - fi) * la1))
            dcomb[h] = jnp.where(ii == jj, 2.0, dec)
            ein0[h] = jnp.exp((fi + 1.0) * la0)
            ein1[h] = jnp.exp((CHUNK - fi) * la1)
            wcol[h] = jnp.exp((CHUNK - 1.0 - fi) * la0)

    def ret_head(h):
        return (qkv_ref[:, h * RET_DK:(h + 1) * RET_DK],
                qkv_ref[:, RET_Q + h * RET_DK:RET_Q + (h + 1) * RET_DK],
                qkv_ref[:, 2 * RET_Q + h * RET_DV:2 * RET_Q + (h + 1) * RET_DV])

    def ssm_group(g):
        return (bc_ref[:, g * SSM_STATE:(g + 1) * SSM_STATE].astype(BF16),
                bc_ref[:, SSM_BC + g * SSM_STATE:SSM_BC + (g + 1) * SSM_STATE].astype(BF16))

    dt_raw = dt_ref[...]
    dt0, la0 = _ssm_log_decay(dt_raw, dtb_ref[0:1], alog_ref[0:1], lane)
    dt1, la1 = _ssm_log_decay(dt_raw, dtb_ref[1:2], alog_ref[1:2], lane)
    c0 = jnp.dot(lower.astype(F32), la0, precision=HIGHEST, preferred_element_type=F32)
    c1 = jnp.dot((jj >= ii).astype(F32), la1, precision=HIGHEST, preferred_element_type=F32)

    ret_scores, ret_inter = [], []
    for h in range(RET_HEADS):
        qh, kh, _ = ret_head(h)
        ret_scores.append(lax.dot_general(qh, kh, (((1,), (1,)), ((), ())), preferred_element_type=F32))
        states = jnp.concatenate([s_ret[h], sret_in_ref[0, 0, h]], axis=1).astype(BF16)
        ret_inter.append(jnp.dot(qh, states, preferred_element_type=F32))
    ssm_scores, ssm_inter = [], []
    for g in range(SSM_GROUPS):
        bg, cg = ssm_group(g)
        ssm_scores.append(lax.dot_general(cg, bg, (((1,), (1,)), ((), ())), preferred_element_type=F32))
        states = jnp.concatenate([s_ssm[g], sssm_in_ref[0, 0, g]], axis=1).astype(BF16)
        ssm_inter.append(jnp.dot(cg, states, preferred_element_type=F32))

    full = jnp.full((1, LANES), float(CHUNK), F32)
    for h in range(RET_HEADS):
        _, kh, vh = ret_head(h)
        y = jnp.dot((ret_scores[h] * dcomb[h]).astype(BF16), vh, preferred_element_type=F32)
        inter = ret_inter[h]
        y_ref[:, h * RET_DV:(h + 1) * RET_DV] = y + inter[:, 0:RET_DV] * ein0[h] + inter[:, RET_DV:] * ein1[h]
        vw = (vh.astype(F32) * wcol[h]).astype(BF16)
        upd = lax.dot_general(kh, vw, (((0,), (0,)), ((), ())), preferred_element_type=F32)
        s_ret[h] = s_ret[h] * jnp.exp(full * logdec_ref[0, h]) + upd

    c0kt, c1kt = (c0 - jnp.log(dt0)).T, (c1 - jnp.log(dt1)).T
    dst = (dt0 + dt1).T
    last0 = c0[CHUNK - 1:CHUNK]
    w0x = _expand_heads(jnp.exp(last0 - c0) * dt0, spread_ref)
    efull0 = jnp.exp(last0)
    diag = jj == ii
    for g in range(SSM_GROUPS):
        bg, _ = ssm_group(g)
        s = ssm_scores[g]
        inter = ssm_inter[g]
        vws = []
        for m in range(SSM_RATIO // 2):
            ha = g * SSM_RATIO + 2 * m
            ps, c0bs, c1bs = [], [], []
            for h in (ha, ha + 1):
                c0b, c1b = _bcol(c0, h), _bcol(c1, h)
                arg = jnp.where(lower, c0b - c0kt[h:h + 1], c1b - c1kt[h:h + 1])
                ps.append((s * jnp.where(diag, dst[h:h + 1], jnp.exp(arg))).astype(BF16))
                c0bs.append(c0b)
                c1bs.append(c1b)
            col = (g * 2 + m) * LANES
            xs_pair = xs_ref[:, col:col + LANES]
            vals = jnp.concatenate([jnp.where(left, xs_pair, 0.0), jnp.where(left, 0.0, xs_pair)], axis=0)
            y = jnp.dot(jnp.concatenate(ps, axis=1), vals.astype(BF16), preferred_element_type=F32)
            y = y + inter[:, m * LANES:(m + 1) * LANES] * jnp.exp(jnp.where(left, c0bs[0], c0bs[1]))
            y = y + (inter[:, SSM_GW + m * LANES:SSM_GW + (m + 1) * LANES]
                     * jnp.exp(jnp.where(left, c1bs[0], c1bs[1])))
            y_ref[:, RET_V + col:RET_V + col + LANES] = y + dsk_ref[:, col:col + LANES] * xs_pair
            vws.append((xs_pair * w0x[:, col:col + LANES]).astype(BF16))
        upd = lax.dot_general(bg, jnp.concatenate(vws, axis=1), (((0,), (0,)), ((), ())),
                              preferred_element_type=F32)
        s_ssm[g] = s_ssm[g] * _ssm_decay_row(left, efull0, g) + upd

    @pl.when(i == n - 1)
    def _():
        _store_states(s_ret, s_ssm, sret_fin_ref, sssm_fin_ref)


def _hybrid_mixer(qkv, rest, dt_bias, a_log, log_decay, d_skip, s_ret0, s_ssm0, batch, seq_len):
    m = qkv.shape[0]
    n = seq_len // CHUNK
    has_init = s_ret0 is not None
    dt_blk = (XBC_COL + SSM_XBC) // LANES
    pad = lambda v: jnp.pad(v.astype(F32), ((0, 0), (0, LANES - v.shape[1])))
    dtb, alog = pad(dt_bias), pad(a_log)
    logdec = log_decay.astype(F32)
    spread = (lax.broadcasted_iota(jnp.int32, (LANES, SSM_D_INNER), 0)
              == lax.broadcasted_iota(jnp.int32, (LANES, SSM_D_INNER), 1) // SSM_HEADDIM).astype(BF16)
    spread_spec = pl.BlockSpec((LANES, SSM_D_INNER), lambda b, i: (0, 0))
    dsk = jnp.repeat((d_skip[0] + d_skip[1]).astype(F32), SSM_HEADDIM).reshape(1, SSM_D_INNER)
    smem = pl.BlockSpec(memory_space=pltpu.SMEM)
    ret_spec = pl.BlockSpec((1, RET_HEADS, RET_DK, RET_DV), lambda b, i: (b, 0, 0, 0))
    ssm_spec = pl.BlockSpec((1, SSM_HEADS, SSM_STATE, SSM_HEADDIM), lambda b, i: (b, 0, 0, 0))
    ret_shape = jax.ShapeDtypeStruct((batch, RET_HEADS, RET_DK, RET_DV), F32)
    ssm_shape = jax.ShapeDtypeStruct((batch, SSM_HEADS, SSM_STATE, SSM_HEADDIM), F32)
    ret_scr = pltpu.VMEM((RET_HEADS, RET_DK, RET_DV), F32)
    ssm_scr = pltpu.VMEM((SSM_GROUPS, SSM_STATE, SSM_GW), F32)
    const_scr = pltpu.VMEM((RET_HEADS, CHUNK, CHUNK), F32)

    def rev(b, i):
        return b * n + n - 1 - i

    in_specs = [
        pl.BlockSpec((CHUNK, RET_Q), lambda b, i: (rev(b, i), 1)),
        pl.BlockSpec((CHUNK, RET_V), lambda b, i: (rev(b, i), 1)),
        pl.BlockSpec((CHUNK, SSM_D_INNER), lambda b, i: (rev(b, i), XBC_COL // SSM_D_INNER)),
        pl.BlockSpec((CHUNK, SSM_BC), lambda b, i: (rev(b, i), (XBC_COL + SSM_D_INNER) // SSM_BC)),
        pl.BlockSpec((CHUNK, LANES), lambda b, i: (rev(b, i), dt_blk)),
        pl.BlockSpec((1, LANES), lambda b, i: (0, 0)),
        pl.BlockSpec((1, LANES), lambda b, i: (0, 0)),
        spread_spec,
        smem,
    ]
    args = [qkv, qkv, rest, rest, rest, dtb[1:2], alog[1:2], spread, logdec]
    if has_init:
        in_specs += [ret_spec, ssm_spec]
        args += [s_ret0[:, 1], s_ssm0[:, 1]]
    sret_in, sssm_in, sret1, sssm1 = pl.pallas_call(
        functools.partial(_bwd_state_body, has_init),
        grid=(batch, n),
        in_specs=in_specs,
        out_specs=[
            pl.BlockSpec((1, 1, RET_HEADS, RET_DK, RET_DV), lambda b, i: (b, n - 1 - i, 0, 0, 0)),
            pl.BlockSpec((1, 1, SSM_GROUPS, SSM_STATE, SSM_GW), lambda b, i: (b, n - 1 - i, 0, 0, 0)),
            ret_spec, ssm_spec,
        ],
        out_shape=[
            jax.ShapeDtypeStruct((batch, n, RET_HEADS, RET_DK, RET_DV), F32),
            jax.ShapeDtypeStruct((batch, n, SSM_GROUPS, SSM_STATE, SSM_GW), F32),
            ret_shape, ssm_shape,
        ],
        scratch_shapes=[ret_scr, ssm_scr, const_scr],
        compiler_params=_cparams(("arbitrary", "arbitrary")),
        name="hybrid_reverse_states",
    )(*args)

    def fwd(b, i):
        return b * n + i

    in_specs = [
        pl.BlockSpec((CHUNK, HYB_QKV), lambda b, i: (fwd(b, i), 0)),
        pl.BlockSpec((CHUNK, SSM_D_INNER), lambda b, i: (fwd(b, i), XBC_COL // SSM_D_INNER)),
        pl.BlockSpec((CHUNK, 2 * SSM_BC), lambda b, i: (fwd(b, i), (XBC_COL + SSM_D_INNER) // (2 * SSM_BC))),
        pl.BlockSpec((CHUNK, LANES), lambda b, i: (fwd(b, i), dt_blk)),
        pl.BlockSpec((1, 1, RET_HEADS, RET_DK, RET_DV), lambda b, i: (b, i, 0, 0, 0)),
        pl.BlockSpec((1, 1, SSM_GROUPS, SSM_STATE, SSM_GW), lambda b, i: (b, i, 0, 0, 0)),
        pl.BlockSpec((2, LANES), lambda b, i: (0, 0)),
        pl.BlockSpec((2, LANES), lambda b, i: (0, 0)),
        pl.BlockSpec((1, SSM_D_INNER), lambda b, i: (0, 0)),
        spread_spec,
        smem,
    ]
    args = [qkv, rest, rest, rest, sret_in, sssm_in, dtb, alog, dsk, spread, logdec]
    if has_init:
        in_specs += [ret_spec, ssm_spec]
        args += [s_ret0[:, 0], s_ssm0[:, 0]]
    y, sret0, sssm0 = pl.pallas_call(
        functools.partial(_hyb_fwd_body, has_init),
        grid=(batch, n),
        in_specs=in_specs,
        out_specs=[pl.BlockSpec((CHUNK, HYB_MIX), lambda b, i: (fwd(b, i), 0)), ret_spec, ssm_spec],
        out_shape=[jax.ShapeDtypeStruct((m, HYB_MIX), F32), ret_shape, ssm_shape],
        scratch_shapes=[ret_scr, ssm_scr, const_scr, const_scr, const_scr, const_scr],
        compiler_params=_cparams(("arbitrary", "arbitrary")),
        name="hybrid_forward_mix",
    )(*args)
    return y, jnp.stack([sret0, sret1], axis=1), jnp.stack([sssm0, sssm1], axis=1)


def _hyb_out_body(y_ref, g_ref, z_ref, x_ref, gate_ref, gn_ref, ng_ref, w_ref, o_ref):
    y = y_ref[...]
    parts = []
    for h in range(RET_HEADS):
        yh = y[:, h * RET_DV:(h + 1) * RET_DV]
        mu = jnp.mean(yh, axis=-1, keepdims=True)
        var = jnp.mean(jnp.square(yh - mu), axis=-1, keepdims=True)
        parts.append((yh - mu) * lax.rsqrt(var + EPS))
    y_ret = jax.nn.silu(g_ref[...]) * (jnp.concatenate(parts, axis=1) * gn_ref[...])
    yz = y[:, RET_V:] * jax.nn.silu(z_ref[...])
    y_ssm = yz * lax.rsqrt(jnp.mean(yz * yz, axis=-1, keepdims=True) + EPS) * ng_ref[...]
    out = jnp.dot(y_ret.astype(BF16), w_ref[0:RET_V], preferred_element_type=F32)
    out = out + jnp.dot(y_ssm.astype(BF16), w_ref[RET_V:HYB_MIX], preferred_element_type=F32)
    o_ref[...] = x_ref[...] + gate_ref[0] * out


def _hyb_out(y, rest, x, gate, gn_gain, norm_gain, w_out, seq_len, tm):
    m = x.shape[0]
    per_batch = gate.shape[0] > 1
    tpb = seq_len // tm
    return pl.pallas_call(
        _hyb_out_body,
        grid=(m // tm,),
        in_specs=[
            pl.BlockSpec((tm, HYB_MIX), lambda i: (i, 0)),
            pl.BlockSpec((tm, RET_V), lambda i: (i, 0)),
            pl.BlockSpec((tm, SSM_D_INNER), lambda i: (i, 1)),
            pl.BlockSpec((tm, D_MODEL), lambda i: (i, 0)),
            _vec_spec(D_MODEL, tpb, per_batch, 1),
            pl.BlockSpec((1, RET_V), lambda i: (0, 0)),
            pl.BlockSpec((1, SSM_D_INNER), lambda i: (0, 0)),
            pl.BlockSpec((HYB_MIX, D_MODEL), lambda i: (0, 0)),
        ],
        out_specs=pl.BlockSpec((tm, D_MODEL), lambda i: (i, 0)),
        out_shape=jax.ShapeDtypeStruct((m, D_MODEL), F32),
        compiler_params=_cparams(("arbitrary",)),
        name="hybrid_out_proj",
    )(y, rest, rest, x, gate, gn_gain, norm_gain, w_out)


def _head_rms(t, seg_ref):
    outs = []
    for c in range(t.shape[1] // LANES):
        tc = t[:, c * LANES:(c + 1) * LANES]
        sq = tc * tc
        hi = sq.astype(BF16)
        lo = (sq - hi.astype(F32)).astype(BF16)
        ssum = (jnp.dot(hi, seg_ref[...], preferred_element_type=F32)
                + jnp.dot(lo, seg_ref[...], preferred_element_type=F32))
        outs.append(tc * lax.rsqrt(ssum * (1.0 / ATT_HD) + EPS))
    return outs


def _rope(tc, cos, sin, lane):
    fwd = pltpu.roll(tc, LANES - ROT_FREQS, 1)
    bwd = pltpu.roll(tc, ROT_FREQS, 1)
    return tc * cos + jnp.where(lane % (2 * ROT_FREQS) < ROT_FREQS, fwd, bwd) * sin


def _qkv_body(has_rope, x_ref, gain_ref, sc_ref, sh_ref, w_ref, seg_ref, qg_ref, kg_ref, *rest):
    if has_rope:
        cos_ref, sin_ref, q_ref, k_ref, v_ref = rest
    else:
        q_ref, k_ref, v_ref, kf_ref, vf_ref = rest
    h = _norm_mod(x_ref[...], gain_ref[...], sc_ref[0], sh_ref[0]).astype(BF16)
    qkv = jnp.dot(h, w_ref[...], preferred_element_type=F32)
    q = _head_rms(qkv[:, 0:ATT_Q], seg_ref)
    k = _head_rms(qkv[:, ATT_Q:ATT_Q + ATT_KV], seg_ref)
    v = qkv[:, ATT_Q + ATT_KV:ATT_Q + 2 * ATT_KV]
    q = [t * qg_ref[...] for t in q]
    k = [t * kg_ref[...] for t in k]
    if has_rope:
        lane = lax.broadcasted_iota(jnp.int32, (1, LANES), 1)
        cos, sin = cos_ref[...], sin_ref[...]
        q = [_rope(t, cos, sin, lane) for t in q]
        kr = [_rope(t, cos, sin, lane) for t in k]
    else:
        kr = k
        kf_ref[...] = jnp.concatenate(k, axis=1)
        vf_ref[...] = v
    for c, t in enumerate(q):
        q_ref[0, c * LANES:(c + 1) * LANES, :] = (t * Q_SCALE).T.astype(BF16)
    for c in range(ATT_KV // LANES):
        kc = kr[c].astype(BF16)
        vc = v[:, c * LANES:(c + 1) * LANES].T.astype(BF16)
        for a in range(LANES // ATT_HD):
            k_ref[0, 2 * c + a] = kc[:, a * ATT_HD:(a + 1) * ATT_HD]
            v_ref[0, 2 * c + a] = vc[a * ATT_HD:(a + 1) * ATT_HD]


def _qkv_proj(x, gain, sc, sh, w_qkv, q_gain, k_gain, rope, seq_len, tm):
    m = x.shape[0]
    per_batch = sc.shape[0] > 1
    tpb = seq_len // tm
    has_rope = rope is not None
    seg = (lax.broadcasted_iota(jnp.int32, (LANES, LANES), 0) // ATT_HD
           == lax.broadcasted_iota(jnp.int32, (LANES, LANES), 1) // ATT_HD).astype(BF16)
    tile2 = lambda v: jnp.tile(v.astype(F32), LANES // ATT_HD).reshape(1, LANES)
    in_specs = [
        pl.BlockSpec((tm, D_MODEL), lambda i: (i, 0)),
        pl.BlockSpec((1, D_MODEL), lambda i: (0, 0)),
        _vec_spec(D_MODEL, tpb, per_batch, 1),
        _vec_spec(D_MODEL, tpb, per_batch, 1),
        pl.BlockSpec((D_MODEL, ATT_Q + 2 * ATT_KV), lambda i: (0, 0)),
        pl.BlockSpec((LANES, LANES), lambda i: (0, 0)),
        pl.BlockSpec((1, LANES), lambda i: (0, 0)),
        pl.BlockSpec((1, LANES), lambda i: (0, 0)),
    ]
    args = [x, gain, sc, sh, w_qkv, seg, tile2(q_gain), tile2(k_gain)]
    batch = m // seq_len
    out_specs = [pl.BlockSpec((1, ATT_Q, tm), lambda i: (i // tpb, 0, i % tpb)),
                 pl.BlockSpec((1, ATT_KV_HEADS, tm, ATT_HD), lambda i: (i // tpb, 0, i % tpb, 0)),
                 pl.BlockSpec((1, ATT_KV_HEADS, ATT_HD, tm), lambda i: (i // tpb, 0, 0, i % tpb))]
    out_shape = [jax.ShapeDtypeStruct((batch, ATT_Q, seq_len), BF16),
                 jax.ShapeDtypeStruct((batch, ATT_KV_HEADS, seq_len, ATT_HD), BF16),
                 jax.ShapeDtypeStruct((batch, ATT_KV_HEADS, ATT_HD, seq_len), BF16)]
    if has_rope:
        in_specs += [pl.BlockSpec((tm, LANES), lambda i: (i % tpb, 0))] * 2
        args += list(rope)
    else:
        out_specs += [pl.BlockSpec((tm, ATT_KV), lambda i: (i, 0))] * 2
        out_shape += [jax.ShapeDtypeStruct((m, ATT_KV), F32)] * 2
    return pl.pallas_call(
        functools.partial(_qkv_body, has_rope),
        grid=(m // tm,),
        in_specs=in_specs,
        out_specs=out_specs,
        out_shape=out_shape,
        compiler_params=_cparams(("arbitrary",)),
        name="attn_qkv_proj",
    )(*args)


def _rope_tables(seq_len):
    rows = seq_len // GRID_W
    row = jnp.repeat(jnp.arange(rows, dtype=F32), GRID_W)
    col = jnp.tile(jnp.arange(GRID_W, dtype=F32), rows)
    inv = ROPE_BASE ** (-jnp.arange(ROT_FREQS, dtype=F32) / ROT_FREQS)
    ar, ac = row[:, None] * inv, col[:, None] * inv
    cos = jnp.concatenate([jnp.cos(ar), jnp.cos(ar), jnp.cos(ac), jnp.cos(ac)], axis=1)
    sin = jnp.concatenate([-jnp.sin(ar), jnp.sin(ar), -jnp.sin(ac), jnp.sin(ac)], axis=1)
    return jnp.tile(cos, (1, LANES // ATT_HD)), jnp.tile(sin, (1, LANES // ATT_HD))


V_ROWS = ATT_HD + 16
ATT_SLOTS = 4


def _attn_body(tq, tk, qt_ref, k_ref, vt_ref, x_ref, gate_ref, wo_ref, o_ref, acc_scr, s_scr, ot_scr):
    nkv = k_ref.shape[2] // tk
    for g in range(ATT_KV_HEADS):
        qts = [qt_ref[0, (g * ATT_RATIO + r) * ATT_HD:(g * ATT_RATIO + r + 1) * ATT_HD, :]
               for r in range(ATT_RATIO)]
        acc_scr[...] = jnp.zeros_like(acc_scr)

        def scores(t, slot):
            kt = k_ref[0, g, pl.ds(pl.multiple_of(t * tk, tk), tk), :]
            for r in range(ATT_RATIO):
                s_scr[slot, r] = jnp.dot(kt, qts[r], preferred_element_type=F32)

        def softmax_pv(t, slot, ms):
            vt = vt_ref[0, g, :, pl.ds(pl.multiple_of(t * tk, tk), tk)]
            new_ms = [jnp.maximum(ms[r], jnp.max(s_scr[slot, r], axis=0, keepdims=True)) for r in range(ATT_RATIO)]
            ps = [jnp.exp2(s_scr[slot, r] - new_ms[r]).astype(BF16) for r in range(ATT_RATIO)]
            for r in range(ATT_RATIO):
                acc_scr[r] = (jnp.exp2(ms[r] - new_ms[r]) * acc_scr[r]
                              + jnp.dot(vt, ps[r], preferred_element_type=F32))
            return tuple(new_ms)

        def two_tiles(t, ms, last):
            slot = t % ATT_SLOTS if isinstance(t, int) else None
            for a in range(2):
                if last is None or t + 2 + a <= last:
                    scores(t + 2 + a, (slot + 2 + a) % ATT_SLOTS)
            for a in range(2):
                if last is None or t + a <= last:
                    ms = softmax_pv(t + a, slot + a, ms)
            return ms

        def quad_step(i, ms):
            t = i * ATT_SLOTS
            scores(t + 2, 2)
            scores(t + 3, 3)
            ms = softmax_pv(t, 0, ms)
            ms = softmax_pv(t + 1, 1, ms)
            scores(t + 4, 0)
            scores(t + 5, 1)
            ms = softmax_pv(t + 2, 2, ms)
            return softmax_pv(t + 3, 3, ms)

        for t in range(min(2, nkv)):
            scores(t, t)
        ms = tuple(jnp.full((1, tq), NEG_BIG, F32) for _ in range(ATT_RATIO))
        quads = max(nkv - 2, 0) // ATT_SLOTS
        ms = lax.fori_loop(0, quads, quad_step, ms)
        for t in range(quads * ATT_SLOTS, nkv, 2):
            ms = two_tiles(t, ms, nkv - 1)
        for r in range(ATT_RATIO):
            hd = g * ATT_RATIO + r
            a = acc_scr[r]
            ot_scr[hd * ATT_HD:(hd + 1) * ATT_HD, :] = (a[0:ATT_HD] / a[ATT_HD:ATT_HD + 1]).astype(BF16)
    out = lax.dot_general(ot_scr[...], wo_ref[...], (((0,), (0,)), ((), ())), preferred_element_type=F32)
    o_ref[0] = x_ref[0] + gate_ref[0] * out


def _attention(qt, k4, vt, x, gate, w_o, batch, seq_len, tq, tk):
    per_batch = gate.shape[0] > 1
    nkeys = k4.shape[2]
    x3 = x.reshape(batch, seq_len, D_MODEL)
    out = pl.pallas_call(
        functools.partial(_attn_body, tq, tk),
        grid=(batch, seq_len // tq),
        in_specs=[
            pl.BlockSpec((1, ATT_Q, tq), lambda b, i: (b, 0, i)),
            pl.BlockSpec((1, ATT_KV_HEADS, nkeys, ATT_HD), lambda b, i: (b, 0, 0, 0)),
            pl.BlockSpec((1, ATT_KV_HEADS, V_ROWS, nkeys), lambda b, i: (b, 0, 0, 0)),
            pl.BlockSpec((1, tq, D_MODEL), lambda b, i: (b, i, 0)),
            pl.BlockSpec((1, 1, D_MODEL), lambda b, i: (b if per_batch else 0, 0, 0)),
            pl.BlockSpec((ATT_Q, D_MODEL), lambda b, i: (0, 0)),
        ],
        out_specs=pl.BlockSpec((1, tq, D_MODEL), lambda b, i: (b, i, 0)),
        out_shape=jax.ShapeDtypeStruct((batch, seq_len, D_MODEL), F32),
        scratch_shapes=[pltpu.VMEM((ATT_RATIO, V_ROWS, tq), F32), pltpu.VMEM((ATT_SLOTS, ATT_RATIO, tk, tq), F32),
                        pltpu.VMEM((ATT_Q, tq), BF16)],
        compiler_params=_cparams(("arbitrary", "arbitrary")),
        name="attention",
    )(qt, k4, vt, x3, gate, w_o)
    return out.reshape(batch * seq_len, D_MODEL)


def _keys_values(k4, vt, cache_k, cache_v):
    if cache_k is not None:
        k4 = jnp.concatenate([jnp.swapaxes(cache_k, 1, 2).astype(BF16), k4], axis=2)
        vt = jnp.concatenate([jnp.transpose(cache_v, (0, 2, 3, 1)).astype(BF16), vt], axis=3)
    ones = jnp.ones(vt.shape[:2] + (V_ROWS - ATT_HD, vt.shape[3]), vt.dtype)
    return k4, jnp.concatenate([vt, ones], axis=2)


def _trunk(x, mods, caches, P, batch, seq_len, tm, tq, tk):
    rope = None if caches is None else _rope_tables(seq_len)
    outs = {}
    sh1, sc1, g1, sh2, sc2, g2 = mods[0]
    qkv, rest = _hyb_in_proj(x, P['norm_mix'][0:1], sc1, sh1, P['hyb_w_in'], P['ssm_conv_w'][0],
                             P['ssm_conv_b'][0], seq_len, tm, 512)
    s_ret0 = None if caches is None else caches[0][:, 0]
    s_ssm0 = None if caches is None else caches[1][:, 0]
    y, s_ret, s_ssm = _hybrid_mixer(qkv, rest, P['ssm_dt_bias'][0], P['ssm_a_log'][0], P['ret_log_decay'][0],
                                    P['ssm_d'][0], s_ret0, s_ssm0, batch, seq_len)
    outs['ret'] = s_ret[:, None]
    outs['ssm'] = s_ssm[:, None]
    x = _hyb_out(y, rest, x, g1, P['ret_gn_gain'][0:1], P['ssm_norm_gain'][0:1], P['hyb_w_out'], seq_len, tm)
    x = _conv_ffn(x, P['norm_ffn'][0:1], sc2, sh2, g2, P['ffn_w_up'][0], P['ffn_conv_w'][0], P['ffn_conv_b'][0],
                  P['ffn_w_down'][0], seq_len, tm, 512)
    sh1, sc1, g1, sh2, sc2, g2 = mods[1]
    res = _qkv_proj(x, P['norm_mix'][1:2], sc1, sh1, P['attn_w_qkv'], P['attn_q_gain'][0], P['attn_k_gain'][0],
                    rope, seq_len, tm)
    if caches is None:
        qt, k4, vt, kf, vf = res
        outs['k'] = kf.reshape(batch, 1, seq_len, ATT_KV_HEADS, ATT_HD)
        outs['v'] = vf.reshape(batch, 1, seq_len, ATT_KV_HEADS, ATT_HD)
        k4, vt = _keys_values(k4, vt, None, None)
    else:
        qt, k4, vt = res
        k4, vt = _keys_values(k4, vt, caches[2][:, 0], caches[3][:, 0])
    x = _attention(qt, k4, vt, x, g1, P['attn_w_o'], batch, seq_len, tq, tk)
    x = _conv_ffn(x, P['norm_ffn'][1:2], sc2, sh2, g2, P['ffn_w_up'][1], P['ffn_conv_w'][1], P['ffn_conv_b'][1],
                  P['ffn_w_down'][1], seq_len, tm, 512)
    return x, outs


def kernel(x_prompt, x_sample, state_ret, state_ssm, cache_attn_k, cache_attn_v, c, c_ctx, w_mod, b_mod, norm_mix,
           norm_ffn, ffn_w_up, ffn_conv_w, ffn_conv_b, ffn_w_down, hyb_w_in, hyb_w_out, ret_log_decay, ret_gn_gain,
           ssm_conv_w, ssm_conv_b, ssm_a_log, ssm_dt_bias, ssm_d, ssm_norm_gain, attn_w_qkv, attn_q_gain,
           attn_k_gain, attn_w_o):
    batch, seq, _ = x_prompt.shape
    dec_batch, dec_seq, _ = x_sample.shape
    depth = w_mod.shape[0]

    rows = -(-(dec_batch + 1) // SUBLANES) * SUBLANES
    cond = jnp.concatenate([c, c_ctx[None, :], jnp.zeros((rows - dec_batch - 1, D_MODEL), F32)], axis=0)
    mod = _modulation(cond, w_mod, b_mod).reshape(depth, rows, 6, 1, D_MODEL)
    mods_sample = [[mod[l, 0:dec_batch, t] for t in range(6)] for l in range(depth)]
    mods_prompt = [[mod[l, dec_batch:dec_batch + 1, t] for t in range(6)] for l in range(depth)]

    w_in = hyb_w_in[0]
    kscale = jnp.concatenate([jnp.ones((RET_Q,), F32), jnp.full((RET_Q,), RET_DK ** -0.5, F32),
                              jnp.ones((w_in.shape[1] - 2 * RET_Q,), F32)])
    P = {
        'norm_mix': norm_mix, 'norm_ffn': norm_ffn,
        'hyb_w_in': jnp.pad(w_in * kscale, ((0, 0), (0, HYB_QKV + HYB_REST - w_in.shape[1]))).astype(BF16),
        'hyb_w_out': hyb_w_out[0].astype(BF16),
        'ret_log_decay': ret_log_decay, 'ret_gn_gain': ret_gn_gain,
        'ssm_conv_w': ssm_conv_w, 'ssm_conv_b': ssm_conv_b, 'ssm_a_log': ssm_a_log, 'ssm_dt_bias': ssm_dt_bias,
        'ssm_d': ssm_d, 'ssm_norm_gain': ssm_norm_gain,
        'attn_w_qkv': attn_w_qkv[0].astype(BF16), 'attn_q_gain': attn_q_gain, 'attn_k_gain': attn_k_gain,
        'attn_w_o': attn_w_o[0].astype(BF16),
        'ffn_w_up': ffn_w_up.astype(BF16), 'ffn_conv_w': ffn_conv_w, 'ffn_conv_b': ffn_conv_b,
        'ffn_w_down': ffn_w_down.astype(BF16),
    }

    y_prompt, outs = _trunk(x_prompt.reshape(batch * seq, D_MODEL), mods_prompt, None, P, batch, seq,
                            tm=min(256, seq), tq=min(256, seq), tk=min(256, seq))
    caches = (state_ret, state_ssm, cache_attn_k, cache_attn_v)
    y_sample, _ = _trunk(x_sample.reshape(dec_batch * dec_seq, D_MODEL), mods_sample, caches, P, dec_batch, dec_seq,
                         tm=512, tq=256, tk=256)
    return (y_prompt.reshape(batch, seq, D_MODEL), y_sample.reshape(dec_batch, dec_seq, D_MODEL),
            outs['ret'], outs['ssm'], outs['k'], outs['v'])
```

```python
import functools

import jax
import jax.numpy as jnp
from jax import lax
from jax.experimental import pallas as pl
from jax.experimental.pallas import tpu as pltpu

F32 = jnp.float32
BF16 = jnp.bfloat16
HIGHEST = lax.Precision.HIGHEST

D_MODEL = 1024
EPS = 1e-6
GRID_W = 64
CHUNK = 128
RET_HEADS = 8
RET_DK = 64
RET_DV = 128
RET_Q = RET_HEADS * RET_DK
RET_V = RET_HEADS * RET_DV
SSM_D_INNER = 1024
SSM_HEADDIM = 64
SSM_HEADS = 16
SSM_GROUPS = 4
SSM_RATIO = 4
SSM_STATE = 128
SSM_BC = SSM_GROUPS * SSM_STATE
SSM_XBC = SSM_D_INNER + 2 * SSM_BC
HYB_QKV = 2 * RET_Q + RET_V
HYB_BF = HYB_QKV + 2 * SSM_BC
XS_COL = 2 * RET_V
DT_COL = XS_COL + SSM_D_INNER
HYB_REST = DT_COL + 128
HYB_MIX = RET_V + SSM_D_INNER
ATT_HEADS = 16
ATT_KV_HEADS = 4
ATT_RATIO = 4
ATT_HD = 64
ATT_Q = ATT_HEADS * ATT_HD
ATT_KV = ATT_KV_HEADS * ATT_HD
ROT_FREQS = ATT_HD // 4
ROPE_BASE = 10000.0
FFN_HIDDEN = 2816

LANES = 128
SUBLANES = 8
MXU_COLS = 256
VMEM_LIMIT = 56 * 1024 * 1024
NEG_BIG = -1e30
Q_SCALE = ATT_HD ** -0.5 * 1.4426950408889634


def _cparams(sem):
    return pltpu.CompilerParams(dimension_semantics=sem, vmem_limit_bytes=VMEM_LIMIT)


def _norm_mod(x, gain, sc, sh):
    ms = jnp.mean(x * x, axis=-1, keepdims=True)
    return x * lax.rsqrt(ms + EPS) * gain * (1.0 + sc) + sh


def _vec_spec(width, tiles_per_batch, per_batch, grid_rank):
    if grid_rank == 1:
        return pl.BlockSpec((1, 1, width), lambda i: ((i // tiles_per_batch) if per_batch else 0, 0, 0))
    return pl.BlockSpec((1, 1, width), lambda i, j: ((i // tiles_per_batch) if per_batch else 0, 0, 0))


def _mod_body(c_ref, w_ref, b_ref, o_ref):
    cs = jax.nn.silu(c_ref[...])
    o_ref[0] = jnp.dot(cs, w_ref[0], precision=HIGHEST, preferred_element_type=F32) + b_ref[0]


def _modulation(cond, w_mod, b_mod):
    depth, _, n = w_mod.shape
    rows = cond.shape[0]
    tn = n // 4
    return pl.pallas_call(
        _mod_body,
        grid=(depth, n // tn),
        in_specs=[
            pl.BlockSpec((rows, D_MODEL), lambda l, j: (0, 0)),
            pl.BlockSpec((1, D_MODEL, tn), lambda l, j: (l, 0, j)),
            pl.BlockSpec((1, 1, tn), lambda l, j: (l, 0, j)),
        ],
        out_specs=pl.BlockSpec((1, rows, tn), lambda l, j: (l, 0, j)),
        out_shape=jax.ShapeDtypeStruct((depth, rows, n), F32),
        compiler_params=_cparams(("arbitrary", "arbitrary")),
        name="modulation",
    )(cond, w_mod, b_mod.reshape(depth, 1, n))


def _chunks(total, width):
    return [(s, min(width, total - s)) for s in range(0, total, width)]


def _resident(shape):
    return pl.BlockSpec(shape, lambda *_: (0,) * len(shape), pipeline_mode=pl.Buffered(1))


def _norm_mod_halo(tiles_per_seq, tm, x_ref, xn_ref, xp_ref, gain_ref, sc_ref, sh_ref, h_scr):
    gain, sc, sh = gain_ref[...], sc_ref[0], sh_ref[0]
    h_scr[0:tm] = _norm_mod(x_ref[...], gain, sc, sh).astype(BF16)
    pos = pl.program_id(0) % tiles_per_seq
    hn = jnp.where(pos == tiles_per_seq - 1, 0.0, _norm_mod(xn_ref[0], gain, sc, sh))
    hp = jnp.where(pos == 0, 0.0, _norm_mod(xp_ref[0], gain, sc, sh))
    h_scr[tm:tm + 2 * SUBLANES] = jnp.concatenate([hn, hp], axis=0).astype(BF16)


def _dwconv3_rows(u, cw, cb, tm):
    rows = tm + 2 * SUBLANES
    c = pltpu.roll(u, 1, 0) * cw[0:1] + u * cw[1:2] + pltpu.roll(u, rows - 1, 0) * cw[2:3] + cb
    return c[0:tm]


def _halo_specs(m, tm):
    m8 = m // SUBLANES
    r8 = tm // SUBLANES
    return [pl.BlockSpec((1, SUBLANES, D_MODEL), lambda i: (jnp.minimum((i + 1) * r8, m8 - 1), 0, 0)),
            pl.BlockSpec((1, SUBLANES, D_MODEL), lambda i: (jnp.maximum(i * r8 - 1, 0), 0, 0))]


def _hyb_in_body(tiles_per_seq, tm, tn, x_ref, xn_ref, xp_ref, gain_ref, sc_ref, sh_ref, w_ref, cw_ref, cb_ref,
                 bf_ref, rest_ref, h_scr):
    _norm_mod_halo(tiles_per_seq, tm, x_ref, xn_ref, xp_ref, gain_ref, sc_ref, sh_ref, h_scr)
    h_all = h_scr[...]
    h = h_scr[0:tm]

    def project(wcol, width, conv_col):
        wcols = w_ref[:, wcol:wcol + width]
        if conv_col is None:
            return jnp.dot(h, wcols, preferred_element_type=F32)
        u = jnp.dot(h_all, wcols, preferred_element_type=F32)
        return jax.nn.silu(_dwconv3_rows(u, cw_ref[:, conv_col:conv_col + width],
                                         cb_ref[:, conv_col:conv_col + width], tm))

    for col, width in _chunks(HYB_BF, tn):
        conv_col = col - HYB_QKV if col >= HYB_QKV else None
        bf_ref[:, col:col + width] = project(col, width, conv_col).astype(BF16)
    for col, width in _chunks(HYB_REST, tn):
        conv_col = 2 * SSM_BC + col - XS_COL if XS_COL <= col < DT_COL else None
        rest_ref[:, col:col + width] = project(HYB_BF + col, width, conv_col)


def _hyb_in_proj(x, gain, sc, sh, w, conv_w, conv_b, seq_len, tm, tn):
    m = x.shape[0]
    per_batch = sc.shape[0] > 1
    tpb = seq_len // tm
    assert HYB_QKV % tn == 0 and XS_COL % tn == 0 and DT_COL % tn == 0
    x8 = x.reshape(m // SUBLANES, SUBLANES, D_MODEL)
    return pl.pallas_call(
        functools.partial(_hyb_in_body, tpb, tm, tn),
        grid=(m // tm,),
        in_specs=[
            pl.BlockSpec((tm, D_MODEL), lambda i: (i, 0)),
            *_halo_specs(m, tm),
            pl.BlockSpec((1, D_MODEL), lambda i: (0, 0)),
            _vec_spec(D_MODEL, tpb, per_batch, 1),
            _vec_spec(D_MODEL, tpb, per_batch, 1),
            _resident((D_MODEL, HYB_BF + HYB_REST)),
            _resident((3, SSM_XBC)),
            _resident((1, SSM_XBC)),
        ],
        out_specs=[pl.BlockSpec((tm, HYB_BF), lambda i: (i, 0)), pl.BlockSpec((tm, HYB_REST), lambda i: (i, 0))],
        out_shape=[jax.ShapeDtypeStruct((m, HYB_BF), BF16), jax.ShapeDtypeStruct((m, HYB_REST), F32)],
        scratch_shapes=[pltpu.VMEM((tm + 2 * SUBLANES, D_MODEL), BF16)],
        compiler_params=_cparams(("arbitrary",)),
        name="hybrid_in_proj",
    )(x, x8, x8, gain, sc, sh, w, conv_w, conv_b.reshape(1, SSM_XBC))


def _ffn_body(tiles_per_seq, tm, th, x_ref, xn_ref, xp_ref, gain_ref, sc_ref, sh_ref, gate_ref,
              wu_ref, cw_ref, cb_ref, wd_ref, o_ref, h_scr, act_scr):
    _norm_mod_halo(tiles_per_seq, tm, x_ref, xn_ref, xp_ref, gain_ref, sc_ref, sh_ref, h_scr)
    h = h_scr[...]

    def conv_up(col, width):
        u = jnp.dot(h, wu_ref[:, col:col + width], preferred_element_type=F32)
        return _dwconv3_rows(u, cw_ref[:, col:col + width], cb_ref[:, col:col + width], tm)

    for col, width in _chunks(FFN_HIDDEN, th):
        act = jax.nn.silu(conv_up(col, width)) * conv_up(FFN_HIDDEN + col, width)
        act_scr[:, col:col + width] = act.astype(BF16)
    out = jnp.dot(act_scr[...], wd_ref[...], preferred_element_type=F32)
    o_ref[...] = x_ref[...] + gate_ref[0] * out


def _conv_ffn(x, gain, sc, sh, gate, w_up, conv_w, conv_b, w_down, seq_len, tm, th):
    m = x.shape[0]
    per_batch = sc.shape[0] > 1
    tpb = seq_len // tm
    x8 = x.reshape(m // SUBLANES, SUBLANES, D_MODEL)
    conv_b = conv_b.reshape(1, 2 * FFN_HIDDEN)
    vec = _vec_spec(D_MODEL, tpb, per_batch, 1)
    return pl.pallas_call(
        functools.partial(_ffn_body, tpb, tm, th),
        grid=(m // tm,),
        in_specs=[
            pl.BlockSpec((tm, D_MODEL), lambda i: (i, 0)),
            *_halo_specs(m, tm),
            pl.BlockSpec((1, D_MODEL), lambda i: (0, 0)),
            vec, vec, vec,
            _resident((D_MODEL, 2 * FFN_HIDDEN)),
            _resident((3, 2 * FFN_HIDDEN)),
            _resident((1, 2 * FFN_HIDDEN)),
            _resident((FFN_HIDDEN, D_MODEL)),
        ],
        out_specs=pl.BlockSpec((tm, D_MODEL), lambda i: (i, 0)),
        out_shape=jax.ShapeDtypeStruct((m, D_MODEL), F32),
        scratch_shapes=[pltpu.VMEM((tm + 2 * SUBLANES, D_MODEL), BF16), pltpu.VMEM((tm, FFN_HIDDEN), BF16)],
        compiler_params=_cparams(("arbitrary",)),
        name="conv_ffn",
    )(x, x8, x8, gain, sc, sh, gate, w_up, conv_w, conv_b, w_down)


SSM_GW = SSM_RATIO * SSM_HEADDIM


def _bcol(x, h):
    return jnp.broadcast_to(x[:, h:h + 1], (x.shape[0], LANES))


def _pair_sel(left, a, h):
    return jnp.where(left, _bcol(a, h), _bcol(a, h + 1))


def _expand_heads(a, spread_ref):
    hi = a.astype(BF16)
    lo = (a - hi.astype(F32)).astype(BF16)
    return (jnp.dot(hi, spread_ref[...], preferred_element_type=F32)
            + jnp.dot(lo, spread_ref[...], preferred_element_type=F32))


def _ssm_decay_row(left, efull, g):
    return jnp.concatenate([_pair_sel(left, efull, g * SSM_RATIO + 2 * m) for m in range(SSM_RATIO // 2)], axis=1)


def _load_states(s_ret, s_ssm, sret0_ref, sssm0_ref):
    if sret0_ref is None:
        s_ret[...] = jnp.zeros_like(s_ret)
        s_ssm[...] = jnp.zeros_like(s_ssm)
    else:
        s_ret[...] = sret0_ref[0]
        for g in range(SSM_GROUPS):
            s_ssm[g] = jnp.concatenate([sssm0_ref[0, g * SSM_RATIO + r] for r in range(SSM_RATIO)], axis=1)


def _store_states(s_ret, s_ssm, sret_ref, sssm_ref):
    sret_ref[0] = s_ret[...]
    for g in range(SSM_GROUPS):
        for r in range(SSM_RATIO):
            sssm_ref[0, g * SSM_RATIO + r] = s_ssm[g][:, r * SSM_HEADDIM:(r + 1) * SSM_HEADDIM]


def _ssm_log_decay(dt_raw, dtb, alog, lane):
    dt = jax.nn.softplus(dt_raw + dtb)
    return dt, jnp.where(lane < SSM_HEADS, dt * (-jnp.exp(alog)), 0.0)


def _bwd_state_body(has_init, k_ref, v_ref, xs_ref, bm_ref, dt_ref, dtb_ref, alog_ref, spread_ref, logdec_ref,
                    *rest):
    sret0_ref = sssm0_ref = None
    if has_init:
        sret0_ref, sssm0_ref = rest[0], rest[1]
        rest = rest[2:]
    sret_in_ref, sssm_in_ref, sret_fin_ref, sssm_fin_ref, s_ret, s_ssm, wcol = rest
    i = pl.program_id(1)
    n = pl.num_programs(1)
    ii = lax.broadcasted_iota(jnp.int32, (CHUNK, CHUNK), 0)
    jj = lax.broadcasted_iota(jnp.int32, (CHUNK, CHUNK), 1)
    lane = lax.broadcasted_iota(jnp.int32, (1, LANES), 1)
    left = lane < SSM_HEADDIM

    @pl.when(i == 0)
    def _():
        _load_states(s_ret, s_ssm, sret0_ref, sssm0_ref)
        for h in range(RET_HEADS):
            wcol[h] = jnp.exp(ii.astype(F32) * logdec_ref[1, h])

    sret_in_ref[0, 0] = s_ret[...].astype(BF16)
    sssm_in_ref[0, 0] = s_ssm[...].astype(BF16)

    dt, la = _ssm_log_decay(dt_ref[...], dtb_ref[...], alog_ref[...], lane)
    rc = jnp.dot((jj >= ii).astype(F32), la, precision=HIGHEST, preferred_element_type=F32)

    full = jnp.full((1, LANES), float(CHUNK), F32)
    for h in range(RET_HEADS):
        kh = k_ref[:, h * RET_DK:(h + 1) * RET_DK]
        vw = (v_ref[:, h * RET_DV:(h + 1) * RET_DV].astype(F32) * wcol[h]).astype(BF16)
        upd = lax.dot_general(kh, vw, (((0,), (0,)), ((), ())), preferred_element_type=F32)
        s_ret[h] = s_ret[h] * jnp.exp(full * logdec_ref[1, h]) + upd

    last = rc[0:1]
    wx = _expand_heads(jnp.exp(last - rc) * dt, spread_ref)
    efull = jnp.exp(last)
    for g in range(SSM_GROUPS):
        vw = (xs_ref[:, g * SSM_GW:(g + 1) * SSM_GW] * wx[:, g * SSM_GW:(g + 1) * SSM_GW]).astype(BF16)
        bg = bm_ref[:, g * SSM_STATE:(g + 1) * SSM_STATE]
        upd = lax.dot_general(bg, vw, (((0,), (0,)), ((), ())), preferred_element_type=F32)
        s_ssm[g] = s_ssm[g] * _ssm_decay_row(left, efull, g) + upd

    @pl.when(i == n - 1)
    def _():
        _store_states(s_ret, s_ssm, sret_fin_ref, sssm_fin_ref)


def _hyb_fwd_body(has_init, qkv_ref, xs_ref, bc_ref, dt_ref, sret_in_ref, sssm_in_ref, dtb_ref, alog_ref, dsk_ref,
                  spread_ref, logdec_ref, *rest):
    sret0_ref = sssm0_ref = None
    if has_init:
        sret0_ref, sssm0_ref = rest[0], rest[1]
        rest = rest[2:]
    y_ref, sret_fin_ref, sssm_fin_ref, s_ret, s_ssm, dcomb, ein0, ein1, wcol = rest
    i = pl.program_id(1)
    n = pl.num_programs(1)
    ii = lax.broadcasted_iota(jnp.int32, (CHUNK, CHUNK), 0)
    jj = lax.broadcasted_iota(jnp.int32, (CHUNK, CHUNK), 1)
    lane = lax.broadcasted_iota(jnp.int32, (1, LANES), 1)
    left = lane < SSM_HEADDIM
    lower = jj <= ii

    @pl.when(i == 0)
    def _():
        _load_states(s_ret, s_ssm, sret0_ref, sssm0_ref)
        fi, fj = ii.astype(F32), jj.astype(F32)
        for h in range(RET_HEADS):
            la0, la1 = logdec_ref[0, h], logdec_ref[1, h]
            dec = jnp.exp(jnp.where(lower, (fi - fj) * la0, (fj - fi) * la1))
            dcomb[h] = jnp.where(ii == jj, 2.0, dec)
            ein0[h] = jnp.exp((fi + 1.0) * la0)
            ein1[h] = jnp.exp((CHUNK - fi) * la1)
            wcol[h] = jnp.exp((CHUNK - 1.0 - fi) * la0)

    def ret_head(h):
        return (qkv_ref[:, h * RET_DK:(h + 1) * RET_DK],
                qkv_ref[:, RET_Q + h * RET_DK:RET_Q + (h + 1) * RET_DK],
                qkv_ref[:, 2 * RET_Q + h * RET_DV:2 * RET_Q + (h + 1) * RET_DV])

    def ssm_group(g):
        return (bc_ref[:, g * SSM_STATE:(g + 1) * SSM_STATE],
                bc_ref[:, SSM_BC + g * SSM_STATE:SSM_BC + (g + 1) * SSM_STATE])

    dt_raw = dt_ref[...]
    dt0, la0 = _ssm_log_decay(dt_raw, dtb_ref[0:1], alog_ref[0:1], lane)
    dt1, la1 = _ssm_log_decay(dt_raw, dtb_ref[1:2], alog_ref[1:2], lane)
    c0 = jnp.dot(lower.astype(F32), la0, precision=HIGHEST, preferred_element_type=F32)
    c1 = jnp.dot((jj >= ii).astype(F32), la1, precision=HIGHEST, preferred_element_type=F32)

    ret_scores, ret_inter = [], []
    for h in range(RET_HEADS):
        qh, kh, _ = ret_head(h)
        ret_scores.append(lax.dot_general(qh, kh, (((1,), (1,)), ((), ())), preferred_element_type=F32))
        states = jnp.concatenate([s_ret[h].astype(BF16), sret_in_ref[0, 0, h]], axis=1)
        ret_inter.append(jnp.dot(qh, states, preferred_element_type=F32))
    ssm_scores, ssm_inter = [], []
    for g in range(SSM_GROUPS):
        bg, cg = ssm_group(g)
        ssm_scores.append(lax.dot_general(cg, bg, (((1,), (1,)), ((), ())), preferred_element_type=F32))
        states = jnp.concatenate([s_ssm[g].astype(BF16), sssm_in_ref[0, 0, g]], axis=1)
        ssm_inter.append(jnp.dot(cg, states, preferred_element_type=F32))

    full = jnp.full((1, LANES), float(CHUNK), F32)
    for h in range(RET_HEADS):
        _, kh, vh = ret_head(h)
        y = jnp.dot((ret_scores[h] * dcomb[h]).astype(BF16), vh, preferred_element_type=F32)
        inter = ret_inter[h]
        y_ref[:, h * RET_DV:(h + 1) * RET_DV] = y + inter[:, 0:RET_DV] * ein0[h] + inter[:, RET_DV:] * ein1[h]
        vw = (vh.astype(F32) * wcol[h]).astype(BF16)
        upd = lax.dot_general(kh, vw, (((0,), (0,)), ((), ())), preferred_element_type=F32)
        s_ret[h] = s_ret[h] * jnp.exp(full * logdec_ref[0, h]) + upd

    c0kt, c1kt = (c0 - jnp.log(dt0)).T, (c1 - jnp.log(dt1)).T
    dst = (dt0 + dt1).T
    last0 = c0[CHUNK - 1:CHUNK]
    w0x = _expand_heads(jnp.exp(last0 - c0) * dt0, spread_ref)
    efull0 = jnp.exp(last0)
    diag = jj == ii
    for g in range(SSM_GROUPS):
        bg, _ = ssm_group(g)
        s = ssm_scores[g]
        inter = ssm_inter[g]
        vws = []
        for m in range(SSM_RATIO // 2):
            ha = g * SSM_RATIO + 2 * m
            ps, c0bs, c1bs = [], [], []
            for h in (ha, ha + 1):
                c0b, c1b = _bcol(c0, h), _bcol(c1, h)
                arg = jnp.where(lower, c0b - c0kt[h:h + 1], c1b - c1kt[h:h + 1])
                ps.append((s * jnp.where(diag, dst[h:h + 1], jnp.exp(arg))).astype(BF16))
                c0bs.append(c0b)
                c1bs.append(c1b)
            col = (g * 2 + m) * LANES
            xs_pair = xs_ref[:, col:col + LANES]
            vals = jnp.concatenate([jnp.where(left, xs_pair, 0.0), jnp.where(left, 0.0, xs_pair)], axis=0)
            y = jnp.dot(jnp.concatenate(ps, axis=1), vals.astype(BF16), preferred_element_type=F32)
            y = y + inter[:, m * LANES:(m + 1) * LANES] * jnp.exp(jnp.where(left, c0bs[0], c0bs[1]))
            y = y + (inter[:, SSM_GW + m * LANES:SSM_GW + (m + 1) * LANES]
                     * jnp.exp(jnp.where(left, c1bs[0], c1bs[1])))
            y_ref[:, RET_V + col:RET_V + col + LANES] = y + dsk_ref[:, col:col + LANES] * xs_pair
            vws.append((xs_pair * w0x[:, col:col + LANES]).astype(BF16))
        upd = lax.dot_general(bg, jnp.concatenate(vws, axis=1), (((0,), (0,)), ((), ())),
                              preferred_element_type=F32)
        s_ssm[g] = s_ssm[g] * _ssm_decay_row(left, efull0, g) + upd

    @pl.when(i == n - 1)
    def _():
        _store_states(s_ret, s_ssm, sret_fin_ref, sssm_fin_ref)


def _hybrid_mixer(qkv, rest, dt_bias, a_log, log_decay, d_skip, s_ret0, s_ssm0, batch, seq_len):
    m = qkv.shape[0]
    n = seq_len // CHUNK
    has_init = s_ret0 is not None
    dt_blk = DT_COL // LANES
    pad = lambda v: jnp.pad(v.astype(F32), ((0, 0), (0, LANES - v.shape[1])))
    dtb, alog = pad(dt_bias), pad(a_log)
    logdec = log_decay.astype(F32)
    spread = (lax.broadcasted_iota(jnp.int32, (LANES, SSM_D_INNER), 0)
              == lax.broadcasted_iota(jnp.int32, (LANES, SSM_D_INNER), 1) // SSM_HEADDIM).astype(BF16)
    spread_spec = pl.BlockSpec((LANES, SSM_D_INNER), lambda b, i: (0, 0))
    dsk = jnp.repeat((d_skip[0] + d_skip[1]).astype(F32), SSM_HEADDIM).reshape(1, SSM_D_INNER)
    smem = pl.BlockSpec(memory_space=pltpu.SMEM)
    ret_spec = pl.BlockSpec((1, RET_HEADS, RET_DK, RET_DV), lambda b, i: (b, 0, 0, 0))
    ssm_spec = pl.BlockSpec((1, SSM_HEADS, SSM_STATE, SSM_HEADDIM), lambda b, i: (b, 0, 0, 0))
    ret_shape = jax.ShapeDtypeStruct((batch, RET_HEADS, RET_DK, RET_DV), F32)
    ssm_shape = jax.ShapeDtypeStruct((batch, SSM_HEADS, SSM_STATE, SSM_HEADDIM), F32)
    ret_scr = pltpu.VMEM((RET_HEADS, RET_DK, RET_DV), F32)
    ssm_scr = pltpu.VMEM((SSM_GROUPS, SSM_STATE, SSM_GW), F32)
    const_scr = pltpu.VMEM((RET_HEADS, CHUNK, CHUNK), F32)

    def rev(b, i):
        return b * n + n - 1 - i

    in_specs = [
        pl.BlockSpec((CHUNK, RET_Q), lambda b, i: (rev(b, i), 1)),
        pl.BlockSpec((CHUNK, RET_V), lambda b, i: (rev(b, i), 1)),
        pl.BlockSpec((CHUNK, SSM_D_INNER), lambda b, i: (rev(b, i), XS_COL // SSM_D_INNER)),
        pl.BlockSpec((CHUNK, SSM_BC), lambda b, i: (rev(b, i), HYB_QKV // SSM_BC)),
        pl.BlockSpec((CHUNK, LANES), lambda b, i: (rev(b, i), dt_blk)),
        pl.BlockSpec((1, LANES), lambda b, i: (0, 0)),
        pl.BlockSpec((1, LANES), lambda b, i: (0, 0)),
        spread_spec,
        smem,
    ]
    args = [qkv, qkv, rest, qkv, rest, dtb[1:2], alog[1:2], spread, logdec]
    if has_init:
        in_specs += [ret_spec, ssm_spec]
        args += [s_ret0[:, 1], s_ssm0[:, 1]]
    sret_in, sssm_in, sret1, sssm1 = pl.pallas_call(
        functools.partial(_bwd_state_body, has_init),
        grid=(batch, n),
        in_specs=in_specs,
        out_specs=[
            pl.BlockSpec((1, 1, RET_HEADS, RET_DK, RET_DV), lambda b, i: (b, n - 1 - i, 0, 0, 0)),
            pl.BlockSpec((1, 1, SSM_GROUPS, SSM_STATE, SSM_GW), lambda b, i: (b, n - 1 - i, 0, 0, 0)),
            ret_spec, ssm_spec,
        ],
        out_shape=[
            jax.ShapeDtypeStruct((batch, n, RET_HEADS, RET_DK, RET_DV), BF16),
            jax.ShapeDtypeStruct((batch, n, SSM_GROUPS, SSM_STATE, SSM_GW), BF16),
            ret_shape, ssm_shape,
        ],
        scratch_shapes=[ret_scr, ssm_scr, const_scr],
        compiler_params=_cparams(("arbitrary", "arbitrary")),
        name="hybrid_reverse_states",
    )(*args)

    def fwd(b, i):
        return b * n + i

    in_specs = [
        pl.BlockSpec((CHUNK, HYB_QKV), lambda b, i: (fwd(b, i), 0)),
        pl.BlockSpec((CHUNK, SSM_D_INNER), lambda b, i: (fwd(b, i), XS_COL // SSM_D_INNER)),
        pl.BlockSpec((CHUNK, 2 * SSM_BC), lambda b, i: (fwd(b, i), HYB_QKV // (2 * SSM_BC))),
        pl.BlockSpec((CHUNK, LANES), lambda b, i: (fwd(b, i), dt_blk)),
        pl.BlockSpec((1, 1, RET_HEADS, RET_DK, RET_DV), lambda b, i: (b, i, 0, 0, 0)),
        pl.BlockSpec((1, 1, SSM_GROUPS, SSM_STATE, SSM_GW), lambda b, i: (b, i, 0, 0, 0)),
        pl.BlockSpec((2, LANES), lambda b, i: (0, 0)),
        pl.BlockSpec((2, LANES), lambda b, i: (0, 0)),
        pl.BlockSpec((1, SSM_D_INNER), lambda b, i: (0, 0)),
        spread_spec,
        smem,
    ]
    args = [qkv, rest, qkv, rest, sret_in, sssm_in, dtb, alog, dsk, spread, logdec]
    if has_init:
        in_specs += [ret_spec, ssm_spec]
        args += [s_ret0[:, 0], s_ssm0[:, 0]]
    y, sret0, sssm0 = pl.pallas_call(
        functools.partial(_hyb_fwd_body, has_init),
        grid=(batch, n),
        in_specs=in_specs,
        out_specs=[pl.BlockSpec((CHUNK, HYB_MIX), lambda b, i: (fwd(b, i), 0)), ret_spec, ssm_spec],
        out_shape=[jax.ShapeDtypeStruct((m, HYB_MIX), F32), ret_shape, ssm_shape],
        scratch_shapes=[ret_scr, ssm_scr, const_scr, const_scr, const_scr, const_scr],
        compiler_params=_cparams(("arbitrary", "arbitrary")),
        name="hybrid_forward_mix",
    )(*args)
    return y, jnp.stack([sret0, sret1], axis=1), jnp.stack([sssm0, sssm1], axis=1)


def _hyb_out_body(y_ref, g_ref, z_ref, x_ref, gate_ref, gn_ref, ng_ref, w_ref, o_ref):
    y = y_ref[...]
    parts = []
    for h in range(RET_HEADS):
        yh = y[:, h * RET_DV:(h + 1) * RET_DV]
        mu = jnp.mean(yh, axis=-1, keepdims=True)
        var = jnp.mean(jnp.square(yh - mu), axis=-1, keepdims=True)
        parts.append((yh - mu) * lax.rsqrt(var + EPS))
    y_ret = jax.nn.silu(g_ref[...]) * (jnp.concatenate(parts, axis=1) * gn_ref[...])
    yz = y[:, RET_V:] * jax.nn.silu(z_ref[...])
    y_ssm = yz * lax.rsqrt(jnp.mean(yz * yz, axis=-1, keepdims=True) + EPS) * ng_ref[...]
    out = jnp.dot(y_ret.astype(BF16), w_ref[0:RET_V], preferred_element_type=F32)
    out = out + jnp.dot(y_ssm.astype(BF16), w_ref[RET_V:HYB_MIX], preferred_element_type=F32)
    o_ref[...] = x_ref[...] + gate_ref[0] * out


def _hyb_out(y, rest, x, gate, gn_gain, norm_gain, w_out, seq_len, tm):
    m = x.shape[0]
    per_batch = gate.shape[0] > 1
    tpb = seq_len // tm
    return pl.pallas_call(
        _hyb_out_body,
        grid=(m // tm,),
        in_specs=[
            pl.BlockSpec((tm, HYB_MIX), lambda i: (i, 0)),
            pl.BlockSpec((tm, RET_V), lambda i: (i, 0)),
            pl.BlockSpec((tm, SSM_D_INNER), lambda i: (i, 1)),
            pl.BlockSpec((tm, D_MODEL), lambda i: (i, 0)),
            _vec_spec(D_MODEL, tpb, per_batch, 1),
            pl.BlockSpec((1, RET_V), lambda i: (0, 0)),
            pl.BlockSpec((1, SSM_D_INNER), lambda i: (0, 0)),
            pl.BlockSpec((HYB_MIX, D_MODEL), lambda i: (0, 0)),
        ],
        out_specs=pl.BlockSpec((tm, D_MODEL), lambda i: (i, 0)),
        out_shape=jax.ShapeDtypeStruct((m, D_MODEL), F32),
        compiler_params=_cparams(("arbitrary",)),
        name="hybrid_out_proj",
    )(y, rest, rest, x, gate, gn_gain, norm_gain, w_out)


def _head_rms(t, seg_ref):
    outs = []
    for c in range(t.shape[1] // LANES):
        tc = t[:, c * LANES:(c + 1) * LANES]
        sq = tc * tc
        hi = sq.astype(BF16)
        lo = (sq - hi.astype(F32)).astype(BF16)
        ssum = (jnp.dot(hi, seg_ref[...], preferred_element_type=F32)
                + jnp.dot(lo, seg_ref[...], preferred_element_type=F32))
        outs.append(tc * lax.rsqrt(ssum * (1.0 / ATT_HD) + EPS))
    return outs


def _rope(tc, cos, sin, lane):
    fwd = pltpu.roll(tc, LANES - ROT_FREQS, 1)
    bwd = pltpu.roll(tc, ROT_FREQS, 1)
    return tc * cos + jnp.where(lane % (2 * ROT_FREQS) < ROT_FREQS, fwd, bwd) * sin


def _qkv_body(has_rope, x_ref, gain_ref, sc_ref, sh_ref, w_ref, seg_ref, qg_ref, kg_ref, *rest):
    if has_rope:
        cos_ref, sin_ref, q_ref, k_ref, v_ref = rest
    else:
        q_ref, k_ref, v_ref, kf_ref, vf_ref = rest
    h = _norm_mod(x_ref[...], gain_ref[...], sc_ref[0], sh_ref[0]).astype(BF16)
    lane = lax.broadcasted_iota(jnp.int32, (1, LANES), 1)

    def project(col):
        t = jnp.dot(h, w_ref[:, col:col + MXU_COLS], preferred_element_type=F32)
        return [t[:, c * LANES:(c + 1) * LANES] for c in range(MXU_COLS // LANES)]

    def normed(tiles, gain_ref_):
        tiles = [t * gain_ref_[...] for t in _head_rms(jnp.concatenate(tiles, axis=1), seg_ref)]
        return tiles, ([_rope(t, cos_ref[...], sin_ref[...], lane) for t in tiles] if has_rope else tiles)

    def emit_q(col, tiles):
        _, q = normed(tiles, qg_ref)
        for c, t in enumerate(q):
            row = col + c * LANES
            q_ref[0, row:row + LANES, :] = (t * Q_SCALE).T.astype(BF16)

    def emit_k(col, tiles):
        k, kr = normed(tiles, kg_ref)
        for c in range(MXU_COLS // LANES):
            lo = col + c * LANES
            if not has_rope:
                kf_ref[:, lo:lo + LANES] = k[c]
            kc = kr[c].astype(BF16)
            for a in range(LANES // ATT_HD):
                k_ref[0, lo // ATT_HD + a] = kc[:, a * ATT_HD:(a + 1) * ATT_HD]

    def emit_v(col, tiles):
        for c in range(MXU_COLS // LANES):
            lo = col + c * LANES
            if not has_rope:
                vf_ref[:, lo:lo + LANES] = tiles[c]
            vc = tiles[c].T.astype(BF16)
            for a in range(LANES // ATT_HD):
                v_ref[0, lo // ATT_HD + a] = vc[a * ATT_HD:(a + 1) * ATT_HD]

    work = ([(emit_q, col, col) for col in range(0, ATT_Q, MXU_COLS)]
            + [(emit_k, col, ATT_Q + col) for col in range(0, ATT_KV, MXU_COLS)]
            + [(emit_v, col, ATT_Q + ATT_KV + col) for col in range(0, ATT_KV, MXU_COLS)])
    tiles = project(work[0][2])
    for n, (emit, col, _) in enumerate(work):
        ahead = project(work[n + 1][2]) if n + 1 < len(work) else None
        emit(col, tiles)
        tiles = ahead


def _qkv_proj(x, gain, sc, sh, w_qkv, q_gain, k_gain, rope, seq_len, tm):
    m = x.shape[0]
    per_batch = sc.shape[0] > 1
    tpb = seq_len // tm
    has_rope = rope is not None
    seg = (lax.broadcasted_iota(jnp.int32, (LANES, LANES), 0) // ATT_HD
           == lax.broadcasted_iota(jnp.int32, (LANES, LANES), 1) // ATT_HD).astype(BF16)
    tile2 = lambda v: jnp.tile(v.astype(F32), LANES // ATT_HD).reshape(1, LANES)
    in_specs = [
        pl.BlockSpec((tm, D_MODEL), lambda i: (i, 0)),
        pl.BlockSpec((1, D_MODEL), lambda i: (0, 0)),
        _vec_spec(D_MODEL, tpb, per_batch, 1),
        _vec_spec(D_MODEL, tpb, per_batch, 1),
        pl.BlockSpec((D_MODEL, ATT_Q + 2 * ATT_KV), lambda i: (0, 0)),
        pl.BlockSpec((LANES, LANES), lambda i: (0, 0)),
        pl.BlockSpec((1, LANES), lambda i: (0, 0)),
        pl.BlockSpec((1, LANES), lambda i: (0, 0)),
    ]
    args = [x, gain, sc, sh, w_qkv, seg, tile2(q_gain), tile2(k_gain)]
    batch = m // seq_len
    out_specs = [pl.BlockSpec((1, ATT_Q, tm), lambda i: (i // tpb, 0, i % tpb)),
                 pl.BlockSpec((1, ATT_KV_HEADS, tm, ATT_HD), lambda i: (i // tpb, 0, i % tpb, 0)),
                 pl.BlockSpec((1, ATT_KV_HEADS, ATT_HD, tm), lambda i: (i // tpb, 0, 0, i % tpb))]
    out_shape = [jax.ShapeDtypeStruct((batch, ATT_Q, seq_len), BF16),
                 jax.ShapeDtypeStruct((batch, ATT_KV_HEADS, seq_len, ATT_HD), BF16),
                 jax.ShapeDtypeStruct((batch, ATT_KV_HEADS, ATT_HD, seq_len), BF16)]
    if has_rope:
        in_specs += [pl.BlockSpec((tm, LANES), lambda i: (i % tpb, 0))] * 2
        args += list(rope)
    else:
        out_specs += [pl.BlockSpec((tm, ATT_KV), lambda i: (i, 0))] * 2
        out_shape += [jax.ShapeDtypeStruct((m, ATT_KV), F32)] * 2
    return pl.pallas_call(
        functools.partial(_qkv_body, has_rope),
        grid=(m // tm,),
        in_specs=in_specs,
        out_specs=out_specs,
        out_shape=out_shape,
        compiler_params=_cparams(("arbitrary",)),
        name="attn_qkv_proj",
    )(*args)


def _rope_tables(seq_len):
    rows = seq_len // GRID_W
    row = jnp.repeat(jnp.arange(rows, dtype=F32), GRID_W)
    col = jnp.tile(jnp.arange(GRID_W, dtype=F32), rows)
    inv = ROPE_BASE ** (-jnp.arange(ROT_FREQS, dtype=F32) / ROT_FREQS)
    ar, ac = row[:, None] * inv, col[:, None] * inv
    cos = jnp.concatenate([jnp.cos(ar), jnp.cos(ar), jnp.cos(ac), jnp.cos(ac)], axis=1)
    sin = jnp.concatenate([-jnp.sin(ar), jnp.sin(ar), -jnp.sin(ac), jnp.sin(ac)], axis=1)
    return jnp.tile(cos, (1, LANES // ATT_HD)), jnp.tile(sin, (1, LANES // ATT_HD))


V_ROWS = ATT_HD + 16
ATT_SLOTS = 4


def _attn_body(tq, tk, qt_ref, k_ref, vt_ref, x_ref, gate_ref, wo_ref, o_ref, acc_scr, s_scr, ot_scr):
    nkv = k_ref.shape[2] // tk
    for g in range(ATT_KV_HEADS):
        qts = [qt_ref[0, (g * ATT_RATIO + r) * ATT_HD:(g * ATT_RATIO + r + 1) * ATT_HD, :]
               for r in range(ATT_RATIO)]
        acc_scr[...] = jnp.zeros_like(acc_scr)

        def scores(t, slot):
            kt = k_ref[0, g, pl.ds(pl.multiple_of(t * tk, tk), tk), :]
            tops = []
            for r in range(ATT_RATIO):
                s = jnp.dot(kt, qts[r], preferred_element_type=F32)
                s_scr[slot, r] = s
                tops.append(jnp.max(s, axis=0, keepdims=True))
            return tuple(tops)

        def softmax_pv(t, slot, ms, tops):
            vt = vt_ref[0, g, :, pl.ds(pl.multiple_of(t * tk, tk), tk)]
            new_ms = [jnp.maximum(ms[r], tops[r]) for r in range(ATT_RATIO)]
            ps = [jnp.exp2(s_scr[slot, r] - new_ms[r]).astype(BF16) for r in range(ATT_RATIO)]
            for r in range(ATT_RATIO):
                acc_scr[r] = (jnp.exp2(ms[r] - new_ms[r]) * acc_scr[r]
                              + jnp.dot(vt, ps[r], preferred_element_type=F32))
            return tuple(new_ms)

        def quad_step(i, carry):
            ms, top0, top1 = carry
            t = i * ATT_SLOTS
            top2 = scores(t + 2, 2)
            top3 = scores(t + 3, 3)
            ms = softmax_pv(t, 0, ms, top0)
            ms = softmax_pv(t + 1, 1, ms, top1)
            top0 = scores(t + 4, 0)
            top1 = scores(t + 5, 1)
            ms = softmax_pv(t + 2, 2, ms, top2)
            return softmax_pv(t + 3, 3, ms, top3), top0, top1

        tops = {t: scores(t, t) for t in range(min(2, nkv))}
        ms = tuple(jnp.full((1, tq), NEG_BIG, F32) for _ in range(ATT_RATIO))
        quads = max(nkv - 2, 0) // ATT_SLOTS
        if quads:
            ms, top0, top1 = lax.fori_loop(0, quads, quad_step, (ms, tops[0], tops[1]))
            tops = {quads * ATT_SLOTS: top0, quads * ATT_SLOTS + 1: top1}
        for t in range(quads * ATT_SLOTS, nkv):
            if t % 2 == 0:
                for ahead in (t + 2, t + 3):
                    if ahead < nkv:
                        tops[ahead] = scores(ahead, ahead % ATT_SLOTS)
            ms = softmax_pv(t, t % ATT_SLOTS, ms, tops.pop(t))
        for r in range(ATT_RATIO):
            hd = g * ATT_RATIO + r
            a = acc_scr[r]
            ot_scr[hd * ATT_HD:(hd + 1) * ATT_HD, :] = (a[0:ATT_HD] / a[ATT_HD:ATT_HD + 1]).astype(BF16)
    out = lax.dot_general(ot_scr[...], wo_ref[...], (((0,), (0,)), ((), ())), preferred_element_type=F32)
    o_ref[0] = x_ref[0] + gate_ref[0] * out


def _attention(qt, k4, vt, x, gate, w_o, batch, seq_len, tq, tk):
    per_batch = gate.shape[0] > 1
    nkeys = k4.shape[2]
    x3 = x.reshape(batch, seq_len, D_MODEL)
    out = pl.pallas_call(
        functools.partial(_attn_body, tq, tk),
        grid=(batch, seq_len // tq),
        in_specs=[
            pl.BlockSpec((1, ATT_Q, tq), lambda b, i: (b, 0, i)),
            pl.BlockSpec((1, ATT_KV_HEADS, nkeys, ATT_HD), lambda b, i: (b, 0, 0, 0)),
            pl.BlockSpec((1, ATT_KV_HEADS, V_ROWS, nkeys), lambda b, i: (b, 0, 0, 0)),
            pl.BlockSpec((1, tq, D_MODEL), lambda b, i: (b, i, 0)),
            pl.BlockSpec((1, 1, D_MODEL), lambda b, i: (b if per_batch else 0, 0, 0)),
            pl.BlockSpec((ATT_Q, D_MODEL), lambda b, i: (0, 0)),
        ],
        out_specs=pl.BlockSpec((1, tq, D_MODEL), lambda b, i: (b, i, 0)),
        out_shape=jax.ShapeDtypeStruct((batch, seq_len, D_MODEL), F32),
        scratch_shapes=[pltpu.VMEM((ATT_RATIO, V_ROWS, tq), F32), pltpu.VMEM((ATT_SLOTS, ATT_RATIO, tk, tq), F32),
                        pltpu.VMEM((ATT_Q, tq), BF16)],
        compiler_params=_cparams(("arbitrary", "arbitrary")),
        name="attention",
    )(qt, k4, vt, x3, gate, w_o)
    return out.reshape(batch * seq_len, D_MODEL)


def _keys_values(k4, vt, cache_k, cache_v):
    if cache_k is not None:
        k4 = jnp.concatenate([jnp.swapaxes(cache_k, 1, 2).astype(BF16), k4], axis=2)
        vt = jnp.concatenate([jnp.transpose(cache_v, (0, 2, 3, 1)).astype(BF16), vt], axis=3)
    ones = jnp.ones(vt.shape[:2] + (V_ROWS - ATT_HD, vt.shape[3]), vt.dtype)
    return k4, jnp.concatenate([vt, ones], axis=2)


def _trunk(x, mods, caches, P, batch, seq_len, tm, tq, tk):
    rope = None if caches is None else _rope_tables(seq_len)
    outs = {}
    sh1, sc1, g1, sh2, sc2, g2 = mods[0]
    qkv, rest = _hyb_in_proj(x, P['norm_mix'][0:1], sc1, sh1, P['hyb_w_in'], P['ssm_conv_w'][0],
                             P['ssm_conv_b'][0], seq_len, tm, 512)
    s_ret0 = None if caches is None else caches[0][:, 0]
    s_ssm0 = None if caches is None else caches[1][:, 0]
    y, s_ret, s_ssm = _hybrid_mixer(qkv, rest, P['ssm_dt_bias'][0], P['ssm_a_log'][0], P['ret_log_decay'][0],
                                    P['ssm_d'][0], s_ret0, s_ssm0, batch, seq_len)
    outs['ret'] = s_ret[:, None]
    outs['ssm'] = s_ssm[:, None]
    x = _hyb_out(y, rest, x, g1, P['ret_gn_gain'][0:1], P['ssm_norm_gain'][0:1], P['hyb_w_out'], seq_len, tm)
    x = _conv_ffn(x, P['norm_ffn'][0:1], sc2, sh2, g2, P['ffn_w_up'][0], P['ffn_conv_w'][0], P['ffn_conv_b'][0],
                  P['ffn_w_down'][0], seq_len, tm, 512)
    sh1, sc1, g1, sh2, sc2, g2 = mods[1]
    res = _qkv_proj(x, P['norm_mix'][1:2], sc1, sh1, P['attn_w_qkv'], P['attn_q_gain'][0], P['attn_k_gain'][0],
                    rope, seq_len, tm)
    if caches is None:
        qt, k4, vt, kf, vf = res
        outs['k'] = kf.reshape(batch, 1, seq_len, ATT_KV_HEADS, ATT_HD)
        outs['v'] = vf.reshape(batch, 1, seq_len, ATT_KV_HEADS, ATT_HD)
        k4, vt = _keys_values(k4, vt, None, None)
    else:
        qt, k4, vt = res
        k4, vt = _keys_values(k4, vt, caches[2][:, 0], caches[3][:, 0])
    x = _attention(qt, k4, vt, x, g1, P['attn_w_o'], batch, seq_len, tq, tk)
    x = _conv_ffn(x, P['norm_ffn'][1:2], sc2, sh2, g2, P['ffn_w_up'][1], P['ffn_conv_w'][1], P['ffn_conv_b'][1],
                  P['ffn_w_down'][1], seq_len, tm, 512)
    return x, outs


def kernel(x_prompt, x_sample, state_ret, state_ssm, cache_attn_k, cache_attn_v, c, c_ctx, w_mod, b_mod, norm_mix,
           norm_ffn, ffn_w_up, ffn_conv_w, ffn_conv_b, ffn_w_down, hyb_w_in, hyb_w_out, ret_log_decay, ret_gn_gain,
           ssm_conv_w, ssm_conv_b, ssm_a_log, ssm_dt_bias, ssm_d, ssm_norm_gain, attn_w_qkv, attn_q_gain,
           attn_k_gain, attn_w_o):
    batch, seq, _ = x_prompt.shape
    dec_batch, dec_seq, _ = x_sample.shape
    depth = w_mod.shape[0]

    rows = -(-(dec_batch + 1) // SUBLANES) * SUBLANES
    cond = jnp.concatenate([c, c_ctx[None, :], jnp.zeros((rows - dec_batch - 1, D_MODEL), F32)], axis=0)
    mod = _modulation(cond, w_mod, b_mod).reshape(depth, rows, 6, 1, D_MODEL)
    mods_sample = [[mod[l, 0:dec_batch, t] for t in range(6)] for l in range(depth)]
    mods_prompt = [[mod[l, dec_batch:dec_batch + 1, t] for t in range(6)] for l in range(depth)]

    w_in = hyb_w_in[0]
    x_end = HYB_QKV + 2 * RET_V + SSM_D_INNER
    bc_end = x_end + 2 * SSM_BC
    w_in = jnp.concatenate([
        w_in[:, 0:RET_Q], w_in[:, RET_Q:2 * RET_Q] * (RET_DK ** -0.5), w_in[:, 2 * RET_Q:HYB_QKV],
        w_in[:, x_end:bc_end], w_in[:, HYB_QKV:x_end], w_in[:, bc_end:],
        jnp.zeros((D_MODEL, HYB_BF + HYB_REST - w_in.shape[1]), F32)], axis=1)
    conv_order = lambda t: jnp.concatenate([t[..., SSM_D_INNER:], t[..., :SSM_D_INNER]], axis=-1)
    P = {
        'norm_mix': norm_mix, 'norm_ffn': norm_ffn,
        'hyb_w_in': w_in.astype(BF16),
        'hyb_w_out': hyb_w_out[0].astype(BF16),
        'ret_log_decay': ret_log_decay, 'ret_gn_gain': ret_gn_gain,
        'ssm_conv_w': conv_order(ssm_conv_w), 'ssm_conv_b': conv_order(ssm_conv_b), 'ssm_a_log': ssm_a_log, 'ssm_dt_bias': ssm_dt_bias,
        'ssm_d': ssm_d, 'ssm_norm_gain': ssm_norm_gain,
        'attn_w_qkv': attn_w_qkv[0].astype(BF16), 'attn_q_gain': attn_q_gain, 'attn_k_gain': attn_k_gain,
        'attn_w_o': attn_w_o[0].astype(BF16),
        'ffn_w_up': ffn_w_up.astype(BF16), 'ffn_conv_w': ffn_conv_w, 'ffn_conv_b': ffn_conv_b,
        'ffn_w_down': ffn_w_down.astype(BF16),
    }

    y_prompt, outs = _trunk(x_prompt.reshape(batch * seq, D_MODEL), mods_prompt, None, P, batch, seq,
                            tm=min(256, seq), tq=min(256, seq), tk=min(256, seq))
    caches = (state_ret, state_ssm, cache_attn_k, cache_attn_v)
    y_sample, _ = _trunk(x_sample.reshape(dec_batch * dec_seq, D_MODEL), mods_sample, caches, P, dec_batch, dec_seq,
                         tm=512, tq=256, tk=256)
    return (y_prompt.reshape(batch, seq, D_MODEL), y_sample.reshape(dec_batch, dec_seq, D_MODEL),
            outs['ret'], outs['ssm'], outs['k'], outs['v'])
```

```python
import functools

import jax
import jax.numpy as jnp
from jax import lax
from jax.experimental import pallas as pl
from jax.experimental.pallas import tpu as pltpu

F32 = jnp.float32
BF16 = jnp.bfloat16
HIGHEST = lax.Precision.HIGHEST

D_MODEL = 1024
EPS = 1e-6
GRID_W = 64
CHUNK = 128
RET_HEADS = 8
RET_DK = 64
RET_DV = 128
RET_Q = RET_HEADS * RET_DK
RET_V = RET_HEADS * RET_DV
SSM_D_INNER = 1024
SSM_HEADDIM = 64
SSM_HEADS = 16
SSM_GROUPS = 4
SSM_RATIO = 4
SSM_STATE = 128
SSM_BC = SSM_GROUPS * SSM_STATE
SSM_XBC = SSM_D_INNER + 2 * SSM_BC
HYB_QKV = 2 * RET_Q + RET_V
HYB_BF = HYB_QKV + 2 * SSM_BC
XS_COL = 2 * RET_V
DT_COL = XS_COL + SSM_D_INNER
HYB_REST = DT_COL + 128
HYB_MIX = RET_V + SSM_D_INNER
ATT_HEADS = 16
ATT_KV_HEADS = 4
ATT_RATIO = 4
ATT_HD = 64
ATT_Q = ATT_HEADS * ATT_HD
ATT_KV = ATT_KV_HEADS * ATT_HD
ROT_FREQS = ATT_HD // 4
ROPE_BASE = 10000.0
FFN_HIDDEN = 2816

LANES = 128
SUBLANES = 8
MXU_COLS = 256
VMEM_LIMIT = 56 * 1024 * 1024
NEG_BIG = -1e30
Q_SCALE = ATT_HD ** -0.5 * 1.4426950408889634


def _cparams(sem):
    return pltpu.CompilerParams(dimension_semantics=sem, vmem_limit_bytes=VMEM_LIMIT)


def _norm_mod(x, gain, sc, sh):
    ms = jnp.mean(x * x, axis=-1, keepdims=True)
    return x * lax.rsqrt(ms + EPS) * gain * (1.0 + sc) + sh


def _vec_spec(width, tiles_per_batch, per_batch, grid_rank):
    if grid_rank == 1:
        return pl.BlockSpec((1, 1, width), lambda i: ((i // tiles_per_batch) if per_batch else 0, 0, 0))
    return pl.BlockSpec((1, 1, width), lambda i, j: ((i // tiles_per_batch) if per_batch else 0, 0, 0))


def _mod_body(c_ref, w_ref, b_ref, o_ref):
    cs = jax.nn.silu(c_ref[...])
    o_ref[0] = jnp.dot(cs, w_ref[0], precision=HIGHEST, preferred_element_type=F32) + b_ref[0]


def _modulation(cond, w_mod, b_mod):
    depth, _, n = w_mod.shape
    rows = cond.shape[0]
    tn = n // 4
    return pl.pallas_call(
        _mod_body,
        grid=(depth, n // tn),
        in_specs=[
            pl.BlockSpec((rows, D_MODEL), lambda l, j: (0, 0)),
            pl.BlockSpec((1, D_MODEL, tn), lambda l, j: (l, 0, j)),
            pl.BlockSpec((1, 1, tn), lambda l, j: (l, 0, j)),
        ],
        out_specs=pl.BlockSpec((1, rows, tn), lambda l, j: (l, 0, j)),
        out_shape=jax.ShapeDtypeStruct((depth, rows, n), F32),
        compiler_params=_cparams(("arbitrary", "arbitrary")),
        name="modulation",
    )(cond, w_mod, b_mod.reshape(depth, 1, n))


def _chunks(total, width):
    return [(s, min(width, total - s)) for s in range(0, total, width)]


def _resident(shape):
    return pl.BlockSpec(shape, lambda *_: (0,) * len(shape), pipeline_mode=pl.Buffered(1))


def _norm_mod_halo(tiles_per_seq, tm, x_ref, xn_ref, xp_ref, gain_ref, sc_ref, sh_ref, h_scr):
    gain, sc, sh = gain_ref[...], sc_ref[0], sh_ref[0]
    h_scr[0:tm] = _norm_mod(x_ref[...], gain, sc, sh).astype(BF16)
    pos = pl.program_id(0) % tiles_per_seq
    hn = jnp.where(pos == tiles_per_seq - 1, 0.0, _norm_mod(xn_ref[0], gain, sc, sh))
    hp = jnp.where(pos == 0, 0.0, _norm_mod(xp_ref[0], gain, sc, sh))
    h_scr[tm:tm + 2 * SUBLANES] = jnp.concatenate([hn, hp], axis=0).astype(BF16)


def _dwconv3_rows(u, cw, cb, tm):
    rows = tm + 2 * SUBLANES
    c = pltpu.roll(u, 1, 0) * cw[0:1] + u * cw[1:2] + pltpu.roll(u, rows - 1, 0) * cw[2:3] + cb
    return c[0:tm]


def _halo_specs(m, tm):
    m8 = m // SUBLANES
    r8 = tm // SUBLANES
    return [pl.BlockSpec((1, SUBLANES, D_MODEL), lambda i: (jnp.minimum((i + 1) * r8, m8 - 1), 0, 0)),
            pl.BlockSpec((1, SUBLANES, D_MODEL), lambda i: (jnp.maximum(i * r8 - 1, 0), 0, 0))]


def _hyb_in_body(tiles_per_seq, tm, tn, x_ref, xn_ref, xp_ref, gain_ref, sc_ref, sh_ref, w_ref, cw_ref, cb_ref,
                 bf_ref, rest_ref, h_scr):
    _norm_mod_halo(tiles_per_seq, tm, x_ref, xn_ref, xp_ref, gain_ref, sc_ref, sh_ref, h_scr)
    h_all = h_scr[...]
    h = h_scr[0:tm]
    w_g = HYB_QKV
    w_x = w_g + 2 * RET_V
    w_bc = w_x + SSM_D_INNER
    w_dt = w_bc + 2 * SSM_BC

    def project(wcol, width, conv_col):
        wcols = w_ref[:, wcol:wcol + width]
        if conv_col is None:
            return jnp.dot(h, wcols, preferred_element_type=F32)
        u = jnp.dot(h_all, wcols, preferred_element_type=F32)
        return jax.nn.silu(_dwconv3_rows(u, cw_ref[:, conv_col:conv_col + width],
                                         cb_ref[:, conv_col:conv_col + width], tm))

    for col, width in _chunks(HYB_BF, tn):
        if col >= HYB_QKV:
            out = project(w_bc + col - HYB_QKV, width, SSM_D_INNER + col - HYB_QKV)
        else:
            out = project(col, width, None)
            if RET_Q <= col < 2 * RET_Q:
                out = out * (RET_DK ** -0.5)
        bf_ref[:, col:col + width] = out.astype(BF16)
    for col, width in _chunks(DT_COL, tn):
        rest_ref[:, col:col + width] = project(w_g + col, width, col - XS_COL if col >= XS_COL else None)
    dt = jnp.dot(h, w_ref[:, w_dt:w_dt + SSM_HEADS], preferred_element_type=F32)
    rest_ref[:, DT_COL:HYB_REST] = jnp.concatenate([dt, jnp.zeros((tm, HYB_REST - DT_COL - SSM_HEADS), F32)], axis=1)


def _hyb_in_proj(x, gain, sc, sh, w, conv_w, conv_b, seq_len, tm, tn):
    m = x.shape[0]
    per_batch = sc.shape[0] > 1
    tpb = seq_len // tm
    assert HYB_QKV % tn == 0 and XS_COL % tn == 0 and DT_COL % tn == 0
    x8 = x.reshape(m // SUBLANES, SUBLANES, D_MODEL)
    return pl.pallas_call(
        functools.partial(_hyb_in_body, tpb, tm, tn),
        grid=(m // tm,),
        in_specs=[
            pl.BlockSpec((tm, D_MODEL), lambda i: (i, 0)),
            *_halo_specs(m, tm),
            pl.BlockSpec((1, D_MODEL), lambda i: (0, 0)),
            _vec_spec(D_MODEL, tpb, per_batch, 1),
            _vec_spec(D_MODEL, tpb, per_batch, 1),
            _resident(w.shape),
            _resident((3, SSM_XBC)),
            _resident((1, SSM_XBC)),
        ],
        out_specs=[pl.BlockSpec((tm, HYB_BF), lambda i: (i, 0)), pl.BlockSpec((tm, HYB_REST), lambda i: (i, 0))],
        out_shape=[jax.ShapeDtypeStruct((m, HYB_BF), BF16), jax.ShapeDtypeStruct((m, HYB_REST), F32)],
        scratch_shapes=[pltpu.VMEM((tm + 2 * SUBLANES, D_MODEL), BF16)],
        compiler_params=_cparams(("arbitrary",)),
        name="hybrid_in_proj",
    )(x, x8, x8, gain, sc, sh, w, conv_w, conv_b.reshape(1, SSM_XBC))


def _ffn_body(tiles_per_seq, tm, th, x_ref, xn_ref, xp_ref, gain_ref, sc_ref, sh_ref, gate_ref,
              wu_ref, cw_ref, cb_ref, wd_ref, o_ref, h_scr, act_scr):
    _norm_mod_halo(tiles_per_seq, tm, x_ref, xn_ref, xp_ref, gain_ref, sc_ref, sh_ref, h_scr)
    h = h_scr[...]

    def conv_up(col, width):
        u = jnp.dot(h, wu_ref[:, col:col + width], preferred_element_type=F32)
        return _dwconv3_rows(u, cw_ref[:, col:col + width], cb_ref[:, col:col + width], tm)

    for col, width in _chunks(FFN_HIDDEN, th):
        act = jax.nn.silu(conv_up(col, width)) * conv_up(FFN_HIDDEN + col, width)
        act_scr[:, col:col + width] = act.astype(BF16)
    out = jnp.dot(act_scr[...], wd_ref[...], preferred_element_type=F32)
    o_ref[...] = x_ref[...] + gate_ref[0] * out


def _conv_ffn(x, gain, sc, sh, gate, w_up, conv_w, conv_b, w_down, seq_len, tm, th):
    m = x.shape[0]
    per_batch = sc.shape[0] > 1
    tpb = seq_len // tm
    x8 = x.reshape(m // SUBLANES, SUBLANES, D_MODEL)
    conv_b = conv_b.reshape(1, 2 * FFN_HIDDEN)
    vec = _vec_spec(D_MODEL, tpb, per_batch, 1)
    return pl.pallas_call(
        functools.partial(_ffn_body, tpb, tm, th),
        grid=(m // tm,),
        in_specs=[
            pl.BlockSpec((tm, D_MODEL), lambda i: (i, 0)),
            *_halo_specs(m, tm),
            pl.BlockSpec((1, D_MODEL), lambda i: (0, 0)),
            vec, vec, vec,
            _resident((D_MODEL, 2 * FFN_HIDDEN)),
            _resident((3, 2 * FFN_HIDDEN)),
            _resident((1, 2 * FFN_HIDDEN)),
            _resident((FFN_HIDDEN, D_MODEL)),
        ],
        out_specs=pl.BlockSpec((tm, D_MODEL), lambda i: (i, 0)),
        out_shape=jax.ShapeDtypeStruct((m, D_MODEL), F32),
        scratch_shapes=[pltpu.VMEM((tm + 2 * SUBLANES, D_MODEL), BF16), pltpu.VMEM((tm, FFN_HIDDEN), BF16)],
        compiler_params=_cparams(("arbitrary",)),
        name="conv_ffn",
    )(x, x8, x8, gain, sc, sh, gate, w_up, conv_w, conv_b, w_down)


SSM_GW = SSM_RATIO * SSM_HEADDIM


def _bcol(x, h):
    return jnp.broadcast_to(x[:, h:h + 1], (x.shape[0], LANES))


def _pair_sel(left, a, h):
    return jnp.where(left, _bcol(a, h), _bcol(a, h + 1))


def _expand_heads(a, spread_ref):
    hi = a.astype(BF16)
    lo = (a - hi.astype(F32)).astype(BF16)
    return (jnp.dot(hi, spread_ref[...], preferred_element_type=F32)
            + jnp.dot(lo, spread_ref[...], preferred_element_type=F32))


def _ssm_decay_row(left, efull, g):
    return jnp.concatenate([_pair_sel(left, efull, g * SSM_RATIO + 2 * m) for m in range(SSM_RATIO // 2)], axis=1)


def _load_states(s_ret, s_ssm, sret0_ref, sssm0_ref):
    if sret0_ref is None:
        s_ret[...] = jnp.zeros_like(s_ret)
        s_ssm[...] = jnp.zeros_like(s_ssm)
    else:
        s_ret[...] = sret0_ref[0]
        for g in range(SSM_GROUPS):
            s_ssm[g] = jnp.concatenate([sssm0_ref[0, g * SSM_RATIO + r] for r in range(SSM_RATIO)], axis=1)


def _store_states(s_ret, s_ssm, sret_ref, sssm_ref):
    sret_ref[0] = s_ret[...]
    for g in range(SSM_GROUPS):
        for r in range(SSM_RATIO):
            sssm_ref[0, g * SSM_RATIO + r] = s_ssm[g][:, r * SSM_HEADDIM:(r + 1) * SSM_HEADDIM]


def _ssm_log_decay(dt_raw, dtb, alog, lane):
    dt = jax.nn.softplus(dt_raw + dtb)
    return dt, jnp.where(lane < SSM_HEADS, dt * (-jnp.exp(alog)), 0.0)


def _bwd_state_body(has_init, k_ref, v_ref, xs_ref, bm_ref, dt_ref, dtb_ref, alog_ref, spread_ref, logdec_ref,
                    *rest):
    sret0_ref = sssm0_ref = None
    if has_init:
        sret0_ref, sssm0_ref = rest[0], rest[1]
        rest = rest[2:]
    sret_in_ref, sssm_in_ref, sret_fin_ref, sssm_fin_ref, s_ret, s_ssm, wcol = rest
    i = pl.program_id(1)
    n = pl.num_programs(1)
    ii = lax.broadcasted_iota(jnp.int32, (CHUNK, CHUNK), 0)
    jj = lax.broadcasted_iota(jnp.int32, (CHUNK, CHUNK), 1)
    lane = lax.broadcasted_iota(jnp.int32, (1, LANES), 1)
    left = lane < SSM_HEADDIM

    @pl.when(i == 0)
    def _():
        _load_states(s_ret, s_ssm, sret0_ref, sssm0_ref)
        for h in range(RET_HEADS):
            wcol[h] = jnp.exp(ii.astype(F32) * logdec_ref[1, h])

    sret_in_ref[0, 0] = s_ret[...].astype(BF16)
    sssm_in_ref[0, 0] = s_ssm[...].astype(BF16)

    dt, la = _ssm_log_decay(dt_ref[...], dtb_ref[...], alog_ref[...], lane)
    rc = jnp.dot((jj >= ii).astype(F32), la, precision=HIGHEST, preferred_element_type=F32)

    full = jnp.full((1, LANES), float(CHUNK), F32)
    for h in range(RET_HEADS):
        kh = k_ref[:, h * RET_DK:(h + 1) * RET_DK]
        vw = (v_ref[:, h * RET_DV:(h + 1) * RET_DV].astype(F32) * wcol[h]).astype(BF16)
        upd = lax.dot_general(kh, vw, (((0,), (0,)), ((), ())), preferred_element_type=F32)
        s_ret[h] = s_ret[h] * jnp.exp(full * logdec_ref[1, h]) + upd

    last = rc[0:1]
    wx = _expand_heads(jnp.exp(last - rc) * dt, spread_ref)
    efull = jnp.exp(last)
    for g in range(SSM_GROUPS):
        vw = (xs_ref[:, g * SSM_GW:(g + 1) * SSM_GW] * wx[:, g * SSM_GW:(g + 1) * SSM_GW]).astype(BF16)
        bg = bm_ref[:, g * SSM_STATE:(g + 1) * SSM_STATE]
        upd = lax.dot_general(bg, vw, (((0,), (0,)), ((), ())), preferred_element_type=F32)
        s_ssm[g] = s_ssm[g] * _ssm_decay_row(left, efull, g) + upd

    @pl.when(i == n - 1)
    def _():
        _store_states(s_ret, s_ssm, sret_fin_ref, sssm_fin_ref)


def _hyb_fwd_body(has_init, qkv_ref, xs_ref, bc_ref, dt_ref, sret_in_ref, sssm_in_ref, dtb_ref, alog_ref, dsk_ref,
                  spread_ref, logdec_ref, *rest):
    sret0_ref = sssm0_ref = None
    if has_init:
        sret0_ref, sssm0_ref = rest[0], rest[1]
        rest = rest[2:]
    y_ref, sret_fin_ref, sssm_fin_ref, s_ret, s_ssm, dcomb, ein0, ein1, wcol = rest
    i = pl.program_id(1)
    n = pl.num_programs(1)
    ii = lax.broadcasted_iota(jnp.int32, (CHUNK, CHUNK), 0)
    jj = lax.broadcasted_iota(jnp.int32, (CHUNK, CHUNK), 1)
    lane = lax.broadcasted_iota(jnp.int32, (1, LANES), 1)
    left = lane < SSM_HEADDIM
    lower = jj <= ii

    @pl.when(i == 0)
    def _():
        _load_states(s_ret, s_ssm, sret0_ref, sssm0_ref)
        fi, fj = ii.astype(F32), jj.astype(F32)
        for h in range(RET_HEADS):
            la0, la1 = logdec_ref[0, h], logdec_ref[1, h]
            dec = jnp.exp(jnp.where(lower, (fi - fj) * la0, (fj - fi) * la1))
            dcomb[h] = jnp.where(ii == jj, 2.0, dec)
            ein0[h] = jnp.exp((fi + 1.0) * la0)
            ein1[h] = jnp.exp((CHUNK - fi) * la1)
            wcol[h] = jnp.exp((CHUNK - 1.0 - fi) * la0)

    def ret_head(h):
        return (qkv_ref[:, h * RET_DK:(h + 1) * RET_DK],
                qkv_ref[:, RET_Q + h * RET_DK:RET_Q + (h + 1) * RET_DK],
                qkv_ref[:, 2 * RET_Q + h * RET_DV:2 * RET_Q + (h + 1) * RET_DV])

    def ssm_group(g):
        return (bc_ref[:, g * SSM_STATE:(g + 1) * SSM_STATE],
                bc_ref[:, SSM_BC + g * SSM_STATE:SSM_BC + (g + 1) * SSM_STATE])

    dt_raw = dt_ref[...]
    dt0, la0 = _ssm_log_decay(dt_raw, dtb_ref[0:1], alog_ref[0:1], lane)
    dt1, la1 = _ssm_log_decay(dt_raw, dtb_ref[1:2], alog_ref[1:2], lane)
    c0 = jnp.dot(lower.astype(F32), la0, precision=HIGHEST, preferred_element_type=F32)
    c1 = jnp.dot((jj >= ii).astype(F32), la1, precision=HIGHEST, preferred_element_type=F32)

    ret_scores, ret_inter = [], []
    for h in range(RET_HEADS):
        qh, kh, _ = ret_head(h)
        ret_scores.append(lax.dot_general(qh, kh, (((1,), (1,)), ((), ())), preferred_element_type=F32))
        states = jnp.concatenate([s_ret[h].astype(BF16), sret_in_ref[0, 0, h]], axis=1)
        ret_inter.append(jnp.dot(qh, states, preferred_element_type=F32))
    ssm_scores, ssm_inter = [], []
    for g in range(SSM_GROUPS):
        bg, cg = ssm_group(g)
        ssm_scores.append(lax.dot_general(cg, bg, (((1,), (1,)), ((), ())), preferred_element_type=F32))
        states = jnp.concatenate([s_ssm[g].astype(BF16), sssm_in_ref[0, 0, g]], axis=1)
        ssm_inter.append(jnp.dot(cg, states, preferred_element_type=F32))

    full = jnp.full((1, LANES), float(CHUNK), F32)
    for h in range(RET_HEADS):
        _, kh, vh = ret_head(h)
        y = jnp.dot((ret_scores[h] * dcomb[h]).astype(BF16), vh, preferred_element_type=F32)
        inter = ret_inter[h]
        y_ref[:, h * RET_DV:(h + 1) * RET_DV] = y + inter[:, 0:RET_DV] * ein0[h] + inter[:, RET_DV:] * ein1[h]
        vw = (vh.astype(F32) * wcol[h]).astype(BF16)
        upd = lax.dot_general(kh, vw, (((0,), (0,)), ((), ())), preferred_element_type=F32)
        s_ret[h] = s_ret[h] * jnp.exp(full * logdec_ref[0, h]) + upd

    c0kt, c1kt = (c0 - jnp.log(dt0)).T, (c1 - jnp.log(dt1)).T
    dst = (dt0 + dt1).T
    last0 = c0[CHUNK - 1:CHUNK]
    w0x = _expand_heads(jnp.exp(last0 - c0) * dt0, spread_ref)
    efull0 = jnp.exp(last0)
    diag = jj == ii
    for g in range(SSM_GROUPS):
        bg, _ = ssm_group(g)
        s = ssm_scores[g]
        inter = ssm_inter[g]
        vws = []
        for m in range(SSM_RATIO // 2):
            ha = g * SSM_RATIO + 2 * m
            ps, c0bs, c1bs = [], [], []
            for h in (ha, ha + 1):
                c0b, c1b = _bcol(c0, h), _bcol(c1, h)
                arg = jnp.where(lower, c0b - c0kt[h:h + 1], c1b - c1kt[h:h + 1])
                ps.append((s * jnp.where(diag, dst[h:h + 1], jnp.exp(arg))).astype(BF16))
                c0bs.append(c0b)
                c1bs.append(c1b)
            col = (g * 2 + m) * LANES
            xs_pair = xs_ref[:, col:col + LANES]
            vals = jnp.concatenate([jnp.where(left, xs_pair, 0.0), jnp.where(left, 0.0, xs_pair)], axis=0)
            y = jnp.dot(jnp.concatenate(ps, axis=1), vals.astype(BF16), preferred_element_type=F32)
            y = y + inter[:, m * LANES:(m + 1) * LANES] * jnp.exp(jnp.where(left, c0bs[0], c0bs[1]))
            y = y + (inter[:, SSM_GW + m * LANES:SSM_GW + (m + 1) * LANES]
                     * jnp.exp(jnp.where(left, c1bs[0], c1bs[1])))
            y_ref[:, RET_V + col:RET_V + col + LANES] = y + dsk_ref[:, col:col + LANES] * xs_pair
            vws.append((xs_pair * w0x[:, col:col + LANES]).astype(BF16))
        upd = lax.dot_general(bg, jnp.concatenate(vws, axis=1), (((0,), (0,)), ((), ())),
                              preferred_element_type=F32)
        s_ssm[g] = s_ssm[g] * _ssm_decay_row(left, efull0, g) + upd

    @pl.when(i == n - 1)
    def _():
        _store_states(s_ret, s_ssm, sret_fin_ref, sssm_fin_ref)


def _hybrid_mixer(qkv, rest, dt_bias, a_log, log_decay, d_skip, s_ret0, s_ssm0, batch, seq_len):
    m = qkv.shape[0]
    n = seq_len // CHUNK
    has_init = s_ret0 is not None
    dt_blk = DT_COL // LANES
    pad = lambda v: jnp.pad(v.astype(F32), ((0, 0), (0, LANES - v.shape[1])))
    dtb, alog = pad(dt_bias), pad(a_log)
    logdec = log_decay.astype(F32)
    spread = (lax.broadcasted_iota(jnp.int32, (LANES, SSM_D_INNER), 0)
              == lax.broadcasted_iota(jnp.int32, (LANES, SSM_D_INNER), 1) // SSM_HEADDIM).astype(BF16)
    spread_spec = pl.BlockSpec((LANES, SSM_D_INNER), lambda b, i: (0, 0))
    dsk = jnp.repeat((d_skip[0] + d_skip[1]).astype(F32), SSM_HEADDIM).reshape(1, SSM_D_INNER)
    smem = pl.BlockSpec(memory_space=pltpu.SMEM)
    ret_spec = pl.BlockSpec((1, RET_HEADS, RET_DK, RET_DV), lambda b, i: (b, 0, 0, 0))
    ssm_spec = pl.BlockSpec((1, SSM_HEADS, SSM_STATE, SSM_HEADDIM), lambda b, i: (b, 0, 0, 0))
    ret_shape = jax.ShapeDtypeStruct((batch, RET_HEADS, RET_DK, RET_DV), F32)
    ssm_shape = jax.ShapeDtypeStruct((batch, SSM_HEADS, SSM_STATE, SSM_HEADDIM), F32)
    ret_scr = pltpu.VMEM((RET_HEADS, RET_DK, RET_DV), F32)
    ssm_scr = pltpu.VMEM((SSM_GROUPS, SSM_STATE, SSM_GW), F32)
    const_scr = pltpu.VMEM((RET_HEADS, CHUNK, CHUNK), F32)

    def rev(b, i):
        return b * n + n - 1 - i

    in_specs = [
        pl.BlockSpec((CHUNK, RET_Q), lambda b, i: (rev(b, i), 1)),
        pl.BlockSpec((CHUNK, RET_V), lambda b, i: (rev(b, i), 1)),
        pl.BlockSpec((CHUNK, SSM_D_INNER), lambda b, i: (rev(b, i), XS_COL // SSM_D_INNER)),
        pl.BlockSpec((CHUNK, SSM_BC), lambda b, i: (rev(b, i), HYB_QKV // SSM_BC)),
        pl.BlockSpec((CHUNK, LANES), lambda b, i: (rev(b, i), dt_blk)),
        pl.BlockSpec((1, LANES), lambda b, i: (0, 0)),
        pl.BlockSpec((1, LANES), lambda b, i: (0, 0)),
        spread_spec,
        smem,
    ]
    args = [qkv, qkv, rest, qkv, rest, dtb[1:2], alog[1:2], spread, logdec]
    if has_init:
        in_specs += [ret_spec, ssm_spec]
        args += [s_ret0[:, 1], s_ssm0[:, 1]]
    sret_in, sssm_in, sret1, sssm1 = pl.pallas_call(
        functools.partial(_bwd_state_body, has_init),
        grid=(batch, n),
        in_specs=in_specs,
        out_specs=[
            pl.BlockSpec((1, 1, RET_HEADS, RET_DK, RET_DV), lambda b, i: (b, n - 1 - i, 0, 0, 0)),
            pl.BlockSpec((1, 1, SSM_GROUPS, SSM_STATE, SSM_GW), lambda b, i: (b, n - 1 - i, 0, 0, 0)),
            ret_spec, ssm_spec,
        ],
        out_shape=[
            jax.ShapeDtypeStruct((batch, n, RET_HEADS, RET_DK, RET_DV), BF16),
            jax.ShapeDtypeStruct((batch, n, SSM_GROUPS, SSM_STATE, SSM_GW), BF16),
            ret_shape, ssm_shape,
        ],
        scratch_shapes=[ret_scr, ssm_scr, const_scr],
        compiler_params=_cparams(("arbitrary", "arbitrary")),
        name="hybrid_reverse_states",
    )(*args)

    def fwd(b, i):
        return b * n + i

    in_specs = [
        pl.BlockSpec((CHUNK, HYB_QKV), lambda b, i: (fwd(b, i), 0)),
        pl.BlockSpec((CHUNK, SSM_D_INNER), lambda b, i: (fwd(b, i), XS_COL // SSM_D_INNER)),
        pl.BlockSpec((CHUNK, 2 * SSM_BC), lambda b, i: (fwd(b, i), HYB_QKV // (2 * SSM_BC))),
        pl.BlockSpec((CHUNK, LANES), lambda b, i: (fwd(b, i), dt_blk)),
        pl.BlockSpec((1, 1, RET_HEADS, RET_DK, RET_DV), lambda b, i: (b, i, 0, 0, 0)),
        pl.BlockSpec((1, 1, SSM_GROUPS, SSM_STATE, SSM_GW), lambda b, i: (b, i, 0, 0, 0)),
        pl.BlockSpec((2, LANES), lambda b, i: (0, 0)),
        pl.BlockSpec((2, LANES), lambda b, i: (0, 0)),
        pl.BlockSpec((1, SSM_D_INNER), lambda b, i: (0, 0)),
        spread_spec,
        smem,
    ]
    args = [qkv, rest, qkv, rest, sret_in, sssm_in, dtb, alog, dsk, spread, logdec]
    if has_init:
        in_specs += [ret_spec, ssm_spec]
        args += [s_ret0[:, 0], s_ssm0[:, 0]]
    y, sret0, sssm0 = pl.pallas_call(
        functools.partial(_hyb_fwd_body, has_init),
        grid=(batch, n),
        in_specs=in_specs,
        out_specs=[pl.BlockSpec((CHUNK, HYB_MIX), lambda b, i: (fwd(b, i), 0)), ret_spec, ssm_spec],
        out_shape=[jax.ShapeDtypeStruct((m, HYB_MIX), F32), ret_shape, ssm_shape],
        scratch_shapes=[ret_scr, ssm_scr, const_scr, const_scr, const_scr, const_scr],
        compiler_params=_cparams(("arbitrary", "arbitrary")),
        name="hybrid_forward_mix",
    )(*args)
    return y, jnp.stack([sret0, sret1], axis=1), jnp.stack([sssm0, sssm1], axis=1)


def _hyb_out_body(y_ref, g_ref, z_ref, x_ref, gate_ref, gn_ref, ng_ref, w_ref, o_ref):
    y = y_ref[...]
    parts = []
    for h in range(RET_HEADS):
        yh = y[:, h * RET_DV:(h + 1) * RET_DV]
        mu = jnp.mean(yh, axis=-1, keepdims=True)
        var = jnp.mean(jnp.square(yh - mu), axis=-1, keepdims=True)
        parts.append((yh - mu) * lax.rsqrt(var + EPS))
    y_ret = jax.nn.silu(g_ref[...]) * (jnp.concatenate(parts, axis=1) * gn_ref[...])
    yz = y[:, RET_V:] * jax.nn.silu(z_ref[...])
    y_ssm = yz * lax.rsqrt(jnp.mean(yz * yz, axis=-1, keepdims=True) + EPS) * ng_ref[...]
    out = jnp.dot(y_ret.astype(BF16), w_ref[0:RET_V], preferred_element_type=F32)
    out = out + jnp.dot(y_ssm.astype(BF16), w_ref[RET_V:HYB_MIX], preferred_element_type=F32)
    o_ref[...] = x_ref[...] + gate_ref[0] * out


def _hyb_out(y, rest, x, gate, gn_gain, norm_gain, w_out, seq_len, tm):
    m = x.shape[0]
    per_batch = gate.shape[0] > 1
    tpb = seq_len // tm
    return pl.pallas_call(
        _hyb_out_body,
        grid=(m // tm,),
        in_specs=[
            pl.BlockSpec((tm, HYB_MIX), lambda i: (i, 0)),
            pl.BlockSpec((tm, RET_V), lambda i: (i, 0)),
            pl.BlockSpec((tm, SSM_D_INNER), lambda i: (i, 1)),
            pl.BlockSpec((tm, D_MODEL), lambda i: (i, 0)),
            _vec_spec(D_MODEL, tpb, per_batch, 1),
            pl.BlockSpec((1, RET_V), lambda i: (0, 0)),
            pl.BlockSpec((1, SSM_D_INNER), lambda i: (0, 0)),
            pl.BlockSpec((HYB_MIX, D_MODEL), lambda i: (0, 0)),
        ],
        out_specs=pl.BlockSpec((tm, D_MODEL), lambda i: (i, 0)),
        out_shape=jax.ShapeDtypeStruct((m, D_MODEL), F32),
        compiler_params=_cparams(("arbitrary",)),
        name="hybrid_out_proj",
    )(y, rest, rest, x, gate, gn_gain, norm_gain, w_out)


def _head_rms(t, seg_ref):
    outs = []
    for c in range(t.shape[1] // LANES):
        tc = t[:, c * LANES:(c + 1) * LANES]
        sq = tc * tc
        hi = sq.astype(BF16)
        lo = (sq - hi.astype(F32)).astype(BF16)
        ssum = (jnp.dot(hi, seg_ref[...], preferred_element_type=F32)
                + jnp.dot(lo, seg_ref[...], preferred_element_type=F32))
        outs.append(tc * lax.rsqrt(ssum * (1.0 / ATT_HD) + EPS))
    return outs


def _rope(tc, cos, sin, lane):
    fwd = pltpu.roll(tc, LANES - ROT_FREQS, 1)
    bwd = pltpu.roll(tc, ROT_FREQS, 1)
    return tc * cos + jnp.where(lane % (2 * ROT_FREQS) < ROT_FREQS, fwd, bwd) * sin


def _qkv_body(has_rope, x_ref, gain_ref, sc_ref, sh_ref, w_ref, seg_ref, qg_ref, kg_ref, *rest):
    if has_rope:
        cos_ref, sin_ref, q_ref, k_ref, v_ref = rest
    else:
        q_ref, k_ref, v_ref, kf_ref, vf_ref = rest
    h = _norm_mod(x_ref[...], gain_ref[...], sc_ref[0], sh_ref[0]).astype(BF16)
    lane = lax.broadcasted_iota(jnp.int32, (1, LANES), 1)

    def project(col):
        t = jnp.dot(h, w_ref[:, col:col + MXU_COLS], preferred_element_type=F32)
        return [t[:, c * LANES:(c + 1) * LANES] for c in range(MXU_COLS // LANES)]

    def normed(tiles, gain_ref_):
        tiles = [t * gain_ref_[...] for t in _head_rms(jnp.concatenate(tiles, axis=1), seg_ref)]
        return tiles, ([_rope(t, cos_ref[...], sin_ref[...], lane) for t in tiles] if has_rope else tiles)

    def emit_q(col, tiles):
        _, q = normed(tiles, qg_ref)
        for c, t in enumerate(q):
            row = col + c * LANES
            q_ref[0, row:row + LANES, :] = (t * Q_SCALE).T.astype(BF16)

    def emit_k(col, tiles):
        k, kr = normed(tiles, kg_ref)
        for c in range(MXU_COLS // LANES):
            lo = col + c * LANES
            if not has_rope:
                kf_ref[:, lo:lo + LANES] = k[c]
            kc = kr[c].astype(BF16)
            for a in range(LANES // ATT_HD):
                k_ref[0, lo // ATT_HD + a] = kc[:, a * ATT_HD:(a + 1) * ATT_HD]

    def emit_v(col, tiles):
        for c in range(MXU_COLS // LANES):
            lo = col + c * LANES
            if not has_rope:
                vf_ref[:, lo:lo + LANES] = tiles[c]
            vc = tiles[c].T.astype(BF16)
            for a in range(LANES // ATT_HD):
                v_ref[0, lo // ATT_HD + a] = vc[a * ATT_HD:(a + 1) * ATT_HD]

    work = ([(emit_q, col, col) for col in range(0, ATT_Q, MXU_COLS)]
            + [(emit_k, col, ATT_Q + col) for col in range(0, ATT_KV, MXU_COLS)]
            + [(emit_v, col, ATT_Q + ATT_KV + col) for col in range(0, ATT_KV, MXU_COLS)])
    tiles = project(work[0][2])
    for n, (emit, col, _) in enumerate(work):
        ahead = project(work[n + 1][2]) if n + 1 < len(work) else None
        emit(col, tiles)
        tiles = ahead


def _qkv_proj(x, gain, sc, sh, w_qkv, q_gain, k_gain, rope, seq_len, tm):
    m = x.shape[0]
    per_batch = sc.shape[0] > 1
    tpb = seq_len // tm
    has_rope = rope is not None
    seg = (lax.broadcasted_iota(jnp.int32, (LANES, LANES), 0) // ATT_HD
           == lax.broadcasted_iota(jnp.int32, (LANES, LANES), 1) // ATT_HD).astype(BF16)
    tile2 = lambda v: jnp.tile(v.astype(F32), LANES // ATT_HD).reshape(1, LANES)
    in_specs = [
        pl.BlockSpec((tm, D_MODEL), lambda i: (i, 0)),
        pl.BlockSpec((1, D_MODEL), lambda i: (0, 0)),
        _vec_spec(D_MODEL, tpb, per_batch, 1),
        _vec_spec(D_MODEL, tpb, per_batch, 1),
        pl.BlockSpec((D_MODEL, ATT_Q + 2 * ATT_KV), lambda i: (0, 0)),
        pl.BlockSpec((LANES, LANES), lambda i: (0, 0)),
        pl.BlockSpec((1, LANES), lambda i: (0, 0)),
        pl.BlockSpec((1, LANES), lambda i: (0, 0)),
    ]
    args = [x, gain, sc, sh, w_qkv, seg, tile2(q_gain), tile2(k_gain)]
    batch = m // seq_len
    out_specs = [pl.BlockSpec((1, ATT_Q, tm), lambda i: (i // tpb, 0, i % tpb)),
                 pl.BlockSpec((1, ATT_KV_HEADS, tm, ATT_HD), lambda i: (i // tpb, 0, i % tpb, 0)),
                 pl.BlockSpec((1, ATT_KV_HEADS, ATT_HD, tm), lambda i: (i // tpb, 0, 0, i % tpb))]
    out_shape = [jax.ShapeDtypeStruct((batch, ATT_Q, seq_len), BF16),
                 jax.ShapeDtypeStruct((batch, ATT_KV_HEADS, seq_len, ATT_HD), BF16),
                 jax.ShapeDtypeStruct((batch, ATT_KV_HEADS, ATT_HD, seq_len), BF16)]
    if has_rope:
        in_specs += [pl.BlockSpec((tm, LANES), lambda i: (i % tpb, 0))] * 2
        args += list(rope)
    else:
        out_specs += [pl.BlockSpec((tm, ATT_KV), lambda i: (i, 0))] * 2
        out_shape += [jax.ShapeDtypeStruct((m, ATT_KV), F32)] * 2
    return pl.pallas_call(
        functools.partial(_qkv_body, has_rope),
        grid=(m // tm,),
        in_specs=in_specs,
        out_specs=out_specs,
        out_shape=out_shape,
        compiler_params=_cparams(("arbitrary",)),
        name="attn_qkv_proj",
    )(*args)


def _rope_tables(seq_len):
    rows = seq_len // GRID_W
    row = jnp.repeat(jnp.arange(rows, dtype=F32), GRID_W)
    col = jnp.tile(jnp.arange(GRID_W, dtype=F32), rows)
    inv = ROPE_BASE ** (-jnp.arange(ROT_FREQS, dtype=F32) / ROT_FREQS)
    ar, ac = row[:, None] * inv, col[:, None] * inv
    cos = jnp.concatenate([jnp.cos(ar), jnp.cos(ar), jnp.cos(ac), jnp.cos(ac)], axis=1)
    sin = jnp.concatenate([-jnp.sin(ar), jnp.sin(ar), -jnp.sin(ac), jnp.sin(ac)], axis=1)
    return jnp.tile(cos, (1, LANES // ATT_HD)), jnp.tile(sin, (1, LANES // ATT_HD))


V_ROWS = ATT_HD + 16
ATT_SLOTS = 4


def _attn_body(tq, tk, qt_ref, k_ref, vt_ref, x_ref, gate_ref, wo_ref, o_ref, acc_scr, s_scr, ot_scr):
    nkv = k_ref.shape[2] // tk
    ones_rows = jnp.ones((V_ROWS - ATT_HD, tk), BF16)
    for g in range(ATT_KV_HEADS):
        qts = [qt_ref[0, (g * ATT_RATIO + r) * ATT_HD:(g * ATT_RATIO + r + 1) * ATT_HD, :]
               for r in range(ATT_RATIO)]
        acc_scr[...] = jnp.zeros_like(acc_scr)

        def scores(t, slot):
            kt = k_ref[0, g, pl.ds(pl.multiple_of(t * tk, tk), tk), :]
            tops = []
            for r in range(ATT_RATIO):
                s = jnp.dot(kt, qts[r], preferred_element_type=F32)
                s_scr[slot, r] = s
                tops.append(jnp.max(s, axis=0, keepdims=True))
            return tuple(tops)

        def softmax_pv(t, slot, ms, tops):
            vt = jnp.concatenate([vt_ref[0, g, :, pl.ds(pl.multiple_of(t * tk, tk), tk)], ones_rows], axis=0)
            new_ms = [jnp.maximum(ms[r], tops[r]) for r in range(ATT_RATIO)]
            ps = [jnp.exp2(s_scr[slot, r] - new_ms[r]).astype(BF16) for r in range(ATT_RATIO)]
            for r in range(ATT_RATIO):
                acc_scr[r] = (jnp.exp2(ms[r] - new_ms[r]) * acc_scr[r]
                              + jnp.dot(vt, ps[r], preferred_element_type=F32))
            return tuple(new_ms)

        def quad_step(i, carry):
            ms, top0, top1 = carry
            t = i * ATT_SLOTS
            top2 = scores(t + 2, 2)
            top3 = scores(t + 3, 3)
            ms = softmax_pv(t, 0, ms, top0)
            ms = softmax_pv(t + 1, 1, ms, top1)
            top0 = scores(t + 4, 0)
            top1 = scores(t + 5, 1)
            ms = softmax_pv(t + 2, 2, ms, top2)
            return softmax_pv(t + 3, 3, ms, top3), top0, top1

        tops = {t: scores(t, t) for t in range(min(2, nkv))}
        ms = tuple(jnp.full((1, tq), NEG_BIG, F32) for _ in range(ATT_RATIO))
        quads = max(nkv - 2, 0) // ATT_SLOTS
        if quads:
            ms, top0, top1 = lax.fori_loop(0, quads, quad_step, (ms, tops[0], tops[1]))
            tops = {quads * ATT_SLOTS: top0, quads * ATT_SLOTS + 1: top1}
        for t in range(quads * ATT_SLOTS, nkv):
            if t % 2 == 0:
                for ahead in (t + 2, t + 3):
                    if ahead < nkv:
                        tops[ahead] = scores(ahead, ahead % ATT_SLOTS)
            ms = softmax_pv(t, t % ATT_SLOTS, ms, tops.pop(t))
        for r in range(ATT_RATIO):
            hd = g * ATT_RATIO + r
            a = acc_scr[r]
            ot_scr[hd * ATT_HD:(hd + 1) * ATT_HD, :] = (a[0:ATT_HD] / a[ATT_HD:ATT_HD + 1]).astype(BF16)
    out = lax.dot_general(ot_scr[...], wo_ref[...], (((0,), (0,)), ((), ())), preferred_element_type=F32)
    o_ref[0] = x_ref[0] + gate_ref[0] * out


def _attention(qt, k4, vt, x, gate, w_o, batch, seq_len, tq, tk):
    per_batch = gate.shape[0] > 1
    nkeys = k4.shape[2]
    x3 = x.reshape(batch, seq_len, D_MODEL)
    out = pl.pallas_call(
        functools.partial(_attn_body, tq, tk),
        grid=(batch, seq_len // tq),
        in_specs=[
            pl.BlockSpec((1, ATT_Q, tq), lambda b, i: (b, 0, i)),
            pl.BlockSpec((1, ATT_KV_HEADS, nkeys, ATT_HD), lambda b, i: (b, 0, 0, 0)),
            pl.BlockSpec((1, ATT_KV_HEADS, ATT_HD, nkeys), lambda b, i: (b, 0, 0, 0)),
            pl.BlockSpec((1, tq, D_MODEL), lambda b, i: (b, i, 0)),
            pl.BlockSpec((1, 1, D_MODEL), lambda b, i: (b if per_batch else 0, 0, 0)),
            pl.BlockSpec((ATT_Q, D_MODEL), lambda b, i: (0, 0)),
        ],
        out_specs=pl.BlockSpec((1, tq, D_MODEL), lambda b, i: (b, i, 0)),
        out_shape=jax.ShapeDtypeStruct((batch, seq_len, D_MODEL), F32),
        scratch_shapes=[pltpu.VMEM((ATT_RATIO, V_ROWS, tq), F32), pltpu.VMEM((ATT_SLOTS, ATT_RATIO, tk, tq), F32),
                        pltpu.VMEM((ATT_Q, tq), BF16)],
        compiler_params=_cparams(("arbitrary", "arbitrary")),
        name="attention",
    )(qt, k4, vt, x3, gate, w_o)
    return out.reshape(batch * seq_len, D_MODEL)


def _keys_values(k4, vt, cache_k, cache_v):
    k4 = jnp.concatenate([jnp.swapaxes(cache_k, 1, 2).astype(BF16), k4], axis=2)
    vt = jnp.concatenate([jnp.transpose(cache_v, (0, 2, 3, 1)).astype(BF16), vt], axis=3)
    return k4, vt


def _trunk(x, mods, caches, P, batch, seq_len, tm, tq, tk):
    rope = None if caches is None else _rope_tables(seq_len)
    outs = {}
    sh1, sc1, g1, sh2, sc2, g2 = mods[0]
    qkv, rest = _hyb_in_proj(x, P['norm_mix'][0:1], sc1, sh1, P['hyb_w_in'], P['ssm_conv_w'][0],
                             P['ssm_conv_b'][0], seq_len, tm, 512)
    s_ret0 = None if caches is None else caches[0][:, 0]
    s_ssm0 = None if caches is None else caches[1][:, 0]
    y, s_ret, s_ssm = _hybrid_mixer(qkv, rest, P['ssm_dt_bias'][0], P['ssm_a_log'][0], P['ret_log_decay'][0],
                                    P['ssm_d'][0], s_ret0, s_ssm0, batch, seq_len)
    outs['ret'] = s_ret[:, None]
    outs['ssm'] = s_ssm[:, None]
    x = _hyb_out(y, rest, x, g1, P['ret_gn_gain'][0:1], P['ssm_norm_gain'][0:1], P['hyb_w_out'], seq_len, tm)
    x = _conv_ffn(x, P['norm_ffn'][0:1], sc2, sh2, g2, P['ffn_w_up'][0], P['ffn_conv_w'][0], P['ffn_conv_b'][0],
                  P['ffn_w_down'][0], seq_len, tm, 512)
    sh1, sc1, g1, sh2, sc2, g2 = mods[1]
    res = _qkv_proj(x, P['norm_mix'][1:2], sc1, sh1, P['attn_w_qkv'], P['attn_q_gain'][0], P['attn_k_gain'][0],
                    rope, seq_len, tm)
    if caches is None:
        qt, k4, vt, kf, vf = res
        outs['k'] = kf.reshape(batch, 1, seq_len, ATT_KV_HEADS, ATT_HD)
        outs['v'] = vf.reshape(batch, 1, seq_len, ATT_KV_HEADS, ATT_HD)
    else:
        qt, k4, vt = res
        k4, vt = _keys_values(k4, vt, caches[2][:, 0], caches[3][:, 0])
    x = _attention(qt, k4, vt, x, g1, P['attn_w_o'], batch, seq_len, tq, tk)
    x = _conv_ffn(x, P['norm_ffn'][1:2], sc2, sh2, g2, P['ffn_w_up'][1], P['ffn_conv_w'][1], P['ffn_conv_b'][1],
                  P['ffn_w_down'][1], seq_len, tm, 512)
    return x, outs


def kernel(x_prompt, x_sample, state_ret, state_ssm, cache_attn_k, cache_attn_v, c, c_ctx, w_mod, b_mod, norm_mix,
           norm_ffn, ffn_w_up, ffn_conv_w, ffn_conv_b, ffn_w_down, hyb_w_in, hyb_w_out, ret_log_decay, ret_gn_gain,
           ssm_conv_w, ssm_conv_b, ssm_a_log, ssm_dt_bias, ssm_d, ssm_norm_gain, attn_w_qkv, attn_q_gain,
           attn_k_gain, attn_w_o):
    batch, seq, _ = x_prompt.shape
    dec_batch, dec_seq, _ = x_sample.shape
    depth = w_mod.shape[0]

    rows = -(-(dec_batch + 1) // SUBLANES) * SUBLANES
    cond = jnp.concatenate([c, c_ctx[None, :], jnp.zeros((rows - dec_batch - 1, D_MODEL), F32)], axis=0)
    mod = _modulation(cond, w_mod, b_mod).reshape(depth, rows, 6, 1, D_MODEL)
    mods_sample = [[mod[l, 0:dec_batch, t] for t in range(6)] for l in range(depth)]
    mods_prompt = [[mod[l, dec_batch:dec_batch + 1, t] for t in range(6)] for l in range(depth)]

    P = {
        'norm_mix': norm_mix, 'norm_ffn': norm_ffn,
        'hyb_w_in': hyb_w_in[0].astype(BF16),
        'hyb_w_out': hyb_w_out[0].astype(BF16),
        'ret_log_decay': ret_log_decay, 'ret_gn_gain': ret_gn_gain,
        'ssm_conv_w': ssm_conv_w, 'ssm_conv_b': ssm_conv_b, 'ssm_a_log': ssm_a_log, 'ssm_dt_bias': ssm_dt_bias,
        'ssm_d': ssm_d, 'ssm_norm_gain': ssm_norm_gain,
        'attn_w_qkv': attn_w_qkv[0].astype(BF16), 'attn_q_gain': attn_q_gain, 'attn_k_gain': attn_k_gain,
        'attn_w_o': attn_w_o[0].astype(BF16),
        'ffn_w_up': ffn_w_up.astype(BF16), 'ffn_conv_w': ffn_conv_w, 'ffn_conv_b': ffn_conv_b,
        'ffn_w_down': ffn_w_down.astype(BF16),
    }

    y_prompt, outs = _trunk(x_prompt.reshape(batch * seq, D_MODEL), mods_prompt, None, P, batch, seq,
                            tm=min(256, seq), tq=min(256, seq), tk=min(256, seq))
    caches = (state_ret, state_ssm, cache_attn_k, cache_attn_v)
    y_sample, _ = _trunk(x_sample.reshape(dec_batch * dec_seq, D_MODEL), mods_sample, caches, P, dec_batch, dec_seq,
                         tm=512, tq=256, tk=256)
    return (y_prompt.reshape(batch, seq, D_MODEL), y_sample.reshape(dec_batch, dec_seq, D_MODEL),
            outs['ret'], outs['ssm'], outs['k'], outs['v'])
```

```python
import functools

import jax
import jax.numpy as jnp
from jax import lax
from jax.experimental import pallas as pl
from jax.experimental.pallas import tpu as pltpu

F32 = jnp.float32
BF16 = jnp.bfloat16
HIGHEST = lax.Precision.HIGHEST

D_MODEL = 1024
EPS = 1e-6
GRID_W = 64
CHUNK = 128
RET_HEADS = 8
RET_DK = 64
RET_DV = 128
RET_Q = RET_HEADS * RET_DK
RET_V = RET_HEADS * RET_DV
SSM_D_INNER = 1024
SSM_HEADDIM = 64
SSM_HEADS = 16
SSM_GROUPS = 4
SSM_RATIO = 4
SSM_STATE = 128
SSM_BC = SSM_GROUPS * SSM_STATE
SSM_XBC = SSM_D_INNER + 2 * SSM_BC
HYB_QKV = 2 * RET_Q + RET_V
HYB_BF = HYB_QKV + 2 * SSM_BC
XS_COL = 2 * RET_V
DT_COL = XS_COL + SSM_D_INNER
HYB_REST = DT_COL + 128
HYB_MIX = RET_V + SSM_D_INNER
ATT_HEADS = 16
ATT_KV_HEADS = 4
ATT_RATIO = 4
ATT_HD = 64
ATT_Q = ATT_HEADS * ATT_HD
ATT_KV = ATT_KV_HEADS * ATT_HD
ROT_FREQS = ATT_HD // 4
ROPE_BASE = 10000.0
FFN_HIDDEN = 2816

LANES = 128
SUBLANES = 8
MXU_COLS = 256
VMEM_LIMIT = 56 * 1024 * 1024
NEG_BIG = -1e30
Q_SCALE = ATT_HD ** -0.5 * 1.4426950408889634


def _cparams(sem):
    return pltpu.CompilerParams(dimension_semantics=sem, vmem_limit_bytes=VMEM_LIMIT)


def _norm_mod(x, gain, sc, sh):
    ms = jnp.mean(x * x, axis=-1, keepdims=True)
    return x * lax.rsqrt(ms + EPS) * gain * (1.0 + sc) + sh


def _vec_spec(width, tiles_per_batch, per_batch, grid_rank):
    if grid_rank == 1:
        return pl.BlockSpec((1, 1, width), lambda i: ((i // tiles_per_batch) if per_batch else 0, 0, 0))
    return pl.BlockSpec((1, 1, width), lambda i, j: ((i // tiles_per_batch) if per_batch else 0, 0, 0))


def _mod_body(c_ref, w_ref, b_ref, o_ref):
    cs = jax.nn.silu(c_ref[...])
    o_ref[0] = jnp.dot(cs, w_ref[0], precision=HIGHEST, preferred_element_type=F32) + b_ref[0]


def _modulation(cond, w_mod, b_mod):
    depth, _, n = w_mod.shape
    rows = cond.shape[0]
    tn = n // 4
    return pl.pallas_call(
        _mod_body,
        grid=(depth, n // tn),
        in_specs=[
            pl.BlockSpec((rows, D_MODEL), lambda l, j: (0, 0)),
            pl.BlockSpec((1, D_MODEL, tn), lambda l, j: (l, 0, j)),
            pl.BlockSpec((1, 1, tn), lambda l, j: (l, 0, j)),
        ],
        out_specs=pl.BlockSpec((1, rows, tn), lambda l, j: (l, 0, j)),
        out_shape=jax.ShapeDtypeStruct((depth, rows, n), F32),
        compiler_params=_cparams(("arbitrary", "arbitrary")),
        name="modulation",
    )(cond, w_mod, b_mod.reshape(depth, 1, n))


def _chunks(total, width):
    return [(s, min(width, total - s)) for s in range(0, total, width)]


def _resident(shape):
    return pl.BlockSpec(shape, lambda *_: (0,) * len(shape), pipeline_mode=pl.Buffered(1))


def _norm_mod_halo(tiles_per_seq, tm, x_ref, xn_ref, xp_ref, gain_ref, sc_ref, sh_ref, h_scr):
    gain, sc, sh = gain_ref[...], sc_ref[0], sh_ref[0]
    h_scr[0:tm] = _norm_mod(x_ref[...], gain, sc, sh).astype(BF16)
    pos = pl.program_id(0) % tiles_per_seq
    hn = jnp.where(pos == tiles_per_seq - 1, 0.0, _norm_mod(xn_ref[0], gain, sc, sh))
    hp = jnp.where(pos == 0, 0.0, _norm_mod(xp_ref[0], gain, sc, sh))
    h_scr[tm:tm + 2 * SUBLANES] = jnp.concatenate([hn, hp], axis=0).astype(BF16)


def _dwconv3_rows(u, cw, cb, tm):
    rows = tm + 2 * SUBLANES
    c = pltpu.roll(u, 1, 0) * cw[0:1] + u * cw[1:2] + pltpu.roll(u, rows - 1, 0) * cw[2:3] + cb
    return c[0:tm]


def _halo_specs(m, tm):
    m8 = m // SUBLANES
    r8 = tm // SUBLANES
    return [pl.BlockSpec((1, SUBLANES, D_MODEL), lambda i: (jnp.minimum((i + 1) * r8, m8 - 1), 0, 0)),
            pl.BlockSpec((1, SUBLANES, D_MODEL), lambda i: (jnp.maximum(i * r8 - 1, 0), 0, 0))]


def _hyb_in_body(tiles_per_seq, tm, tn, x_ref, xn_ref, xp_ref, gain_ref, sc_ref, sh_ref, w_ref, cw_ref, cb_ref,
                 bf_ref, rest_ref, h_scr):
    _norm_mod_halo(tiles_per_seq, tm, x_ref, xn_ref, xp_ref, gain_ref, sc_ref, sh_ref, h_scr)
    h_all = h_scr[...]
    h = h_scr[0:tm]
    w_g = HYB_QKV
    w_x = w_g + 2 * RET_V
    w_bc = w_x + SSM_D_INNER
    w_dt = w_bc + 2 * SSM_BC

    def project(wcol, width, conv_col):
        wcols = w_ref[:, wcol:wcol + width]
        if conv_col is None:
            return jnp.dot(h, wcols, preferred_element_type=F32)
        u = jnp.dot(h_all, wcols, preferred_element_type=F32)
        return jax.nn.silu(_dwconv3_rows(u, cw_ref[:, conv_col:conv_col + width],
                                         cb_ref[:, conv_col:conv_col + width], tm))

    for col, width in _chunks(HYB_BF, tn):
        if col >= HYB_QKV:
            out = project(w_bc + col - HYB_QKV, width, SSM_D_INNER + col - HYB_QKV)
        else:
            out = project(col, width, None)
            if RET_Q <= col < 2 * RET_Q:
                out = out * (RET_DK ** -0.5)
        bf_ref[:, col:col + width] = out.astype(BF16)
    for col, width in _chunks(DT_COL, tn):
        rest_ref[:, col:col + width] = project(w_g + col, width, col - XS_COL if col >= XS_COL else None)
    dt = jnp.dot(h, w_ref[:, w_dt:w_dt + SSM_HEADS], preferred_element_type=F32)
    rest_ref[:, DT_COL:HYB_REST] = jnp.concatenate([dt, jnp.zeros((tm, HYB_REST - DT_COL - SSM_HEADS), F32)], axis=1)


def _hyb_in_proj(x, gain, sc, sh, w, conv_w, conv_b, seq_len, tm, tn):
    m = x.shape[0]
    per_batch = sc.shape[0] > 1
    tpb = seq_len // tm
    assert HYB_QKV % tn == 0 and XS_COL % tn == 0 and DT_COL % tn == 0
    x8 = x.reshape(m // SUBLANES, SUBLANES, D_MODEL)
    return pl.pallas_call(
        functools.partial(_hyb_in_body, tpb, tm, tn),
        grid=(m // tm,),
        in_specs=[
            pl.BlockSpec((tm, D_MODEL), lambda i: (i, 0)),
            *_halo_specs(m, tm),
            pl.BlockSpec((1, D_MODEL), lambda i: (0, 0)),
            _vec_spec(D_MODEL, tpb, per_batch, 1),
            _vec_spec(D_MODEL, tpb, per_batch, 1),
            _resident(w.shape),
            _resident((3, SSM_XBC)),
            _resident((1, SSM_XBC)),
        ],
        out_specs=[pl.BlockSpec((tm, HYB_BF), lambda i: (i, 0)), pl.BlockSpec((tm, HYB_REST), lambda i: (i, 0))],
        out_shape=[jax.ShapeDtypeStruct((m, HYB_BF), BF16), jax.ShapeDtypeStruct((m, HYB_REST), F32)],
        scratch_shapes=[pltpu.VMEM((tm + 2 * SUBLANES, D_MODEL), BF16)],
        compiler_params=_cparams(("arbitrary",)),
        name="hybrid_in_proj",
    )(x, x8, x8, gain, sc, sh, w, conv_w, conv_b.reshape(1, SSM_XBC))


def _ffn_body(tiles_per_seq, tm, th, x_ref, xn_ref, xp_ref, gain_ref, sc_ref, sh_ref, gate_ref,
              wu_ref, cw_ref, cb_ref, wd_ref, o_ref, h_scr, act_scr):
    _norm_mod_halo(tiles_per_seq, tm, x_ref, xn_ref, xp_ref, gain_ref, sc_ref, sh_ref, h_scr)
    h = h_scr[...]

    def conv_up(col, width):
        u = jnp.dot(h, wu_ref[:, col:col + width], preferred_element_type=F32)
        return _dwconv3_rows(u, cw_ref[:, col:col + width], cb_ref[:, col:col + width], tm)

    for col, width in _chunks(FFN_HIDDEN, th):
        act = jax.nn.silu(conv_up(col, width)) * conv_up(FFN_HIDDEN + col, width)
        act_scr[:, col:col + width] = act.astype(BF16)
    out = jnp.dot(act_scr[...], wd_ref[...], preferred_element_type=F32)
    o_ref[...] = x_ref[...] + gate_ref[0] * out


def _conv_ffn(x, gain, sc, sh, gate, w_up, conv_w, conv_b, w_down, seq_len, tm, th):
    m = x.shape[0]
    per_batch = sc.shape[0] > 1
    tpb = seq_len // tm
    x8 = x.reshape(m // SUBLANES, SUBLANES, D_MODEL)
    conv_b = conv_b.reshape(1, 2 * FFN_HIDDEN)
    vec = _vec_spec(D_MODEL, tpb, per_batch, 1)
    return pl.pallas_call(
        functools.partial(_ffn_body, tpb, tm, th),
        grid=(m // tm,),
        in_specs=[
            pl.BlockSpec((tm, D_MODEL), lambda i: (i, 0)),
            *_halo_specs(m, tm),
            pl.BlockSpec((1, D_MODEL), lambda i: (0, 0)),
            vec, vec, vec,
            _resident((D_MODEL, 2 * FFN_HIDDEN)),
            _resident((3, 2 * FFN_HIDDEN)),
            _resident((1, 2 * FFN_HIDDEN)),
            _resident((FFN_HIDDEN, D_MODEL)),
        ],
        out_specs=pl.BlockSpec((tm, D_MODEL), lambda i: (i, 0)),
        out_shape=jax.ShapeDtypeStruct((m, D_MODEL), F32),
        scratch_shapes=[pltpu.VMEM((tm + 2 * SUBLANES, D_MODEL), BF16), pltpu.VMEM((tm, FFN_HIDDEN), BF16)],
        compiler_params=_cparams(("arbitrary",)),
        name="conv_ffn",
    )(x, x8, x8, gain, sc, sh, gate, w_up, conv_w, conv_b, w_down)


SSM_GW = SSM_RATIO * SSM_HEADDIM
FWD_CHUNKS_PER_STEP = 2


def _bcol(x, h):
    return jnp.broadcast_to(x[:, h:h + 1], (x.shape[0], LANES))


def _pair_sel(left, a, h):
    return jnp.where(left, _bcol(a, h), _bcol(a, h + 1))


def _expand_heads(a, spread_ref):
    hi = a.astype(BF16)
    lo = (a - hi.astype(F32)).astype(BF16)
    return (jnp.dot(hi, spread_ref[...], preferred_element_type=F32)
            + jnp.dot(lo, spread_ref[...], preferred_element_type=F32))


def _ssm_decay_row(left, efull, g):
    return jnp.concatenate([_pair_sel(left, efull, g * SSM_RATIO + 2 * m) for m in range(SSM_RATIO // 2)], axis=1)


def _load_states(s_ret, s_ssm, sret0_ref, sssm0_ref):
    if sret0_ref is None:
        s_ret[...] = jnp.zeros_like(s_ret)
        s_ssm[...] = jnp.zeros_like(s_ssm)
    else:
        s_ret[...] = sret0_ref[0]
        for g in range(SSM_GROUPS):
            s_ssm[g] = jnp.concatenate([sssm0_ref[0, g * SSM_RATIO + r] for r in range(SSM_RATIO)], axis=1)


def _store_states(s_ret, s_ssm, sret_ref, sssm_ref):
    sret_ref[0] = s_ret[...]
    for g in range(SSM_GROUPS):
        for r in range(SSM_RATIO):
            sssm_ref[0, g * SSM_RATIO + r] = s_ssm[g][:, r * SSM_HEADDIM:(r + 1) * SSM_HEADDIM]


def _ssm_log_decay(dt_raw, dtb, alog, lane):
    dt = jax.nn.softplus(dt_raw + dtb)
    return dt, jnp.where(lane < SSM_HEADS, dt * (-jnp.exp(alog)), 0.0)


def _bwd_state_body(has_init, k_ref, v_ref, xs_ref, bm_ref, dt_ref, dtb_ref, alog_ref, spread_ref, logdec_ref,
                    *rest):
    sret0_ref = sssm0_ref = None
    if has_init:
        sret0_ref, sssm0_ref = rest[0], rest[1]
        rest = rest[2:]
    sret_in_ref, sssm_in_ref, sret_fin_ref, sssm_fin_ref, s_ret, s_ssm, wcol = rest
    i = pl.program_id(1)
    n = pl.num_programs(1)
    ii = lax.broadcasted_iota(jnp.int32, (CHUNK, CHUNK), 0)
    jj = lax.broadcasted_iota(jnp.int32, (CHUNK, CHUNK), 1)
    lane = lax.broadcasted_iota(jnp.int32, (1, LANES), 1)
    left = lane < SSM_HEADDIM

    @pl.when(i == 0)
    def _():
        _load_states(s_ret, s_ssm, sret0_ref, sssm0_ref)
        for h in range(RET_HEADS):
            wcol[h] = jnp.exp(ii.astype(F32) * logdec_ref[1, h])

    cps = sret_in_ref.shape[1]
    full = jnp.full((1, LANES), float(CHUNK), F32)
    prep = []
    for c in range(cps):
        rows = slice(c * CHUNK, (c + 1) * CHUNK)
        dt, la = _ssm_log_decay(dt_ref[rows], dtb_ref[...], alog_ref[...], lane)
        rc = jnp.dot((jj >= ii).astype(F32), la, precision=HIGHEST, preferred_element_type=F32)
        last = rc[0:1]
        prep.append((_expand_heads(jnp.exp(last - rc) * dt, spread_ref), jnp.exp(last)))

    for c in reversed(range(cps)):
        rows = slice(c * CHUNK, (c + 1) * CHUNK)
        sret_in_ref[0, c] = s_ret[...].astype(BF16)
        sssm_in_ref[0, c] = s_ssm[...].astype(BF16)
        for h in range(RET_HEADS):
            kh = k_ref[rows, h * RET_DK:(h + 1) * RET_DK]
            vw = (v_ref[rows, h * RET_DV:(h + 1) * RET_DV].astype(F32) * wcol[h]).astype(BF16)
            upd = lax.dot_general(kh, vw, (((0,), (0,)), ((), ())), preferred_element_type=F32)
            s_ret[h] = s_ret[h] * jnp.exp(full * logdec_ref[1, h]) + upd
        wx, efull = prep[c]
        for g in range(SSM_GROUPS):
            vw = (xs_ref[rows, g * SSM_GW:(g + 1) * SSM_GW] * wx[:, g * SSM_GW:(g + 1) * SSM_GW]).astype(BF16)
            bg = bm_ref[rows, g * SSM_STATE:(g + 1) * SSM_STATE]
            upd = lax.dot_general(bg, vw, (((0,), (0,)), ((), ())), preferred_element_type=F32)
            s_ssm[g] = s_ssm[g] * _ssm_decay_row(left, efull, g) + upd

    @pl.when(i == n - 1)
    def _():
        _store_states(s_ret, s_ssm, sret_fin_ref, sssm_fin_ref)


def _hyb_fwd_body(has_init, qkv_ref, xs_ref, bc_ref, dt_ref, sret_in_ref, sssm_in_ref, dtb_ref, alog_ref, dsk_ref,
                  spread_ref, logdec_ref, *rest):
    sret0_ref = sssm0_ref = None
    if has_init:
        sret0_ref, sssm0_ref = rest[0], rest[1]
        rest = rest[2:]
    y_ref, sret_fin_ref, sssm_fin_ref, s_ret, s_ssm, dcomb, ein0, ein1, wcol = rest
    i = pl.program_id(1)
    n = pl.num_programs(1)
    ii = lax.broadcasted_iota(jnp.int32, (CHUNK, CHUNK), 0)
    jj = lax.broadcasted_iota(jnp.int32, (CHUNK, CHUNK), 1)
    lane = lax.broadcasted_iota(jnp.int32, (1, LANES), 1)
    left = lane < SSM_HEADDIM
    lower = jj <= ii

    @pl.when(i == 0)
    def _():
        _load_states(s_ret, s_ssm, sret0_ref, sssm0_ref)
        fi, fj = ii.astype(F32), jj.astype(F32)
        for h in range(RET_HEADS):
            la0, la1 = logdec_ref[0, h], logdec_ref[1, h]
            dec = jnp.exp(jnp.where(lower, (fi - fj) * la0, (fj - fi) * la1))
            dcomb[h] = jnp.where(ii == jj, 2.0, dec)
            ein0[h] = jnp.exp((fi + 1.0) * la0)
            ein1[h] = jnp.exp((CHUNK - fi) * la1)
            wcol[h] = jnp.exp((CHUNK - 1.0 - fi) * la0)

    for c in range(sret_in_ref.shape[1]):
        _fwd_chunk(c, qkv_ref, xs_ref, bc_ref, dt_ref, sret_in_ref, sssm_in_ref, dtb_ref, alog_ref, dsk_ref,
                   spread_ref, logdec_ref, y_ref, s_ret, s_ssm, dcomb, ein0, ein1, wcol)

    @pl.when(i == n - 1)
    def _():
        _store_states(s_ret, s_ssm, sret_fin_ref, sssm_fin_ref)


def _fwd_chunk(c, qkv_ref, xs_ref, bc_ref, dt_ref, sret_in_ref, sssm_in_ref, dtb_ref, alog_ref, dsk_ref,
               spread_ref, logdec_ref, y_ref, s_ret, s_ssm, dcomb, ein0, ein1, wcol):
    rows = pl.ds(c * CHUNK, CHUNK)
    qkv_ref, xs_ref, bc_ref, dt_ref, y_ref = (r.at[rows] for r in (qkv_ref, xs_ref, bc_ref, dt_ref, y_ref))
    ii = lax.broadcasted_iota(jnp.int32, (CHUNK, CHUNK), 0)
    jj = lax.broadcasted_iota(jnp.int32, (CHUNK, CHUNK), 1)
    lane = lax.broadcasted_iota(jnp.int32, (1, LANES), 1)
    left = lane < SSM_HEADDIM
    lower = jj <= ii

    def ret_head(h):
        return (qkv_ref[:, h * RET_DK:(h + 1) * RET_DK],
                qkv_ref[:, RET_Q + h * RET_DK:RET_Q + (h + 1) * RET_DK],
                qkv_ref[:, 2 * RET_Q + h * RET_DV:2 * RET_Q + (h + 1) * RET_DV])

    def ssm_group(g):
        return (bc_ref[:, g * SSM_STATE:(g + 1) * SSM_STATE],
                bc_ref[:, SSM_BC + g * SSM_STATE:SSM_BC + (g + 1) * SSM_STATE])

    dt_raw = dt_ref[...]
    dt0, la0 = _ssm_log_decay(dt_raw, dtb_ref[0:1], alog_ref[0:1], lane)
    dt1, la1 = _ssm_log_decay(dt_raw, dtb_ref[1:2], alog_ref[1:2], lane)
    c0 = jnp.dot(lower.astype(F32), la0, precision=HIGHEST, preferred_element_type=F32)
    c1 = jnp.dot((jj >= ii).astype(F32), la1, precision=HIGHEST, preferred_element_type=F32)

    ret_scores, ret_inter = [], []
    for h in range(RET_HEADS):
        qh, kh, _ = ret_head(h)
        ret_scores.append(lax.dot_general(qh, kh, (((1,), (1,)), ((), ())), preferred_element_type=F32))
        states = jnp.concatenate([s_ret[h].astype(BF16), sret_in_ref[0, c, h]], axis=1)
        ret_inter.append(jnp.dot(qh, states, preferred_element_type=F32))
    ssm_scores, ssm_inter = [], []
    for g in range(SSM_GROUPS):
        bg, cg = ssm_group(g)
        ssm_scores.append(lax.dot_general(cg, bg, (((1,), (1,)), ((), ())), preferred_element_type=F32))
        states = jnp.concatenate([s_ssm[g].astype(BF16), sssm_in_ref[0, c, g]], axis=1)
        ssm_inter.append(jnp.dot(cg, states, preferred_element_type=F32))

    full = jnp.full((1, LANES), float(CHUNK), F32)
    for h in range(RET_HEADS):
        _, kh, vh = ret_head(h)
        y = jnp.dot((ret_scores[h] * dcomb[h]).astype(BF16), vh, preferred_element_type=F32)
        inter = ret_inter[h]
        y_ref[:, h * RET_DV:(h + 1) * RET_DV] = y + inter[:, 0:RET_DV] * ein0[h] + inter[:, RET_DV:] * ein1[h]
        vw = (vh.astype(F32) * wcol[h]).astype(BF16)
        upd = lax.dot_general(kh, vw, (((0,), (0,)), ((), ())), preferred_element_type=F32)
        s_ret[h] = s_ret[h] * jnp.exp(full * logdec_ref[0, h]) + upd

    c0kt, c1kt = (c0 - jnp.log(dt0)).T, (c1 - jnp.log(dt1)).T
    dst = (dt0 + dt1).T
    last0 = c0[CHUNK - 1:CHUNK]
    w0x = _expand_heads(jnp.exp(last0 - c0) * dt0, spread_ref)
    efull0 = jnp.exp(last0)
    diag = jj == ii
    for g in range(SSM_GROUPS):
        bg, _ = ssm_group(g)
        s = ssm_scores[g]
        inter = ssm_inter[g]
        vws = []
        for m in range(SSM_RATIO // 2):
            ha = g * SSM_RATIO + 2 * m
            ps, c0bs, c1bs = [], [], []
            for h in (ha, ha + 1):
                c0b, c1b = _bcol(c0, h), _bcol(c1, h)
                arg = jnp.where(lower, c0b - c0kt[h:h + 1], c1b - c1kt[h:h + 1])
                ps.append((s * jnp.where(diag, dst[h:h + 1], jnp.exp(arg))).astype(BF16))
                c0bs.append(c0b)
                c1bs.append(c1b)
            col = (g * 2 + m) * LANES
            xs_pair = xs_ref[:, col:col + LANES]
            vals = jnp.concatenate([jnp.where(left, xs_pair, 0.0), jnp.where(left, 0.0, xs_pair)], axis=0)
            y = jnp.dot(jnp.concatenate(ps, axis=1), vals.astype(BF16), preferred_element_type=F32)
            y = y + inter[:, m * LANES:(m + 1) * LANES] * jnp.exp(jnp.where(left, c0bs[0], c0bs[1]))
            y = y + (inter[:, SSM_GW + m * LANES:SSM_GW + (m + 1) * LANES]
                     * jnp.exp(jnp.where(left, c1bs[0], c1bs[1])))
            y_ref[:, RET_V + col:RET_V + col + LANES] = y + dsk_ref[:, col:col + LANES] * xs_pair
            vws.append((xs_pair * w0x[:, col:col + LANES]).astype(BF16))
        upd = lax.dot_general(bg, jnp.concatenate(vws, axis=1), (((0,), (0,)), ((), ())),
                              preferred_element_type=F32)
        s_ssm[g] = s_ssm[g] * _ssm_decay_row(left, efull0, g) + upd


def _hybrid_mixer(qkv, rest, dt_bias, a_log, log_decay, d_skip, s_ret0, s_ssm0, batch, seq_len):
    m = qkv.shape[0]
    n = seq_len // CHUNK
    has_init = s_ret0 is not None
    dt_blk = DT_COL // LANES
    pad = lambda v: jnp.pad(v.astype(F32), ((0, 0), (0, LANES - v.shape[1])))
    dtb, alog = pad(dt_bias), pad(a_log)
    logdec = log_decay.astype(F32)
    spread = (lax.broadcasted_iota(jnp.int32, (LANES, SSM_D_INNER), 0)
              == lax.broadcasted_iota(jnp.int32, (LANES, SSM_D_INNER), 1) // SSM_HEADDIM).astype(BF16)
    spread_spec = pl.BlockSpec((LANES, SSM_D_INNER), lambda b, i: (0, 0))
    dsk = jnp.repeat((d_skip[0] + d_skip[1]).astype(F32), SSM_HEADDIM).reshape(1, SSM_D_INNER)
    smem = pl.BlockSpec(memory_space=pltpu.SMEM)
    ret_spec = pl.BlockSpec((1, RET_HEADS, RET_DK, RET_DV), lambda b, i: (b, 0, 0, 0))
    ssm_spec = pl.BlockSpec((1, SSM_HEADS, SSM_STATE, SSM_HEADDIM), lambda b, i: (b, 0, 0, 0))
    ret_shape = jax.ShapeDtypeStruct((batch, RET_HEADS, RET_DK, RET_DV), F32)
    ssm_shape = jax.ShapeDtypeStruct((batch, SSM_HEADS, SSM_STATE, SSM_HEADDIM), F32)
    ret_scr = pltpu.VMEM((RET_HEADS, RET_DK, RET_DV), F32)
    ssm_scr = pltpu.VMEM((SSM_GROUPS, SSM_STATE, SSM_GW), F32)
    const_scr = pltpu.VMEM((RET_HEADS, CHUNK, CHUNK), F32)

    cps = next(c for c in (4, 2, 1) if n % c == 0)
    rsteps = n // cps
    rrows = cps * CHUNK

    def rev(b, i):
        return b * rsteps + rsteps - 1 - i

    in_specs = [
        pl.BlockSpec((rrows, RET_Q), lambda b, i: (rev(b, i), 1)),
        pl.BlockSpec((rrows, RET_V), lambda b, i: (rev(b, i), 1)),
        pl.BlockSpec((rrows, SSM_D_INNER), lambda b, i: (rev(b, i), XS_COL // SSM_D_INNER)),
        pl.BlockSpec((rrows, SSM_BC), lambda b, i: (rev(b, i), HYB_QKV // SSM_BC)),
        pl.BlockSpec((rrows, LANES), lambda b, i: (rev(b, i), dt_blk)),
        pl.BlockSpec((1, LANES), lambda b, i: (0, 0)),
        pl.BlockSpec((1, LANES), lambda b, i: (0, 0)),
        spread_spec,
        smem,
    ]
    args = [qkv, qkv, rest, qkv, rest, dtb[1:2], alog[1:2], spread, logdec]
    if has_init:
        in_specs += [ret_spec, ssm_spec]
        args += [s_ret0[:, 1], s_ssm0[:, 1]]
    sret_in, sssm_in, sret1, sssm1 = pl.pallas_call(
        functools.partial(_bwd_state_body, has_init),
        grid=(batch, rsteps),
        in_specs=in_specs,
        out_specs=[
            pl.BlockSpec((1, cps, RET_HEADS, RET_DK, RET_DV), lambda b, i: (b, rsteps - 1 - i, 0, 0, 0)),
            pl.BlockSpec((1, cps, SSM_GROUPS, SSM_STATE, SSM_GW), lambda b, i: (b, rsteps - 1 - i, 0, 0, 0)),
            ret_spec, ssm_spec,
        ],
        out_shape=[
            jax.ShapeDtypeStruct((batch, n, RET_HEADS, RET_DK, RET_DV), BF16),
            jax.ShapeDtypeStruct((batch, n, SSM_GROUPS, SSM_STATE, SSM_GW), BF16),
            ret_shape, ssm_shape,
        ],
        scratch_shapes=[ret_scr, ssm_scr, const_scr],
        compiler_params=_cparams(("arbitrary", "arbitrary")),
        name="hybrid_reverse_states",
    )(*args)

    fcps = FWD_CHUNKS_PER_STEP if n % FWD_CHUNKS_PER_STEP == 0 else 1
    fsteps = n // fcps
    frows = fcps * CHUNK

    def fwd(b, i):
        return b * fsteps + i

    in_specs = [
        pl.BlockSpec((frows, HYB_QKV), lambda b, i: (fwd(b, i), 0)),
        pl.BlockSpec((frows, SSM_D_INNER), lambda b, i: (fwd(b, i), XS_COL // SSM_D_INNER)),
        pl.BlockSpec((frows, 2 * SSM_BC), lambda b, i: (fwd(b, i), HYB_QKV // (2 * SSM_BC))),
        pl.BlockSpec((frows, LANES), lambda b, i: (fwd(b, i), dt_blk)),
        pl.BlockSpec((1, fcps, RET_HEADS, RET_DK, RET_DV), lambda b, i: (b, i, 0, 0, 0)),
        pl.BlockSpec((1, fcps, SSM_GROUPS, SSM_STATE, SSM_GW), lambda b, i: (b, i, 0, 0, 0)),
        pl.BlockSpec((2, LANES), lambda b, i: (0, 0)),
        pl.BlockSpec((2, LANES), lambda b, i: (0, 0)),
        pl.BlockSpec((1, SSM_D_INNER), lambda b, i: (0, 0)),
        spread_spec,
        smem,
    ]
    args = [qkv, rest, qkv, rest, sret_in, sssm_in, dtb, alog, dsk, spread, logdec]
    if has_init:
        in_specs += [ret_spec, ssm_spec]
        args += [s_ret0[:, 0], s_ssm0[:, 0]]
    y, sret0, sssm0 = pl.pallas_call(
        functools.partial(_hyb_fwd_body, has_init),
        grid=(batch, fsteps),
        in_specs=in_specs,
        out_specs=[pl.BlockSpec((frows, HYB_MIX), lambda b, i: (fwd(b, i), 0)), ret_spec, ssm_spec],
        out_shape=[jax.ShapeDtypeStruct((m, HYB_MIX), F32), ret_shape, ssm_shape],
        scratch_shapes=[ret_scr, ssm_scr, const_scr, const_scr, const_scr, const_scr],
        compiler_params=_cparams(("arbitrary", "arbitrary")),
        name="hybrid_forward_mix",
    )(*args)
    return y, jnp.stack([sret0, sret1], axis=1), jnp.stack([sssm0, sssm1], axis=1)


def _hyb_out_body(y_ref, g_ref, z_ref, x_ref, gate_ref, gn_ref, ng_ref, w_ref, o_ref):
    y = y_ref[...]
    parts = []
    for h in range(RET_HEADS):
        yh = y[:, h * RET_DV:(h + 1) * RET_DV]
        mu = jnp.mean(yh, axis=-1, keepdims=True)
        var = jnp.mean(jnp.square(yh - mu), axis=-1, keepdims=True)
        parts.append((yh - mu) * lax.rsqrt(var + EPS))
    y_ret = jax.nn.silu(g_ref[...]) * (jnp.concatenate(parts, axis=1) * gn_ref[...])
    yz = y[:, RET_V:] * jax.nn.silu(z_ref[...])
    y_ssm = yz * lax.rsqrt(jnp.mean(yz * yz, axis=-1, keepdims=True) + EPS) * ng_ref[...]
    out = jnp.dot(y_ret.astype(BF16), w_ref[0:RET_V], preferred_element_type=F32)
    out = out + jnp.dot(y_ssm.astype(BF16), w_ref[RET_V:HYB_MIX], preferred_element_type=F32)
    o_ref[...] = x_ref[...] + gate_ref[0] * out


def _hyb_out(y, rest, x, gate, gn_gain, norm_gain, w_out, seq_len, tm):
    m = x.shape[0]
    per_batch = gate.shape[0] > 1
    tpb = seq_len // tm
    return pl.pallas_call(
        _hyb_out_body,
        grid=(m // tm,),
        in_specs=[
            pl.BlockSpec((tm, HYB_MIX), lambda i: (i, 0)),
            pl.BlockSpec((tm, RET_V), lambda i: (i, 0)),
            pl.BlockSpec((tm, SSM_D_INNER), lambda i: (i, 1)),
            pl.BlockSpec((tm, D_MODEL), lambda i: (i, 0)),
            _vec_spec(D_MODEL, tpb, per_batch, 1),
            pl.BlockSpec((1, RET_V), lambda i: (0, 0)),
            pl.BlockSpec((1, SSM_D_INNER), lambda i: (0, 0)),
            pl.BlockSpec((HYB_MIX, D_MODEL), lambda i: (0, 0)),
        ],
        out_specs=pl.BlockSpec((tm, D_MODEL), lambda i: (i, 0)),
        out_shape=jax.ShapeDtypeStruct((m, D_MODEL), F32),
        compiler_params=_cparams(("arbitrary",)),
        name="hybrid_out_proj",
    )(y, rest, rest, x, gate, gn_gain, norm_gain, w_out)


def _head_rms(t, seg_ref):
    outs = []
    for c in range(t.shape[1] // LANES):
        tc = t[:, c * LANES:(c + 1) * LANES]
        sq = tc * tc
        hi = sq.astype(BF16)
        lo = (sq - hi.astype(F32)).astype(BF16)
        ssum = (jnp.dot(hi, seg_ref[...], preferred_element_type=F32)
                + jnp.dot(lo, seg_ref[...], preferred_element_type=F32))
        outs.append(tc * lax.rsqrt(ssum * (1.0 / ATT_HD) + EPS))
    return outs


def _rope(tc, cos, sin, lane):
    fwd = pltpu.roll(tc, LANES - ROT_FREQS, 1)
    bwd = pltpu.roll(tc, ROT_FREQS, 1)
    return tc * cos + jnp.where(lane % (2 * ROT_FREQS) < ROT_FREQS, fwd, bwd) * sin


def _qkv_body(has_rope, x_ref, gain_ref, sc_ref, sh_ref, w_ref, seg_ref, qg_ref, kg_ref, *rest):
    if has_rope:
        cos_ref, sin_ref, q_ref, k_ref, v_ref = rest
    else:
        q_ref, k_ref, v_ref, kf_ref, vf_ref = rest
    h = _norm_mod(x_ref[...], gain_ref[...], sc_ref[0], sh_ref[0]).astype(BF16)
    lane = lax.broadcasted_iota(jnp.int32, (1, LANES), 1)

    def project(col):
        t = jnp.dot(h, w_ref[:, col:col + MXU_COLS], preferred_element_type=F32)
        return [t[:, c * LANES:(c + 1) * LANES] for c in range(MXU_COLS // LANES)]

    def normed(tiles, gain_ref_):
        tiles = [t * gain_ref_[...] for t in _head_rms(jnp.concatenate(tiles, axis=1), seg_ref)]
        return tiles, ([_rope(t, cos_ref[...], sin_ref[...], lane) for t in tiles] if has_rope else tiles)

    def emit_q(col, tiles):
        _, q = normed(tiles, qg_ref)
        for c, t in enumerate(q):
            row = col + c * LANES
            q_ref[0, row:row + LANES, :] = (t * Q_SCALE).T.astype(BF16)

    def emit_k(col, tiles):
        k, kr = normed(tiles, kg_ref)
        for c in range(MXU_COLS // LANES):
            lo = col + c * LANES
            if not has_rope:
                kf_ref[:, lo:lo + LANES] = k[c]
            kc = kr[c].astype(BF16)
            for a in range(LANES // ATT_HD):
                k_ref[0, lo // ATT_HD + a] = kc[:, a * ATT_HD:(a + 1) * ATT_HD]

    def emit_v(col, tiles):
        for c in range(MXU_COLS // LANES):
            lo = col + c * LANES
            if not has_rope:
                vf_ref[:, lo:lo + LANES] = tiles[c]
            vc = tiles[c].T.astype(BF16)
            for a in range(LANES // ATT_HD):
                v_ref[0, lo // ATT_HD + a] = vc[a * ATT_HD:(a + 1) * ATT_HD]

    work = ([(emit_q, col, col) for col in range(0, ATT_Q, MXU_COLS)]
            + [(emit_k, col, ATT_Q + col) for col in range(0, ATT_KV, MXU_COLS)]
            + [(emit_v, col, ATT_Q + ATT_KV + col) for col in range(0, ATT_KV, MXU_COLS)])
    tiles = project(work[0][2])
    for n, (emit, col, _) in enumerate(work):
        ahead = project(work[n + 1][2]) if n + 1 < len(work) else None
        emit(col, tiles)
        tiles = ahead


def _qkv_proj(x, gain, sc, sh, w_qkv, q_gain, k_gain, rope, seq_len, tm):
    m = x.shape[0]
    per_batch = sc.shape[0] > 1
    tpb = seq_len // tm
    has_rope = rope is not None
    seg = (lax.broadcasted_iota(jnp.int32, (LANES, LANES), 0) // ATT_HD
           == lax.broadcasted_iota(jnp.int32, (LANES, LANES), 1) // ATT_HD).astype(BF16)
    tile2 = lambda v: jnp.tile(v.astype(F32), LANES // ATT_HD).reshape(1, LANES)
    in_specs = [
        pl.BlockSpec((tm, D_MODEL), lambda i: (i, 0)),
        pl.BlockSpec((1, D_MODEL), lambda i: (0, 0)),
        _vec_spec(D_MODEL, tpb, per_batch, 1),
        _vec_spec(D_MODEL, tpb, per_batch, 1),
        pl.BlockSpec((D_MODEL, ATT_Q + 2 * ATT_KV), lambda i: (0, 0)),
        pl.BlockSpec((LANES, LANES), lambda i: (0, 0)),
        pl.BlockSpec((1, LANES), lambda i: (0, 0)),
        pl.BlockSpec((1, LANES), lambda i: (0, 0)),
    ]
    args = [x, gain, sc, sh, w_qkv, seg, tile2(q_gain), tile2(k_gain)]
    batch = m // seq_len
    out_specs = [pl.BlockSpec((1, ATT_Q, tm), lambda i: (i // tpb, 0, i % tpb)),
                 pl.BlockSpec((1, ATT_KV_HEADS, tm, ATT_HD), lambda i: (i // tpb, 0, i % tpb, 0)),
                 pl.BlockSpec((1, ATT_KV_HEADS, ATT_HD, tm), lambda i: (i // tpb, 0, 0, i % tpb))]
    out_shape = [jax.ShapeDtypeStruct((batch, ATT_Q, seq_len), BF16),
                 jax.ShapeDtypeStruct((batch, ATT_KV_HEADS, seq_len, ATT_HD), BF16),
                 jax.ShapeDtypeStruct((batch, ATT_KV_HEADS, ATT_HD, seq_len), BF16)]
    if has_rope:
        in_specs += [pl.BlockSpec((tm, LANES), lambda i: (i % tpb, 0))] * 2
        args += list(rope)
    else:
        out_specs += [pl.BlockSpec((tm, ATT_KV), lambda i: (i, 0))] * 2
        out_shape += [jax.ShapeDtypeStruct((m, ATT_KV), F32)] * 2
    return pl.pallas_call(
        functools.partial(_qkv_body, has_rope),
        grid=(m // tm,),
        in_specs=in_specs,
        out_specs=out_specs,
        out_shape=out_shape,
        compiler_params=_cparams(("arbitrary",)),
        name="attn_qkv_proj",
    )(*args)


def _rope_tables(seq_len):
    rows = seq_len // GRID_W
    row = jnp.repeat(jnp.arange(rows, dtype=F32), GRID_W)
    col = jnp.tile(jnp.arange(GRID_W, dtype=F32), rows)
    inv = ROPE_BASE ** (-jnp.arange(ROT_FREQS, dtype=F32) / ROT_FREQS)
    ar, ac = row[:, None] * inv, col[:, None] * inv
    cos = jnp.concatenate([jnp.cos(ar), jnp.cos(ar), jnp.cos(ac), jnp.cos(ac)], axis=1)
    sin = jnp.concatenate([-jnp.sin(ar), jnp.sin(ar), -jnp.sin(ac), jnp.sin(ac)], axis=1)
    return jnp.tile(cos, (1, LANES // ATT_HD)), jnp.tile(sin, (1, LANES // ATT_HD))


V_ROWS = ATT_HD + 16
ATT_SLOTS = 4


def _attn_body(tq, tk, qt_ref, k_ref, vt_ref, x_ref, gate_ref, wo_ref, o_ref, acc_scr, s_scr, ot_scr):
    nkv = k_ref.shape[2] // tk
    ones_rows = jnp.ones((V_ROWS - ATT_HD, tk), BF16)
    for g in range(ATT_KV_HEADS):
        qts = [qt_ref[0, (g * ATT_RATIO + r) * ATT_HD:(g * ATT_RATIO + r + 1) * ATT_HD, :]
               for r in range(ATT_RATIO)]
        acc_scr[...] = jnp.zeros_like(acc_scr)

        def scores(t, slot):
            kt = k_ref[0, g, pl.ds(pl.multiple_of(t * tk, tk), tk), :]
            tops = []
            for r in range(ATT_RATIO):
                s = jnp.dot(kt, qts[r], preferred_element_type=F32)
                s_scr[slot, r] = s
                tops.append(jnp.max(s, axis=0, keepdims=True))
            return tuple(tops)

        def softmax_pv(t, slot, ms, tops):
            vt = jnp.concatenate([vt_ref[0, g, :, pl.ds(pl.multiple_of(t * tk, tk), tk)], ones_rows], axis=0)
            new_ms = [jnp.maximum(ms[r], tops[r]) for r in range(ATT_RATIO)]
            ps = [jnp.exp2(s_scr[slot, r] - new_ms[r]).astype(BF16) for r in range(ATT_RATIO)]
            for r in range(ATT_RATIO):
                acc_scr[r] = (jnp.exp2(ms[r] - new_ms[r]) * acc_scr[r]
                              + jnp.dot(vt, ps[r], preferred_element_type=F32))
            return tuple(new_ms)

        def quad_step(i, carry):
            ms, top0, top1 = carry
            t = i * ATT_SLOTS
            top2 = scores(t + 2, 2)
            top3 = scores(t + 3, 3)
            ms = softmax_pv(t, 0, ms, top0)
            ms = softmax_pv(t + 1, 1, ms, top1)
            top0 = scores(t + 4, 0)
            top1 = scores(t + 5, 1)
            ms = softmax_pv(t + 2, 2, ms, top2)
            return softmax_pv(t + 3, 3, ms, top3), top0, top1

        tops = {t: scores(t, t) for t in range(min(2, nkv))}
        ms = tuple(jnp.full((1, tq), NEG_BIG, F32) for _ in range(ATT_RATIO))
        quads = max(nkv - 2, 0) // ATT_SLOTS
        if quads:
            ms, top0, top1 = lax.fori_loop(0, quads, quad_step, (ms, tops[0], tops[1]))
            tops = {quads * ATT_SLOTS: top0, quads * ATT_SLOTS + 1: top1}
        for t in range(quads * ATT_SLOTS, nkv):
            if t % 2 == 0:
                for ahead in (t + 2, t + 3):
                    if ahead < nkv:
                        tops[ahead] = scores(ahead, ahead % ATT_SLOTS)
            ms = softmax_pv(t, t % ATT_SLOTS, ms, tops.pop(t))
        for r in range(ATT_RATIO):
            hd = g * ATT_RATIO + r
            a = acc_scr[r]
            ot_scr[hd * ATT_HD:(hd + 1) * ATT_HD, :] = (a[0:ATT_HD] / a[ATT_HD:ATT_HD + 1]).astype(BF16)
    out = lax.dot_general(ot_scr[...], wo_ref[...], (((0,), (0,)), ((), ())), preferred_element_type=F32)
    o_ref[0] = x_ref[0] + gate_ref[0] * out


def _attention(qt, k4, vt, x, gate, w_o, batch, seq_len, tq, tk):
    per_batch = gate.shape[0] > 1
    nkeys = k4.shape[2]
    x3 = x.reshape(batch, seq_len, D_MODEL)
    out = pl.pallas_call(
        functools.partial(_attn_body, tq, tk),
        grid=(batch, seq_len // tq),
        in_specs=[
            pl.BlockSpec((1, ATT_Q, tq), lambda b, i: (b, 0, i)),
            pl.BlockSpec((1, ATT_KV_HEADS, nkeys, ATT_HD), lambda b, i: (b, 0, 0, 0)),
            pl.BlockSpec((1, ATT_KV_HEADS, ATT_HD, nkeys), lambda b, i: (b, 0, 0, 0)),
            pl.BlockSpec((1, tq, D_MODEL), lambda b, i: (b, i, 0)),
            pl.BlockSpec((1, 1, D_MODEL), lambda b, i: (b if per_batch else 0, 0, 0)),
            pl.BlockSpec((ATT_Q, D_MODEL), lambda b, i: (0, 0)),
        ],
        out_specs=pl.BlockSpec((1, tq, D_MODEL), lambda b, i: (b, i, 0)),
        out_shape=jax.ShapeDtypeStruct((batch, seq_len, D_MODEL), F32),
        scratch_shapes=[pltpu.VMEM((ATT_RATIO, V_ROWS, tq), F32), pltpu.VMEM((ATT_SLOTS, ATT_RATIO, tk, tq), F32),
                        pltpu.VMEM((ATT_Q, tq), BF16)],
        compiler_params=_cparams(("arbitrary", "arbitrary")),
        name="attention",
    )(qt, k4, vt, x3, gate, w_o)
    return out.reshape(batch * seq_len, D_MODEL)


def _keys_values(k4, vt, cache_k, cache_v):
    k4 = jnp.concatenate([jnp.swapaxes(cache_k, 1, 2).astype(BF16), k4], axis=2)
    vt = jnp.concatenate([jnp.transpose(cache_v, (0, 2, 3, 1)).astype(BF16), vt], axis=3)
    return k4, vt


def _trunk(x, mods, caches, P, batch, seq_len, tm, tq, tk):
    rope = None if caches is None else _rope_tables(seq_len)
    outs = {}
    sh1, sc1, g1, sh2, sc2, g2 = mods[0]
    qkv, rest = _hyb_in_proj(x, P['norm_mix'][0:1], sc1, sh1, P['hyb_w_in'], P['ssm_conv_w'][0],
                             P['ssm_conv_b'][0], seq_len, tm, 512)
    s_ret0 = None if caches is None else caches[0][:, 0]
    s_ssm0 = None if caches is None else caches[1][:, 0]
    y, s_ret, s_ssm = _hybrid_mixer(qkv, rest, P['ssm_dt_bias'][0], P['ssm_a_log'][0], P['ret_log_decay'][0],
                                    P['ssm_d'][0], s_ret0, s_ssm0, batch, seq_len)
    outs['ret'] = s_ret[:, None]
    outs['ssm'] = s_ssm[:, None]
    x = _hyb_out(y, rest, x, g1, P['ret_gn_gain'][0:1], P['ssm_norm_gain'][0:1], P['hyb_w_out'], seq_len, tm)
    x = _conv_ffn(x, P['norm_ffn'][0:1], sc2, sh2, g2, P['ffn_w_up'][0], P['ffn_conv_w'][0], P['ffn_conv_b'][0],
                  P['ffn_w_down'][0], seq_len, tm, 512)
    sh1, sc1, g1, sh2, sc2, g2 = mods[1]
    res = _qkv_proj(x, P['norm_mix'][1:2], sc1, sh1, P['attn_w_qkv'], P['attn_q_gain'][0], P['attn_k_gain'][0],
                    rope, seq_len, tm)
    if caches is None:
        qt, k4, vt, kf, vf = res
        outs['k'] = kf.reshape(batch, 1, seq_len, ATT_KV_HEADS, ATT_HD)
        outs['v'] = vf.reshape(batch, 1, seq_len, ATT_KV_HEADS, ATT_HD)
    else:
        qt, k4, vt = res
        k4, vt = _keys_values(k4, vt, caches[2][:, 0], caches[3][:, 0])
    x = _attention(qt, k4, vt, x, g1, P['attn_w_o'], batch, seq_len, tq, tk)
    x = _conv_ffn(x, P['norm_ffn'][1:2], sc2, sh2, g2, P['ffn_w_up'][1], P['ffn_conv_w'][1], P['ffn_conv_b'][1],
                  P['ffn_w_down'][1], seq_len, tm, 512)
    return x, outs


def kernel(x_prompt, x_sample, state_ret, state_ssm, cache_attn_k, cache_attn_v, c, c_ctx, w_mod, b_mod, norm_mix,
           norm_ffn, ffn_w_up, ffn_conv_w, ffn_conv_b, ffn_w_down, hyb_w_in, hyb_w_out, ret_log_decay, ret_gn_gain,
           ssm_conv_w, ssm_conv_b, ssm_a_log, ssm_dt_bias, ssm_d, ssm_norm_gain, attn_w_qkv, attn_q_gain,
           attn_k_gain, attn_w_o):
    batch, seq, _ = x_prompt.shape
    dec_batch, dec_seq, _ = x_sample.shape
    depth = w_mod.shape[0]

    rows = -(-(dec_batch + 1) // SUBLANES) * SUBLANES
    cond = jnp.concatenate([c, c_ctx[None, :], jnp.zeros((rows - dec_batch - 1, D_MODEL), F32)], axis=0)
    mod = _modulation(cond, w_mod, b_mod).reshape(depth, rows, 6, 1, D_MODEL)
    mods_sample = [[mod[l, 0:dec_batch, t] for t in range(6)] for l in range(depth)]
    mods_prompt = [[mod[l, dec_batch:dec_batch + 1, t] for t in range(6)] for l in range(depth)]

    P = {
        'norm_mix': norm_mix, 'norm_ffn': norm_ffn,
        'hyb_w_in': hyb_w_in[0].astype(BF16),
        'hyb_w_out': hyb_w_out[0].astype(BF16),
        'ret_log_decay': ret_log_decay, 'ret_gn_gain': ret_gn_gain,
        'ssm_conv_w': ssm_conv_w, 'ssm_conv_b': ssm_conv_b, 'ssm_a_log': ssm_a_log, 'ssm_dt_bias': ssm_dt_bias,
        'ssm_d': ssm_d, 'ssm_norm_gain': ssm_norm_gain,
        'attn_w_qkv': attn_w_qkv[0].astype(BF16), 'attn_q_gain': attn_q_gain, 'attn_k_gain': attn_k_gain,
        'attn_w_o': attn_w_o[0].astype(BF16),
        'ffn_w_up': ffn_w_up.astype(BF16), 'ffn_conv_w': ffn_conv_w, 'ffn_conv_b': ffn_conv_b,
        'ffn_w_down': ffn_w_down.astype(BF16),
    }

    y_prompt, outs = _trunk(x_prompt.reshape(batch * seq, D_MODEL), mods_prompt, None, P, batch, seq,
                            tm=min(256, seq), tq=min(256, seq), tk=min(256, seq))
    caches = (state_ret, state_ssm, cache_attn_k, cache_attn_v)
    y_sample, _ = _trunk(x_sample.reshape(dec_batch * dec_seq, D_MODEL), mods_sample, caches, P, dec_batch, dec_seq,
                         tm=512, tq=256, tk=256)
    return (y_prompt.reshape(batch, seq, D_MODEL), y_sample.reshape(dec_batch, dec_seq, D_MODEL),
            outs['ret'], outs['ssm'], outs['k'], outs['v'])
```

```python
import functools

import jax
import jax.numpy as jnp
from jax import lax
from jax.experimental import pallas as pl
from jax.experimental.pallas import tpu as pltpu

F32 = jnp.float32
BF16 = jnp.bfloat16
HIGHEST = lax.Precision.HIGHEST

D_MODEL = 1024
EPS = 1e-6
GRID_W = 64
CHUNK = 128
RET_HEADS = 8
RET_DK = 64
RET_DV = 128
RET_Q = RET_HEADS * RET_DK
RET_V = RET_HEADS * RET_DV
SSM_D_INNER = 1024
SSM_HEADDIM = 64
SSM_HEADS = 16
SSM_GROUPS = 4
SSM_RATIO = 4
SSM_STATE = 128
SSM_BC = SSM_GROUPS * SSM_STATE
SSM_XBC = SSM_D_INNER + 2 * SSM_BC
HYB_QKV = 2 * RET_Q + RET_V
HYB_BF = HYB_QKV + 2 * SSM_BC
XS_COL = 2 * RET_V
DT_COL = XS_COL + SSM_D_INNER
HYB_REST = DT_COL + 128
HYB_MIX = RET_V + SSM_D_INNER
ATT_HEADS = 16
ATT_KV_HEADS = 4
ATT_RATIO = 4
ATT_HD = 64
ATT_Q = ATT_HEADS * ATT_HD
ATT_KV = ATT_KV_HEADS * ATT_HD
ROT_FREQS = ATT_HD // 4
ROPE_BASE = 10000.0
FFN_HIDDEN = 2816

LANES = 128
SUBLANES = 8
MXU_COLS = 256
VMEM_LIMIT = 56 * 1024 * 1024
NEG_BIG = -1e30
Q_SCALE = ATT_HD ** -0.5 * 1.4426950408889634


def _cparams(sem):
    return pltpu.CompilerParams(dimension_semantics=sem, vmem_limit_bytes=VMEM_LIMIT)


def _norm_mod(x, gain, sc, sh):
    ms = jnp.mean(x * x, axis=-1, keepdims=True)
    return x * lax.rsqrt(ms + EPS) * gain * (1.0 + sc) + sh


def _vec_spec(width, tiles_per_batch, per_batch, grid_rank):
    if grid_rank == 1:
        return pl.BlockSpec((1, 1, width), lambda i: ((i // tiles_per_batch) if per_batch else 0, 0, 0))
    return pl.BlockSpec((1, 1, width), lambda i, j: ((i // tiles_per_batch) if per_batch else 0, 0, 0))


def _mod_body(c_ref, w_ref, b_ref, o_ref):
    cs = jax.nn.silu(c_ref[...])
    o_ref[0] = jnp.dot(cs, w_ref[0], precision=HIGHEST, preferred_element_type=F32) + b_ref[0]


def _modulation(cond, w_mod, b_mod):
    depth, _, n = w_mod.shape
    rows = cond.shape[0]
    tn = n // 4
    return pl.pallas_call(
        _mod_body,
        grid=(depth, n // tn),
        in_specs=[
            pl.BlockSpec((rows, D_MODEL), lambda l, j: (0, 0)),
            pl.BlockSpec((1, D_MODEL, tn), lambda l, j: (l, 0, j)),
            pl.BlockSpec((1, 1, tn), lambda l, j: (l, 0, j)),
        ],
        out_specs=pl.BlockSpec((1, rows, tn), lambda l, j: (l, 0, j)),
        out_shape=jax.ShapeDtypeStruct((depth, rows, n), F32),
        compiler_params=_cparams(("arbitrary", "arbitrary")),
        name="modulation",
    )(cond, w_mod, b_mod.reshape(depth, 1, n))


def _chunks(total, width):
    return [(s, min(width, total - s)) for s in range(0, total, width)]


def _resident(shape):
    return pl.BlockSpec(shape, lambda *_: (0,) * len(shape), pipeline_mode=pl.Buffered(1))


def _norm_mod_halo(seq_len, tm, x_ref, xn_ref, xp_ref, gain_ref, sc_ref, sh_ref, h_scr):
    gain, sc, sh = gain_ref[...], sc_ref[0], sh_ref[0]
    h_scr[0:tm] = _norm_mod(x_ref[...], gain, sc, sh).astype(BF16)
    if tm >= seq_len:
        h_scr[tm:tm + 2 * SUBLANES] = jnp.zeros((2 * SUBLANES, D_MODEL), BF16)
    else:
        tiles_per_seq = seq_len // tm
        pos = pl.program_id(0) % tiles_per_seq
        hn = jnp.where(pos == tiles_per_seq - 1, 0.0, _norm_mod(xn_ref[0], gain, sc, sh))
        hp = jnp.where(pos == 0, 0.0, _norm_mod(xp_ref[0], gain, sc, sh))
        h_scr[tm:tm + 2 * SUBLANES] = jnp.concatenate([hn, hp], axis=0).astype(BF16)


def _dwconv3_rows(u, cw, cb, tm, seq_len):
    rows = tm + 2 * SUBLANES
    before, after = pltpu.roll(u, 1, 0), pltpu.roll(u, rows - 1, 0)
    if tm > seq_len:
        pos = lax.broadcasted_iota(jnp.int32, u.shape, 0) % seq_len
        before = jnp.where(pos == 0, 0.0, before)
        after = jnp.where(pos == seq_len - 1, 0.0, after)
    c = before * cw[0:1] + u * cw[1:2] + after * cw[2:3] + cb
    return c[0:tm]


def _halo_specs(m, tm):
    m8 = m // SUBLANES
    r8 = tm // SUBLANES
    return [pl.BlockSpec((1, SUBLANES, D_MODEL), lambda i: (jnp.minimum((i + 1) * r8, m8 - 1), 0, 0)),
            pl.BlockSpec((1, SUBLANES, D_MODEL), lambda i: (jnp.maximum(i * r8 - 1, 0), 0, 0))]


def _hyb_in_body(seq_len, tm, tn, x_ref, xn_ref, xp_ref, gain_ref, sc_ref, sh_ref, w_ref, cw_ref, cb_ref,
                 bf_ref, rest_ref, h_scr):
    _norm_mod_halo(seq_len, tm, x_ref, xn_ref, xp_ref, gain_ref, sc_ref, sh_ref, h_scr)
    h_all = h_scr[...]
    h = h_scr[0:tm]
    w_g = HYB_QKV
    w_x = w_g + 2 * RET_V
    w_bc = w_x + SSM_D_INNER
    w_dt = w_bc + 2 * SSM_BC

    def project(wcol, width, conv_col):
        wcols = w_ref[:, wcol:wcol + width]
        if conv_col is None:
            return jnp.dot(h, wcols, preferred_element_type=F32)
        u = jnp.dot(h_all, wcols, preferred_element_type=F32)
        return jax.nn.silu(_dwconv3_rows(u, cw_ref[:, conv_col:conv_col + width],
                                         cb_ref[:, conv_col:conv_col + width], tm, seq_len))

    for col, width in _chunks(HYB_BF, tn):
        if col >= HYB_QKV:
            out = project(w_bc + col - HYB_QKV, width, SSM_D_INNER + col - HYB_QKV)
        else:
            out = project(col, width, None)
            if RET_Q <= col < 2 * RET_Q:
                out = out * (RET_DK ** -0.5)
        bf_ref[:, col:col + width] = out.astype(BF16)
    for col, width in _chunks(DT_COL, tn):
        rest_ref[:, col:col + width] = project(w_g + col, width, col - XS_COL if col >= XS_COL else None)
    dt = jnp.dot(h, w_ref[:, w_dt:w_dt + SSM_HEADS], preferred_element_type=F32)
    rest_ref[:, DT_COL:HYB_REST] = jnp.concatenate([dt, jnp.zeros((tm, HYB_REST - DT_COL - SSM_HEADS), F32)], axis=1)


def _hyb_in_proj(x, gain, sc, sh, w, conv_w, conv_b, seq_len, tm, tn):
    m = x.shape[0]
    per_batch = sc.shape[0] > 1
    tpb = seq_len // tm
    assert HYB_QKV % tn == 0 and XS_COL % tn == 0 and DT_COL % tn == 0
    x8 = x.reshape(m // SUBLANES, SUBLANES, D_MODEL)
    return pl.pallas_call(
        functools.partial(_hyb_in_body, seq_len, tm, tn),
        grid=(m // tm,),
        in_specs=[
            pl.BlockSpec((tm, D_MODEL), lambda i: (i, 0)),
            *_halo_specs(m, tm),
            pl.BlockSpec((1, D_MODEL), lambda i: (0, 0)),
            _vec_spec(D_MODEL, tpb, per_batch, 1),
            _vec_spec(D_MODEL, tpb, per_batch, 1),
            _resident(w.shape),
            _resident((3, SSM_XBC)),
            _resident((1, SSM_XBC)),
        ],
        out_specs=[pl.BlockSpec((tm, HYB_BF), lambda i: (i, 0)), pl.BlockSpec((tm, HYB_REST), lambda i: (i, 0))],
        out_shape=[jax.ShapeDtypeStruct((m, HYB_BF), BF16), jax.ShapeDtypeStruct((m, HYB_REST), F32)],
        scratch_shapes=[pltpu.VMEM((tm + 2 * SUBLANES, D_MODEL), BF16)],
        compiler_params=_cparams(("arbitrary",)),
        name="hybrid_in_proj",
    )(x, x8, x8, gain, sc, sh, w, conv_w, conv_b.reshape(1, SSM_XBC))


def _ffn_body(seq_len, tm, th, x_ref, xn_ref, xp_ref, gain_ref, sc_ref, sh_ref, gate_ref,
              wu_ref, cw_ref, cb_ref, wd_ref, o_ref, h_scr, act_scr):
    _norm_mod_halo(seq_len, tm, x_ref, xn_ref, xp_ref, gain_ref, sc_ref, sh_ref, h_scr)
    h = h_scr[...]

    def conv_up(col, width):
        u = jnp.dot(h, wu_ref[:, col:col + width], preferred_element_type=F32)
        return _dwconv3_rows(u, cw_ref[:, col:col + width], cb_ref[:, col:col + width], tm, seq_len)

    for col, width in _chunks(FFN_HIDDEN, th):
        act = jax.nn.silu(conv_up(col, width)) * conv_up(FFN_HIDDEN + col, width)
        act_scr[:, col:col + width] = act.astype(BF16)
    out = jnp.dot(act_scr[...], wd_ref[...], preferred_element_type=F32)
    o_ref[...] = x_ref[...] + gate_ref[0] * out


def _conv_ffn(x, gain, sc, sh, gate, w_up, conv_w, conv_b, w_down, seq_len, tm, th):
    m = x.shape[0]
    per_batch = sc.shape[0] > 1
    tpb = seq_len // tm
    x8 = x.reshape(m // SUBLANES, SUBLANES, D_MODEL)
    conv_b = conv_b.reshape(1, 2 * FFN_HIDDEN)
    vec = _vec_spec(D_MODEL, tpb, per_batch, 1)
    return pl.pallas_call(
        functools.partial(_ffn_body, seq_len, tm, th),
        grid=(m // tm,),
        in_specs=[
            pl.BlockSpec((tm, D_MODEL), lambda i: (i, 0)),
            *_halo_specs(m, tm),
            pl.BlockSpec((1, D_MODEL), lambda i: (0, 0)),
            vec, vec, vec,
            _resident((D_MODEL, 2 * FFN_HIDDEN)),
            _resident((3, 2 * FFN_HIDDEN)),
            _resident((1, 2 * FFN_HIDDEN)),
            _resident((FFN_HIDDEN, D_MODEL)),
        ],
        out_specs=pl.BlockSpec((tm, D_MODEL), lambda i: (i, 0)),
        out_shape=jax.ShapeDtypeStruct((m, D_MODEL), F32),
        scratch_shapes=[pltpu.VMEM((tm + 2 * SUBLANES, D_MODEL), BF16), pltpu.VMEM((tm, FFN_HIDDEN), BF16)],
        compiler_params=_cparams(("arbitrary",)),
        name="conv_ffn",
    )(x, x8, x8, gain, sc, sh, gate, w_up, conv_w, conv_b, w_down)


SSM_GW = SSM_RATIO * SSM_HEADDIM
FWD_CHUNKS_PER_STEP = 4


def _bcol(x, h):
    return jnp.broadcast_to(x[:, h:h + 1], (x.shape[0], LANES))


def _pair_sel(left, a, h):
    return jnp.where(left, _bcol(a, h), _bcol(a, h + 1))


def _expand_heads(a, spread_ref):
    hi = a.astype(BF16)
    lo = (a - hi.astype(F32)).astype(BF16)
    return (jnp.dot(hi, spread_ref[...], preferred_element_type=F32)
            + jnp.dot(lo, spread_ref[...], preferred_element_type=F32))


def _ssm_decay_row(left, efull, g):
    return jnp.concatenate([_pair_sel(left, efull, g * SSM_RATIO + 2 * m) for m in range(SSM_RATIO // 2)], axis=1)


def _load_states(s_ret, s_ssm, sret0_ref, sssm0_ref):
    if sret0_ref is None:
        s_ret[...] = jnp.zeros_like(s_ret)
        s_ssm[...] = jnp.zeros_like(s_ssm)
    else:
        s_ret[...] = sret0_ref[0]
        for g in range(SSM_GROUPS):
            s_ssm[g] = jnp.concatenate([sssm0_ref[0, g * SSM_RATIO + r] for r in range(SSM_RATIO)], axis=1)


def _store_states(s_ret, s_ssm, sret_ref, sssm_ref):
    sret_ref[0] = s_ret[...]
    for g in range(SSM_GROUPS):
        for r in range(SSM_RATIO):
            sssm_ref[0, g * SSM_RATIO + r] = s_ssm[g][:, r * SSM_HEADDIM:(r + 1) * SSM_HEADDIM]


def _ssm_log_decay(dt_raw, dtb, alog, lane):
    dt = jax.nn.softplus(dt_raw + dtb)
    return dt, jnp.where(lane < SSM_HEADS, dt * (-jnp.exp(alog)), 0.0)


def _bwd_state_body(has_init, k_ref, v_ref, xs_ref, bm_ref, dt_ref, dtb_ref, alog_ref, spread_ref, logdec_ref,
                    *rest):
    sret0_ref = sssm0_ref = None
    if has_init:
        sret0_ref, sssm0_ref = rest[0], rest[1]
        rest = rest[2:]
    sret_in_ref, sssm_in_ref, sret_fin_ref, sssm_fin_ref, s_ret, s_ssm, wcol = rest
    i = pl.program_id(1)
    n = pl.num_programs(1)
    ii = lax.broadcasted_iota(jnp.int32, (CHUNK, CHUNK), 0)
    jj = lax.broadcasted_iota(jnp.int32, (CHUNK, CHUNK), 1)
    lane = lax.broadcasted_iota(jnp.int32, (1, LANES), 1)
    left = lane < SSM_HEADDIM

    @pl.when(i == 0)
    def _():
        _load_states(s_ret, s_ssm, sret0_ref, sssm0_ref)
        for h in range(RET_HEADS):
            wcol[h] = jnp.exp(ii.astype(F32) * logdec_ref[1, h])

    cps = sret_in_ref.shape[1]
    full = jnp.full((1, LANES), float(CHUNK), F32)
    prep = []
    for c in range(cps):
        rows = slice(c * CHUNK, (c + 1) * CHUNK)
        dt, la = _ssm_log_decay(dt_ref[rows], dtb_ref[...], alog_ref[...], lane)
        rc = jnp.dot((jj >= ii).astype(F32), la, precision=HIGHEST, preferred_element_type=F32)
        last = rc[0:1]
        prep.append((_expand_heads(jnp.exp(last - rc) * dt, spread_ref), jnp.exp(last)))

    for c in reversed(range(cps)):
        rows = slice(c * CHUNK, (c + 1) * CHUNK)
        sret_in_ref[0, c] = s_ret[...].astype(BF16)
        sssm_in_ref[0, c] = s_ssm[...].astype(BF16)
        for h in range(RET_HEADS):
            kh = k_ref[rows, h * RET_DK:(h + 1) * RET_DK]
            vw = (v_ref[rows, h * RET_DV:(h + 1) * RET_DV].astype(F32) * wcol[h]).astype(BF16)
            upd = lax.dot_general(kh, vw, (((0,), (0,)), ((), ())), preferred_element_type=F32)
            s_ret[h] = s_ret[h] * jnp.exp(full * logdec_ref[1, h]) + upd
        wx, efull = prep[c]
        for g in range(SSM_GROUPS):
            vw = (xs_ref[rows, g * SSM_GW:(g + 1) * SSM_GW] * wx[:, g * SSM_GW:(g + 1) * SSM_GW]).astype(BF16)
            bg = bm_ref[rows, g * SSM_STATE:(g + 1) * SSM_STATE]
            upd = lax.dot_general(bg, vw, (((0,), (0,)), ((), ())), preferred_element_type=F32)
            s_ssm[g] = s_ssm[g] * _ssm_decay_row(left, efull, g) + upd

    @pl.when(i == n - 1)
    def _():
        _store_states(s_ret, s_ssm, sret_fin_ref, sssm_fin_ref)


def _hyb_fwd_body(has_init, qkv_ref, xs_ref, bc_ref, dt_ref, sret_in_ref, sssm_in_ref, dtb_ref, alog_ref, dsk_ref,
                  spread_ref, logdec_ref, *rest):
    sret0_ref = sssm0_ref = None
    if has_init:
        sret0_ref, sssm0_ref = rest[0], rest[1]
        rest = rest[2:]
    y_ref, sret_fin_ref, sssm_fin_ref, s_ret, s_ssm, dcomb, ein0, ein1, wcol = rest
    i = pl.program_id(1)
    n = pl.num_programs(1)
    ii = lax.broadcasted_iota(jnp.int32, (CHUNK, CHUNK), 0)
    jj = lax.broadcasted_iota(jnp.int32, (CHUNK, CHUNK), 1)
    lane = lax.broadcasted_iota(jnp.int32, (1, LANES), 1)
    left = lane < SSM_HEADDIM
    lower = jj <= ii

    @pl.when(i == 0)
    def _():
        _load_states(s_ret, s_ssm, sret0_ref, sssm0_ref)
        fi, fj = ii.astype(F32), jj.astype(F32)
        for h in range(RET_HEADS):
            la0, la1 = logdec_ref[0, h], logdec_ref[1, h]
            dec = jnp.exp(jnp.where(lower, (fi - fj) * la0, (fj - fi) * la1))
            dcomb[h] = jnp.where(ii == jj, 2.0, dec)
            ein0[h] = jnp.exp((fi + 1.0) * la0)
            ein1[h] = jnp.exp((CHUNK - fi) * la1)
            wcol[h] = jnp.exp((CHUNK - 1.0 - fi) * la0)

    for c in range(sret_in_ref.shape[1]):
        _fwd_chunk(c, qkv_ref, xs_ref, bc_ref, dt_ref, sret_in_ref, sssm_in_ref, dtb_ref, alog_ref, dsk_ref,
                   spread_ref, logdec_ref, y_ref, s_ret, s_ssm, dcomb, ein0, ein1, wcol)

    @pl.when(i == n - 1)
    def _():
        _store_states(s_ret, s_ssm, sret_fin_ref, sssm_fin_ref)


def _fwd_chunk(c, qkv_ref, xs_ref, bc_ref, dt_ref, sret_in_ref, sssm_in_ref, dtb_ref, alog_ref, dsk_ref,
               spread_ref, logdec_ref, y_ref, s_ret, s_ssm, dcomb, ein0, ein1, wcol):
    rows = pl.ds(c * CHUNK, CHUNK)
    qkv_ref, xs_ref, bc_ref, dt_ref, y_ref = (r.at[rows] for r in (qkv_ref, xs_ref, bc_ref, dt_ref, y_ref))
    ii = lax.broadcasted_iota(jnp.int32, (CHUNK, CHUNK), 0)
    jj = lax.broadcasted_iota(jnp.int32, (CHUNK, CHUNK), 1)
    lane = lax.broadcasted_iota(jnp.int32, (1, LANES), 1)
    left = lane < SSM_HEADDIM
    lower = jj <= ii

    def ret_head(h):
        return (qkv_ref[:, h * RET_DK:(h + 1) * RET_DK],
                qkv_ref[:, RET_Q + h * RET_DK:RET_Q + (h + 1) * RET_DK],
                qkv_ref[:, 2 * RET_Q + h * RET_DV:2 * RET_Q + (h + 1) * RET_DV])

    def ssm_group(g):
        return (bc_ref[:, g * SSM_STATE:(g + 1) * SSM_STATE],
                bc_ref[:, SSM_BC + g * SSM_STATE:SSM_BC + (g + 1) * SSM_STATE])

    dt_raw = dt_ref[...]
    dt0, la0 = _ssm_log_decay(dt_raw, dtb_ref[0:1], alog_ref[0:1], lane)
    dt1, la1 = _ssm_log_decay(dt_raw, dtb_ref[1:2], alog_ref[1:2], lane)
    c0 = jnp.dot(lower.astype(F32), la0, precision=HIGHEST, preferred_element_type=F32)
    c1 = jnp.dot((jj >= ii).astype(F32), la1, precision=HIGHEST, preferred_element_type=F32)

    ret_scores, ret_inter = [], []
    for h in range(RET_HEADS):
        qh, kh, _ = ret_head(h)
        ret_scores.append(lax.dot_general(qh, kh, (((1,), (1,)), ((), ())), preferred_element_type=F32))
        states = jnp.concatenate([s_ret[h].astype(BF16), sret_in_ref[0, c, h]], axis=1)
        ret_inter.append(jnp.dot(qh, states, preferred_element_type=F32))
    ssm_scores, ssm_inter = [], []
    for g in range(SSM_GROUPS):
        bg, cg = ssm_group(g)
        ssm_scores.append(lax.dot_general(cg, bg, (((1,), (1,)), ((), ())), preferred_element_type=F32))
        states = jnp.concatenate([s_ssm[g].astype(BF16), sssm_in_ref[0, c, g]], axis=1)
        ssm_inter.append(jnp.dot(cg, states, preferred_element_type=F32))

    full = jnp.full((1, LANES), float(CHUNK), F32)
    for h in range(RET_HEADS):
        _, kh, vh = ret_head(h)
        y = jnp.dot((ret_scores[h] * dcomb[h]).astype(BF16), vh, preferred_element_type=F32)
        inter = ret_inter[h]
        y_ref[:, h * RET_DV:(h + 1) * RET_DV] = y + inter[:, 0:RET_DV] * ein0[h] + inter[:, RET_DV:] * ein1[h]
        vw = (vh.astype(F32) * wcol[h]).astype(BF16)
        upd = lax.dot_general(kh, vw, (((0,), (0,)), ((), ())), preferred_element_type=F32)
        s_ret[h] = s_ret[h] * jnp.exp(full * logdec_ref[0, h]) + upd

    c0kt, c1kt = (c0 - jnp.log(dt0)).T, (c1 - jnp.log(dt1)).T
    dst = (dt0 + dt1).T
    last0 = c0[CHUNK - 1:CHUNK]
    w0x = _expand_heads(jnp.exp(last0 - c0) * dt0, spread_ref)
    efull0 = jnp.exp(last0)
    diag = jj == ii
    for g in range(SSM_GROUPS):
        bg, _ = ssm_group(g)
        s = ssm_scores[g]
        inter = ssm_inter[g]
        vws = []
        for m in range(SSM_RATIO // 2):
            ha = g * SSM_RATIO + 2 * m
            ps, c0bs, c1bs = [], [], []
            for h in (ha, ha + 1):
                c0b, c1b = _bcol(c0, h), _bcol(c1, h)
                arg = jnp.where(lower, c0b - c0kt[h:h + 1], c1b - c1kt[h:h + 1])
                ps.append((s * jnp.where(diag, dst[h:h + 1], jnp.exp(arg))).astype(BF16))
                c0bs.append(c0b)
                c1bs.append(c1b)
            col = (g * 2 + m) * LANES
            xs_pair = xs_ref[:, col:col + LANES]
            vals = jnp.concatenate([jnp.where(left, xs_pair, 0.0), jnp.where(left, 0.0, xs_pair)], axis=0)
            y = jnp.dot(jnp.concatenate(ps, axis=1), vals.astype(BF16), preferred_element_type=F32)
            y = y + inter[:, m * LANES:(m + 1) * LANES] * jnp.exp(jnp.where(left, c0bs[0], c0bs[1]))
            y = y + (inter[:, SSM_GW + m * LANES:SSM_GW + (m + 1) * LANES]
                     * jnp.exp(jnp.where(left, c1bs[0], c1bs[1])))
            y_ref[:, RET_V + col:RET_V + col + LANES] = y + dsk_ref[:, col:col + LANES] * xs_pair
            vws.append((xs_pair * w0x[:, col:col + LANES]).astype(BF16))
        upd = lax.dot_general(bg, jnp.concatenate(vws, axis=1), (((0,), (0,)), ((), ())),
                              preferred_element_type=F32)
        s_ssm[g] = s_ssm[g] * _ssm_decay_row(left, efull0, g) + upd


def _hybrid_mixer(qkv, rest, dt_bias, a_log, log_decay, d_skip, s_ret0, s_ssm0, batch, seq_len):
    m = qkv.shape[0]
    n = seq_len // CHUNK
    has_init = s_ret0 is not None
    dt_blk = DT_COL // LANES
    pad = lambda v: jnp.pad(v.astype(F32), ((0, 0), (0, LANES - v.shape[1])))
    dtb, alog = pad(dt_bias), pad(a_log)
    logdec = log_decay.astype(F32)
    spread = (lax.broadcasted_iota(jnp.int32, (LANES, SSM_D_INNER), 0)
              == lax.broadcasted_iota(jnp.int32, (LANES, SSM_D_INNER), 1) // SSM_HEADDIM).astype(BF16)
    spread_spec = pl.BlockSpec((LANES, SSM_D_INNER), lambda b, i: (0, 0))
    dsk = jnp.repeat((d_skip[0] + d_skip[1]).astype(F32), SSM_HEADDIM).reshape(1, SSM_D_INNER)
    smem = pl.BlockSpec(memory_space=pltpu.SMEM)
    ret_spec = pl.BlockSpec((1, RET_HEADS, RET_DK, RET_DV), lambda b, i: (b, 0, 0, 0))
    ssm_spec = pl.BlockSpec((1, SSM_HEADS, SSM_STATE, SSM_HEADDIM), lambda b, i: (b, 0, 0, 0))
    ret_shape = jax.ShapeDtypeStruct((batch, RET_HEADS, RET_DK, RET_DV), F32)
    ssm_shape = jax.ShapeDtypeStruct((batch, SSM_HEADS, SSM_STATE, SSM_HEADDIM), F32)
    ret_scr = pltpu.VMEM((RET_HEADS, RET_DK, RET_DV), F32)
    ssm_scr = pltpu.VMEM((SSM_GROUPS, SSM_STATE, SSM_GW), F32)
    const_scr = pltpu.VMEM((RET_HEADS, CHUNK, CHUNK), F32)

    cps = next(c for c in (8, 4, 2, 1) if n % c == 0)
    rsteps = n // cps
    rrows = cps * CHUNK

    def rev(b, i):
        return b * rsteps + rsteps - 1 - i

    in_specs = [
        pl.BlockSpec((rrows, RET_Q), lambda b, i: (rev(b, i), 1)),
        pl.BlockSpec((rrows, RET_V), lambda b, i: (rev(b, i), 1)),
        pl.BlockSpec((rrows, SSM_D_INNER), lambda b, i: (rev(b, i), XS_COL // SSM_D_INNER)),
        pl.BlockSpec((rrows, SSM_BC), lambda b, i: (rev(b, i), HYB_QKV // SSM_BC)),
        pl.BlockSpec((rrows, LANES), lambda b, i: (rev(b, i), dt_blk)),
        pl.BlockSpec((1, LANES), lambda b, i: (0, 0)),
        pl.BlockSpec((1, LANES), lambda b, i: (0, 0)),
        spread_spec,
        smem,
    ]
    args = [qkv, qkv, rest, qkv, rest, dtb[1:2], alog[1:2], spread, logdec]
    if has_init:
        in_specs += [ret_spec, ssm_spec]
        args += [s_ret0[:, 1], s_ssm0[:, 1]]
    sret_in, sssm_in, sret1, sssm1 = pl.pallas_call(
        functools.partial(_bwd_state_body, has_init),
        grid=(batch, rsteps),
        in_specs=in_specs,
        out_specs=[
            pl.BlockSpec((1, cps, RET_HEADS, RET_DK, RET_DV), lambda b, i: (b, rsteps - 1 - i, 0, 0, 0)),
            pl.BlockSpec((1, cps, SSM_GROUPS, SSM_STATE, SSM_GW), lambda b, i: (b, rsteps - 1 - i, 0, 0, 0)),
            ret_spec, ssm_spec,
        ],
        out_shape=[
            jax.ShapeDtypeStruct((batch, n, RET_HEADS, RET_DK, RET_DV), BF16),
            jax.ShapeDtypeStruct((batch, n, SSM_GROUPS, SSM_STATE, SSM_GW), BF16),
            ret_shape, ssm_shape,
        ],
        scratch_shapes=[ret_scr, ssm_scr, const_scr],
        compiler_params=_cparams(("arbitrary", "arbitrary")),
        name="hybrid_reverse_states",
    )(*args)

    fcps = next(c for c in (FWD_CHUNKS_PER_STEP, 2, 1) if n % c == 0)
    fsteps = n // fcps
    frows = fcps * CHUNK

    def fwd(b, i):
        return b * fsteps + i

    in_specs = [
        pl.BlockSpec((frows, HYB_QKV), lambda b, i: (fwd(b, i), 0)),
        pl.BlockSpec((frows, SSM_D_INNER), lambda b, i: (fwd(b, i), XS_COL // SSM_D_INNER)),
        pl.BlockSpec((frows, 2 * SSM_BC), lambda b, i: (fwd(b, i), HYB_QKV // (2 * SSM_BC))),
        pl.BlockSpec((frows, LANES), lambda b, i: (fwd(b, i), dt_blk)),
        pl.BlockSpec((1, fcps, RET_HEADS, RET_DK, RET_DV), lambda b, i: (b, i, 0, 0, 0)),
        pl.BlockSpec((1, fcps, SSM_GROUPS, SSM_STATE, SSM_GW), lambda b, i: (b, i, 0, 0, 0)),
        pl.BlockSpec((2, LANES), lambda b, i: (0, 0)),
        pl.BlockSpec((2, LANES), lambda b, i: (0, 0)),
        pl.BlockSpec((1, SSM_D_INNER), lambda b, i: (0, 0)),
        spread_spec,
        smem,
    ]
    args = [qkv, rest, qkv, rest, sret_in, sssm_in, dtb, alog, dsk, spread, logdec]
    if has_init:
        in_specs += [ret_spec, ssm_spec]
        args += [s_ret0[:, 0], s_ssm0[:, 0]]
    y, sret0, sssm0 = pl.pallas_call(
        functools.partial(_hyb_fwd_body, has_init),
        grid=(batch, fsteps),
        in_specs=in_specs,
        out_specs=[pl.BlockSpec((frows, HYB_MIX), lambda b, i: (fwd(b, i), 0)), ret_spec, ssm_spec],
        out_shape=[jax.ShapeDtypeStruct((m, HYB_MIX), F32), ret_shape, ssm_shape],
        scratch_shapes=[ret_scr, ssm_scr, const_scr, const_scr, const_scr, const_scr],
        compiler_params=_cparams(("arbitrary", "arbitrary")),
        name="hybrid_forward_mix",
    )(*args)
    return y, jnp.stack([sret0, sret1], axis=1), jnp.stack([sssm0, sssm1], axis=1)


def _hyb_out_body(y_ref, g_ref, z_ref, x_ref, gate_ref, gn_ref, ng_ref, w_ref, o_ref):
    y = y_ref[...]
    parts = []
    for h in range(RET_HEADS):
        yh = y[:, h * RET_DV:(h + 1) * RET_DV]
        mu = jnp.mean(yh, axis=-1, keepdims=True)
        var = jnp.mean(jnp.square(yh - mu), axis=-1, keepdims=True)
        parts.append((yh - mu) * lax.rsqrt(var + EPS))
    y_ret = jax.nn.silu(g_ref[...]) * (jnp.concatenate(parts, axis=1) * gn_ref[...])
    yz = y[:, RET_V:] * jax.nn.silu(z_ref[...])
    y_ssm = yz * lax.rsqrt(jnp.mean(yz * yz, axis=-1, keepdims=True) + EPS) * ng_ref[...]
    out = jnp.dot(y_ret.astype(BF16), w_ref[0:RET_V], preferred_element_type=F32)
    out = out + jnp.dot(y_ssm.astype(BF16), w_ref[RET_V:HYB_MIX], preferred_element_type=F32)
    o_ref[...] = x_ref[...] + gate_ref[0] * out


def _hyb_out(y, rest, x, gate, gn_gain, norm_gain, w_out, seq_len, tm):
    m = x.shape[0]
    per_batch = gate.shape[0] > 1
    tpb = seq_len // tm
    return pl.pallas_call(
        _hyb_out_body,
        grid=(m // tm,),
        in_specs=[
            pl.BlockSpec((tm, HYB_MIX), lambda i: (i, 0)),
            pl.BlockSpec((tm, RET_V), lambda i: (i, 0)),
            pl.BlockSpec((tm, SSM_D_INNER), lambda i: (i, 1)),
            pl.BlockSpec((tm, D_MODEL), lambda i: (i, 0)),
            _vec_spec(D_MODEL, tpb, per_batch, 1),
            pl.BlockSpec((1, RET_V), lambda i: (0, 0)),
            pl.BlockSpec((1, SSM_D_INNER), lambda i: (0, 0)),
            pl.BlockSpec((HYB_MIX, D_MODEL), lambda i: (0, 0)),
        ],
        out_specs=pl.BlockSpec((tm, D_MODEL), lambda i: (i, 0)),
        out_shape=jax.ShapeDtypeStruct((m, D_MODEL), F32),
        compiler_params=_cparams(("arbitrary",)),
        name="hybrid_out_proj",
    )(y, rest, rest, x, gate, gn_gain, norm_gain, w_out)


def _head_rms(t, seg_ref):
    outs = []
    for c in range(t.shape[1] // LANES):
        tc = t[:, c * LANES:(c + 1) * LANES]
        sq = tc * tc
        hi = sq.astype(BF16)
        lo = (sq - hi.astype(F32)).astype(BF16)
        ssum = (jnp.dot(hi, seg_ref[...], preferred_element_type=F32)
                + jnp.dot(lo, seg_ref[...], preferred_element_type=F32))
        outs.append(tc * lax.rsqrt(ssum * (1.0 / ATT_HD) + EPS))
    return outs


def _rope(tc, cos, sin, lane):
    fwd = pltpu.roll(tc, LANES - ROT_FREQS, 1)
    bwd = pltpu.roll(tc, ROT_FREQS, 1)
    return tc * cos + jnp.where(lane % (2 * ROT_FREQS) < ROT_FREQS, fwd, bwd) * sin


def _qkv_body(has_rope, x_ref, gain_ref, sc_ref, sh_ref, w_ref, seg_ref, qg_ref, kg_ref, *rest):
    if has_rope:
        cos_ref, sin_ref, q_ref, k_ref, v_ref = rest
    else:
        q_ref, k_ref, v_ref, kf_ref, vf_ref = rest
    h = _norm_mod(x_ref[...], gain_ref[...], sc_ref[0], sh_ref[0]).astype(BF16)
    lane = lax.broadcasted_iota(jnp.int32, (1, LANES), 1)

    def project(col):
        t = jnp.dot(h, w_ref[:, col:col + MXU_COLS], preferred_element_type=F32)
        return [t[:, c * LANES:(c + 1) * LANES] for c in range(MXU_COLS // LANES)]

    def normed(tiles, gain_ref_):
        tiles = [t * gain_ref_[...] for t in _head_rms(jnp.concatenate(tiles, axis=1), seg_ref)]
        return tiles, ([_rope(t, cos_ref[...], sin_ref[...], lane) for t in tiles] if has_rope else tiles)

    def emit_q(col, tiles):
        _, q = normed(tiles, qg_ref)
        for c, t in enumerate(q):
            row = col + c * LANES
            q_ref[0, row:row + LANES, :] = (t * Q_SCALE).T.astype(BF16)

    def emit_k(col, tiles):
        k, kr = normed(tiles, kg_ref)
        for c in range(MXU_COLS // LANES):
            lo = col + c * LANES
            if not has_rope:
                kf_ref[:, lo:lo + LANES] = k[c]
            kc = kr[c].astype(BF16)
            for a in range(LANES // ATT_HD):
                k_ref[0, lo // ATT_HD + a] = kc[:, a * ATT_HD:(a + 1) * ATT_HD]

    def emit_v(col, tiles):
        for c in range(MXU_COLS // LANES):
            lo = col + c * LANES
            if not has_rope:
                vf_ref[:, lo:lo + LANES] = tiles[c]
            vc = tiles[c].T.astype(BF16)
            for a in range(LANES // ATT_HD):
                v_ref[0, lo // ATT_HD + a] = vc[a * ATT_HD:(a + 1) * ATT_HD]

    work = ([(emit_q, col, col) for col in range(0, ATT_Q, MXU_COLS)]
            + [(emit_k, col, ATT_Q + col) for col in range(0, ATT_KV, MXU_COLS)]
            + [(emit_v, col, ATT_Q + ATT_KV + col) for col in range(0, ATT_KV, MXU_COLS)])
    tiles = project(work[0][2])
    for n, (emit, col, _) in enumerate(work):
        ahead = project(work[n + 1][2]) if n + 1 < len(work) else None
        emit(col, tiles)
        tiles = ahead


def _qkv_proj(x, gain, sc, sh, w_qkv, q_gain, k_gain, rope, seq_len, tm):
    m = x.shape[0]
    per_batch = sc.shape[0] > 1
    tpb = seq_len // tm
    has_rope = rope is not None
    seg = (lax.broadcasted_iota(jnp.int32, (LANES, LANES), 0) // ATT_HD
           == lax.broadcasted_iota(jnp.int32, (LANES, LANES), 1) // ATT_HD).astype(BF16)
    tile2 = lambda v: jnp.tile(v.astype(F32), LANES // ATT_HD).reshape(1, LANES)
    in_specs = [
        pl.BlockSpec((tm, D_MODEL), lambda i: (i, 0)),
        pl.BlockSpec((1, D_MODEL), lambda i: (0, 0)),
        _vec_spec(D_MODEL, tpb, per_batch, 1),
        _vec_spec(D_MODEL, tpb, per_batch, 1),
        pl.BlockSpec((D_MODEL, ATT_Q + 2 * ATT_KV), lambda i: (0, 0)),
        pl.BlockSpec((LANES, LANES), lambda i: (0, 0)),
        pl.BlockSpec((1, LANES), lambda i: (0, 0)),
        pl.BlockSpec((1, LANES), lambda i: (0, 0)),
    ]
    args = [x, gain, sc, sh, w_qkv, seg, tile2(q_gain), tile2(k_gain)]
    batch = m // seq_len
    out_specs = [pl.BlockSpec((1, ATT_Q, tm), lambda i: (i // tpb, 0, i % tpb)),
                 pl.BlockSpec((1, ATT_KV_HEADS, tm, ATT_HD), lambda i: (i // tpb, 0, i % tpb, 0)),
                 pl.BlockSpec((1, ATT_KV_HEADS, ATT_HD, tm), lambda i: (i // tpb, 0, 0, i % tpb))]
    out_shape = [jax.ShapeDtypeStruct((batch, ATT_Q, seq_len), BF16),
                 jax.ShapeDtypeStruct((batch, ATT_KV_HEADS, seq_len, ATT_HD), BF16),
                 jax.ShapeDtypeStruct((batch, ATT_KV_HEADS, ATT_HD, seq_len), BF16)]
    if has_rope:
        in_specs += [pl.BlockSpec((tm, LANES), lambda i: (i % tpb, 0))] * 2
        args += list(rope)
    else:
        out_specs += [pl.BlockSpec((tm, ATT_KV), lambda i: (i, 0))] * 2
        out_shape += [jax.ShapeDtypeStruct((m, ATT_KV), F32)] * 2
    return pl.pallas_call(
        functools.partial(_qkv_body, has_rope),
        grid=(m // tm,),
        in_specs=in_specs,
        out_specs=out_specs,
        out_shape=out_shape,
        compiler_params=_cparams(("arbitrary",)),
        name="attn_qkv_proj",
    )(*args)


def _rope_tables(seq_len):
    rows = seq_len // GRID_W
    row = jnp.repeat(jnp.arange(rows, dtype=F32), GRID_W)
    col = jnp.tile(jnp.arange(GRID_W, dtype=F32), rows)
    inv = ROPE_BASE ** (-jnp.arange(ROT_FREQS, dtype=F32) / ROT_FREQS)
    ar, ac = row[:, None] * inv, col[:, None] * inv
    cos = jnp.concatenate([jnp.cos(ar), jnp.cos(ar), jnp.cos(ac), jnp.cos(ac)], axis=1)
    sin = jnp.concatenate([-jnp.sin(ar), jnp.sin(ar), -jnp.sin(ac), jnp.sin(ac)], axis=1)
    return jnp.tile(cos, (1, LANES // ATT_HD)), jnp.tile(sin, (1, LANES // ATT_HD))


V_ROWS = ATT_HD + 16
ATT_SLOTS = 4


def _attn_body(tq, tk, qt_ref, k_ref, vt_ref, x_ref, gate_ref, wo_ref, o_ref, acc_scr, s_scr, ot_scr):
    nkv = k_ref.shape[2] // tk
    ones_rows = jnp.ones((V_ROWS - ATT_HD, tk), BF16)
    for g in range(ATT_KV_HEADS):
        qts = [qt_ref[0, (g * ATT_RATIO + r) * ATT_HD:(g * ATT_RATIO + r + 1) * ATT_HD, :]
               for r in range(ATT_RATIO)]
        acc_scr[...] = jnp.zeros_like(acc_scr)

        def scores(t, slot):
            kt = k_ref[0, g, pl.ds(pl.multiple_of(t * tk, tk), tk), :]
            tops = []
            for r in range(ATT_RATIO):
                s = jnp.dot(kt, qts[r], preferred_element_type=F32)
                s_scr[slot, r] = s
                tops.append(jnp.max(s, axis=0, keepdims=True))
            return tuple(tops)

        def softmax_pv(t, slot, ms, tops):
            vt = jnp.concatenate([vt_ref[0, g, :, pl.ds(pl.multiple_of(t * tk, tk), tk)], ones_rows], axis=0)
            new_ms = [jnp.maximum(ms[r], tops[r]) for r in range(ATT_RATIO)]
            ps = [jnp.exp2(s_scr[slot, r] - new_ms[r]).astype(BF16) for r in range(ATT_RATIO)]
            for r in range(ATT_RATIO):
                acc_scr[r] = (jnp.exp2(ms[r] - new_ms[r]) * acc_scr[r]
                              + jnp.dot(vt, ps[r], preferred_element_type=F32))
            return tuple(new_ms)

        def quad_step(i, carry):
            ms, top0, top1 = carry
            t = i * ATT_SLOTS
            top2 = scores(t + 2, 2)
            top3 = scores(t + 3, 3)
            ms = softmax_pv(t, 0, ms, top0)
            ms = softmax_pv(t + 1, 1, ms, top1)
            top0 = scores(t + 4, 0)
            top1 = scores(t + 5, 1)
            ms = softmax_pv(t + 2, 2, ms, top2)
            return softmax_pv(t + 3, 3, ms, top3), top0, top1

        tops = {t: scores(t, t) for t in range(min(2, nkv))}
        ms = tuple(jnp.full((1, tq), NEG_BIG, F32) for _ in range(ATT_RATIO))
        quads = max(nkv - 2, 0) // ATT_SLOTS
        if quads:
            ms, top0, top1 = lax.fori_loop(0, quads, quad_step, (ms, tops[0], tops[1]))
            tops = {quads * ATT_SLOTS: top0, quads * ATT_SLOTS + 1: top1}
        for t in range(quads * ATT_SLOTS, nkv):
            if t % 2 == 0:
                for ahead in (t + 2, t + 3):
                    if ahead < nkv:
                        tops[ahead] = scores(ahead, ahead % ATT_SLOTS)
            ms = softmax_pv(t, t % ATT_SLOTS, ms, tops.pop(t))
        for r in range(ATT_RATIO):
            hd = g * ATT_RATIO + r
            a = acc_scr[r]
            ot_scr[hd * ATT_HD:(hd + 1) * ATT_HD, :] = (a[0:ATT_HD] / a[ATT_HD:ATT_HD + 1]).astype(BF16)
    out = lax.dot_general(ot_scr[...], wo_ref[...], (((0,), (0,)), ((), ())), preferred_element_type=F32)
    o_ref[0] = x_ref[0] + gate_ref[0] * out


def _attention(qt, k4, vt, x, gate, w_o, batch, seq_len, tq, tk):
    per_batch = gate.shape[0] > 1
    nkeys = k4.shape[2]
    x3 = x.reshape(batch, seq_len, D_MODEL)
    out = pl.pallas_call(
        functools.partial(_attn_body, tq, tk),
        grid=(batch, seq_len // tq),
        in_specs=[
            pl.BlockSpec((1, ATT_Q, tq), lambda b, i: (b, 0, i)),
            pl.BlockSpec((1, ATT_KV_HEADS, nkeys, ATT_HD), lambda b, i: (b, 0, 0, 0)),
            pl.BlockSpec((1, ATT_KV_HEADS, ATT_HD, nkeys), lambda b, i: (b, 0, 0, 0)),
            pl.BlockSpec((1, tq, D_MODEL), lambda b, i: (b, i, 0)),
            pl.BlockSpec((1, 1, D_MODEL), lambda b, i: (b if per_batch else 0, 0, 0)),
            pl.BlockSpec((ATT_Q, D_MODEL), lambda b, i: (0, 0)),
        ],
        out_specs=pl.BlockSpec((1, tq, D_MODEL), lambda b, i: (b, i, 0)),
        out_shape=jax.ShapeDtypeStruct((batch, seq_len, D_MODEL), F32),
        scratch_shapes=[pltpu.VMEM((ATT_RATIO, V_ROWS, tq), F32), pltpu.VMEM((ATT_SLOTS, ATT_RATIO, tk, tq), F32),
                        pltpu.VMEM((ATT_Q, tq), BF16)],
        compiler_params=_cparams(("arbitrary", "arbitrary")),
        name="attention",
    )(qt, k4, vt, x3, gate, w_o)
    return out.reshape(batch * seq_len, D_MODEL)


def _keys_values(k4, vt, cache_k, cache_v):
    k4 = jnp.concatenate([jnp.swapaxes(cache_k, 1, 2).astype(BF16), k4], axis=2)
    vt = jnp.concatenate([jnp.transpose(cache_v, (0, 2, 3, 1)).astype(BF16), vt], axis=3)
    return k4, vt


def _trunk(x, mods, caches, P, batch, seq_len, tm, tq, tk):
    rope = None if caches is None else _rope_tables(seq_len)
    outs = {}
    sh1, sc1, g1, sh2, sc2, g2 = mods[0]
    qkv, rest = _hyb_in_proj(x, P['norm_mix'][0:1], sc1, sh1, P['hyb_w_in'], P['ssm_conv_w'][0],
                             P['ssm_conv_b'][0], seq_len, tm, 512)
    s_ret0 = None if caches is None else caches[0][:, 0]
    s_ssm0 = None if caches is None else caches[1][:, 0]
    y, s_ret, s_ssm = _hybrid_mixer(qkv, rest, P['ssm_dt_bias'][0], P['ssm_a_log'][0], P['ret_log_decay'][0],
                                    P['ssm_d'][0], s_ret0, s_ssm0, batch, seq_len)
    outs['ret'] = s_ret[:, None]
    outs['ssm'] = s_ssm[:, None]
    x = _hyb_out(y, rest, x, g1, P['ret_gn_gain'][0:1], P['ssm_norm_gain'][0:1], P['hyb_w_out'], seq_len, tm)
    x = _conv_ffn(x, P['norm_ffn'][0:1], sc2, sh2, g2, P['ffn_w_up'][0], P['ffn_conv_w'][0], P['ffn_conv_b'][0],
                  P['ffn_w_down'][0], seq_len, tm, 512)
    sh1, sc1, g1, sh2, sc2, g2 = mods[1]
    res = _qkv_proj(x, P['norm_mix'][1:2], sc1, sh1, P['attn_w_qkv'], P['attn_q_gain'][0], P['attn_k_gain'][0],
                    rope, seq_len, min(tm, seq_len))
    if caches is None:
        qt, k4, vt, kf, vf = res
        outs['k'] = kf.reshape(batch, 1, seq_len, ATT_KV_HEADS, ATT_HD)
        outs['v'] = vf.reshape(batch, 1, seq_len, ATT_KV_HEADS, ATT_HD)
    else:
        qt, k4, vt = res
        k4, vt = _keys_values(k4, vt, caches[2][:, 0], caches[3][:, 0])
    x = _attention(qt, k4, vt, x, g1, P['attn_w_o'], batch, seq_len, tq, tk)
    x = _conv_ffn(x, P['norm_ffn'][1:2], sc2, sh2, g2, P['ffn_w_up'][1], P['ffn_conv_w'][1], P['ffn_conv_b'][1],
                  P['ffn_w_down'][1], seq_len, tm, 512)
    return x, outs


def kernel(x_prompt, x_sample, state_ret, state_ssm, cache_attn_k, cache_attn_v, c, c_ctx, w_mod, b_mod, norm_mix,
           norm_ffn, ffn_w_up, ffn_conv_w, ffn_conv_b, ffn_w_down, hyb_w_in, hyb_w_out, ret_log_decay, ret_gn_gain,
           ssm_conv_w, ssm_conv_b, ssm_a_log, ssm_dt_bias, ssm_d, ssm_norm_gain, attn_w_qkv, attn_q_gain,
           attn_k_gain, attn_w_o):
    batch, seq, _ = x_prompt.shape
    dec_batch, dec_seq, _ = x_sample.shape
    depth = w_mod.shape[0]

    rows = -(-(dec_batch + 1) // SUBLANES) * SUBLANES
    cond = jnp.concatenate([c, c_ctx[None, :], jnp.zeros((rows - dec_batch - 1, D_MODEL), F32)], axis=0)
    mod = _modulation(cond, w_mod, b_mod).reshape(depth, rows, 6, 1, D_MODEL)
    mods_sample = [[mod[l, 0:dec_batch, t] for t in range(6)] for l in range(depth)]
    mods_prompt = [[mod[l, dec_batch:dec_batch + 1, t] for t in range(6)] for l in range(depth)]

    P = {
        'norm_mix': norm_mix, 'norm_ffn': norm_ffn,
        'hyb_w_in': hyb_w_in[0].astype(BF16),
        'hyb_w_out': hyb_w_out[0].astype(BF16),
        'ret_log_decay': ret_log_decay, 'ret_gn_gain': ret_gn_gain,
        'ssm_conv_w': ssm_conv_w, 'ssm_conv_b': ssm_conv_b, 'ssm_a_log': ssm_a_log, 'ssm_dt_bias': ssm_dt_bias,
        'ssm_d': ssm_d, 'ssm_norm_gain': ssm_norm_gain,
        'attn_w_qkv': attn_w_qkv[0].astype(BF16), 'attn_q_gain': attn_q_gain, 'attn_k_gain': attn_k_gain,
        'attn_w_o': attn_w_o[0].astype(BF16),
        'ffn_w_up': ffn_w_up.astype(BF16), 'ffn_conv_w': ffn_conv_w, 'ffn_conv_b': ffn_conv_b,
        'ffn_w_down': ffn_w_down.astype(BF16),
    }

    y_prompt, outs = _trunk(x_prompt.reshape(batch * seq, D_MODEL), mods_prompt, None, P, batch, seq,
                            tm=512 if (batch * seq) % 512 == 0 and 512 % seq == 0 else min(256, seq),
                            tq=min(256, seq), tk=min(256, seq))
    caches = (state_ret, state_ssm, cache_attn_k, cache_attn_v)
    y_sample, _ = _trunk(x_sample.reshape(dec_batch * dec_seq, D_MODEL), mods_sample, caches, P, dec_batch, dec_seq,
                         tm=512, tq=256, tk=256)
    return (y_prompt.reshape(batch, seq, D_MODEL), y_sample.reshape(dec_batch, dec_seq, D_MODEL),
            outs['ret'], outs['ssm'], outs['k'], outs['v'])
```

```python
import functools

import jax
import jax.numpy as jnp
from jax import lax
from jax.experimental import pallas as pl
from jax.experimental.pallas import tpu as pltpu

F32 = jnp.float32
BF16 = jnp.bfloat16
HIGHEST = lax.Precision.HIGHEST

D_MODEL = 1024
EPS = 1e-6
GRID_W = 64
CHUNK = 128
RET_HEADS = 8
RET_DK = 64
RET_DV = 128
RET_Q = RET_HEADS * RET_DK
RET_V = RET_HEADS * RET_DV
SSM_D_INNER = 1024
SSM_HEADDIM = 64
SSM_HEADS = 16
SSM_GROUPS = 4
SSM_RATIO = 4
SSM_STATE = 128
SSM_BC = SSM_GROUPS * SSM_STATE
SSM_XBC = SSM_D_INNER + 2 * SSM_BC
HYB_QKV = 2 * RET_Q + RET_V
HYB_BF = HYB_QKV + 2 * SSM_BC
XS_COL = 2 * RET_V
DT_COL = XS_COL + SSM_D_INNER
HYB_REST = DT_COL + 128
HYB_MIX = RET_V + SSM_D_INNER
ATT_HEADS = 16
ATT_KV_HEADS = 4
ATT_RATIO = 4
ATT_HD = 64
ATT_Q = ATT_HEADS * ATT_HD
ATT_KV = ATT_KV_HEADS * ATT_HD
ROT_FREQS = ATT_HD // 4
ROPE_BASE = 10000.0
FFN_HIDDEN = 2816

LANES = 128
SUBLANES = 8
MXU_COLS = 256
VMEM_LIMIT = 56 * 1024 * 1024
NEG_BIG = -1e30
Q_SCALE = ATT_HD ** -0.5 * 1.4426950408889634


def _cparams(sem):
    return pltpu.CompilerParams(dimension_semantics=sem, vmem_limit_bytes=VMEM_LIMIT)


def _norm_mod(x, gain, sc, sh):
    ms = jnp.mean(x * x, axis=-1, keepdims=True)
    return x * lax.rsqrt(ms + EPS) * gain * (1.0 + sc) + sh


def _vec_spec(width, tiles_per_batch, per_batch, grid_rank):
    if grid_rank == 1:
        return pl.BlockSpec((1, 1, width), lambda i: ((i // tiles_per_batch) if per_batch else 0, 0, 0))
    return pl.BlockSpec((1, 1, width), lambda i, j: ((i // tiles_per_batch) if per_batch else 0, 0, 0))


def _mod_body(c_ref, w_ref, b_ref, o_ref):
    cs = jax.nn.silu(c_ref[...])
    o_ref[0] = jnp.dot(cs, w_ref[0], precision=HIGHEST, preferred_element_type=F32) + b_ref[0]


def _modulation(cond, w_mod, b_mod):
    depth, _, n = w_mod.shape
    rows = cond.shape[0]
    tn = n // 4
    return pl.pallas_call(
        _mod_body,
        grid=(depth, n // tn),
        in_specs=[
            pl.BlockSpec((rows, D_MODEL), lambda l, j: (0, 0)),
            pl.BlockSpec((1, D_MODEL, tn), lambda l, j: (l, 0, j)),
            pl.BlockSpec((1, 1, tn), lambda l, j: (l, 0, j)),
        ],
        out_specs=pl.BlockSpec((1, rows, tn), lambda l, j: (l, 0, j)),
        out_shape=jax.ShapeDtypeStruct((depth, rows, n), F32),
        compiler_params=_cparams(("arbitrary", "arbitrary")),
        name="modulation",
    )(cond, w_mod, b_mod.reshape(depth, 1, n))


def _chunks(total, width):
    return [(s, min(width, total - s)) for s in range(0, total, width)]


def _resident(shape):
    return pl.BlockSpec(shape, lambda *_: (0,) * len(shape), pipeline_mode=pl.Buffered(1))


def _norm_mod_halo(seq_len, tm, x_ref, xn_ref, xp_ref, gain_ref, sc_ref, sh_ref, h_scr):
    gain, sc, sh = gain_ref[...], sc_ref[0], sh_ref[0]
    h_scr[0:tm] = _norm_mod(x_ref[...], gain, sc, sh).astype(BF16)
    if tm >= seq_len:
        h_scr[tm:tm + 2 * SUBLANES] = jnp.zeros((2 * SUBLANES, D_MODEL), BF16)
    else:
        tiles_per_seq = seq_len // tm
        pos = pl.program_id(0) % tiles_per_seq
        hn = jnp.where(pos == tiles_per_seq - 1, 0.0, _norm_mod(xn_ref[0], gain, sc, sh))
        hp = jnp.where(pos == 0, 0.0, _norm_mod(xp_ref[0], gain, sc, sh))
        h_scr[tm:tm + 2 * SUBLANES] = jnp.concatenate([hn, hp], axis=0).astype(BF16)


def _dwconv3_rows(u, cw, cb, tm, seq_len):
    rows = tm + 2 * SUBLANES
    before, after = pltpu.roll(u, 1, 0), pltpu.roll(u, rows - 1, 0)
    if tm > seq_len:
        pos = lax.broadcasted_iota(jnp.int32, u.shape, 0) % seq_len
        before = jnp.where(pos == 0, 0.0, before)
        after = jnp.where(pos == seq_len - 1, 0.0, after)
    c = before * cw[0:1] + u * cw[1:2] + after * cw[2:3] + cb
    return c[0:tm]


def _halo_specs(m, tm):
    m8 = m // SUBLANES
    r8 = tm // SUBLANES
    return [pl.BlockSpec((1, SUBLANES, D_MODEL), lambda i: (jnp.minimum((i + 1) * r8, m8 - 1), 0, 0)),
            pl.BlockSpec((1, SUBLANES, D_MODEL), lambda i: (jnp.maximum(i * r8 - 1, 0), 0, 0))]


def _hyb_in_body(seq_len, tm, tn, x_ref, xn_ref, xp_ref, gain_ref, sc_ref, sh_ref, w_ref, cw_ref, cb_ref,
                 bf_ref, rest_ref, h_scr):
    _norm_mod_halo(seq_len, tm, x_ref, xn_ref, xp_ref, gain_ref, sc_ref, sh_ref, h_scr)
    h_all = h_scr[...]
    h = h_scr[0:tm]
    w_g = HYB_QKV
    w_x = w_g + 2 * RET_V
    w_bc = w_x + SSM_D_INNER
    w_dt = w_bc + 2 * SSM_BC

    def project(wcol, width, conv_col):
        wcols = w_ref[:, wcol:wcol + width]
        if conv_col is None:
            return jnp.dot(h, wcols, preferred_element_type=F32)
        u = jnp.dot(h_all, wcols, preferred_element_type=F32)
        return jax.nn.silu(_dwconv3_rows(u, cw_ref[:, conv_col:conv_col + width],
                                         cb_ref[:, conv_col:conv_col + width], tm, seq_len))

    for col, width in _chunks(HYB_BF, tn):
        if col >= HYB_QKV:
            out = project(w_bc + col - HYB_QKV, width, SSM_D_INNER + col - HYB_QKV)
        else:
            out = project(col, width, None)
            if RET_Q <= col < 2 * RET_Q:
                out = out * (RET_DK ** -0.5)
        bf_ref[:, col:col + width] = out.astype(BF16)
    for col, width in _chunks(DT_COL, tn):
        rest_ref[:, col:col + width] = project(w_g + col, width, col - XS_COL if col >= XS_COL else None)
    dt = jnp.dot(h, w_ref[:, w_dt:w_dt + SSM_HEADS], preferred_element_type=F32)
    rest_ref[:, DT_COL:HYB_REST] = jnp.concatenate([dt, jnp.zeros((tm, HYB_REST - DT_COL - SSM_HEADS), F32)], axis=1)


def _hyb_in_proj(x, gain, sc, sh, w, conv_w, conv_b, seq_len, tm, tn):
    m = x.shape[0]
    per_batch = sc.shape[0] > 1
    tpb = seq_len // tm
    assert HYB_QKV % tn == 0 and XS_COL % tn == 0 and DT_COL % tn == 0
    x8 = x.reshape(m // SUBLANES, SUBLANES, D_MODEL)
    return pl.pallas_call(
        functools.partial(_hyb_in_body, seq_len, tm, tn),
        grid=(m // tm,),
        in_specs=[
            pl.BlockSpec((tm, D_MODEL), lambda i: (i, 0)),
            *_halo_specs(m, tm),
            pl.BlockSpec((1, D_MODEL), lambda i: (0, 0)),
            _vec_spec(D_MODEL, tpb, per_batch, 1),
            _vec_spec(D_MODEL, tpb, per_batch, 1),
            _resident(w.shape),
            _resident((3, SSM_XBC)),
            _resident((1, SSM_XBC)),
        ],
        out_specs=[pl.BlockSpec((tm, HYB_BF), lambda i: (i, 0)), pl.BlockSpec((tm, HYB_REST), lambda i: (i, 0))],
        out_shape=[jax.ShapeDtypeStruct((m, HYB_BF), BF16), jax.ShapeDtypeStruct((m, HYB_REST), F32)],
        scratch_shapes=[pltpu.VMEM((tm + 2 * SUBLANES, D_MODEL), BF16)],
        compiler_params=_cparams(("arbitrary",)),
        name="hybrid_in_proj",
    )(x, x8, x8, gain, sc, sh, w, conv_w, conv_b.reshape(1, SSM_XBC))


def _ffn_body(seq_len, tm, th, x_ref, xn_ref, xp_ref, gain_ref, sc_ref, sh_ref, gate_ref,
              wu_ref, cw_ref, cb_ref, wd_ref, o_ref, h_scr, act_scr):
    _norm_mod_halo(seq_len, tm, x_ref, xn_ref, xp_ref, gain_ref, sc_ref, sh_ref, h_scr)
    h = h_scr[...]

    def conv_up(col, width):
        u = jnp.dot(h, wu_ref[:, col:col + width], preferred_element_type=F32)
        return _dwconv3_rows(u, cw_ref[:, col:col + width], cb_ref[:, col:col + width], tm, seq_len)

    for col, width in _chunks(FFN_HIDDEN, th):
        act = jax.nn.silu(conv_up(col, width)) * conv_up(FFN_HIDDEN + col, width)
        act_scr[:, col:col + width] = act.astype(BF16)
    out = jnp.dot(act_scr[...], wd_ref[...], preferred_element_type=F32)
    o_ref[...] = x_ref[...] + gate_ref[0] * out


def _conv_ffn(x, gain, sc, sh, gate, w_up, conv_w, conv_b, w_down, seq_len, tm, th):
    m = x.shape[0]
    per_batch = sc.shape[0] > 1
    tpb = seq_len // tm
    x8 = x.reshape(m // SUBLANES, SUBLANES, D_MODEL)
    conv_b = conv_b.reshape(1, 2 * FFN_HIDDEN)
    vec = _vec_spec(D_MODEL, tpb, per_batch, 1)
    return pl.pallas_call(
        functools.partial(_ffn_body, seq_len, tm, th),
        grid=(m // tm,),
        in_specs=[
            pl.BlockSpec((tm, D_MODEL), lambda i: (i, 0)),
            *_halo_specs(m, tm),
            pl.BlockSpec((1, D_MODEL), lambda i: (0, 0)),
            vec, vec, vec,
            _resident((D_MODEL, 2 * FFN_HIDDEN)),
            _resident((3, 2 * FFN_HIDDEN)),
            _resident((1, 2 * FFN_HIDDEN)),
            _resident((FFN_HIDDEN, D_MODEL)),
        ],
        out_specs=pl.BlockSpec((tm, D_MODEL), lambda i: (i, 0)),
        out_shape=jax.ShapeDtypeStruct((m, D_MODEL), F32),
        scratch_shapes=[pltpu.VMEM((tm + 2 * SUBLANES, D_MODEL), BF16), pltpu.VMEM((tm, FFN_HIDDEN), BF16)],
        compiler_params=_cparams(("arbitrary",)),
        name="conv_ffn",
    )(x, x8, x8, gain, sc, sh, gate, w_up, conv_w, conv_b, w_down)


SSM_GW = SSM_RATIO * SSM_HEADDIM
FWD_CHUNKS_PER_STEP = 4


def _bcol(x, h):
    return jnp.broadcast_to(x[:, h:h + 1], (x.shape[0], LANES))


def _pair_sel(left, a, h):
    return jnp.where(left, _bcol(a, h), _bcol(a, h + 1))


def _expand_heads(a, spread_ref):
    hi = a.astype(BF16)
    lo = (a - hi.astype(F32)).astype(BF16)
    return (jnp.dot(hi, spread_ref[...], preferred_element_type=F32)
            + jnp.dot(lo, spread_ref[...], preferred_element_type=F32))


def _ssm_decay_row(left, efull, g):
    return jnp.concatenate([_pair_sel(left, efull, g * SSM_RATIO + 2 * m) for m in range(SSM_RATIO // 2)], axis=1)


def _load_states(s_ret, s_ssm, sret0_ref, sssm0_ref):
    if sret0_ref is None:
        s_ret[...] = jnp.zeros_like(s_ret)
        s_ssm[...] = jnp.zeros_like(s_ssm)
    else:
        s_ret[...] = sret0_ref[0]
        for g in range(SSM_GROUPS):
            s_ssm[g] = jnp.concatenate([sssm0_ref[0, g * SSM_RATIO + r] for r in range(SSM_RATIO)], axis=1)


def _store_states(s_ret, s_ssm, sret_ref, sssm_ref):
    sret_ref[0] = s_ret[...]
    for g in range(SSM_GROUPS):
        for r in range(SSM_RATIO):
            sssm_ref[0, g * SSM_RATIO + r] = s_ssm[g][:, r * SSM_HEADDIM:(r + 1) * SSM_HEADDIM]


def _ssm_log_decay(dt_raw, dtb, alog, lane):
    dt = jax.nn.softplus(dt_raw + dtb)
    return dt, jnp.where(lane < SSM_HEADS, dt * (-jnp.exp(alog)), 0.0)


def _bwd_state_body(has_init, k_ref, v_ref, xs_ref, bm_ref, dt_ref, dtb_ref, alog_ref, spread_ref, logdec_ref,
                    *rest):
    sret0_ref = sssm0_ref = None
    if has_init:
        sret0_ref, sssm0_ref = rest[0], rest[1]
        rest = rest[2:]
    sret_in_ref, sssm_in_ref, sret_fin_ref, sssm_fin_ref, s_ret, s_ssm, wcol = rest
    i = pl.program_id(1)
    n = pl.num_programs(1)
    ii = lax.broadcasted_iota(jnp.int32, (CHUNK, CHUNK), 0)
    jj = lax.broadcasted_iota(jnp.int32, (CHUNK, CHUNK), 1)
    lane = lax.broadcasted_iota(jnp.int32, (1, LANES), 1)
    left = lane < SSM_HEADDIM

    @pl.when(i == 0)
    def _():
        _load_states(s_ret, s_ssm, sret0_ref, sssm0_ref)
        for h in range(RET_HEADS):
            wcol[h] = jnp.exp(ii.astype(F32) * logdec_ref[1, h])

    cps = sret_in_ref.shape[1]
    full = jnp.full((1, LANES), float(CHUNK), F32)
    prep = []
    for c in range(cps):
        rows = slice(c * CHUNK, (c + 1) * CHUNK)
        dt, la = _ssm_log_decay(dt_ref[rows], dtb_ref[...], alog_ref[...], lane)
        rc = jnp.dot((jj >= ii).astype(F32), la, precision=HIGHEST, preferred_element_type=F32)
        last = rc[0:1]
        prep.append((_expand_heads(jnp.exp(last - rc) * dt, spread_ref), jnp.exp(last)))

    for c in reversed(range(cps)):
        rows = slice(c * CHUNK, (c + 1) * CHUNK)
        sret_in_ref[0, c] = s_ret[...].astype(BF16)
        sssm_in_ref[0, c] = s_ssm[...].astype(BF16)
        for h in range(RET_HEADS):
            kh = k_ref[rows, h * RET_DK:(h + 1) * RET_DK]
            vw = (v_ref[rows, h * RET_DV:(h + 1) * RET_DV].astype(F32) * wcol[h]).astype(BF16)
            upd = lax.dot_general(kh, vw, (((0,), (0,)), ((), ())), preferred_element_type=F32)
            s_ret[h] = s_ret[h] * jnp.exp(full * logdec_ref[1, h]) + upd
        wx, efull = prep[c]
        for g in range(SSM_GROUPS):
            vw = (xs_ref[rows, g * SSM_GW:(g + 1) * SSM_GW] * wx[:, g * SSM_GW:(g + 1) * SSM_GW]).astype(BF16)
            bg = bm_ref[rows, g * SSM_STATE:(g + 1) * SSM_STATE]
            upd = lax.dot_general(bg, vw, (((0,), (0,)), ((), ())), preferred_element_type=F32)
            s_ssm[g] = s_ssm[g] * _ssm_decay_row(left, efull, g) + upd

    @pl.when(i == n - 1)
    def _():
        _store_states(s_ret, s_ssm, sret_fin_ref, sssm_fin_ref)


def _hyb_fwd_body(has_init, qkv_ref, xs_ref, bc_ref, dt_ref, sret_in_ref, sssm_in_ref, dtb_ref, alog_ref, dsk_ref,
                  spread_ref, logdec_ref, *rest):
    sret0_ref = sssm0_ref = None
    if has_init:
        sret0_ref, sssm0_ref = rest[0], rest[1]
        rest = rest[2:]
    y_ref, sret_fin_ref, sssm_fin_ref, s_ret, s_ssm, dcomb, ein0, ein1, wcol = rest
    i = pl.program_id(1)
    n = pl.num_programs(1)
    ii = lax.broadcasted_iota(jnp.int32, (CHUNK, CHUNK), 0)
    jj = lax.broadcasted_iota(jnp.int32, (CHUNK, CHUNK), 1)
    lane = lax.broadcasted_iota(jnp.int32, (1, LANES), 1)
    left = lane < SSM_HEADDIM
    lower = jj <= ii

    @pl.when(i == 0)
    def _():
        _load_states(s_ret, s_ssm, sret0_ref, sssm0_ref)
        fi, fj = ii.astype(F32), jj.astype(F32)
        for h in range(RET_HEADS):
            la0, la1 = logdec_ref[0, h], logdec_ref[1, h]
            dec = jnp.exp(jnp.where(lower, (fi - fj) * la0, (fj - fi) * la1))
            dcomb[h] = jnp.where(ii == jj, 2.0, dec)
            ein0[h] = jnp.exp((fi + 1.0) * la0)
            ein1[h] = jnp.exp((CHUNK - fi) * la1)
            wcol[h] = jnp.exp((CHUNK - 1.0 - fi) * la0)

    for c in range(sret_in_ref.shape[1]):
        _fwd_chunk(c, qkv_ref, xs_ref, bc_ref, dt_ref, sret_in_ref, sssm_in_ref, dtb_ref, alog_ref, dsk_ref,
                   spread_ref, logdec_ref, y_ref, s_ret, s_ssm, dcomb, ein0, ein1, wcol)

    @pl.when(i == n - 1)
    def _():
        _store_states(s_ret, s_ssm, sret_fin_ref, sssm_fin_ref)


def _fwd_chunk(c, qkv_ref, xs_ref, bc_ref, dt_ref, sret_in_ref, sssm_in_ref, dtb_ref, alog_ref, dsk_ref,
               spread_ref, logdec_ref, y_ref, s_ret, s_ssm, dcomb, ein0, ein1, wcol):
    rows = pl.ds(c * CHUNK, CHUNK)
    qkv_ref, xs_ref, bc_ref, dt_ref, y_ref = (r.at[rows] for r in (qkv_ref, xs_ref, bc_ref, dt_ref, y_ref))
    ii = lax.broadcasted_iota(jnp.int32, (CHUNK, CHUNK), 0)
    jj = lax.broadcasted_iota(jnp.int32, (CHUNK, CHUNK), 1)
    lane = lax.broadcasted_iota(jnp.int32, (1, LANES), 1)
    left = lane < SSM_HEADDIM
    lower = jj <= ii

    def ret_head(h):
        return (qkv_ref[:, h * RET_DK:(h + 1) * RET_DK],
                qkv_ref[:, RET_Q + h * RET_DK:RET_Q + (h + 1) * RET_DK],
                qkv_ref[:, 2 * RET_Q + h * RET_DV:2 * RET_Q + (h + 1) * RET_DV])

    def ssm_group(g):
        return (bc_ref[:, g * SSM_STATE:(g + 1) * SSM_STATE],
                bc_ref[:, SSM_BC + g * SSM_STATE:SSM_BC + (g + 1) * SSM_STATE])

    dt_raw = dt_ref[...]
    dt0, la0 = _ssm_log_decay(dt_raw, dtb_ref[0:1], alog_ref[0:1], lane)
    dt1, la1 = _ssm_log_decay(dt_raw, dtb_ref[1:2], alog_ref[1:2], lane)
    c0 = jnp.dot(lower.astype(F32), la0, precision=HIGHEST, preferred_element_type=F32)
    c1 = jnp.dot((jj >= ii).astype(F32), la1, precision=HIGHEST, preferred_element_type=F32)

    ret_scores, ret_inter = [], []
    for h in range(RET_HEADS):
        qh, kh, _ = ret_head(h)
        ret_scores.append(lax.dot_general(qh, kh, (((1,), (1,)), ((), ())), preferred_element_type=F32))
        states = jnp.concatenate([s_ret[h].astype(BF16), sret_in_ref[0, c, h]], axis=1)
        ret_inter.append(jnp.dot(qh, states, preferred_element_type=F32))
    ssm_scores, ssm_inter = [], []
    for g in range(SSM_GROUPS):
        bg, cg = ssm_group(g)
        ssm_scores.append(lax.dot_general(cg, bg, (((1,), (1,)), ((), ())), preferred_element_type=F32))
        states = jnp.concatenate([s_ssm[g].astype(BF16), sssm_in_ref[0, c, g]], axis=1)
        ssm_inter.append(jnp.dot(cg, states, preferred_element_type=F32))

    full = jnp.full((1, LANES), float(CHUNK), F32)
    for h in range(RET_HEADS):
        _, kh, vh = ret_head(h)
        y = jnp.dot((ret_scores[h] * dcomb[h]).astype(BF16), vh, preferred_element_type=F32)
        inter = ret_inter[h]
        y_ref[:, h * RET_DV:(h + 1) * RET_DV] = y + inter[:, 0:RET_DV] * ein0[h] + inter[:, RET_DV:] * ein1[h]
        vw = (vh.astype(F32) * wcol[h]).astype(BF16)
        upd = lax.dot_general(kh, vw, (((0,), (0,)), ((), ())), preferred_element_type=F32)
        s_ret[h] = s_ret[h] * jnp.exp(full * logdec_ref[0, h]) + upd

    c0kt, c1kt = (c0 - jnp.log(dt0)).T, (c1 - jnp.log(dt1)).T
    dst = (dt0 + dt1).T
    last0 = c0[CHUNK - 1:CHUNK]
    w0x = _expand_heads(jnp.exp(last0 - c0) * dt0, spread_ref)
    efull0 = jnp.exp(last0)
    diag = jj == ii
    for g in range(SSM_GROUPS):
        bg, _ = ssm_group(g)
        s = ssm_scores[g]
        inter = ssm_inter[g]
        vws = []
        for m in range(SSM_RATIO // 2):
            ha = g * SSM_RATIO + 2 * m
            ps, c0bs, c1bs = [], [], []
            for h in (ha, ha + 1):
                c0b, c1b = _bcol(c0, h), _bcol(c1, h)
                arg = jnp.where(lower, c0b - c0kt[h:h + 1], c1b - c1kt[h:h + 1])
                ps.append((s * jnp.where(diag, dst[h:h + 1], jnp.exp(arg))).astype(BF16))
                c0bs.append(c0b)
                c1bs.append(c1b)
            col = (g * 2 + m) * LANES
            xs_pair = xs_ref[:, col:col + LANES]
            vals = jnp.concatenate([jnp.where(left, xs_pair, 0.0), jnp.where(left, 0.0, xs_pair)], axis=0)
            y = jnp.dot(jnp.concatenate(ps, axis=1), vals.astype(BF16), preferred_element_type=F32)
            y = y + inter[:, m * LANES:(m + 1) * LANES] * jnp.exp(jnp.where(left, c0bs[0], c0bs[1]))
            y = y + (inter[:, SSM_GW + m * LANES:SSM_GW + (m + 1) * LANES]
                     * jnp.exp(jnp.where(left, c1bs[0], c1bs[1])))
            y_ref[:, RET_V + col:RET_V + col + LANES] = y + dsk_ref[:, col:col + LANES] * xs_pair
            vws.append((xs_pair * w0x[:, col:col + LANES]).astype(BF16))
        upd = lax.dot_general(bg, jnp.concatenate(vws, axis=1), (((0,), (0,)), ((), ())),
                              preferred_element_type=F32)
        s_ssm[g] = s_ssm[g] * _ssm_decay_row(left, efull0, g) + upd


def _hybrid_mixer(qkv, rest, dt_bias, a_log, log_decay, d_skip, s_ret0, s_ssm0, batch, seq_len):
    m = qkv.shape[0]
    n = seq_len // CHUNK
    has_init = s_ret0 is not None
    dt_blk = DT_COL // LANES
    pad = lambda v: jnp.pad(v.astype(F32), ((0, 0), (0, LANES - v.shape[1])))
    dtb, alog = pad(dt_bias), pad(a_log)
    logdec = log_decay.astype(F32)
    spread = (lax.broadcasted_iota(jnp.int32, (LANES, SSM_D_INNER), 0)
              == lax.broadcasted_iota(jnp.int32, (LANES, SSM_D_INNER), 1) // SSM_HEADDIM).astype(BF16)
    spread_spec = pl.BlockSpec((LANES, SSM_D_INNER), lambda b, i: (0, 0))
    dsk = jnp.repeat((d_skip[0] + d_skip[1]).astype(F32), SSM_HEADDIM).reshape(1, SSM_D_INNER)
    smem = pl.BlockSpec(memory_space=pltpu.SMEM)
    ret_spec = pl.BlockSpec((1, RET_HEADS, RET_DK, RET_DV), lambda b, i: (b, 0, 0, 0))
    ssm_spec = pl.BlockSpec((1, SSM_HEADS, SSM_STATE, SSM_HEADDIM), lambda b, i: (b, 0, 0, 0))
    ret_shape = jax.ShapeDtypeStruct((batch, RET_HEADS, RET_DK, RET_DV), F32)
    ssm_shape = jax.ShapeDtypeStruct((batch, SSM_HEADS, SSM_STATE, SSM_HEADDIM), F32)
    ret_scr = pltpu.VMEM((RET_HEADS, RET_DK, RET_DV), F32)
    ssm_scr = pltpu.VMEM((SSM_GROUPS, SSM_STATE, SSM_GW), F32)
    const_scr = pltpu.VMEM((RET_HEADS, CHUNK, CHUNK), F32)

    cps = next(c for c in (8, 4, 2, 1) if n % c == 0)
    rsteps = n // cps
    rrows = cps * CHUNK

    def rev(b, i):
        return b * rsteps + rsteps - 1 - i

    in_specs = [
        pl.BlockSpec((rrows, RET_Q), lambda b, i: (rev(b, i), 1)),
        pl.BlockSpec((rrows, RET_V), lambda b, i: (rev(b, i), 1)),
        pl.BlockSpec((rrows, SSM_D_INNER), lambda b, i: (rev(b, i), XS_COL // SSM_D_INNER)),
        pl.BlockSpec((rrows, SSM_BC), lambda b, i: (rev(b, i), HYB_QKV // SSM_BC)),
        pl.BlockSpec((rrows, LANES), lambda b, i: (rev(b, i), dt_blk)),
        pl.BlockSpec((1, LANES), lambda b, i: (0, 0)),
        pl.BlockSpec((1, LANES), lambda b, i: (0, 0)),
        spread_spec,
        smem,
    ]
    args = [qkv, qkv, rest, qkv, rest, dtb[1:2], alog[1:2], spread, logdec]
    if has_init:
        in_specs += [ret_spec, ssm_spec]
        args += [s_ret0[:, 1], s_ssm0[:, 1]]
    sret_in, sssm_in, sret1, sssm1 = pl.pallas_call(
        functools.partial(_bwd_state_body, has_init),
        grid=(batch, rsteps),
        in_specs=in_specs,
        out_specs=[
            pl.BlockSpec((1, cps, RET_HEADS, RET_DK, RET_DV), lambda b, i: (b, rsteps - 1 - i, 0, 0, 0)),
            pl.BlockSpec((1, cps, SSM_GROUPS, SSM_STATE, SSM_GW), lambda b, i: (b, rsteps - 1 - i, 0, 0, 0)),
            ret_spec, ssm_spec,
        ],
        out_shape=[
            jax.ShapeDtypeStruct((batch, n, RET_HEADS, RET_DK, RET_DV), BF16),
            jax.ShapeDtypeStruct((batch, n, SSM_GROUPS, SSM_STATE, SSM_GW), BF16),
            ret_shape, ssm_shape,
        ],
        scratch_shapes=[ret_scr, ssm_scr, const_scr],
        compiler_params=_cparams(("arbitrary", "arbitrary")),
        name="hybrid_reverse_states",
    )(*args)

    fcps = next(c for c in (FWD_CHUNKS_PER_STEP, 2, 1) if n % c == 0)
    fsteps = n // fcps
    frows = fcps * CHUNK

    def fwd(b, i):
        return b * fsteps + i

    in_specs = [
        pl.BlockSpec((frows, HYB_QKV), lambda b, i: (fwd(b, i), 0)),
        pl.BlockSpec((frows, SSM_D_INNER), lambda b, i: (fwd(b, i), XS_COL // SSM_D_INNER)),
        pl.BlockSpec((frows, 2 * SSM_BC), lambda b, i: (fwd(b, i), HYB_QKV // (2 * SSM_BC))),
        pl.BlockSpec((frows, LANES), lambda b, i: (fwd(b, i), dt_blk)),
        pl.BlockSpec((1, fcps, RET_HEADS, RET_DK, RET_DV), lambda b, i: (b, i, 0, 0, 0)),
        pl.BlockSpec((1, fcps, SSM_GROUPS, SSM_STATE, SSM_GW), lambda b, i: (b, i, 0, 0, 0)),
        pl.BlockSpec((2, LANES), lambda b, i: (0, 0)),
        pl.BlockSpec((2, LANES), lambda b, i: (0, 0)),
        pl.BlockSpec((1, SSM_D_INNER), lambda b, i: (0, 0)),
        spread_spec,
        smem,
    ]
    args = [qkv, rest, qkv, rest, sret_in, sssm_in, dtb, alog, dsk, spread, logdec]
    if has_init:
        in_specs += [ret_spec, ssm_spec]
        args += [s_ret0[:, 0], s_ssm0[:, 0]]
    y, sret0, sssm0 = pl.pallas_call(
        functools.partial(_hyb_fwd_body, has_init),
        grid=(batch, fsteps),
        in_specs=in_specs,
        out_specs=[pl.BlockSpec((frows, HYB_MIX), lambda b, i: (fwd(b, i), 0)), ret_spec, ssm_spec],
        out_shape=[jax.ShapeDtypeStruct((m, HYB_MIX), F32), ret_shape, ssm_shape],
        scratch_shapes=[ret_scr, ssm_scr, const_scr, const_scr, const_scr, const_scr],
        compiler_params=_cparams(("arbitrary", "arbitrary")),
        name="hybrid_forward_mix",
    )(*args)
    return y, jnp.stack([sret0, sret1], axis=1), jnp.stack([sssm0, sssm1], axis=1)


def _hyb_out_body(y_ref, g_ref, z_ref, x_ref, gate_ref, gn_ref, ng_ref, w_ref, o_ref):
    y = y_ref[...]
    parts = []
    for h in range(RET_HEADS):
        yh = y[:, h * RET_DV:(h + 1) * RET_DV]
        mu = jnp.mean(yh, axis=-1, keepdims=True)
        var = jnp.mean(jnp.square(yh - mu), axis=-1, keepdims=True)
        parts.append((yh - mu) * lax.rsqrt(var + EPS))
    y_ret = jax.nn.silu(g_ref[...]) * (jnp.concatenate(parts, axis=1) * gn_ref[...])
    yz = y[:, RET_V:] * jax.nn.silu(z_ref[...])
    y_ssm = yz * lax.rsqrt(jnp.mean(yz * yz, axis=-1, keepdims=True) + EPS) * ng_ref[...]
    out = jnp.dot(y_ret.astype(BF16), w_ref[0:RET_V], preferred_element_type=F32)
    out = out + jnp.dot(y_ssm.astype(BF16), w_ref[RET_V:HYB_MIX], preferred_element_type=F32)
    o_ref[...] = x_ref[...] + gate_ref[0] * out


def _hyb_out(y, rest, x, gate, gn_gain, norm_gain, w_out, seq_len, tm):
    m = x.shape[0]
    per_batch = gate.shape[0] > 1
    tpb = seq_len // tm
    return pl.pallas_call(
        _hyb_out_body,
        grid=(m // tm,),
        in_specs=[
            pl.BlockSpec((tm, HYB_MIX), lambda i: (i, 0)),
            pl.BlockSpec((tm, RET_V), lambda i: (i, 0)),
            pl.BlockSpec((tm, SSM_D_INNER), lambda i: (i, 1)),
            pl.BlockSpec((tm, D_MODEL), lambda i: (i, 0)),
            _vec_spec(D_MODEL, tpb, per_batch, 1),
            pl.BlockSpec((1, RET_V), lambda i: (0, 0)),
            pl.BlockSpec((1, SSM_D_INNER), lambda i: (0, 0)),
            pl.BlockSpec((HYB_MIX, D_MODEL), lambda i: (0, 0)),
        ],
        out_specs=pl.BlockSpec((tm, D_MODEL), lambda i: (i, 0)),
        out_shape=jax.ShapeDtypeStruct((m, D_MODEL), F32),
        compiler_params=_cparams(("arbitrary",)),
        name="hybrid_out_proj",
    )(y, rest, rest, x, gate, gn_gain, norm_gain, w_out)


def _head_rms(t, seg_ref):
    outs = []
    for c in range(t.shape[1] // LANES):
        tc = t[:, c * LANES:(c + 1) * LANES]
        sq = tc * tc
        hi = sq.astype(BF16)
        lo = (sq - hi.astype(F32)).astype(BF16)
        ssum = (jnp.dot(hi, seg_ref[...], preferred_element_type=F32)
                + jnp.dot(lo, seg_ref[...], preferred_element_type=F32))
        outs.append(tc * lax.rsqrt(ssum * (1.0 / ATT_HD) + EPS))
    return outs


def _rope(tc, cos, sin, lane):
    fwd = pltpu.roll(tc, LANES - ROT_FREQS, 1)
    bwd = pltpu.roll(tc, ROT_FREQS, 1)
    return tc * cos + jnp.where(lane % (2 * ROT_FREQS) < ROT_FREQS, fwd, bwd) * sin


def _qkv_body(has_rope, x_ref, gain_ref, sc_ref, sh_ref, w_ref, seg_ref, qg_ref, kg_ref, *rest):
    if has_rope:
        cos_ref, sin_ref, q_ref, k_ref, v_ref = rest
    else:
        q_ref, k_ref, v_ref, kf_ref, vf_ref = rest
    h = _norm_mod(x_ref[...], gain_ref[...], sc_ref[0], sh_ref[0]).astype(BF16)
    lane = lax.broadcasted_iota(jnp.int32, (1, LANES), 1)

    def project(col):
        t = jnp.dot(h, w_ref[:, col:col + MXU_COLS], preferred_element_type=F32)
        return [t[:, c * LANES:(c + 1) * LANES] for c in range(MXU_COLS // LANES)]

    def normed(tiles, gain_ref_):
        tiles = [t * gain_ref_[...] for t in _head_rms(jnp.concatenate(tiles, axis=1), seg_ref)]
        return tiles, ([_rope(t, cos_ref[...], sin_ref[...], lane) for t in tiles] if has_rope else tiles)

    def emit_q(col, tiles):
        _, q = normed(tiles, qg_ref)
        for c, t in enumerate(q):
            row = col + c * LANES
            q_ref[0, row:row + LANES, :] = (t * Q_SCALE).T.astype(BF16)

    def emit_k(col, tiles):
        k, kr = normed(tiles, kg_ref)
        for c in range(MXU_COLS // LANES):
            lo = col + c * LANES
            if not has_rope:
                kf_ref[:, lo:lo + LANES] = k[c]
            kc = kr[c].astype(BF16)
            for a in range(LANES // ATT_HD):
                k_ref[0, lo // ATT_HD + a] = kc[:, a * ATT_HD:(a + 1) * ATT_HD]

    def emit_v(col, tiles):
        for c in range(MXU_COLS // LANES):
            lo = col + c * LANES
            if not has_rope:
                vf_ref[:, lo:lo + LANES] = tiles[c]
            vc = tiles[c].T.astype(BF16)
            for a in range(LANES // ATT_HD):
                v_ref[0, lo // ATT_HD + a] = vc[a * ATT_HD:(a + 1) * ATT_HD]

    work = ([(emit_q, col, col) for col in range(0, ATT_Q, MXU_COLS)]
            + [(emit_k, col, ATT_Q + col) for col in range(0, ATT_KV, MXU_COLS)]
            + [(emit_v, col, ATT_Q + ATT_KV + col) for col in range(0, ATT_KV, MXU_COLS)])
    tiles = project(work[0][2])
    for n, (emit, col, _) in enumerate(work):
        ahead = project(work[n + 1][2]) if n + 1 < len(work) else None
        emit(col, tiles)
        tiles = ahead


def _qkv_proj(x, gain, sc, sh, w_qkv, q_gain, k_gain, rope, seq_len, tm):
    m = x.shape[0]
    per_batch = sc.shape[0] > 1
    tpb = seq_len // tm
    has_rope = rope is not None
    seg = (lax.broadcasted_iota(jnp.int32, (LANES, LANES), 0) // ATT_HD
           == lax.broadcasted_iota(jnp.int32, (LANES, LANES), 1) // ATT_HD).astype(BF16)
    tile2 = lambda v: jnp.tile(v.astype(F32), LANES // ATT_HD).reshape(1, LANES)
    in_specs = [
        pl.BlockSpec((tm, D_MODEL), lambda i: (i, 0)),
        pl.BlockSpec((1, D_MODEL), lambda i: (0, 0)),
        _vec_spec(D_MODEL, tpb, per_batch, 1),
        _vec_spec(D_MODEL, tpb, per_batch, 1),
        pl.BlockSpec((D_MODEL, ATT_Q + 2 * ATT_KV), lambda i: (0, 0)),
        pl.BlockSpec((LANES, LANES), lambda i: (0, 0)),
        pl.BlockSpec((1, LANES), lambda i: (0, 0)),
        pl.BlockSpec((1, LANES), lambda i: (0, 0)),
    ]
    args = [x, gain, sc, sh, w_qkv, seg, tile2(q_gain), tile2(k_gain)]
    batch = m // seq_len
    out_specs = [pl.BlockSpec((1, ATT_Q, tm), lambda i: (i // tpb, 0, i % tpb)),
                 pl.BlockSpec((1, ATT_KV_HEADS, tm, ATT_HD), lambda i: (i // tpb, 0, i % tpb, 0)),
                 pl.BlockSpec((1, ATT_KV_HEADS, ATT_HD, tm), lambda i: (i // tpb, 0, 0, i % tpb))]
    out_shape = [jax.ShapeDtypeStruct((batch, ATT_Q, seq_len), BF16),
                 jax.ShapeDtypeStruct((batch, ATT_KV_HEADS, seq_len, ATT_HD), BF16),
                 jax.ShapeDtypeStruct((batch, ATT_KV_HEADS, ATT_HD, seq_len), BF16)]
    if has_rope:
        in_specs += [pl.BlockSpec((tm, LANES), lambda i: (i % tpb, 0))] * 2
        args += list(rope)
    else:
        out_specs += [pl.BlockSpec((tm, ATT_KV), lambda i: (i, 0))] * 2
        out_shape += [jax.ShapeDtypeStruct((m, ATT_KV), F32)] * 2
    return pl.pallas_call(
        functools.partial(_qkv_body, has_rope),
        grid=(m // tm,),
        in_specs=in_specs,
        out_specs=out_specs,
        out_shape=out_shape,
        compiler_params=_cparams(("arbitrary",)),
        name="attn_qkv_proj",
    )(*args)


def _rope_tables(seq_len):
    rows = seq_len // GRID_W
    row = jnp.repeat(jnp.arange(rows, dtype=F32), GRID_W)
    col = jnp.tile(jnp.arange(GRID_W, dtype=F32), rows)
    inv = ROPE_BASE ** (-jnp.arange(ROT_FREQS, dtype=F32) / ROT_FREQS)
    ar, ac = row[:, None] * inv, col[:, None] * inv
    cos = jnp.concatenate([jnp.cos(ar), jnp.cos(ar), jnp.cos(ac), jnp.cos(ac)], axis=1)
    sin = jnp.concatenate([-jnp.sin(ar), jnp.sin(ar), -jnp.sin(ac), jnp.sin(ac)], axis=1)
    return jnp.tile(cos, (1, LANES // ATT_HD)), jnp.tile(sin, (1, LANES // ATT_HD))


V_ROWS = ATT_HD + 16
ATT_SLOTS = 4


def _attn_body(tq, tk, qt_ref, k_ref, vt_ref, x_ref, gate_ref, wo_ref, o_ref, acc_scr, s_scr, ot_scr):
    nkv = k_ref.shape[2] // tk
    ones_rows = jnp.ones((V_ROWS - ATT_HD, tk), BF16)
    for g in range(ATT_KV_HEADS):
        qts = [qt_ref[0, (g * ATT_RATIO + r) * ATT_HD:(g * ATT_RATIO + r + 1) * ATT_HD, :]
               for r in range(ATT_RATIO)]
        acc_scr[...] = jnp.zeros_like(acc_scr)

        def scores(t, slot):
            kt = k_ref[0, g, pl.ds(pl.multiple_of(t * tk, tk), tk), :]
            tops = []
            for r in range(ATT_RATIO):
                s = jnp.dot(kt, qts[r], preferred_element_type=F32)
                s_scr[slot, r, 0:tk] = s
                tops.append(jnp.max(s, axis=0, keepdims=True))
            return tuple(tops)

        def softmax_pv(t, slot, ms, tops):
            vt = jnp.concatenate([vt_ref[0, g, :, pl.ds(pl.multiple_of(t * tk, tk), tk)], ones_rows], axis=0)
            new_ms = [jnp.maximum(ms[r], tops[r]) for r in range(ATT_RATIO)]
            ps = [jnp.exp2(s_scr[slot, r, 0:tk] - new_ms[r]).astype(BF16) for r in range(ATT_RATIO)]
            for r in range(ATT_RATIO):
                acc_scr[r] = (jnp.exp2(ms[r] - new_ms[r]) * acc_scr[r]
                              + jnp.dot(vt, ps[r], preferred_element_type=F32))
            return tuple(new_ms)

        def quad_step(i, carry):
            ms, top0, top1 = carry
            t = i * ATT_SLOTS
            top2 = scores(t + 2, 2)
            top3 = scores(t + 3, 3)
            ms = softmax_pv(t, 0, ms, top0)
            ms = softmax_pv(t + 1, 1, ms, top1)
            top0 = scores(t + 4, 0)
            top1 = scores(t + 5, 1)
            ms = softmax_pv(t + 2, 2, ms, top2)
            return softmax_pv(t + 3, 3, ms, top3), top0, top1

        tops = {t: scores(t, t) for t in range(min(2, nkv))}
        ms = tuple(jnp.full((1, tq), NEG_BIG, F32) for _ in range(ATT_RATIO))
        quads = max(nkv - 2, 0) // ATT_SLOTS
        if quads:
            ms, top0, top1 = lax.fori_loop(0, quads, quad_step, (ms, tops[0], tops[1]))
            tops = {quads * ATT_SLOTS: top0, quads * ATT_SLOTS + 1: top1}
        for t in range(quads * ATT_SLOTS, nkv):
            if t % 2 == 0:
                for ahead in (t + 2, t + 3):
                    if ahead < nkv:
                        tops[ahead] = scores(ahead, ahead % ATT_SLOTS)
            ms = softmax_pv(t, t % ATT_SLOTS, ms, tops.pop(t))
        for r in range(ATT_RATIO):
            hd = g * ATT_RATIO + r
            a = acc_scr[r]
            ot_scr[hd * ATT_HD:(hd + 1) * ATT_HD, :] = (a[0:ATT_HD] / a[ATT_HD:ATT_HD + 1]).astype(BF16)
    out = lax.dot_general(ot_scr[...], wo_ref[...], (((0,), (0,)), ((), ())), preferred_element_type=F32)
    o_ref[0] = x_ref[0] + gate_ref[0] * out


def _attention(qt, k4, vt, x, gate, w_o, batch, seq_len, tq, tk):
    per_batch = gate.shape[0] > 1
    nkeys = k4.shape[2]
    x3 = x.reshape(batch, seq_len, D_MODEL)
    out = pl.pallas_call(
        functools.partial(_attn_body, tq, tk),
        grid=(batch, seq_len // tq),
        in_specs=[
            pl.BlockSpec((1, ATT_Q, tq), lambda b, i: (b, 0, i)),
            pl.BlockSpec((1, ATT_KV_HEADS, nkeys, ATT_HD), lambda b, i: (b, 0, 0, 0)),
            pl.BlockSpec((1, ATT_KV_HEADS, ATT_HD, nkeys), lambda b, i: (b, 0, 0, 0)),
            pl.BlockSpec((1, tq, D_MODEL), lambda b, i: (b, i, 0)),
            pl.BlockSpec((1, 1, D_MODEL), lambda b, i: (b if per_batch else 0, 0, 0)),
            pl.BlockSpec((ATT_Q, D_MODEL), lambda b, i: (0, 0)),
        ],
        out_specs=pl.BlockSpec((1, tq, D_MODEL), lambda b, i: (b, i, 0)),
        out_shape=jax.ShapeDtypeStruct((batch, seq_len, D_MODEL), F32),
        scratch_shapes=[pltpu.VMEM((ATT_RATIO, V_ROWS, tq), F32), pltpu.VMEM((ATT_SLOTS, ATT_RATIO, tk + SUBLANES, tq), F32),
                        pltpu.VMEM((ATT_Q, tq), BF16)],
        compiler_params=_cparams(("arbitrary", "arbitrary")),
        name="attention",
    )(qt, k4, vt, x3, gate, w_o)
    return out.reshape(batch * seq_len, D_MODEL)


def _keys_values(k4, vt, cache_k, cache_v):
    k4 = jnp.concatenate([jnp.swapaxes(cache_k, 1, 2).astype(BF16), k4], axis=2)
    vt = jnp.concatenate([jnp.transpose(cache_v, (0, 2, 3, 1)).astype(BF16), vt], axis=3)
    return k4, vt


def _trunk(x, mods, caches, P, batch, seq_len, tm, tq, tk):
    rope = None if caches is None else _rope_tables(seq_len)
    outs = {}
    sh1, sc1, g1, sh2, sc2, g2 = mods[0]
    qkv, rest = _hyb_in_proj(x, P['norm_mix'][0:1], sc1, sh1, P['hyb_w_in'], P['ssm_conv_w'][0],
                             P['ssm_conv_b'][0], seq_len, tm, 512)
    s_ret0 = None if caches is None else caches[0][:, 0]
    s_ssm0 = None if caches is None else caches[1][:, 0]
    y, s_ret, s_ssm = _hybrid_mixer(qkv, rest, P['ssm_dt_bias'][0], P['ssm_a_log'][0], P['ret_log_decay'][0],
                                    P['ssm_d'][0], s_ret0, s_ssm0, batch, seq_len)
    outs['ret'] = s_ret[:, None]
    outs['ssm'] = s_ssm[:, None]
    x = _hyb_out(y, rest, x, g1, P['ret_gn_gain'][0:1], P['ssm_norm_gain'][0:1], P['hyb_w_out'], seq_len, tm)
    x = _conv_ffn(x, P['norm_ffn'][0:1], sc2, sh2, g2, P['ffn_w_up'][0], P['ffn_conv_w'][0], P['ffn_conv_b'][0],
                  P['ffn_w_down'][0], seq_len, tm, 512)
    sh1, sc1, g1, sh2, sc2, g2 = mods[1]
    res = _qkv_proj(x, P['norm_mix'][1:2], sc1, sh1, P['attn_w_qkv'], P['attn_q_gain'][0], P['attn_k_gain'][0],
                    rope, seq_len, min(tm, seq_len))
    if caches is None:
        qt, k4, vt, kf, vf = res
        outs['k'] = kf.reshape(batch, 1, seq_len, ATT_KV_HEADS, ATT_HD)
        outs['v'] = vf.reshape(batch, 1, seq_len, ATT_KV_HEADS, ATT_HD)
    else:
        qt, k4, vt = res
        k4, vt = _keys_values(k4, vt, caches[2][:, 0], caches[3][:, 0])
    x = _attention(qt, k4, vt, x, g1, P['attn_w_o'], batch, seq_len, tq, tk)
    x = _conv_ffn(x, P['norm_ffn'][1:2], sc2, sh2, g2, P['ffn_w_up'][1], P['ffn_conv_w'][1], P['ffn_conv_b'][1],
                  P['ffn_w_down'][1], seq_len, tm, 512)
    return x, outs


def kernel(x_prompt, x_sample, state_ret, state_ssm, cache_attn_k, cache_attn_v, c, c_ctx, w_mod, b_mod, norm_mix,
           norm_ffn, ffn_w_up, ffn_conv_w, ffn_conv_b, ffn_w_down, hyb_w_in, hyb_w_out, ret_log_decay, ret_gn_gain,
           ssm_conv_w, ssm_conv_b, ssm_a_log, ssm_dt_bias, ssm_d, ssm_norm_gain, attn_w_qkv, attn_q_gain,
           attn_k_gain, attn_w_o):
    batch, seq, _ = x_prompt.shape
    dec_batch, dec_seq, _ = x_sample.shape
    depth = w_mod.shape[0]

    rows = -(-(dec_batch + 1) // SUBLANES) * SUBLANES
    cond = jnp.concatenate([c, c_ctx[None, :], jnp.zeros((rows - dec_batch - 1, D_MODEL), F32)], axis=0)
    mod = _modulation(cond, w_mod, b_mod).reshape(depth, rows, 6, 1, D_MODEL)
    mods_sample = [[mod[l, 0:dec_batch, t] for t in range(6)] for l in range(depth)]
    mods_prompt = [[mod[l, dec_batch:dec_batch + 1, t] for t in range(6)] for l in range(depth)]

    P = {
        'norm_mix': norm_mix, 'norm_ffn': norm_ffn,
        'hyb_w_in': hyb_w_in[0].astype(BF16),
        'hyb_w_out': hyb_w_out[0].astype(BF16),
        'ret_log_decay': ret_log_decay, 'ret_gn_gain': ret_gn_gain,
        'ssm_conv_w': ssm_conv_w, 'ssm_conv_b': ssm_conv_b, 'ssm_a_log': ssm_a_log, 'ssm_dt_bias': ssm_dt_bias,
        'ssm_d': ssm_d, 'ssm_norm_gain': ssm_norm_gain,
        'attn_w_qkv': attn_w_qkv[0].astype(BF16), 'attn_q_gain': attn_q_gain, 'attn_k_gain': attn_k_gain,
        'attn_w_o': attn_w_o[0].astype(BF16),
        'ffn_w_up': ffn_w_up.astype(BF16), 'ffn_conv_w': ffn_conv_w, 'ffn_conv_b': ffn_conv_b,
        'ffn_w_down': ffn_w_down.astype(BF16),
    }

    y_prompt, outs = _trunk(x_prompt.reshape(batch * seq, D_MODEL), mods_prompt, None, P, batch, seq,
                            tm=512 if (batch * seq) % 512 == 0 and 512 % seq == 0 else min(256, seq),
                            tq=min(256, seq), tk=min(256, seq))
    caches = (state_ret, state_ssm, cache_attn_k, cache_attn_v)
    y_sample, _ = _trunk(x_sample.reshape(dec_batch * dec_seq, D_MODEL), mods_sample, caches, P, dec_batch, dec_seq,
                         tm=512, tq=256, tk=256)
    return (y_prompt.reshape(batch, seq, D_MODEL), y_sample.reshape(dec_batch, dec_seq, D_MODEL),
            outs['ret'], outs['ssm'], outs['k'], outs['v'])
```

```python
import functools

import jax
import jax.numpy as jnp
from jax import lax
from jax.experimental import pallas as pl
from jax.experimental.pallas import tpu as pltpu

F32 = jnp.float32
BF16 = jnp.bfloat16
HIGHEST = lax.Precision.HIGHEST

D_MODEL = 1024
EPS = 1e-6
GRID_W = 64
CHUNK = 128
RET_HEADS = 8
RET_DK = 64
RET_DV = 128
RET_Q = RET_HEADS * RET_DK
RET_V = RET_HEADS * RET_DV
SSM_D_INNER = 1024
SSM_HEADDIM = 64
SSM_HEADS = 16
SSM_GROUPS = 4
SSM_RATIO = 4
SSM_STATE = 128
SSM_BC = SSM_GROUPS * SSM_STATE
SSM_XBC = SSM_D_INNER + 2 * SSM_BC
HYB_QKV = 2 * RET_Q + RET_V
HYB_BF = HYB_QKV + 2 * SSM_BC
XS_COL = 2 * RET_V
DT_COL = XS_COL + SSM_D_INNER
HYB_REST = DT_COL + 128
HYB_MIX = RET_V + SSM_D_INNER
ATT_HEADS = 16
ATT_KV_HEADS = 4
ATT_RATIO = 4
ATT_HD = 64
ATT_Q = ATT_HEADS * ATT_HD
ATT_KV = ATT_KV_HEADS * ATT_HD
ROT_FREQS = ATT_HD // 4
ROPE_BASE = 10000.0
FFN_HIDDEN = 2816

LANES = 128
SUBLANES = 8
MXU_COLS = 256
VMEM_LIMIT = 56 * 1024 * 1024
NEG_BIG = -1e30
Q_SCALE = ATT_HD ** -0.5 * 1.4426950408889634


def _cparams(sem):
    return pltpu.CompilerParams(dimension_semantics=sem, vmem_limit_bytes=VMEM_LIMIT)


def _norm_mod(x, gain, sc, sh):
    ms = jnp.mean(x * x, axis=-1, keepdims=True)
    return x * lax.rsqrt(ms + EPS) * gain * (1.0 + sc) + sh


def _vec_spec(width, tiles_per_batch, per_batch, grid_rank):
    if grid_rank == 1:
        return pl.BlockSpec((1, 1, width), lambda i: ((i // tiles_per_batch) if per_batch else 0, 0, 0))
    return pl.BlockSpec((1, 1, width), lambda i, j: ((i // tiles_per_batch) if per_batch else 0, 0, 0))


def _mod_body(c_ref, w_ref, b_ref, o_ref):
    cs = jax.nn.silu(c_ref[...])
    o_ref[0] = jnp.dot(cs, w_ref[0], precision=HIGHEST, preferred_element_type=F32) + b_ref[0]


def _modulation(cond, w_mod, b_mod):
    depth, _, n = w_mod.shape
    rows = cond.shape[0]
    tn = n // 4
    return pl.pallas_call(
        _mod_body,
        grid=(depth, n // tn),
        in_specs=[
            pl.BlockSpec((rows, D_MODEL), lambda l, j: (0, 0)),
            pl.BlockSpec((1, D_MODEL, tn), lambda l, j: (l, 0, j)),
            pl.BlockSpec((1, 1, tn), lambda l, j: (l, 0, j)),
        ],
        out_specs=pl.BlockSpec((1, rows, tn), lambda l, j: (l, 0, j)),
        out_shape=jax.ShapeDtypeStruct((depth, rows, n), F32),
        compiler_params=_cparams(("arbitrary", "arbitrary")),
        name="modulation",
    )(cond, w_mod, b_mod.reshape(depth, 1, n))


def _chunks(total, width):
    return [(s, min(width, total - s)) for s in range(0, total, width)]


def _resident(shape):
    return pl.BlockSpec(shape, lambda *_: (0,) * len(shape), pipeline_mode=pl.Buffered(1))


def _norm_mod_halo(seq_len, tm, x_ref, xn_ref, xp_ref, gain_ref, sc_ref, sh_ref, h_scr):
    gain, sc, sh = gain_ref[...], sc_ref[0], sh_ref[0]
    h_scr[0:tm] = _norm_mod(x_ref[...], gain, sc, sh).astype(BF16)
    if tm >= seq_len:
        h_scr[tm:tm + 2 * SUBLANES] = jnp.zeros((2 * SUBLANES, D_MODEL), BF16)
    else:
        tiles_per_seq = seq_len // tm
        pos = pl.program_id(0) % tiles_per_seq
        hn = jnp.where(pos == tiles_per_seq - 1, 0.0, _norm_mod(xn_ref[0], gain, sc, sh))
        hp = jnp.where(pos == 0, 0.0, _norm_mod(xp_ref[0], gain, sc, sh))
        h_scr[tm:tm + 2 * SUBLANES] = jnp.concatenate([hn, hp], axis=0).astype(BF16)


def _dwconv3_rows(u, cw, cb, tm, seq_len):
    rows = tm + 2 * SUBLANES
    before, after = pltpu.roll(u, 1, 0), pltpu.roll(u, rows - 1, 0)
    if tm > seq_len:
        pos = lax.broadcasted_iota(jnp.int32, u.shape, 0) % seq_len
        before = jnp.where(pos == 0, 0.0, before)
        after = jnp.where(pos == seq_len - 1, 0.0, after)
    c = before * cw[0:1] + u * cw[1:2] + after * cw[2:3] + cb
    return c[0:tm]


def _halo_specs(m, tm):
    m8 = m // SUBLANES
    r8 = tm // SUBLANES
    return [pl.BlockSpec((1, SUBLANES, D_MODEL), lambda i: (jnp.minimum((i + 1) * r8, m8 - 1), 0, 0)),
            pl.BlockSpec((1, SUBLANES, D_MODEL), lambda i: (jnp.maximum(i * r8 - 1, 0), 0, 0))]


def _hyb_in_body(seq_len, tm, tn, x_ref, xn_ref, xp_ref, gain_ref, sc_ref, sh_ref, w_ref, cw_ref, cb_ref,
                 bf_ref, rest_ref, h_scr):
    _norm_mod_halo(seq_len, tm, x_ref, xn_ref, xp_ref, gain_ref, sc_ref, sh_ref, h_scr)
    h_all = h_scr[...]
    h = h_scr[0:tm]
    w_g = HYB_QKV
    w_x = w_g + 2 * RET_V
    w_bc = w_x + SSM_D_INNER
    w_dt = w_bc + 2 * SSM_BC

    def project(wcol, width, conv_col):
        wcols = w_ref[:, wcol:wcol + width]
        if conv_col is None:
            return jnp.dot(h, wcols, preferred_element_type=F32)
        u = jnp.dot(h_all, wcols, preferred_element_type=F32)
        return jax.nn.silu(_dwconv3_rows(u, cw_ref[:, conv_col:conv_col + width],
                                         cb_ref[:, conv_col:conv_col + width], tm, seq_len))

    for col, width in _chunks(HYB_BF, tn):
        if col >= HYB_QKV:
            out = project(w_bc + col - HYB_QKV, width, SSM_D_INNER + col - HYB_QKV)
        else:
            out = project(col, width, None)
            if RET_Q <= col < 2 * RET_Q:
                out = out * (RET_DK ** -0.5)
        bf_ref[:, col:col + width] = out.astype(BF16)
    for col, width in _chunks(DT_COL, tn):
        rest_ref[:, col:col + width] = project(w_g + col, width, col - XS_COL if col >= XS_COL else None)
    dt = jnp.dot(h, w_ref[:, w_dt:w_dt + SSM_HEADS], preferred_element_type=F32)
    rest_ref[:, DT_COL:HYB_REST] = jnp.concatenate([dt, jnp.zeros((tm, HYB_REST - DT_COL - SSM_HEADS), F32)], axis=1)


def _hyb_in_proj(x, gain, sc, sh, w, conv_w, conv_b, seq_len, tm, tn):
    m = x.shape[0]
    per_batch = sc.shape[0] > 1
    tpb = seq_len // tm
    assert HYB_QKV % tn == 0 and XS_COL % tn == 0 and DT_COL % tn == 0
    x8 = x.reshape(m // SUBLANES, SUBLANES, D_MODEL)
    return pl.pallas_call(
        functools.partial(_hyb_in_body, seq_len, tm, tn),
        grid=(m // tm,),
        in_specs=[
            pl.BlockSpec((tm, D_MODEL), lambda i: (i, 0)),
            *_halo_specs(m, tm),
            pl.BlockSpec((1, D_MODEL), lambda i: (0, 0)),
            _vec_spec(D_MODEL, tpb, per_batch, 1),
            _vec_spec(D_MODEL, tpb, per_batch, 1),
            _resident(w.shape),
            _resident((3, SSM_XBC)),
            _resident((1, SSM_XBC)),
        ],
        out_specs=[pl.BlockSpec((tm, HYB_BF), lambda i: (i, 0)), pl.BlockSpec((tm, HYB_REST), lambda i: (i, 0))],
        out_shape=[jax.ShapeDtypeStruct((m, HYB_BF), BF16), jax.ShapeDtypeStruct((m, HYB_REST), F32)],
        scratch_shapes=[pltpu.VMEM((tm + 2 * SUBLANES, D_MODEL), BF16)],
        compiler_params=_cparams(("arbitrary",)),
        name="hybrid_in_proj",
    )(x, x8, x8, gain, sc, sh, w, conv_w, conv_b.reshape(1, SSM_XBC))


def _ffn_body(seq_len, tm, th, x_ref, xn_ref, xp_ref, gain_ref, sc_ref, sh_ref, gate_ref,
              wu_ref, cw_ref, cb_ref, wd_ref, o_ref, h_scr, act_scr):
    _norm_mod_halo(seq_len, tm, x_ref, xn_ref, xp_ref, gain_ref, sc_ref, sh_ref, h_scr)
    h = h_scr[...]

    def conv_up(col, width):
        u = jnp.dot(h, wu_ref[:, col:col + width], preferred_element_type=F32)
        return _dwconv3_rows(u, cw_ref[:, col:col + width], cb_ref[:, col:col + width], tm, seq_len)

    for col, width in _chunks(FFN_HIDDEN, th):
        act = jax.nn.silu(conv_up(col, width)) * conv_up(FFN_HIDDEN + col, width)
        act_scr[:, col:col + width] = act.astype(BF16)
    out = jnp.dot(act_scr[...], wd_ref[...], preferred_element_type=F32)
    o_ref[...] = x_ref[...] + gate_ref[0] * out


def _conv_ffn(x, gain, sc, sh, gate, w_up, conv_w, conv_b, w_down, seq_len, tm, th):
    m = x.shape[0]
    per_batch = sc.shape[0] > 1
    tpb = seq_len // tm
    x8 = x.reshape(m // SUBLANES, SUBLANES, D_MODEL)
    conv_b = conv_b.reshape(1, 2 * FFN_HIDDEN)
    vec = _vec_spec(D_MODEL, tpb, per_batch, 1)
    return pl.pallas_call(
        functools.partial(_ffn_body, seq_len, tm, th),
        grid=(m // tm,),
        in_specs=[
            pl.BlockSpec((tm, D_MODEL), lambda i: (i, 0)),
            *_halo_specs(m, tm),
            pl.BlockSpec((1, D_MODEL), lambda i: (0, 0)),
            vec, vec, vec,
            _resident((D_MODEL, 2 * FFN_HIDDEN)),
            _resident((3, 2 * FFN_HIDDEN)),
            _resident((1, 2 * FFN_HIDDEN)),
            _resident((FFN_HIDDEN, D_MODEL)),
        ],
        out_specs=pl.BlockSpec((tm, D_MODEL), lambda i: (i, 0)),
        out_shape=jax.ShapeDtypeStruct((m, D_MODEL), F32),
        scratch_shapes=[pltpu.VMEM((tm + 2 * SUBLANES, D_MODEL), BF16), pltpu.VMEM((tm, FFN_HIDDEN), BF16)],
        compiler_params=_cparams(("arbitrary",)),
        name="conv_ffn",
    )(x, x8, x8, gain, sc, sh, gate, w_up, conv_w, conv_b, w_down)


SSM_GW = SSM_RATIO * SSM_HEADDIM
FWD_CHUNKS_PER_STEP = 4


def _bcol(x, h):
    return jnp.broadcast_to(x[:, h:h + 1], (x.shape[0], LANES))


def _pair_sel(left, a, h):
    return jnp.where(left, _bcol(a, h), _bcol(a, h + 1))


def _expand_heads(a, spread_ref):
    hi = a.astype(BF16)
    lo = (a - hi.astype(F32)).astype(BF16)
    return (jnp.dot(hi, spread_ref[...], preferred_element_type=F32)
            + jnp.dot(lo, spread_ref[...], preferred_element_type=F32))


def _ssm_decay_row(left, efull, g):
    return jnp.concatenate([_pair_sel(left, efull, g * SSM_RATIO + 2 * m) for m in range(SSM_RATIO // 2)], axis=1)


def _load_states(s_ret, s_ssm, sret0_ref, sssm0_ref):
    if sret0_ref is None:
        s_ret[...] = jnp.zeros_like(s_ret)
        s_ssm[...] = jnp.zeros_like(s_ssm)
    else:
        s_ret[...] = sret0_ref[0]
        for g in range(SSM_GROUPS):
            s_ssm[g] = jnp.concatenate([sssm0_ref[0, g * SSM_RATIO + r] for r in range(SSM_RATIO)], axis=1)


def _store_states(s_ret, s_ssm, sret_ref, sssm_ref):
    sret_ref[0] = s_ret[...]
    for g in range(SSM_GROUPS):
        for r in range(SSM_RATIO):
            sssm_ref[0, g * SSM_RATIO + r] = s_ssm[g][:, r * SSM_HEADDIM:(r + 1) * SSM_HEADDIM]


def _ssm_log_decay(dt_raw, dtb, alog, lane):
    dt = jax.nn.softplus(dt_raw + dtb)
    return dt, jnp.where(lane < SSM_HEADS, dt * (-jnp.exp(alog)), 0.0)


def _bwd_state_body(has_init, k_ref, v_ref, xs_ref, bm_ref, dt_ref, dtb_ref, alog_ref, spread_ref, logdec_ref,
                    *rest):
    sret0_ref = sssm0_ref = None
    if has_init:
        sret0_ref, sssm0_ref = rest[0], rest[1]
        rest = rest[2:]
    sret_in_ref, sssm_in_ref, sret_fin_ref, sssm_fin_ref, s_ret, s_ssm, wcol = rest
    i = pl.program_id(1)
    n = pl.num_programs(1)
    ii = lax.broadcasted_iota(jnp.int32, (CHUNK, CHUNK), 0)
    jj = lax.broadcasted_iota(jnp.int32, (CHUNK, CHUNK), 1)
    lane = lax.broadcasted_iota(jnp.int32, (1, LANES), 1)
    left = lane < SSM_HEADDIM

    @pl.when(i == 0)
    def _():
        _load_states(s_ret, s_ssm, sret0_ref, sssm0_ref)
        for h in range(RET_HEADS):
            wcol[h] = jnp.exp(ii.astype(F32) * logdec_ref[1, h])

    cps = sret_in_ref.shape[1]
    full = jnp.full((1, LANES), float(CHUNK), F32)
    prep = []
    for c in range(cps):
        rows = slice(c * CHUNK, (c + 1) * CHUNK)
        dt, la = _ssm_log_decay(dt_ref[rows], dtb_ref[...], alog_ref[...], lane)
        rc = jnp.dot((jj >= ii).astype(F32), la, precision=HIGHEST, preferred_element_type=F32)
        last = rc[0:1]
        prep.append((_expand_heads(jnp.exp(last - rc) * dt, spread_ref), jnp.exp(last)))

    for c in reversed(range(cps)):
        rows = slice(c * CHUNK, (c + 1) * CHUNK)
        sret_in_ref[0, c] = s_ret[...].astype(BF16)
        sssm_in_ref[0, c] = s_ssm[...].astype(BF16)
        for h in range(RET_HEADS):
            kh = k_ref[rows, h * RET_DK:(h + 1) * RET_DK]
            vw = (v_ref[rows, h * RET_DV:(h + 1) * RET_DV].astype(F32) * wcol[h]).astype(BF16)
            upd = lax.dot_general(kh, vw, (((0,), (0,)), ((), ())), preferred_element_type=F32)
            s_ret[h] = s_ret[h] * jnp.exp(full * logdec_ref[1, h]) + upd
        wx, efull = prep[c]
        for g in range(SSM_GROUPS):
            vw = (xs_ref[rows, g * SSM_GW:(g + 1) * SSM_GW] * wx[:, g * SSM_GW:(g + 1) * SSM_GW]).astype(BF16)
            bg = bm_ref[rows, g * SSM_STATE:(g + 1) * SSM_STATE]
            upd = lax.dot_general(bg, vw, (((0,), (0,)), ((), ())), preferred_element_type=F32)
            s_ssm[g] = s_ssm[g] * _ssm_decay_row(left, efull, g) + upd

    @pl.when(i == n - 1)
    def _():
        _store_states(s_ret, s_ssm, sret_fin_ref, sssm_fin_ref)


def _hyb_fwd_body(has_init, qkv_ref, xs_ref, bc_ref, dt_ref, sret_in_ref, sssm_in_ref, dtb_ref, alog_ref, dsk_ref,
                  spread_ref, logdec_ref, *rest):
    sret0_ref = sssm0_ref = None
    if has_init:
        sret0_ref, sssm0_ref = rest[0], rest[1]
        rest = rest[2:]
    y_ref, sret_fin_ref, sssm_fin_ref, s_ret, s_ssm, dcomb, ein0, ein1, wcol = rest
    i = pl.program_id(1)
    n = pl.num_programs(1)
    ii = lax.broadcasted_iota(jnp.int32, (CHUNK, CHUNK), 0)
    jj = lax.broadcasted_iota(jnp.int32, (CHUNK, CHUNK), 1)
    lane = lax.broadcasted_iota(jnp.int32, (1, LANES), 1)
    left = lane < SSM_HEADDIM
    lower = jj <= ii

    @pl.when(i == 0)
    def _():
        _load_states(s_ret, s_ssm, sret0_ref, sssm0_ref)
        fi, fj = ii.astype(F32), jj.astype(F32)
        for h in range(RET_HEADS):
            la0, la1 = logdec_ref[0, h], logdec_ref[1, h]
            dec = jnp.exp(jnp.where(lower, (fi - fj) * la0, (fj - fi) * la1))
            dcomb[h] = jnp.where(ii == jj, 2.0, dec)
            ein0[h] = jnp.exp((fi + 1.0) * la0)
            ein1[h] = jnp.exp((CHUNK - fi) * la1)
            wcol[h] = jnp.exp((CHUNK - 1.0 - fi) * la0)

    for c in range(sret_in_ref.shape[1]):
        _fwd_chunk(c, qkv_ref, xs_ref, bc_ref, dt_ref, sret_in_ref, sssm_in_ref, dtb_ref, alog_ref, dsk_ref,
                   spread_ref, logdec_ref, y_ref, s_ret, s_ssm, dcomb, ein0, ein1, wcol)

    @pl.when(i == n - 1)
    def _():
        _store_states(s_ret, s_ssm, sret_fin_ref, sssm_fin_ref)


def _fwd_chunk(c, qkv_ref, xs_ref, bc_ref, dt_ref, sret_in_ref, sssm_in_ref, dtb_ref, alog_ref, dsk_ref,
               spread_ref, logdec_ref, y_ref, s_ret, s_ssm, dcomb, ein0, ein1, wcol):
    rows = pl.ds(c * CHUNK, CHUNK)
    qkv_ref, xs_ref, bc_ref, dt_ref, y_ref = (r.at[rows] for r in (qkv_ref, xs_ref, bc_ref, dt_ref, y_ref))
    ii = lax.broadcasted_iota(jnp.int32, (CHUNK, CHUNK), 0)
    jj = lax.broadcasted_iota(jnp.int32, (CHUNK, CHUNK), 1)
    lane = lax.broadcasted_iota(jnp.int32, (1, LANES), 1)
    left = lane < SSM_HEADDIM
    lower = jj <= ii

    def ret_head(h):
        return (qkv_ref[:, h * RET_DK:(h + 1) * RET_DK],
                qkv_ref[:, RET_Q + h * RET_DK:RET_Q + (h + 1) * RET_DK],
                qkv_ref[:, 2 * RET_Q + h * RET_DV:2 * RET_Q + (h + 1) * RET_DV])

    def ssm_group(g):
        return (bc_ref[:, g * SSM_STATE:(g + 1) * SSM_STATE],
                bc_ref[:, SSM_BC + g * SSM_STATE:SSM_BC + (g + 1) * SSM_STATE])

    dt_raw = dt_ref[...]
    dt0, la0 = _ssm_log_decay(dt_raw, dtb_ref[0:1], alog_ref[0:1], lane)
    dt1, la1 = _ssm_log_decay(dt_raw, dtb_ref[1:2], alog_ref[1:2], lane)
    c0 = jnp.dot(lower.astype(F32), la0, precision=HIGHEST, preferred_element_type=F32)
    c1 = jnp.dot((jj >= ii).astype(F32), la1, precision=HIGHEST, preferred_element_type=F32)

    ret_scores, ret_inter = [], []
    for h in range(RET_HEADS):
        qh, kh, _ = ret_head(h)
        ret_scores.append(lax.dot_general(qh, kh, (((1,), (1,)), ((), ())), preferred_element_type=F32))
        states = jnp.concatenate([s_ret[h].astype(BF16), sret_in_ref[0, c, h]], axis=1)
        ret_inter.append(jnp.dot(qh, states, preferred_element_type=F32))
    ssm_scores, ssm_inter = [], []
    for g in range(SSM_GROUPS):
        bg, cg = ssm_group(g)
        ssm_scores.append(lax.dot_general(cg, bg, (((1,), (1,)), ((), ())), preferred_element_type=F32))
        states = jnp.concatenate([s_ssm[g].astype(BF16), sssm_in_ref[0, c, g]], axis=1)
        ssm_inter.append(jnp.dot(cg, states, preferred_element_type=F32))

    full = jnp.full((1, LANES), float(CHUNK), F32)
    for h in range(RET_HEADS):
        _, kh, vh = ret_head(h)
        y = jnp.dot((ret_scores[h] * dcomb[h]).astype(BF16), vh, preferred_element_type=F32)
        inter = ret_inter[h]
        y_ref[:, h * RET_DV:(h + 1) * RET_DV] = y + inter[:, 0:RET_DV] * ein0[h] + inter[:, RET_DV:] * ein1[h]
        vw = (vh.astype(F32) * wcol[h]).astype(BF16)
        upd = lax.dot_general(kh, vw, (((0,), (0,)), ((), ())), preferred_element_type=F32)
        s_ret[h] = s_ret[h] * jnp.exp(full * logdec_ref[0, h]) + upd

    c0kt, c1kt = (c0 - jnp.log(dt0)).T, (c1 - jnp.log(dt1)).T
    dst = (dt0 + dt1).T
    last0 = c0[CHUNK - 1:CHUNK]
    w0x = _expand_heads(jnp.exp(last0 - c0) * dt0, spread_ref)
    efull0 = jnp.exp(last0)
    diag = jj == ii
    for g in range(SSM_GROUPS):
        bg, _ = ssm_group(g)
        s = ssm_scores[g]
        inter = ssm_inter[g]
        vws = []
        for m in range(SSM_RATIO // 2):
            ha = g * SSM_RATIO + 2 * m
            ps, c0bs, c1bs = [], [], []
            for h in (ha, ha + 1):
                c0b, c1b = _bcol(c0, h), _bcol(c1, h)
                arg = jnp.where(lower, c0b - c0kt[h:h + 1], c1b - c1kt[h:h + 1])
                ps.append((s * jnp.where(diag, dst[h:h + 1], jnp.exp(arg))).astype(BF16))
                c0bs.append(c0b)
                c1bs.append(c1b)
            col = (g * 2 + m) * LANES
            xs_pair = xs_ref[:, col:col + LANES]
            vals = jnp.concatenate([jnp.where(left, xs_pair, 0.0), jnp.where(left, 0.0, xs_pair)], axis=0)
            y = jnp.dot(jnp.concatenate(ps, axis=1), vals.astype(BF16), preferred_element_type=F32)
            y = y + inter[:, m * LANES:(m + 1) * LANES] * jnp.exp(jnp.where(left, c0bs[0], c0bs[1]))
            y = y + (inter[:, SSM_GW + m * LANES:SSM_GW + (m + 1) * LANES]
                     * jnp.exp(jnp.where(left, c1bs[0], c1bs[1])))
            y_ref[:, RET_V + col:RET_V + col + LANES] = y + dsk_ref[:, col:col + LANES] * xs_pair
            vws.append((xs_pair * w0x[:, col:col + LANES]).astype(BF16))
        upd = lax.dot_general(bg, jnp.concatenate(vws, axis=1), (((0,), (0,)), ((), ())),
                              preferred_element_type=F32)
        s_ssm[g] = s_ssm[g] * _ssm_decay_row(left, efull0, g) + upd


def _hybrid_mixer(qkv, rest, dt_bias, a_log, log_decay, d_skip, s_ret0, s_ssm0, batch, seq_len):
    m = qkv.shape[0]
    n = seq_len // CHUNK
    has_init = s_ret0 is not None
    dt_blk = DT_COL // LANES
    pad = lambda v: jnp.pad(v.astype(F32), ((0, 0), (0, LANES - v.shape[1])))
    dtb, alog = pad(dt_bias), pad(a_log)
    logdec = log_decay.astype(F32)
    spread = (lax.broadcasted_iota(jnp.int32, (LANES, SSM_D_INNER), 0)
              == lax.broadcasted_iota(jnp.int32, (LANES, SSM_D_INNER), 1) // SSM_HEADDIM).astype(BF16)
    spread_spec = pl.BlockSpec((LANES, SSM_D_INNER), lambda b, i: (0, 0))
    dsk = jnp.repeat((d_skip[0] + d_skip[1]).astype(F32), SSM_HEADDIM).reshape(1, SSM_D_INNER)
    smem = pl.BlockSpec(memory_space=pltpu.SMEM)
    ret_spec = pl.BlockSpec((1, RET_HEADS, RET_DK, RET_DV), lambda b, i: (b, 0, 0, 0))
    ssm_spec = pl.BlockSpec((1, SSM_HEADS, SSM_STATE, SSM_HEADDIM), lambda b, i: (b, 0, 0, 0))
    ret_shape = jax.ShapeDtypeStruct((batch, RET_HEADS, RET_DK, RET_DV), F32)
    ssm_shape = jax.ShapeDtypeStruct((batch, SSM_HEADS, SSM_STATE, SSM_HEADDIM), F32)
    ret_scr = pltpu.VMEM((RET_HEADS, RET_DK, RET_DV), F32)
    ssm_scr = pltpu.VMEM((SSM_GROUPS, SSM_STATE, SSM_GW), F32)
    const_scr = pltpu.VMEM((RET_HEADS, CHUNK, CHUNK), F32)

    cps = next(c for c in (8, 4, 2, 1) if n % c == 0)
    rsteps = n // cps
    rrows = cps * CHUNK

    def rev(b, i):
        return b * rsteps + rsteps - 1 - i

    in_specs = [
        pl.BlockSpec((rrows, RET_Q), lambda b, i: (rev(b, i), 1)),
        pl.BlockSpec((rrows, RET_V), lambda b, i: (rev(b, i), 1)),
        pl.BlockSpec((rrows, SSM_D_INNER), lambda b, i: (rev(b, i), XS_COL // SSM_D_INNER)),
        pl.BlockSpec((rrows, SSM_BC), lambda b, i: (rev(b, i), HYB_QKV // SSM_BC)),
        pl.BlockSpec((rrows, LANES), lambda b, i: (rev(b, i), dt_blk)),
        pl.BlockSpec((1, LANES), lambda b, i: (0, 0)),
        pl.BlockSpec((1, LANES), lambda b, i: (0, 0)),
        spread_spec,
        smem,
    ]
    args = [qkv, qkv, rest, qkv, rest, dtb[1:2], alog[1:2], spread, logdec]
    if has_init:
        in_specs += [ret_spec, ssm_spec]
        args += [s_ret0[:, 1], s_ssm0[:, 1]]
    sret_in, sssm_in, sret1, sssm1 = pl.pallas_call(
        functools.partial(_bwd_state_body, has_init),
        grid=(batch, rsteps),
        in_specs=in_specs,
        out_specs=[
            pl.BlockSpec((1, cps, RET_HEADS, RET_DK, RET_DV), lambda b, i: (b, rsteps - 1 - i, 0, 0, 0)),
            pl.BlockSpec((1, cps, SSM_GROUPS, SSM_STATE, SSM_GW), lambda b, i: (b, rsteps - 1 - i, 0, 0, 0)),
            ret_spec, ssm_spec,
        ],
        out_shape=[
            jax.ShapeDtypeStruct((batch, n, RET_HEADS, RET_DK, RET_DV), BF16),
            jax.ShapeDtypeStruct((batch, n, SSM_GROUPS, SSM_STATE, SSM_GW), BF16),
            ret_shape, ssm_shape,
        ],
        scratch_shapes=[ret_scr, ssm_scr, const_scr],
        compiler_params=_cparams(("arbitrary", "arbitrary")),
        name="hybrid_reverse_states",
    )(*args)

    fcps = next(c for c in (FWD_CHUNKS_PER_STEP, 2, 1) if n % c == 0)
    fsteps = n // fcps
    frows = fcps * CHUNK

    def fwd(b, i):
        return b * fsteps + i

    in_specs = [
        pl.BlockSpec((frows, HYB_QKV), lambda b, i: (fwd(b, i), 0)),
        pl.BlockSpec((frows, SSM_D_INNER), lambda b, i: (fwd(b, i), XS_COL // SSM_D_INNER)),
        pl.BlockSpec((frows, 2 * SSM_BC), lambda b, i: (fwd(b, i), HYB_QKV // (2 * SSM_BC))),
        pl.BlockSpec((frows, LANES), lambda b, i: (fwd(b, i), dt_blk)),
        pl.BlockSpec((1, fcps, RET_HEADS, RET_DK, RET_DV), lambda b, i: (b, i, 0, 0, 0)),
        pl.BlockSpec((1, fcps, SSM_GROUPS, SSM_STATE, SSM_GW), lambda b, i: (b, i, 0, 0, 0)),
        pl.BlockSpec((2, LANES), lambda b, i: (0, 0)),
        pl.BlockSpec((2, LANES), lambda b, i: (0, 0)),
        pl.BlockSpec((1, SSM_D_INNER), lambda b, i: (0, 0)),
        spread_spec,
        smem,
    ]
    args = [qkv, rest, qkv, rest, sret_in, sssm_in, dtb, alog, dsk, spread, logdec]
    if has_init:
        in_specs += [ret_spec, ssm_spec]
        args += [s_ret0[:, 0], s_ssm0[:, 0]]
    y, sret0, sssm0 = pl.pallas_call(
        functools.partial(_hyb_fwd_body, has_init),
        grid=(batch, fsteps),
        in_specs=in_specs,
        out_specs=[pl.BlockSpec((frows, HYB_MIX), lambda b, i: (fwd(b, i), 0)), ret_spec, ssm_spec],
        out_shape=[jax.ShapeDtypeStruct((m, HYB_MIX), F32), ret_shape, ssm_shape],
        scratch_shapes=[ret_scr, ssm_scr, const_scr, const_scr, const_scr, const_scr],
        compiler_params=_cparams(("arbitrary", "arbitrary")),
        name="hybrid_forward_mix",
    )(*args)
    return y, jnp.stack([sret0, sret1], axis=1), jnp.stack([sssm0, sssm1], axis=1)


def _hyb_out_body(y_ref, g_ref, z_ref, x_ref, gate_ref, gn_ref, ng_ref, w_ref, o_ref):
    y = y_ref[...]
    parts = []
    for h in range(RET_HEADS):
        yh = y[:, h * RET_DV:(h + 1) * RET_DV]
        mu = jnp.mean(yh, axis=-1, keepdims=True)
        var = jnp.mean(jnp.square(yh - mu), axis=-1, keepdims=True)
        parts.append((yh - mu) * lax.rsqrt(var + EPS))
    y_ret = jax.nn.silu(g_ref[...]) * (jnp.concatenate(parts, axis=1) * gn_ref[...])
    yz = y[:, RET_V:] * jax.nn.silu(z_ref[...])
    y_ssm = yz * lax.rsqrt(jnp.mean(yz * yz, axis=-1, keepdims=True) + EPS) * ng_ref[...]
    out = jnp.dot(y_ret.astype(BF16), w_ref[0:RET_V], preferred_element_type=F32)
    out = out + jnp.dot(y_ssm.astype(BF16), w_ref[RET_V:HYB_MIX], preferred_element_type=F32)
    o_ref[...] = x_ref[...] + gate_ref[0] * out


def _hyb_out(y, rest, x, gate, gn_gain, norm_gain, w_out, seq_len, tm):
    m = x.shape[0]
    per_batch = gate.shape[0] > 1
    tpb = seq_len // tm
    return pl.pallas_call(
        _hyb_out_body,
        grid=(m // tm,),
        in_specs=[
            pl.BlockSpec((tm, HYB_MIX), lambda i: (i, 0)),
            pl.BlockSpec((tm, RET_V), lambda i: (i, 0)),
            pl.BlockSpec((tm, SSM_D_INNER), lambda i: (i, 1)),
            pl.BlockSpec((tm, D_MODEL), lambda i: (i, 0)),
            _vec_spec(D_MODEL, tpb, per_batch, 1),
            pl.BlockSpec((1, RET_V), lambda i: (0, 0)),
            pl.BlockSpec((1, SSM_D_INNER), lambda i: (0, 0)),
            pl.BlockSpec((HYB_MIX, D_MODEL), lambda i: (0, 0)),
        ],
        out_specs=pl.BlockSpec((tm, D_MODEL), lambda i: (i, 0)),
        out_shape=jax.ShapeDtypeStruct((m, D_MODEL), F32),
        compiler_params=_cparams(("arbitrary",)),
        name="hybrid_out_proj",
    )(y, rest, rest, x, gate, gn_gain, norm_gain, w_out)


def _head_rms(t, seg_ref):
    outs = []
    for c in range(t.shape[1] // LANES):
        tc = t[:, c * LANES:(c + 1) * LANES]
        sq = tc * tc
        hi = sq.astype(BF16)
        lo = (sq - hi.astype(F32)).astype(BF16)
        ssum = (jnp.dot(hi, seg_ref[...], preferred_element_type=F32)
                + jnp.dot(lo, seg_ref[...], preferred_element_type=F32))
        outs.append(tc * lax.rsqrt(ssum * (1.0 / ATT_HD) + EPS))
    return outs


def _rope(tc, cos, sin, lane):
    fwd = pltpu.roll(tc, LANES - ROT_FREQS, 1)
    bwd = pltpu.roll(tc, ROT_FREQS, 1)
    return tc * cos + jnp.where(lane % (2 * ROT_FREQS) < ROT_FREQS, fwd, bwd) * sin


def _qkv_body(has_rope, x_ref, gain_ref, sc_ref, sh_ref, w_ref, seg_ref, qg_ref, kg_ref, *rest):
    if has_rope:
        cos_ref, sin_ref, q_ref, k_ref, v_ref = rest
    else:
        q_ref, k_ref, v_ref, kf_ref, vf_ref = rest
    h = _norm_mod(x_ref[...], gain_ref[...], sc_ref[0], sh_ref[0]).astype(BF16)
    lane = lax.broadcasted_iota(jnp.int32, (1, LANES), 1)

    def project(col):
        t = jnp.dot(h, w_ref[:, col:col + MXU_COLS], preferred_element_type=F32)
        return [t[:, c * LANES:(c + 1) * LANES] for c in range(MXU_COLS // LANES)]

    def normed(tiles, gain_ref_):
        tiles = [t * gain_ref_[...] for t in _head_rms(jnp.concatenate(tiles, axis=1), seg_ref)]
        return tiles, ([_rope(t, cos_ref[...], sin_ref[...], lane) for t in tiles] if has_rope else tiles)

    def emit_q(col, tiles):
        _, q = normed(tiles, qg_ref)
        for c, t in enumerate(q):
            row = col + c * LANES
            q_ref[0, row:row + LANES, :] = (t * Q_SCALE).T.astype(BF16)

    def emit_k(col, tiles):
        k, kr = normed(tiles, kg_ref)
        for c in range(MXU_COLS // LANES):
            lo = col + c * LANES
            if not has_rope:
                kf_ref[:, lo:lo + LANES] = k[c]
            kc = kr[c].astype(BF16)
            for a in range(LANES // ATT_HD):
                k_ref[0, lo // ATT_HD + a] = kc[:, a * ATT_HD:(a + 1) * ATT_HD]

    def emit_v(col, tiles):
        for c in range(MXU_COLS // LANES):
            lo = col + c * LANES
            if not has_rope:
                vf_ref[:, lo:lo + LANES] = tiles[c]
            vc = tiles[c].T.astype(BF16)
            for a in range(LANES // ATT_HD):
                v_ref[0, lo // ATT_HD + a] = vc[a * ATT_HD:(a + 1) * ATT_HD]

    work = ([(emit_q, col, col) for col in range(0, ATT_Q, MXU_COLS)]
            + [(emit_k, col, ATT_Q + col) for col in range(0, ATT_KV, MXU_COLS)]
            + [(emit_v, col, ATT_Q + ATT_KV + col) for col in range(0, ATT_KV, MXU_COLS)])
    tiles = project(work[0][2])
    for n, (emit, col, _) in enumerate(work):
        ahead = project(work[n + 1][2]) if n + 1 < len(work) else None
        emit(col, tiles)
        tiles = ahead


def _qkv_proj(x, gain, sc, sh, w_qkv, q_gain, k_gain, rope, seq_len, tm):
    m = x.shape[0]
    per_batch = sc.shape[0] > 1
    tpb = seq_len // tm
    has_rope = rope is not None
    seg = (lax.broadcasted_iota(jnp.int32, (LANES, LANES), 0) // ATT_HD
           == lax.broadcasted_iota(jnp.int32, (LANES, LANES), 1) // ATT_HD).astype(BF16)
    tile2 = lambda v: jnp.tile(v.astype(F32), LANES // ATT_HD).reshape(1, LANES)
    in_specs = [
        pl.BlockSpec((tm, D_MODEL), lambda i: (i, 0)),
        pl.BlockSpec((1, D_MODEL), lambda i: (0, 0)),
        _vec_spec(D_MODEL, tpb, per_batch, 1),
        _vec_spec(D_MODEL, tpb, per_batch, 1),
        pl.BlockSpec((D_MODEL, ATT_Q + 2 * ATT_KV), lambda i: (0, 0)),
        pl.BlockSpec((LANES, LANES), lambda i: (0, 0)),
        pl.BlockSpec((1, LANES), lambda i: (0, 0)),
        pl.BlockSpec((1, LANES), lambda i: (0, 0)),
    ]
    args = [x, gain, sc, sh, w_qkv, seg, tile2(q_gain), tile2(k_gain)]
    batch = m // seq_len
    out_specs = [pl.BlockSpec((1, ATT_Q, tm), lambda i: (i // tpb, 0, i % tpb)),
                 pl.BlockSpec((1, ATT_KV_HEADS, tm, ATT_HD), lambda i: (i // tpb, 0, i % tpb, 0)),
                 pl.BlockSpec((1, ATT_KV_HEADS, ATT_HD, tm), lambda i: (i // tpb, 0, 0, i % tpb))]
    out_shape = [jax.ShapeDtypeStruct((batch, ATT_Q, seq_len), BF16),
                 jax.ShapeDtypeStruct((batch, ATT_KV_HEADS, seq_len, ATT_HD), BF16),
                 jax.ShapeDtypeStruct((batch, ATT_KV_HEADS, ATT_HD, seq_len), BF16)]
    if has_rope:
        in_specs += [pl.BlockSpec((tm, LANES), lambda i: (i % tpb, 0))] * 2
        args += list(rope)
    else:
        out_specs += [pl.BlockSpec((tm, ATT_KV), lambda i: (i, 0))] * 2
        out_shape += [jax.ShapeDtypeStruct((m, ATT_KV), F32)] * 2
    return pl.pallas_call(
        functools.partial(_qkv_body, has_rope),
        grid=(m // tm,),
        in_specs=in_specs,
        out_specs=out_specs,
        out_shape=out_shape,
        compiler_params=_cparams(("arbitrary",)),
        name="attn_qkv_proj",
    )(*args)


def _rope_tables(seq_len):
    rows = seq_len // GRID_W
    row = jnp.repeat(jnp.arange(rows, dtype=F32), GRID_W)
    col = jnp.tile(jnp.arange(GRID_W, dtype=F32), rows)
    inv = ROPE_BASE ** (-jnp.arange(ROT_FREQS, dtype=F32) / ROT_FREQS)
    ar, ac = row[:, None] * inv, col[:, None] * inv
    cos = jnp.concatenate([jnp.cos(ar), jnp.cos(ar), jnp.cos(ac), jnp.cos(ac)], axis=1)
    sin = jnp.concatenate([-jnp.sin(ar), jnp.sin(ar), -jnp.sin(ac), jnp.sin(ac)], axis=1)
    return jnp.tile(cos, (1, LANES // ATT_HD)), jnp.tile(sin, (1, LANES // ATT_HD))


V_ROWS = ATT_HD + 16
ATT_SLOTS = 4


def _attn_body(tq, tk, qt_ref, k_ref, vt_ref, x_ref, gate_ref, wo_ref, o_ref, acc_scr, s_scr, ot_scr):
    nkv = k_ref.shape[2] // tk
    ones_rows = jnp.ones((V_ROWS - ATT_HD, tk), BF16)
    def scores(g, t, slot):
        kt = k_ref[0, g, pl.ds(pl.multiple_of(t * tk, tk), tk), :]
        tops = []
        for r in range(ATT_RATIO):
            hd = g * ATT_RATIO + r
            s = jnp.dot(kt, qt_ref[0, hd * ATT_HD:(hd + 1) * ATT_HD, :], preferred_element_type=F32)
            s_scr[slot, r] = s
            tops.append(jnp.max(s, axis=0, keepdims=True))
        return tuple(tops)

    def softmax_pv(g, t, slot, ms, tops):
        vt = jnp.concatenate([vt_ref[0, g, :, pl.ds(pl.multiple_of(t * tk, tk), tk)], ones_rows], axis=0)
        new_ms = [jnp.maximum(ms[r], tops[r]) for r in range(ATT_RATIO)]
        ps = [jnp.exp2(s_scr[slot, r] - new_ms[r]).astype(BF16) for r in range(ATT_RATIO)]
        for r in range(ATT_RATIO):
            acc_scr[r] = (jnp.exp2(ms[r] - new_ms[r]) * acc_scr[r]
                          + jnp.dot(vt, ps[r], preferred_element_type=F32))
        return tuple(new_ms)

    first_tiles = range(min(2, nkv))
    quads = max(nkv - 2, 0) // ATT_SLOTS
    off = 0
    tops = {t: scores(0, t, t) for t in first_tiles}
    for g in range(ATT_KV_HEADS):
        acc_scr[...] = jnp.zeros_like(acc_scr)
        slot_of = lambda t, off=off: (t + off) % ATT_SLOTS

        def quad_step(i, carry, g=g, slot_of=slot_of):
            ms, top0, top1 = carry
            t = i * ATT_SLOTS
            top2 = scores(g, t + 2, slot_of(2))
            top3 = scores(g, t + 3, slot_of(3))
            ms = softmax_pv(g, t, slot_of(0), ms, top0)
            ms = softmax_pv(g, t + 1, slot_of(1), ms, top1)
            top0 = scores(g, t + 4, slot_of(0))
            top1 = scores(g, t + 5, slot_of(1))
            ms = softmax_pv(g, t + 2, slot_of(2), ms, top2)
            return softmax_pv(g, t + 3, slot_of(3), ms, top3), top0, top1

        ms = tuple(jnp.full((1, tq), NEG_BIG, F32) for _ in range(ATT_RATIO))
        if quads:
            ms, top0, top1 = lax.fori_loop(0, quads, quad_step, (ms, tops[0], tops[1]))
            tops = {quads * ATT_SLOTS: top0, quads * ATT_SLOTS + 1: top1}
        next_off, next_tops = (nkv + off) % ATT_SLOTS, {}
        for t in range(quads * ATT_SLOTS, nkv):
            if t % 2 == 0:
                for ahead in (t + 2, t + 3):
                    if ahead < nkv:
                        tops[ahead] = scores(g, ahead, slot_of(ahead))
            if t == max(nkv - 2, quads * ATT_SLOTS) and g + 1 < ATT_KV_HEADS:
                next_tops = {n: scores(g + 1, n, (n + next_off) % ATT_SLOTS) for n in first_tiles}
            ms = softmax_pv(g, t, slot_of(t), ms, tops.pop(t))
        off, tops = next_off, next_tops
        for r in range(ATT_RATIO):
            hd = g * ATT_RATIO + r
            a = acc_scr[r]
            ot_scr[hd * ATT_HD:(hd + 1) * ATT_HD, :] = (a[0:ATT_HD] / a[ATT_HD:ATT_HD + 1]).astype(BF16)
    out = lax.dot_general(ot_scr[...], wo_ref[...], (((0,), (0,)), ((), ())), preferred_element_type=F32)
    o_ref[0] = x_ref[0] + gate_ref[0] * out


def _attention(qt, k4, vt, x, gate, w_o, batch, seq_len, tq, tk):
    per_batch = gate.shape[0] > 1
    nkeys = k4.shape[2]
    x3 = x.reshape(batch, seq_len, D_MODEL)
    out = pl.pallas_call(
        functools.partial(_attn_body, tq, tk),
        grid=(batch, seq_len // tq),
        in_specs=[
            pl.BlockSpec((1, ATT_Q, tq), lambda b, i: (b, 0, i)),
            pl.BlockSpec((1, ATT_KV_HEADS, nkeys, ATT_HD), lambda b, i: (b, 0, 0, 0)),
            pl.BlockSpec((1, ATT_KV_HEADS, ATT_HD, nkeys), lambda b, i: (b, 0, 0, 0)),
            pl.BlockSpec((1, tq, D_MODEL), lambda b, i: (b, i, 0)),
            pl.BlockSpec((1, 1, D_MODEL), lambda b, i: (b if per_batch else 0, 0, 0)),
            pl.BlockSpec((ATT_Q, D_MODEL), lambda b, i: (0, 0)),
        ],
        out_specs=pl.BlockSpec((1, tq, D_MODEL), lambda b, i: (b, i, 0)),
        out_shape=jax.ShapeDtypeStruct((batch, seq_len, D_MODEL), F32),
        scratch_shapes=[pltpu.VMEM((ATT_RATIO, V_ROWS, tq), F32), pltpu.VMEM((ATT_SLOTS, ATT_RATIO, tk, tq), F32),
                        pltpu.VMEM((ATT_Q, tq), BF16)],
        compiler_params=_cparams(("arbitrary", "arbitrary")),
        name="attention",
    )(qt, k4, vt, x3, gate, w_o)
    return out.reshape(batch * seq_len, D_MODEL)


def _keys_values(k4, vt, cache_k, cache_v):
    k4 = jnp.concatenate([jnp.swapaxes(cache_k, 1, 2).astype(BF16), k4], axis=2)
    vt = jnp.concatenate([jnp.transpose(cache_v, (0, 2, 3, 1)).astype(BF16), vt], axis=3)
    return k4, vt


def _trunk(x, mods, caches, P, batch, seq_len, tm, tq, tk):
    rope = None if caches is None else _rope_tables(seq_len)
    outs = {}
    sh1, sc1, g1, sh2, sc2, g2 = mods[0]
    qkv, rest = _hyb_in_proj(x, P['norm_mix'][0:1], sc1, sh1, P['hyb_w_in'], P['ssm_conv_w'][0],
                             P['ssm_conv_b'][0], seq_len, tm, 512)
    s_ret0 = None if caches is None else caches[0][:, 0]
    s_ssm0 = None if caches is None else caches[1][:, 0]
    y, s_ret, s_ssm = _hybrid_mixer(qkv, rest, P['ssm_dt_bias'][0], P['ssm_a_log'][0], P['ret_log_decay'][0],
                                    P['ssm_d'][0], s_ret0, s_ssm0, batch, seq_len)
    outs['ret'] = s_ret[:, None]
    outs['ssm'] = s_ssm[:, None]
    x = _hyb_out(y, rest, x, g1, P['ret_gn_gain'][0:1], P['ssm_norm_gain'][0:1], P['hyb_w_out'], seq_len, tm)
    x = _conv_ffn(x, P['norm_ffn'][0:1], sc2, sh2, g2, P['ffn_w_up'][0], P['ffn_conv_w'][0], P['ffn_conv_b'][0],
                  P['ffn_w_down'][0], seq_len, tm, 512)
    sh1, sc1, g1, sh2, sc2, g2 = mods[1]
    res = _qkv_proj(x, P['norm_mix'][1:2], sc1, sh1, P['attn_w_qkv'], P['attn_q_gain'][0], P['attn_k_gain'][0],
                    rope, seq_len, min(tm, seq_len))
    if caches is None:
        qt, k4, vt, kf, vf = res
        outs['k'] = kf.reshape(batch, 1, seq_len, ATT_KV_HEADS, ATT_HD)
        outs['v'] = vf.reshape(batch, 1, seq_len, ATT_KV_HEADS, ATT_HD)
    else:
        qt, k4, vt = res
        k4, vt = _keys_values(k4, vt, caches[2][:, 0], caches[3][:, 0])
    x = _attention(qt, k4, vt, x, g1, P['attn_w_o'], batch, seq_len, tq, tk)
    x = _conv_ffn(x, P['norm_ffn'][1:2], sc2, sh2, g2, P['ffn_w_up'][1], P['ffn_conv_w'][1], P['ffn_conv_b'][1],
                  P['ffn_w_down'][1], seq_len, tm, 512)
    return x, outs


def kernel(x_prompt, x_sample, state_ret, state_ssm, cache_attn_k, cache_attn_v, c, c_ctx, w_mod, b_mod, norm_mix,
           norm_ffn, ffn_w_up, ffn_conv_w, ffn_conv_b, ffn_w_down, hyb_w_in, hyb_w_out, ret_log_decay, ret_gn_gain,
           ssm_conv_w, ssm_conv_b, ssm_a_log, ssm_dt_bias, ssm_d, ssm_norm_gain, attn_w_qkv, attn_q_gain,
           attn_k_gain, attn_w_o):
    batch, seq, _ = x_prompt.shape
    dec_batch, dec_seq, _ = x_sample.shape
    depth = w_mod.shape[0]

    rows = -(-(dec_batch + 1) // SUBLANES) * SUBLANES
    cond = jnp.concatenate([c, c_ctx[None, :], jnp.zeros((rows - dec_batch - 1, D_MODEL), F32)], axis=0)
    mod = _modulation(cond, w_mod, b_mod).reshape(depth, rows, 6, 1, D_MODEL)
    mods_sample = [[mod[l, 0:dec_batch, t] for t in range(6)] for l in range(depth)]
    mods_prompt = [[mod[l, dec_batch:dec_batch + 1, t] for t in range(6)] for l in range(depth)]

    P = {
        'norm_mix': norm_mix, 'norm_ffn': norm_ffn,
        'hyb_w_in': hyb_w_in[0].astype(BF16),
        'hyb_w_out': hyb_w_out[0].astype(BF16),
        'ret_log_decay': ret_log_decay, 'ret_gn_gain': ret_gn_gain,
        'ssm_conv_w': ssm_conv_w, 'ssm_conv_b': ssm_conv_b, 'ssm_a_log': ssm_a_log, 'ssm_dt_bias': ssm_dt_bias,
        'ssm_d': ssm_d, 'ssm_norm_gain': ssm_norm_gain,
        'attn_w_qkv': attn_w_qkv[0].astype(BF16), 'attn_q_gain': attn_q_gain, 'attn_k_gain': attn_k_gain,
        'attn_w_o': attn_w_o[0].astype(BF16),
        'ffn_w_up': ffn_w_up.astype(BF16), 'ffn_conv_w': ffn_conv_w, 'ffn_conv_b': ffn_conv_b,
        'ffn_w_down': ffn_w_down.astype(BF16),
    }

    y_prompt, outs = _trunk(x_prompt.reshape(batch * seq, D_MODEL), mods_prompt, None, P, batch, seq,
                            tm=512 if (batch * seq) % 512 == 0 and 512 % seq == 0 else min(256, seq),
                            tq=min(256, seq), tk=min(256, seq))
    caches = (state_ret, state_ssm, cache_attn_k, cache_attn_v)
    y_sample, _ = _trunk(x_sample.reshape(dec_batch * dec_seq, D_MODEL), mods_sample, caches, P, dec_batch, dec_seq,
                         tm=512, tq=256, tk=256)
    return (y_prompt.reshape(batch, seq, D_MODEL), y_sample.reshape(dec_batch, dec_seq, D_MODEL),
            outs['ret'], outs['ssm'], outs['k'], outs['v'])
```

```python
import functools

import jax
import jax.numpy as jnp
from jax import lax
from jax.experimental import pallas as pl
from jax.experimental.pallas import tpu as pltpu

F32 = jnp.float32
BF16 = jnp.bfloat16
HIGHEST = lax.Precision.HIGHEST

LANES = 128
SUBLANES = 8
MXU_COLS = 256
VMEM_LIMIT = 56 * 1024 * 1024

D_MODEL = 1024
EPS = 1e-6
GRID_W = 64
CHUNK = 128
RET_HEADS = 8
RET_DK = 64
RET_DV = 128
RET_Q = RET_HEADS * RET_DK
RET_V = RET_HEADS * RET_DV
SSM_D_INNER = 1024
SSM_HEADDIM = 64
SSM_HEADS = 16
SSM_GROUPS = 4
SSM_RATIO = 4
SSM_STATE = 128
SSM_BC = SSM_GROUPS * SSM_STATE
SSM_XBC = SSM_D_INNER + 2 * SSM_BC
HYB_QKV = 2 * RET_Q + RET_V
HYB_BF = HYB_QKV + 2 * SSM_BC
XS_COL = 2 * RET_V
DT_COL = XS_COL + SSM_D_INNER
HYB_REST = DT_COL + LANES
HYB_MIX = RET_V + SSM_D_INNER
ATT_HEADS = 16
ATT_KV_HEADS = 4
ATT_RATIO = 4
ATT_HD = 64
ATT_Q = ATT_HEADS * ATT_HD
ATT_KV = ATT_KV_HEADS * ATT_HD
ROT_FREQS = ATT_HD // 4
ROPE_BASE = 10000.0
FFN_HIDDEN = 2816

TOKEN_TILE = 512
PROJ_CHUNK = 512
ATT_TILE = 256
NEG_BIG = -1e30
Q_SCALE = ATT_HD ** -0.5 * 1.4426950408889634


def _cparams(sem):
    return pltpu.CompilerParams(dimension_semantics=sem, vmem_limit_bytes=VMEM_LIMIT)


def _norm_mod(x, gain, sc, sh):
    ms = jnp.mean(x * x, axis=-1, keepdims=True)
    return x * lax.rsqrt(ms + EPS) * gain * (1.0 + sc) + sh


def _vec_spec(width, tiles_per_batch, per_batch):
    return pl.BlockSpec((1, 1, width), lambda i: ((i // tiles_per_batch) if per_batch else 0, 0, 0))


def _mod_body(c_ref, w_ref, b_ref, o_ref):
    cs = jax.nn.silu(c_ref[...])
    o_ref[0] = jnp.dot(cs, w_ref[0], precision=HIGHEST, preferred_element_type=F32) + b_ref[0]


def _modulation(cond, w_mod, b_mod):
    depth, _, n = w_mod.shape
    rows = cond.shape[0]
    tn = n // 4
    return pl.pallas_call(
        _mod_body,
        grid=(depth, n // tn),
        in_specs=[
            pl.BlockSpec((rows, D_MODEL), lambda l, j: (0, 0)),
            pl.BlockSpec((1, D_MODEL, tn), lambda l, j: (l, 0, j)),
            pl.BlockSpec((1, 1, tn), lambda l, j: (l, 0, j)),
        ],
        out_specs=pl.BlockSpec((1, rows, tn), lambda l, j: (l, 0, j)),
        out_shape=jax.ShapeDtypeStruct((depth, rows, n), F32),
        compiler_params=_cparams(("arbitrary", "arbitrary")),
        name="modulation",
    )(cond, w_mod, b_mod.reshape(depth, 1, n))


def _chunks(total, width):
    return [(s, min(width, total - s)) for s in range(0, total, width)]


def _resident(shape):
    return pl.BlockSpec(shape, lambda *_: (0,) * len(shape), pipeline_mode=pl.Buffered(1))


def _norm_mod_halo(seq_len, tm, x_ref, xn_ref, xp_ref, gain_ref, sc_ref, sh_ref, h_scr):
    gain, sc, sh = gain_ref[...], sc_ref[0], sh_ref[0]
    h_scr[0:tm] = _norm_mod(x_ref[...], gain, sc, sh).astype(BF16)
    if tm >= seq_len:
        h_scr[tm:tm + 2 * SUBLANES] = jnp.zeros((2 * SUBLANES, D_MODEL), BF16)
    else:
        tiles_per_seq = seq_len // tm
        pos = pl.program_id(0) % tiles_per_seq
        hn = jnp.where(pos == tiles_per_seq - 1, 0.0, _norm_mod(xn_ref[0], gain, sc, sh))
        hp = jnp.where(pos == 0, 0.0, _norm_mod(xp_ref[0], gain, sc, sh))
        h_scr[tm:tm + 2 * SUBLANES] = jnp.concatenate([hn, hp], axis=0).astype(BF16)


def _dwconv3_rows(u, cw, cb, tm, seq_len):
    rows = tm + 2 * SUBLANES
    before, after = pltpu.roll(u, 1, 0), pltpu.roll(u, rows - 1, 0)
    if tm > seq_len:
        pos = lax.broadcasted_iota(jnp.int32, u.shape, 0) % seq_len
        before = jnp.where(pos == 0, 0.0, before)
        after = jnp.where(pos == seq_len - 1, 0.0, after)
    c = before * cw[0:1] + u * cw[1:2] + after * cw[2:3] + cb
    return c[0:tm]


def _halo_specs(m, tm):
    m8 = m // SUBLANES
    r8 = tm // SUBLANES
    return [pl.BlockSpec((1, SUBLANES, D_MODEL), lambda i: (jnp.minimum((i + 1) * r8, m8 - 1), 0, 0)),
            pl.BlockSpec((1, SUBLANES, D_MODEL), lambda i: (jnp.maximum(i * r8 - 1, 0), 0, 0))]


def _hyb_in_body(seq_len, tm, tn, x_ref, xn_ref, xp_ref, gain_ref, sc_ref, sh_ref, w_ref, cw_ref, cb_ref,
                 bf_ref, rest_ref, h_scr):
    _norm_mod_halo(seq_len, tm, x_ref, xn_ref, xp_ref, gain_ref, sc_ref, sh_ref, h_scr)
    h_all = h_scr[...]
    h = h_scr[0:tm]
    w_g = HYB_QKV
    w_x = w_g + 2 * RET_V
    w_bc = w_x + SSM_D_INNER
    w_dt = w_bc + 2 * SSM_BC

    def project(wcol, width, conv_col):
        wcols = w_ref[:, wcol:wcol + width]
        if conv_col is None:
            return jnp.dot(h, wcols, preferred_element_type=F32)
        u = jnp.dot(h_all, wcols, preferred_element_type=F32)
        return jax.nn.silu(_dwconv3_rows(u, cw_ref[:, conv_col:conv_col + width],
                                         cb_ref[:, conv_col:conv_col + width], tm, seq_len))

    for col, width in _chunks(HYB_BF, tn):
        if col >= HYB_QKV:
            out = project(w_bc + col - HYB_QKV, width, SSM_D_INNER + col - HYB_QKV)
        else:
            out = project(col, width, None)
            if RET_Q <= col < 2 * RET_Q:
                out = out * (RET_DK ** -0.5)
        bf_ref[:, col:col + width] = out.astype(BF16)
    for col, width in _chunks(DT_COL, tn):
        rest_ref[:, col:col + width] = project(w_g + col, width, col - XS_COL if col >= XS_COL else None)
    dt = jnp.dot(h, w_ref[:, w_dt:w_dt + SSM_HEADS], preferred_element_type=F32)
    rest_ref[:, DT_COL:HYB_REST] = jnp.concatenate([dt, jnp.zeros((tm, HYB_REST - DT_COL - SSM_HEADS), F32)], axis=1)


def _hyb_in_proj(x, gain, sc, sh, w, conv_w, conv_b, seq_len, tm, tn):
    m = x.shape[0]
    per_batch = sc.shape[0] > 1
    tpb = seq_len // tm
    assert HYB_QKV % tn == 0 and XS_COL % tn == 0 and DT_COL % tn == 0
    x8 = x.reshape(m // SUBLANES, SUBLANES, D_MODEL)
    return pl.pallas_call(
        functools.partial(_hyb_in_body, seq_len, tm, tn),
        grid=(m // tm,),
        in_specs=[
            pl.BlockSpec((tm, D_MODEL), lambda i: (i, 0)),
            *_halo_specs(m, tm),
            pl.BlockSpec((1, D_MODEL), lambda i: (0, 0)),
            _vec_spec(D_MODEL, tpb, per_batch),
            _vec_spec(D_MODEL, tpb, per_batch),
            _resident(w.shape),
            _resident((3, SSM_XBC)),
            _resident((1, SSM_XBC)),
        ],
        out_specs=[pl.BlockSpec((tm, HYB_BF), lambda i: (i, 0)), pl.BlockSpec((tm, HYB_REST), lambda i: (i, 0))],
        out_shape=[jax.ShapeDtypeStruct((m, HYB_BF), BF16), jax.ShapeDtypeStruct((m, HYB_REST), F32)],
        scratch_shapes=[pltpu.VMEM((tm + 2 * SUBLANES, D_MODEL), BF16)],
        compiler_params=_cparams(("arbitrary",)),
        name="hybrid_in_proj",
    )(x, x8, x8, gain, sc, sh, w, conv_w, conv_b.reshape(1, SSM_XBC))


def _ffn_body(seq_len, tm, th, x_ref, xn_ref, xp_ref, gain_ref, sc_ref, sh_ref, gate_ref,
              wu_ref, cw_ref, cb_ref, wd_ref, o_ref, h_scr, act_scr):
    _norm_mod_halo(seq_len, tm, x_ref, xn_ref, xp_ref, gain_ref, sc_ref, sh_ref, h_scr)
    h = h_scr[...]

    def conv_up(col, width):
        u = jnp.dot(h, wu_ref[:, col:col + width], preferred_element_type=F32)
        return _dwconv3_rows(u, cw_ref[:, col:col + width], cb_ref[:, col:col + width], tm, seq_len)

    for col, width in _chunks(FFN_HIDDEN, th):
        act = jax.nn.silu(conv_up(col, width)) * conv_up(FFN_HIDDEN + col, width)
        act_scr[:, col:col + width] = act.astype(BF16)
    out = jnp.dot(act_scr[...], wd_ref[...], preferred_element_type=F32)
    o_ref[...] = x_ref[...] + gate_ref[0] * out


def _conv_ffn(x, gain, sc, sh, gate, w_up, conv_w, conv_b, w_down, seq_len, tm, th):
    m = x.shape[0]
    per_batch = sc.shape[0] > 1
    tpb = seq_len // tm
    x8 = x.reshape(m // SUBLANES, SUBLANES, D_MODEL)
    conv_b = conv_b.reshape(1, 2 * FFN_HIDDEN)
    vec = _vec_spec(D_MODEL, tpb, per_batch)
    return pl.pallas_call(
        functools.partial(_ffn_body, seq_len, tm, th),
        grid=(m // tm,),
        in_specs=[
            pl.BlockSpec((tm, D_MODEL), lambda i: (i, 0)),
            *_halo_specs(m, tm),
            pl.BlockSpec((1, D_MODEL), lambda i: (0, 0)),
            vec, vec, vec,
            _resident((D_MODEL, 2 * FFN_HIDDEN)),
            _resident((3, 2 * FFN_HIDDEN)),
            _resident((1, 2 * FFN_HIDDEN)),
            _resident((FFN_HIDDEN, D_MODEL)),
        ],
        out_specs=pl.BlockSpec((tm, D_MODEL), lambda i: (i, 0)),
        out_shape=jax.ShapeDtypeStruct((m, D_MODEL), F32),
        scratch_shapes=[pltpu.VMEM((tm + 2 * SUBLANES, D_MODEL), BF16), pltpu.VMEM((tm, FFN_HIDDEN), BF16)],
        compiler_params=_cparams(("arbitrary",)),
        name="conv_ffn",
    )(x, x8, x8, gain, sc, sh, gate, w_up, conv_w, conv_b, w_down)


SSM_GW = SSM_RATIO * SSM_HEADDIM
FWD_CHUNKS_PER_STEP = 4


def _bcol(x, h):
    return jnp.broadcast_to(x[:, h:h + 1], (x.shape[0], LANES))


def _pair_sel(left, a, h):
    return jnp.where(left, _bcol(a, h), _bcol(a, h + 1))


def _expand_heads(a, spread_ref):
    hi = a.astype(BF16)
    lo = (a - hi.astype(F32)).astype(BF16)
    return (jnp.dot(hi, spread_ref[...], preferred_element_type=F32)
            + jnp.dot(lo, spread_ref[...], preferred_element_type=F32))


def _ssm_decay_row(left, efull, g):
    return jnp.concatenate([_pair_sel(left, efull, g * SSM_RATIO + 2 * m) for m in range(SSM_RATIO // 2)], axis=1)


def _load_states(s_ret, s_ssm, sret0_ref, sssm0_ref):
    if sret0_ref is None:
        s_ret[...] = jnp.zeros_like(s_ret)
        s_ssm[...] = jnp.zeros_like(s_ssm)
    else:
        s_ret[...] = sret0_ref[0]
        for g in range(SSM_GROUPS):
            s_ssm[g] = jnp.concatenate([sssm0_ref[0, g * SSM_RATIO + r] for r in range(SSM_RATIO)], axis=1)


def _store_states(s_ret, s_ssm, sret_ref, sssm_ref):
    sret_ref[0] = s_ret[...]
    for g in range(SSM_GROUPS):
        for r in range(SSM_RATIO):
            sssm_ref[0, g * SSM_RATIO + r] = s_ssm[g][:, r * SSM_HEADDIM:(r + 1) * SSM_HEADDIM]


def _ssm_log_decay(dt_raw, dtb, alog, lane):
    dt = jax.nn.softplus(dt_raw + dtb)
    return dt, jnp.where(lane < SSM_HEADS, dt * (-jnp.exp(alog)), 0.0)


def _bwd_state_body(has_init, k_ref, v_ref, xs_ref, bm_ref, dt_ref, dtb_ref, alog_ref, spread_ref, logdec_ref,
                    *rest):
    sret0_ref = sssm0_ref = None
    if has_init:
        sret0_ref, sssm0_ref = rest[0], rest[1]
        rest = rest[2:]
    sret_in_ref, sssm_in_ref, sret_fin_ref, sssm_fin_ref, s_ret, s_ssm, wcol = rest
    i = pl.program_id(1)
    n = pl.num_programs(1)
    ii = lax.broadcasted_iota(jnp.int32, (CHUNK, CHUNK), 0)
    jj = lax.broadcasted_iota(jnp.int32, (CHUNK, CHUNK), 1)
    lane = lax.broadcasted_iota(jnp.int32, (1, LANES), 1)
    left = lane < SSM_HEADDIM

    @pl.when(i == 0)
    def _():
        _load_states(s_ret, s_ssm, sret0_ref, sssm0_ref)
        for h in range(RET_HEADS):
            wcol[h] = jnp.exp(ii.astype(F32) * logdec_ref[1, h])

    cps = sret_in_ref.shape[1]
    full = jnp.full((1, LANES), float(CHUNK), F32)
    prep = []
    for c in range(cps):
        rows = slice(c * CHUNK, (c + 1) * CHUNK)
        dt, la = _ssm_log_decay(dt_ref[rows], dtb_ref[...], alog_ref[...], lane)
        rc = jnp.dot((jj >= ii).astype(F32), la, precision=HIGHEST, preferred_element_type=F32)
        last = rc[0:1]
        prep.append((_expand_heads(jnp.exp(last - rc) * dt, spread_ref), jnp.exp(last)))

    for c in reversed(range(cps)):
        rows = slice(c * CHUNK, (c + 1) * CHUNK)
        sret_in_ref[0, c] = s_ret[...].astype(BF16)
        sssm_in_ref[0, c] = s_ssm[...].astype(BF16)
        for h in range(RET_HEADS):
            kh = k_ref[rows, h * RET_DK:(h + 1) * RET_DK]
            vw = (v_ref[rows, h * RET_DV:(h + 1) * RET_DV].astype(F32) * wcol[h]).astype(BF16)
            upd = lax.dot_general(kh, vw, (((0,), (0,)), ((), ())), preferred_element_type=F32)
            s_ret[h] = s_ret[h] * jnp.exp(full * logdec_ref[1, h]) + upd
        wx, efull = prep[c]
        for g in range(SSM_GROUPS):
            vw = (xs_ref[rows, g * SSM_GW:(g + 1) * SSM_GW] * wx[:, g * SSM_GW:(g + 1) * SSM_GW]).astype(BF16)
            bg = bm_ref[rows, g * SSM_STATE:(g + 1) * SSM_STATE]
            upd = lax.dot_general(bg, vw, (((0,), (0,)), ((), ())), preferred_element_type=F32)
            s_ssm[g] = s_ssm[g] * _ssm_decay_row(left, efull, g) + upd

    @pl.when(i == n - 1)
    def _():
        _store_states(s_ret, s_ssm, sret_fin_ref, sssm_fin_ref)


def _hyb_fwd_body(has_init, qkv_ref, xs_ref, bc_ref, dt_ref, sret_in_ref, sssm_in_ref, dtb_ref, alog_ref, dsk_ref,
                  spread_ref, logdec_ref, *rest):
    sret0_ref = sssm0_ref = None
    if has_init:
        sret0_ref, sssm0_ref = rest[0], rest[1]
        rest = rest[2:]
    y_ref, sret_fin_ref, sssm_fin_ref, s_ret, s_ssm, dcomb, ein0, ein1, wcol = rest
    i = pl.program_id(1)
    n = pl.num_programs(1)
    ii = lax.broadcasted_iota(jnp.int32, (CHUNK, CHUNK), 0)
    jj = lax.broadcasted_iota(jnp.int32, (CHUNK, CHUNK), 1)
    lane = lax.broadcasted_iota(jnp.int32, (1, LANES), 1)
    left = lane < SSM_HEADDIM
    lower = jj <= ii

    @pl.when(i == 0)
    def _():
        _load_states(s_ret, s_ssm, sret0_ref, sssm0_ref)
        fi, fj = ii.astype(F32), jj.astype(F32)
        for h in range(RET_HEADS):
            la0, la1 = logdec_ref[0, h], logdec_ref[1, h]
            dec = jnp.exp(jnp.where(lower, (fi - fj) * la0, (fj - fi) * la1))
            dcomb[h] = jnp.where(ii == jj, 2.0, dec)
            ein0[h] = jnp.exp((fi + 1.0) * la0)
            ein1[h] = jnp.exp((CHUNK - fi) * la1)
            wcol[h] = jnp.exp((CHUNK - 1.0 - fi) * la0)

    for c in range(sret_in_ref.shape[1]):
        _fwd_chunk(c, qkv_ref, xs_ref, bc_ref, dt_ref, sret_in_ref, sssm_in_ref, dtb_ref, alog_ref, dsk_ref,
                   spread_ref, logdec_ref, y_ref, s_ret, s_ssm, dcomb, ein0, ein1, wcol)

    @pl.when(i == n - 1)
    def _():
        _store_states(s_ret, s_ssm, sret_fin_ref, sssm_fin_ref)


def _fwd_chunk(c, qkv_ref, xs_ref, bc_ref, dt_ref, sret_in_ref, sssm_in_ref, dtb_ref, alog_ref, dsk_ref,
               spread_ref, logdec_ref, y_ref, s_ret, s_ssm, dcomb, ein0, ein1, wcol):
    rows = pl.ds(c * CHUNK, CHUNK)
    qkv_ref, xs_ref, bc_ref, dt_ref, y_ref = (r.at[rows] for r in (qkv_ref, xs_ref, bc_ref, dt_ref, y_ref))
    ii = lax.broadcasted_iota(jnp.int32, (CHUNK, CHUNK), 0)
    jj = lax.broadcasted_iota(jnp.int32, (CHUNK, CHUNK), 1)
    lane = lax.broadcasted_iota(jnp.int32, (1, LANES), 1)
    left = lane < SSM_HEADDIM
    lower = jj <= ii

    def ret_head(h):
        return (qkv_ref[:, h * RET_DK:(h + 1) * RET_DK],
                qkv_ref[:, RET_Q + h * RET_DK:RET_Q + (h + 1) * RET_DK],
                qkv_ref[:, 2 * RET_Q + h * RET_DV:2 * RET_Q + (h + 1) * RET_DV])

    def ssm_group(g):
        return (bc_ref[:, g * SSM_STATE:(g + 1) * SSM_STATE],
                bc_ref[:, SSM_BC + g * SSM_STATE:SSM_BC + (g + 1) * SSM_STATE])

    dt_raw = dt_ref[...]
    dt0, la0 = _ssm_log_decay(dt_raw, dtb_ref[0:1], alog_ref[0:1], lane)
    dt1, la1 = _ssm_log_decay(dt_raw, dtb_ref[1:2], alog_ref[1:2], lane)
    c0 = jnp.dot(lower.astype(F32), la0, precision=HIGHEST, preferred_element_type=F32)
    c1 = jnp.dot((jj >= ii).astype(F32), la1, precision=HIGHEST, preferred_element_type=F32)

    ret_scores, ret_inter = [], []
    for h in range(RET_HEADS):
        qh, kh, _ = ret_head(h)
        ret_scores.append(lax.dot_general(qh, kh, (((1,), (1,)), ((), ())), preferred_element_type=F32))
        states = jnp.concatenate([s_ret[h].astype(BF16), sret_in_ref[0, c, h]], axis=1)
        ret_inter.append(jnp.dot(qh, states, preferred_element_type=F32))
    ssm_scores, ssm_inter = [], []
    for g in range(SSM_GROUPS):
        bg, cg = ssm_group(g)
        ssm_scores.append(lax.dot_general(cg, bg, (((1,), (1,)), ((), ())), preferred_element_type=F32))
        states = jnp.concatenate([s_ssm[g].astype(BF16), sssm_in_ref[0, c, g]], axis=1)
        ssm_inter.append(jnp.dot(cg, states, preferred_element_type=F32))

    full = jnp.full((1, LANES), float(CHUNK), F32)
    for h in range(RET_HEADS):
        _, kh, vh = ret_head(h)
        y = jnp.dot((ret_scores[h] * dcomb[h]).astype(BF16), vh, preferred_element_type=F32)
        inter = ret_inter[h]
        y_ref[:, h * RET_DV:(h + 1) * RET_DV] = y + inter[:, 0:RET_DV] * ein0[h] + inter[:, RET_DV:] * ein1[h]
        vw = (vh.astype(F32) * wcol[h]).astype(BF16)
        upd = lax.dot_general(kh, vw, (((0,), (0,)), ((), ())), preferred_element_type=F32)
        s_ret[h] = s_ret[h] * jnp.exp(full * logdec_ref[0, h]) + upd

    c0kt, c1kt = (c0 - jnp.log(dt0)).T, (c1 - jnp.log(dt1)).T
    dst = (dt0 + dt1).T
    last0 = c0[CHUNK - 1:CHUNK]
    w0x = _expand_heads(jnp.exp(last0 - c0) * dt0, spread_ref)
    efull0 = jnp.exp(last0)
    diag = jj == ii
    for g in range(SSM_GROUPS):
        bg, _ = ssm_group(g)
        s = ssm_scores[g]
        inter = ssm_inter[g]
        vws = []
        for m in range(SSM_RATIO // 2):
            ha = g * SSM_RATIO + 2 * m
            ps, c0bs, c1bs = [], [], []
            for h in (ha, ha + 1):
                c0b, c1b = _bcol(c0, h), _bcol(c1, h)
                arg = jnp.where(lower, c0b - c0kt[h:h + 1], c1b - c1kt[h:h + 1])
                ps.append((s * jnp.where(diag, dst[h:h + 1], jnp.exp(arg))).astype(BF16))
                c0bs.append(c0b)
                c1bs.append(c1b)
            col = (g * 2 + m) * LANES
            xs_pair = xs_ref[:, col:col + LANES]
            vals = jnp.concatenate([jnp.where(left, xs_pair, 0.0), jnp.where(left, 0.0, xs_pair)], axis=0)
            y = jnp.dot(jnp.concatenate(ps, axis=1), vals.astype(BF16), preferred_element_type=F32)
            y = y + inter[:, m * LANES:(m + 1) * LANES] * jnp.exp(jnp.where(left, c0bs[0], c0bs[1]))
            y = y + (inter[:, SSM_GW + m * LANES:SSM_GW + (m + 1) * LANES]
                     * jnp.exp(jnp.where(left, c1bs[0], c1bs[1])))
            y_ref[:, RET_V + col:RET_V + col + LANES] = y + dsk_ref[:, col:col + LANES] * xs_pair
            vws.append((xs_pair * w0x[:, col:col + LANES]).astype(BF16))
        upd = lax.dot_general(bg, jnp.concatenate(vws, axis=1), (((0,), (0,)), ((), ())),
                              preferred_element_type=F32)
        s_ssm[g] = s_ssm[g] * _ssm_decay_row(left, efull0, g) + upd


def _hybrid_mixer(qkv, rest, dt_bias, a_log, log_decay, d_skip, s_ret0, s_ssm0, batch, seq_len):
    m = qkv.shape[0]
    n = seq_len // CHUNK
    has_init = s_ret0 is not None
    dt_blk = DT_COL // LANES
    pad = lambda v: jnp.pad(v.astype(F32), ((0, 0), (0, LANES - v.shape[1])))
    dtb, alog = pad(dt_bias), pad(a_log)
    logdec = log_decay.astype(F32)
    spread = (lax.broadcasted_iota(jnp.int32, (LANES, SSM_D_INNER), 0)
              == lax.broadcasted_iota(jnp.int32, (LANES, SSM_D_INNER), 1) // SSM_HEADDIM).astype(BF16)
    spread_spec = pl.BlockSpec((LANES, SSM_D_INNER), lambda b, i: (0, 0))
    dsk = jnp.repeat((d_skip[0] + d_skip[1]).astype(F32), SSM_HEADDIM).reshape(1, SSM_D_INNER)
    smem = pl.BlockSpec(memory_space=pltpu.SMEM)
    ret_spec = pl.BlockSpec((1, RET_HEADS, RET_DK, RET_DV), lambda b, i: (b, 0, 0, 0))
    ssm_spec = pl.BlockSpec((1, SSM_HEADS, SSM_STATE, SSM_HEADDIM), lambda b, i: (b, 0, 0, 0))
    ret_shape = jax.ShapeDtypeStruct((batch, RET_HEADS, RET_DK, RET_DV), F32)
    ssm_shape = jax.ShapeDtypeStruct((batch, SSM_HEADS, SSM_STATE, SSM_HEADDIM), F32)
    ret_scr = pltpu.VMEM((RET_HEADS, RET_DK, RET_DV), F32)
    ssm_scr = pltpu.VMEM((SSM_GROUPS, SSM_STATE, SSM_GW), F32)
    const_scr = pltpu.VMEM((RET_HEADS, CHUNK, CHUNK), F32)

    cps = next(c for c in (8, 4, 2, 1) if n % c == 0)
    rsteps = n // cps
    rrows = cps * CHUNK

    def rev(b, i):
        return b * rsteps + rsteps - 1 - i

    in_specs = [
        pl.BlockSpec((rrows, RET_Q), lambda b, i: (rev(b, i), 1)),
        pl.BlockSpec((rrows, RET_V), lambda b, i: (rev(b, i), 1)),
        pl.BlockSpec((rrows, SSM_D_INNER), lambda b, i: (rev(b, i), XS_COL // SSM_D_INNER)),
        pl.BlockSpec((rrows, SSM_BC), lambda b, i: (rev(b, i), HYB_QKV // SSM_BC)),
        pl.BlockSpec((rrows, LANES), lambda b, i: (rev(b, i), dt_blk)),
        pl.BlockSpec((1, LANES), lambda b, i: (0, 0)),
        pl.BlockSpec((1, LANES), lambda b, i: (0, 0)),
        spread_spec,
        smem,
    ]
    args = [qkv, qkv, rest, qkv, rest, dtb[1:2], alog[1:2], spread, logdec]
    if has_init:
        in_specs += [ret_spec, ssm_spec]
        args += [s_ret0[:, 1], s_ssm0[:, 1]]
    sret_in, sssm_in, sret1, sssm1 = pl.pallas_call(
        functools.partial(_bwd_state_body, has_init),
        grid=(batch, rsteps),
        in_specs=in_specs,
        out_specs=[
            pl.BlockSpec((1, cps, RET_HEADS, RET_DK, RET_DV), lambda b, i: (b, rsteps - 1 - i, 0, 0, 0)),
            pl.BlockSpec((1, cps, SSM_GROUPS, SSM_STATE, SSM_GW), lambda b, i: (b, rsteps - 1 - i, 0, 0, 0)),
            ret_spec, ssm_spec,
        ],
        out_shape=[
            jax.ShapeDtypeStruct((batch, n, RET_HEADS, RET_DK, RET_DV), BF16),
            jax.ShapeDtypeStruct((batch, n, SSM_GROUPS, SSM_STATE, SSM_GW), BF16),
            ret_shape, ssm_shape,
        ],
        scratch_shapes=[ret_scr, ssm_scr, const_scr],
        compiler_params=_cparams(("arbitrary", "arbitrary")),
        name="hybrid_reverse_states",
    )(*args)

    fcps = next(c for c in (FWD_CHUNKS_PER_STEP, 2, 1) if n % c == 0)
    fsteps = n // fcps
    frows = fcps * CHUNK

    def fwd(b, i):
        return b * fsteps + i

    in_specs = [
        pl.BlockSpec((frows, HYB_QKV), lambda b, i: (fwd(b, i), 0)),
        pl.BlockSpec((frows, SSM_D_INNER), lambda b, i: (fwd(b, i), XS_COL // SSM_D_INNER)),
        pl.BlockSpec((frows, 2 * SSM_BC), lambda b, i: (fwd(b, i), HYB_QKV // (2 * SSM_BC))),
        pl.BlockSpec((frows, LANES), lambda b, i: (fwd(b, i), dt_blk)),
        pl.BlockSpec((1, fcps, RET_HEADS, RET_DK, RET_DV), lambda b, i: (b, i, 0, 0, 0)),
        pl.BlockSpec((1, fcps, SSM_GROUPS, SSM_STATE, SSM_GW), lambda b, i: (b, i, 0, 0, 0)),
        pl.BlockSpec((2, LANES), lambda b, i: (0, 0)),
        pl.BlockSpec((2, LANES), lambda b, i: (0, 0)),
        pl.BlockSpec((1, SSM_D_INNER), lambda b, i: (0, 0)),
        spread_spec,
        smem,
    ]
    args = [qkv, rest, qkv, rest, sret_in, sssm_in, dtb, alog, dsk, spread, logdec]
    if has_init:
        in_specs += [ret_spec, ssm_spec]
        args += [s_ret0[:, 0], s_ssm0[:, 0]]
    y, sret0, sssm0 = pl.pallas_call(
        functools.partial(_hyb_fwd_body, has_init),
        grid=(batch, fsteps),
        in_specs=in_specs,
        out_specs=[pl.BlockSpec((frows, HYB_MIX), lambda b, i: (fwd(b, i), 0)), ret_spec, ssm_spec],
        out_shape=[jax.ShapeDtypeStruct((m, HYB_MIX), F32), ret_shape, ssm_shape],
        scratch_shapes=[ret_scr, ssm_scr, const_scr, const_scr, const_scr, const_scr],
        compiler_params=_cparams(("arbitrary", "arbitrary")),
        name="hybrid_forward_mix",
    )(*args)
    return y, jnp.stack([sret0, sret1], axis=1), jnp.stack([sssm0, sssm1], axis=1)


def _hyb_out_body(y_ref, g_ref, z_ref, x_ref, gate_ref, gn_ref, ng_ref, w_ref, o_ref):
    y = y_ref[...]
    parts = []
    for h in range(RET_HEADS):
        yh = y[:, h * RET_DV:(h + 1) * RET_DV]
        mu = jnp.mean(yh, axis=-1, keepdims=True)
        var = jnp.mean(jnp.square(yh - mu), axis=-1, keepdims=True)
        parts.append((yh - mu) * lax.rsqrt(var + EPS))
    y_ret = jax.nn.silu(g_ref[...]) * (jnp.concatenate(parts, axis=1) * gn_ref[...])
    yz = y[:, RET_V:] * jax.nn.silu(z_ref[...])
    y_ssm = yz * lax.rsqrt(jnp.mean(yz * yz, axis=-1, keepdims=True) + EPS) * ng_ref[...]
    out = jnp.dot(y_ret.astype(BF16), w_ref[0:RET_V], preferred_element_type=F32)
    out = out + jnp.dot(y_ssm.astype(BF16), w_ref[RET_V:HYB_MIX], preferred_element_type=F32)
    o_ref[...] = x_ref[...] + gate_ref[0] * out


def _hyb_out(y, rest, x, gate, gn_gain, norm_gain, w_out, seq_len, tm):
    m = x.shape[0]
    per_batch = gate.shape[0] > 1
    tpb = seq_len // tm
    return pl.pallas_call(
        _hyb_out_body,
        grid=(m // tm,),
        in_specs=[
            pl.BlockSpec((tm, HYB_MIX), lambda i: (i, 0)),
            pl.BlockSpec((tm, RET_V), lambda i: (i, 0)),
            pl.BlockSpec((tm, SSM_D_INNER), lambda i: (i, 1)),
            pl.BlockSpec((tm, D_MODEL), lambda i: (i, 0)),
            _vec_spec(D_MODEL, tpb, per_batch),
            pl.BlockSpec((1, RET_V), lambda i: (0, 0)),
            pl.BlockSpec((1, SSM_D_INNER), lambda i: (0, 0)),
            pl.BlockSpec((HYB_MIX, D_MODEL), lambda i: (0, 0)),
        ],
        out_specs=pl.BlockSpec((tm, D_MODEL), lambda i: (i, 0)),
        out_shape=jax.ShapeDtypeStruct((m, D_MODEL), F32),
        compiler_params=_cparams(("arbitrary",)),
        name="hybrid_out_proj",
    )(y, rest, rest, x, gate, gn_gain, norm_gain, w_out)


def _head_rms(t, seg_ref):
    outs = []
    for c in range(t.shape[1] // LANES):
        tc = t[:, c * LANES:(c + 1) * LANES]
        sq = tc * tc
        hi = sq.astype(BF16)
        lo = (sq - hi.astype(F32)).astype(BF16)
        ssum = (jnp.dot(hi, seg_ref[...], preferred_element_type=F32)
                + jnp.dot(lo, seg_ref[...], preferred_element_type=F32))
        outs.append(tc * lax.rsqrt(ssum * (1.0 / ATT_HD) + EPS))
    return outs


def _rope(tc, cos, sin, lane):
    fwd = pltpu.roll(tc, LANES - ROT_FREQS, 1)
    bwd = pltpu.roll(tc, ROT_FREQS, 1)
    return tc * cos + jnp.where(lane % (2 * ROT_FREQS) < ROT_FREQS, fwd, bwd) * sin


def _qkv_body(has_rope, x_ref, gain_ref, sc_ref, sh_ref, w_ref, seg_ref, qg_ref, kg_ref, *rest):
    if has_rope:
        cos_ref, sin_ref, q_ref, k_ref, v_ref = rest
    else:
        q_ref, k_ref, v_ref, kf_ref, vf_ref = rest
    h = _norm_mod(x_ref[...], gain_ref[...], sc_ref[0], sh_ref[0]).astype(BF16)
    lane = lax.broadcasted_iota(jnp.int32, (1, LANES), 1)

    def project(col):
        t = jnp.dot(h, w_ref[:, col:col + MXU_COLS], preferred_element_type=F32)
        return [t[:, c * LANES:(c + 1) * LANES] for c in range(MXU_COLS // LANES)]

    def normed(tiles, gain_ref_):
        tiles = [t * gain_ref_[...] for t in _head_rms(jnp.concatenate(tiles, axis=1), seg_ref)]
        return tiles, ([_rope(t, cos_ref[...], sin_ref[...], lane) for t in tiles] if has_rope else tiles)

    def emit_q(col, tiles):
        _, q = normed(tiles, qg_ref)
        for c, t in enumerate(q):
            row = col + c * LANES
            q_ref[0, row:row + LANES, :] = (t * Q_SCALE).T.astype(BF16)

    def emit_k(col, tiles):
        k, kr = normed(tiles, kg_ref)
        for c in range(MXU_COLS // LANES):
            lo = col + c * LANES
            if not has_rope:
                kf_ref[:, lo:lo + LANES] = k[c]
            kc = kr[c].astype(BF16)
            for a in range(LANES // ATT_HD):
                k_ref[0, lo // ATT_HD + a] = kc[:, a * ATT_HD:(a + 1) * ATT_HD]

    def emit_v(col, tiles):
        for c in range(MXU_COLS // LANES):
            lo = col + c * LANES
            if not has_rope:
                vf_ref[:, lo:lo + LANES] = tiles[c]
            vc = tiles[c].T.astype(BF16)
            for a in range(LANES // ATT_HD):
                v_ref[0, lo // ATT_HD + a] = vc[a * ATT_HD:(a + 1) * ATT_HD]

    work = ([(emit_q, col, col) for col in range(0, ATT_Q, MXU_COLS)]
            + [(emit_k, col, ATT_Q + col) for col in range(0, ATT_KV, MXU_COLS)]
            + [(emit_v, col, ATT_Q + ATT_KV + col) for col in range(0, ATT_KV, MXU_COLS)])
    tiles = project(work[0][2])
    for n, (emit, col, _) in enumerate(work):
        ahead = project(work[n + 1][2]) if n + 1 < len(work) else None
        emit(col, tiles)
        tiles = ahead


def _qkv_proj(x, gain, sc, sh, w_qkv, q_gain, k_gain, rope, seq_len, tm):
    m = x.shape[0]
    per_batch = sc.shape[0] > 1
    tpb = seq_len // tm
    has_rope = rope is not None
    seg = (lax.broadcasted_iota(jnp.int32, (LANES, LANES), 0) // ATT_HD
           == lax.broadcasted_iota(jnp.int32, (LANES, LANES), 1) // ATT_HD).astype(BF16)
    tile2 = lambda v: jnp.tile(v.astype(F32), LANES // ATT_HD).reshape(1, LANES)
    in_specs = [
        pl.BlockSpec((tm, D_MODEL), lambda i: (i, 0)),
        pl.BlockSpec((1, D_MODEL), lambda i: (0, 0)),
        _vec_spec(D_MODEL, tpb, per_batch),
        _vec_spec(D_MODEL, tpb, per_batch),
        pl.BlockSpec((D_MODEL, ATT_Q + 2 * ATT_KV), lambda i: (0, 0)),
        pl.BlockSpec((LANES, LANES), lambda i: (0, 0)),
        pl.BlockSpec((1, LANES), lambda i: (0, 0)),
        pl.BlockSpec((1, LANES), lambda i: (0, 0)),
    ]
    args = [x, gain, sc, sh, w_qkv, seg, tile2(q_gain), tile2(k_gain)]
    batch = m // seq_len
    out_specs = [pl.BlockSpec((1, ATT_Q, tm), lambda i: (i // tpb, 0, i % tpb)),
                 pl.BlockSpec((1, ATT_KV_HEADS, tm, ATT_HD), lambda i: (i // tpb, 0, i % tpb, 0)),
                 pl.BlockSpec((1, ATT_KV_HEADS, ATT_HD, tm), lambda i: (i // tpb, 0, 0, i % tpb))]
    out_shape = [jax.ShapeDtypeStruct((batch, ATT_Q, seq_len), BF16),
                 jax.ShapeDtypeStruct((batch, ATT_KV_HEADS, seq_len, ATT_HD), BF16),
                 jax.ShapeDtypeStruct((batch, ATT_KV_HEADS, ATT_HD, seq_len), BF16)]
    if has_rope:
        in_specs += [pl.BlockSpec((tm, LANES), lambda i: (i % tpb, 0))] * 2
        args += list(rope)
    else:
        out_specs += [pl.BlockSpec((tm, ATT_KV), lambda i: (i, 0))] * 2
        out_shape += [jax.ShapeDtypeStruct((m, ATT_KV), F32)] * 2
    return pl.pallas_call(
        functools.partial(_qkv_body, has_rope),
        grid=(m // tm,),
        in_specs=in_specs,
        out_specs=out_specs,
        out_shape=out_shape,
        compiler_params=_cparams(("arbitrary",)),
        name="attn_qkv_proj",
    )(*args)


def _rope_tables(seq_len):
    rows = seq_len // GRID_W
    row = jnp.repeat(jnp.arange(rows, dtype=F32), GRID_W)
    col = jnp.tile(jnp.arange(GRID_W, dtype=F32), rows)
    inv = ROPE_BASE ** (-jnp.arange(ROT_FREQS, dtype=F32) / ROT_FREQS)
    ar, ac = row[:, None] * inv, col[:, None] * inv
    cos = jnp.concatenate([jnp.cos(ar), jnp.cos(ar), jnp.cos(ac), jnp.cos(ac)], axis=1)
    sin = jnp.concatenate([-jnp.sin(ar), jnp.sin(ar), -jnp.sin(ac), jnp.sin(ac)], axis=1)
    return jnp.tile(cos, (1, LANES // ATT_HD)), jnp.tile(sin, (1, LANES // ATT_HD))


V_ROWS = ATT_HD + 16
ATT_SLOTS = 4


def _attn_body(tq, tk, qt_ref, k_ref, vt_ref, x_ref, gate_ref, wo_ref, o_ref, acc_scr, s_scr, ot_scr):
    nkv = k_ref.shape[2] // tk
    ones_rows = jnp.ones((V_ROWS - ATT_HD, tk), BF16)
    def scores(g, t, slot):
        kt = k_ref[0, g, pl.ds(pl.multiple_of(t * tk, tk), tk), :]
        tops = []
        for r in range(ATT_RATIO):
            hd = g * ATT_RATIO + r
            s = jnp.dot(kt, qt_ref[0, hd * ATT_HD:(hd + 1) * ATT_HD, :], preferred_element_type=F32)
            s_scr[slot, r] = s
            tops.append(jnp.max(s, axis=0, keepdims=True))
        return tuple(tops)

    def softmax_pv(g, t, slot, ms, tops):
        vt = jnp.concatenate([vt_ref[0, g, :, pl.ds(pl.multiple_of(t * tk, tk), tk)], ones_rows], axis=0)
        new_ms = [jnp.maximum(ms[r], tops[r]) for r in range(ATT_RATIO)]
        ps = [jnp.exp2(s_scr[slot, r] - new_ms[r]).astype(BF16) for r in range(ATT_RATIO)]
        for r in range(ATT_RATIO):
            acc_scr[r] = (jnp.exp2(ms[r] - new_ms[r]) * acc_scr[r]
                          + jnp.dot(vt, ps[r], preferred_element_type=F32))
        return tuple(new_ms)

    first_tiles = range(min(2, nkv))
    quads = max(nkv - 2, 0) // ATT_SLOTS
    off = 0
    tops = {t: scores(0, t, t) for t in first_tiles}
    for g in range(ATT_KV_HEADS):
        acc_scr[...] = jnp.zeros_like(acc_scr)
        slot_of = lambda t, off=off: (t + off) % ATT_SLOTS

        def quad_step(i, carry, g=g, slot_of=slot_of):
            ms, top0, top1 = carry
            t = i * ATT_SLOTS
            top2 = scores(g, t + 2, slot_of(2))
            top3 = scores(g, t + 3, slot_of(3))
            ms = softmax_pv(g, t, slot_of(0), ms, top0)
            ms = softmax_pv(g, t + 1, slot_of(1), ms, top1)
            top0 = scores(g, t + 4, slot_of(0))
            top1 = scores(g, t + 5, slot_of(1))
            ms = softmax_pv(g, t + 2, slot_of(2), ms, top2)
            return softmax_pv(g, t + 3, slot_of(3), ms, top3), top0, top1

        ms = tuple(jnp.full((1, tq), NEG_BIG, F32) for _ in range(ATT_RATIO))
        if quads:
            ms, top0, top1 = lax.fori_loop(0, quads, quad_step, (ms, tops[0], tops[1]))
            tops = {quads * ATT_SLOTS: top0, quads * ATT_SLOTS + 1: top1}
        next_off, next_tops = (nkv + off) % ATT_SLOTS, {}
        for t in range(quads * ATT_SLOTS, nkv):
            if t % 2 == 0:
                for ahead in (t + 2, t + 3):
                    if ahead < nkv:
                        tops[ahead] = scores(g, ahead, slot_of(ahead))
            if t == max(nkv - 2, quads * ATT_SLOTS) and g + 1 < ATT_KV_HEADS:
                next_tops = {n: scores(g + 1, n, (n + next_off) % ATT_SLOTS) for n in first_tiles}
            ms = softmax_pv(g, t, slot_of(t), ms, tops.pop(t))
        off, tops = next_off, next_tops
        for r in range(ATT_RATIO):
            hd = g * ATT_RATIO + r
            a = acc_scr[r]
            ot_scr[hd * ATT_HD:(hd + 1) * ATT_HD, :] = (a[0:ATT_HD] / a[ATT_HD:ATT_HD + 1]).astype(BF16)
    out = lax.dot_general(ot_scr[...], wo_ref[...], (((0,), (0,)), ((), ())), preferred_element_type=F32)
    o_ref[0] = x_ref[0] + gate_ref[0] * out


def _attention(qt, k4, vt, x, gate, w_o, batch, seq_len, tq, tk):
    per_batch = gate.shape[0] > 1
    nkeys = k4.shape[2]
    x3 = x.reshape(batch, seq_len, D_MODEL)
    out = pl.pallas_call(
        functools.partial(_attn_body, tq, tk),
        grid=(batch, seq_len // tq),
        in_specs=[
            pl.BlockSpec((1, ATT_Q, tq), lambda b, i: (b, 0, i)),
            pl.BlockSpec((1, ATT_KV_HEADS, nkeys, ATT_HD), lambda b, i: (b, 0, 0, 0)),
            pl.BlockSpec((1, ATT_KV_HEADS, ATT_HD, nkeys), lambda b, i: (b, 0, 0, 0)),
            pl.BlockSpec((1, tq, D_MODEL), lambda b, i: (b, i, 0)),
            pl.BlockSpec((1, 1, D_MODEL), lambda b, i: (b if per_batch else 0, 0, 0)),
            pl.BlockSpec((ATT_Q, D_MODEL), lambda b, i: (0, 0)),
        ],
        out_specs=pl.BlockSpec((1, tq, D_MODEL), lambda b, i: (b, i, 0)),
        out_shape=jax.ShapeDtypeStruct((batch, seq_len, D_MODEL), F32),
        scratch_shapes=[pltpu.VMEM((ATT_RATIO, V_ROWS, tq), F32), pltpu.VMEM((ATT_SLOTS, ATT_RATIO, tk, tq), F32),
                        pltpu.VMEM((ATT_Q, tq), BF16)],
        compiler_params=_cparams(("arbitrary", "arbitrary")),
        name="attention",
    )(qt, k4, vt, x3, gate, w_o)
    return out.reshape(batch * seq_len, D_MODEL)


def _keys_values(k4, vt, cache_k, cache_v):
    k4 = jnp.concatenate([jnp.swapaxes(cache_k, 1, 2).astype(BF16), k4], axis=2)
    vt = jnp.concatenate([jnp.transpose(cache_v, (0, 2, 3, 1)).astype(BF16), vt], axis=3)
    return k4, vt


def _trunk(x, mods, caches, P, batch, seq_len, tm, tq, tk):
    rope = None if caches is None else _rope_tables(seq_len)
    outs = {}
    sh1, sc1, g1, sh2, sc2, g2 = mods[0]
    qkv, rest = _hyb_in_proj(x, P['norm_mix'][0:1], sc1, sh1, P['hyb_w_in'], P['ssm_conv_w'][0],
                             P['ssm_conv_b'][0], seq_len, tm, PROJ_CHUNK)
    s_ret0 = None if caches is None else caches[0][:, 0]
    s_ssm0 = None if caches is None else caches[1][:, 0]
    y, s_ret, s_ssm = _hybrid_mixer(qkv, rest, P['ssm_dt_bias'][0], P['ssm_a_log'][0], P['ret_log_decay'][0],
                                    P['ssm_d'][0], s_ret0, s_ssm0, batch, seq_len)
    outs['ret'] = s_ret[:, None]
    outs['ssm'] = s_ssm[:, None]
    x = _hyb_out(y, rest, x, g1, P['ret_gn_gain'][0:1], P['ssm_norm_gain'][0:1], P['hyb_w_out'], seq_len, tm)
    x = _conv_ffn(x, P['norm_ffn'][0:1], sc2, sh2, g2, P['ffn_w_up'][0], P['ffn_conv_w'][0], P['ffn_conv_b'][0],
                  P['ffn_w_down'][0], seq_len, tm, PROJ_CHUNK)
    sh1, sc1, g1, sh2, sc2, g2 = mods[1]
    res = _qkv_proj(x, P['norm_mix'][1:2], sc1, sh1, P['attn_w_qkv'], P['attn_q_gain'][0], P['attn_k_gain'][0],
                    rope, seq_len, min(tm, seq_len))
    if caches is None:
        qt, k4, vt, kf, vf = res
        outs['k'] = kf.reshape(batch, 1, seq_len, ATT_KV_HEADS, ATT_HD)
        outs['v'] = vf.reshape(batch, 1, seq_len, ATT_KV_HEADS, ATT_HD)
    else:
        qt, k4, vt = res
        k4, vt = _keys_values(k4, vt, caches[2][:, 0], caches[3][:, 0])
    x = _attention(qt, k4, vt, x, g1, P['attn_w_o'], batch, seq_len, tq, tk)
    x = _conv_ffn(x, P['norm_ffn'][1:2], sc2, sh2, g2, P['ffn_w_up'][1], P['ffn_conv_w'][1], P['ffn_conv_b'][1],
                  P['ffn_w_down'][1], seq_len, tm, PROJ_CHUNK)
    return x, outs


def kernel(x_prompt, x_sample, state_ret, state_ssm, cache_attn_k, cache_attn_v, c, c_ctx, w_mod, b_mod, norm_mix,
           norm_ffn, ffn_w_up, ffn_conv_w, ffn_conv_b, ffn_w_down, hyb_w_in, hyb_w_out, ret_log_decay, ret_gn_gain,
           ssm_conv_w, ssm_conv_b, ssm_a_log, ssm_dt_bias, ssm_d, ssm_norm_gain, attn_w_qkv, attn_q_gain,
           attn_k_gain, attn_w_o):
    batch, seq, _ = x_prompt.shape
    dec_batch, dec_seq, _ = x_sample.shape
    depth = w_mod.shape[0]

    rows = -(-(dec_batch + 1) // SUBLANES) * SUBLANES
    cond = jnp.concatenate([c, c_ctx[None, :], jnp.zeros((rows - dec_batch - 1, D_MODEL), F32)], axis=0)
    mod = _modulation(cond, w_mod, b_mod).reshape(depth, rows, 6, 1, D_MODEL)
    mods_sample = [[mod[l, 0:dec_batch, t] for t in range(6)] for l in range(depth)]
    mods_prompt = [[mod[l, dec_batch:dec_batch + 1, t] for t in range(6)] for l in range(depth)]

    P = {
        'norm_mix': norm_mix, 'norm_ffn': norm_ffn,
        'hyb_w_in': hyb_w_in[0].astype(BF16),
        'hyb_w_out': hyb_w_out[0].astype(BF16),
        'ret_log_decay': ret_log_decay, 'ret_gn_gain': ret_gn_gain,
        'ssm_conv_w': ssm_conv_w, 'ssm_conv_b': ssm_conv_b, 'ssm_a_log': ssm_a_log, 'ssm_dt_bias': ssm_dt_bias,
        'ssm_d': ssm_d, 'ssm_norm_gain': ssm_norm_gain,
        'attn_w_qkv': attn_w_qkv[0].astype(BF16), 'attn_q_gain': attn_q_gain, 'attn_k_gain': attn_k_gain,
        'attn_w_o': attn_w_o[0].astype(BF16),
        'ffn_w_up': ffn_w_up.astype(BF16), 'ffn_conv_w': ffn_conv_w, 'ffn_conv_b': ffn_conv_b,
        'ffn_w_down': ffn_w_down.astype(BF16),
    }

    y_prompt, outs = _trunk(x_prompt.reshape(batch * seq, D_MODEL), mods_prompt, None, P, batch, seq,
                            tm=TOKEN_TILE if (batch * seq) % TOKEN_TILE == 0 and TOKEN_TILE % seq == 0 else min(ATT_TILE, seq),
                            tq=min(ATT_TILE, seq), tk=min(ATT_TILE, seq))
    caches = (state_ret, state_ssm, cache_attn_k, cache_attn_v)
    y_sample, _ = _trunk(x_sample.reshape(dec_batch * dec_seq, D_MODEL), mods_sample, caches, P, dec_batch, dec_seq,
                         tm=TOKEN_TILE, tq=ATT_TILE, tk=ATT_TILE)
    return (y_prompt.reshape(batch, seq, D_MODEL), y_sample.reshape(dec_batch, dec_seq, D_MODEL),
            outs['ret'], outs['ssm'], outs['k'], outs['v'])
```

```python
import functools

import jax
import jax.numpy as jnp
from jax import lax
from jax.experimental import pallas as pl
from jax.experimental.pallas import tpu as pltpu

F32 = jnp.float32
BF16 = jnp.bfloat16
HIGHEST = lax.Precision.HIGHEST

LANES = 128
SUBLANES = 8
MXU_COLS = 256
VMEM_LIMIT = 56 * 1024 * 1024

D_MODEL = 1024
EPS = 1e-6
GRID_W = 64
CHUNK = 128
RET_HEADS = 8
RET_DK = 64
RET_DV = 128
RET_Q = RET_HEADS * RET_DK
RET_V = RET_HEADS * RET_DV
SSM_D_INNER = 1024
SSM_HEADDIM = 64
SSM_HEADS = 16
SSM_GROUPS = 4
SSM_RATIO = 4
SSM_STATE = 128
SSM_BC = SSM_GROUPS * SSM_STATE
SSM_XBC = SSM_D_INNER + 2 * SSM_BC
HYB_QKV = 2 * RET_Q + RET_V
HYB_BF = HYB_QKV + 2 * SSM_BC
XS_COL = 2 * RET_V
DT_COL = XS_COL + SSM_D_INNER
HYB_REST = DT_COL + LANES
HYB_MIX = RET_V + SSM_D_INNER
ATT_HEADS = 16
ATT_KV_HEADS = 4
ATT_RATIO = 4
ATT_HD = 64
ATT_Q = ATT_HEADS * ATT_HD
ATT_KV = ATT_KV_HEADS * ATT_HD
ROT_FREQS = ATT_HD // 4
ROPE_BASE = 10000.0
FFN_HIDDEN = 2816

TOKEN_TILE = 512
PROJ_CHUNK = 512
ATT_TILE = 256
NEG_BIG = -1e30
Q_SCALE = ATT_HD ** -0.5 * 1.4426950408889634


def _cparams(sem):
    return pltpu.CompilerParams(dimension_semantics=sem, vmem_limit_bytes=VMEM_LIMIT)


def _norm_mod(x, gain, sc, sh):
    ms = jnp.mean(x * x, axis=-1, keepdims=True)
    return x * lax.rsqrt(ms + EPS) * gain * (1.0 + sc) + sh


def _vec_spec(width, tiles_per_batch, per_batch):
    return pl.BlockSpec((1, 1, width), lambda i: ((i // tiles_per_batch) if per_batch else 0, 0, 0))


def _mod_body(c_ref, w_ref, b_ref, o_ref):
    cs = jax.nn.silu(c_ref[...])
    o_ref[0] = jnp.dot(cs, w_ref[0], precision=HIGHEST, preferred_element_type=F32) + b_ref[0]


def _modulation(cond, w_mod, b_mod):
    depth, _, n = w_mod.shape
    rows = cond.shape[0]
    tn = n // 4
    return pl.pallas_call(
        _mod_body,
        grid=(depth, n // tn),
        in_specs=[
            pl.BlockSpec((rows, D_MODEL), lambda l, j: (0, 0)),
            pl.BlockSpec((1, D_MODEL, tn), lambda l, j: (l, 0, j)),
            pl.BlockSpec((1, 1, tn), lambda l, j: (l, 0, j)),
        ],
        out_specs=pl.BlockSpec((1, rows, tn), lambda l, j: (l, 0, j)),
        out_shape=jax.ShapeDtypeStruct((depth, rows, n), F32),
        compiler_params=_cparams(("arbitrary", "arbitrary")),
        name="modulation",
    )(cond, w_mod, b_mod.reshape(depth, 1, n))


def _chunks(total, width):
    return [(s, min(width, total - s)) for s in range(0, total, width)]


def _resident(shape):
    return pl.BlockSpec(shape, lambda *_: (0,) * len(shape), pipeline_mode=pl.Buffered(1))


def _norm_mod_halo(seq_len, tm, x_ref, xn_ref, xp_ref, gain_ref, sc_ref, sh_ref, h_scr):
    gain, sc, sh = gain_ref[...], sc_ref[0], sh_ref[0]
    h_scr[0:tm] = _norm_mod(x_ref[...], gain, sc, sh).astype(BF16)
    if tm >= seq_len:
        h_scr[tm:tm + 2 * SUBLANES] = jnp.zeros((2 * SUBLANES, D_MODEL), BF16)
    else:
        tiles_per_seq = seq_len // tm
        pos = pl.program_id(0) % tiles_per_seq
        hn = jnp.where(pos == tiles_per_seq - 1, 0.0, _norm_mod(xn_ref[0], gain, sc, sh))
        hp = jnp.where(pos == 0, 0.0, _norm_mod(xp_ref[0], gain, sc, sh))
        h_scr[tm:tm + 2 * SUBLANES] = jnp.concatenate([hn, hp], axis=0).astype(BF16)


def _dwconv3_rows(u, cw, cb, tm, seq_len):
    rows = tm + 2 * SUBLANES
    before, after = pltpu.roll(u, 1, 0), pltpu.roll(u, rows - 1, 0)
    if tm > seq_len:
        pos = lax.broadcasted_iota(jnp.int32, u.shape, 0) % seq_len
        before = jnp.where(pos == 0, 0.0, before)
        after = jnp.where(pos == seq_len - 1, 0.0, after)
    c = before * cw[0:1] + u * cw[1:2] + after * cw[2:3] + cb
    return c[0:tm]


def _halo_specs(m, tm):
    m8 = m // SUBLANES
    r8 = tm // SUBLANES
    return [pl.BlockSpec((1, SUBLANES, D_MODEL), lambda i: (jnp.minimum((i + 1) * r8, m8 - 1), 0, 0)),
            pl.BlockSpec((1, SUBLANES, D_MODEL), lambda i: (jnp.maximum(i * r8 - 1, 0), 0, 0))]


def _hyb_in_body(seq_len, tm, tn, x_ref, xn_ref, xp_ref, gain_ref, sc_ref, sh_ref, w_ref, cw_ref, cb_ref,
                 bf_ref, rest_ref, h_scr):
    _norm_mod_halo(seq_len, tm, x_ref, xn_ref, xp_ref, gain_ref, sc_ref, sh_ref, h_scr)
    h_all = h_scr[...]
    h = h_scr[0:tm]
    w_g = HYB_QKV
    w_x = w_g + 2 * RET_V
    w_bc = w_x + SSM_D_INNER
    w_dt = w_bc + 2 * SSM_BC

    def project(wcol, width, conv_col):
        wcols = w_ref[:, wcol:wcol + width]
        if conv_col is None:
            return jnp.dot(h, wcols, preferred_element_type=F32)
        u = jnp.dot(h_all, wcols, preferred_element_type=F32)
        return jax.nn.silu(_dwconv3_rows(u, cw_ref[:, conv_col:conv_col + width],
                                         cb_ref[:, conv_col:conv_col + width], tm, seq_len))

    for col, width in _chunks(HYB_BF, tn):
        if col >= HYB_QKV:
            out = project(w_bc + col - HYB_QKV, width, SSM_D_INNER + col - HYB_QKV)
        else:
            out = project(col, width, None)
            if RET_Q <= col < 2 * RET_Q:
                out = out * (RET_DK ** -0.5)
        bf_ref[:, col:col + width] = out.astype(BF16)
    for col, width in _chunks(DT_COL, tn):
        rest_ref[:, col:col + width] = project(w_g + col, width, col - XS_COL if col >= XS_COL else None)
    dt = jnp.dot(h, w_ref[:, w_dt:w_dt + SSM_HEADS], preferred_element_type=F32)
    rest_ref[:, DT_COL:HYB_REST] = jnp.concatenate([dt, jnp.zeros((tm, HYB_REST - DT_COL - SSM_HEADS), F32)], axis=1)


def _hyb_in_proj(x, gain, sc, sh, w, conv_w, conv_b, seq_len, tm, tn):
    m = x.shape[0]
    per_batch = sc.shape[0] > 1
    tpb = seq_len // tm
    assert HYB_QKV % tn == 0 and XS_COL % tn == 0 and DT_COL % tn == 0
    x8 = x.reshape(m // SUBLANES, SUBLANES, D_MODEL)
    return pl.pallas_call(
        functools.partial(_hyb_in_body, seq_len, tm, tn),
        grid=(m // tm,),
        in_specs=[
            pl.BlockSpec((tm, D_MODEL), lambda i: (i, 0)),
            *_halo_specs(m, tm),
            pl.BlockSpec((1, D_MODEL), lambda i: (0, 0)),
            _vec_spec(D_MODEL, tpb, per_batch),
            _vec_spec(D_MODEL, tpb, per_batch),
            _resident(w.shape),
            _resident((3, SSM_XBC)),
            _resident((1, SSM_XBC)),
        ],
        out_specs=[pl.BlockSpec((tm, HYB_BF), lambda i: (i, 0)), pl.BlockSpec((tm, HYB_REST), lambda i: (i, 0))],
        out_shape=[jax.ShapeDtypeStruct((m, HYB_BF), BF16), jax.ShapeDtypeStruct((m, HYB_REST), F32)],
        scratch_shapes=[pltpu.VMEM((tm + 2 * SUBLANES, D_MODEL), BF16)],
        compiler_params=_cparams(("arbitrary",)),
        name="hybrid_in_proj",
    )(x, x8, x8, gain, sc, sh, w, conv_w, conv_b.reshape(1, SSM_XBC))


def _ffn_body(seq_len, tm, th, x_ref, xn_ref, xp_ref, gain_ref, sc_ref, sh_ref, gate_ref,
              wu_ref, cw_ref, cb_ref, wd_ref, o_ref, h_scr, act_scr):
    _norm_mod_halo(seq_len, tm, x_ref, xn_ref, xp_ref, gain_ref, sc_ref, sh_ref, h_scr)
    h = h_scr[...]

    def conv_up(col, width):
        u = jnp.dot(h, wu_ref[:, col:col + width], preferred_element_type=F32)
        return _dwconv3_rows(u, cw_ref[:, col:col + width], cb_ref[:, col:col + width], tm, seq_len)

    for col, width in _chunks(FFN_HIDDEN, th):
        act = jax.nn.silu(conv_up(col, width)) * conv_up(FFN_HIDDEN + col, width)
        act_scr[:, col:col + width] = act.astype(BF16)
    out = jnp.dot(act_scr[...], wd_ref[...], preferred_element_type=F32)
    o_ref[...] = x_ref[...] + gate_ref[0] * out


def _conv_ffn(x, gain, sc, sh, gate, w_up, conv_w, conv_b, w_down, seq_len, tm, th):
    m = x.shape[0]
    per_batch = sc.shape[0] > 1
    tpb = seq_len // tm
    x8 = x.reshape(m // SUBLANES, SUBLANES, D_MODEL)
    conv_b = conv_b.reshape(1, 2 * FFN_HIDDEN)
    vec = _vec_spec(D_MODEL, tpb, per_batch)
    return pl.pallas_call(
        functools.partial(_ffn_body, seq_len, tm, th),
        grid=(m // tm,),
        in_specs=[
            pl.BlockSpec((tm, D_MODEL), lambda i: (i, 0)),
            *_halo_specs(m, tm),
            pl.BlockSpec((1, D_MODEL), lambda i: (0, 0)),
            vec, vec, vec,
            _resident((D_MODEL, 2 * FFN_HIDDEN)),
            _resident((3, 2 * FFN_HIDDEN)),
            _resident((1, 2 * FFN_HIDDEN)),
            _resident((FFN_HIDDEN, D_MODEL)),
        ],
        out_specs=pl.BlockSpec((tm, D_MODEL), lambda i: (i, 0)),
        out_shape=jax.ShapeDtypeStruct((m, D_MODEL), F32),
        scratch_shapes=[pltpu.VMEM((tm + 2 * SUBLANES, D_MODEL), BF16), pltpu.VMEM((tm, FFN_HIDDEN), BF16)],
        compiler_params=_cparams(("arbitrary",)),
        name="conv_ffn",
    )(x, x8, x8, gain, sc, sh, gate, w_up, conv_w, conv_b, w_down)


SSM_GW = SSM_RATIO * SSM_HEADDIM
FWD_CHUNKS_PER_STEP = 4


def _bcol(x, h):
    return jnp.broadcast_to(x[:, h:h + 1], (x.shape[0], LANES))


def _pair_sel(left, a, h):
    return jnp.where(left, _bcol(a, h), _bcol(a, h + 1))


def _expand_heads(a, spread_ref):
    hi = a.astype(BF16)
    lo = (a - hi.astype(F32)).astype(BF16)
    return (jnp.dot(hi, spread_ref[...], preferred_element_type=F32)
            + jnp.dot(lo, spread_ref[...], preferred_element_type=F32))


def _ssm_decay_row(left, efull, g):
    return jnp.concatenate([_pair_sel(left, efull, g * SSM_RATIO + 2 * m) for m in range(SSM_RATIO // 2)], axis=1)


def _load_states(s_ret, s_ssm, sret0_ref, sssm0_ref):
    if sret0_ref is None:
        s_ret[...] = jnp.zeros_like(s_ret)
        s_ssm[...] = jnp.zeros_like(s_ssm)
    else:
        s_ret[...] = sret0_ref[0]
        for g in range(SSM_GROUPS):
            s_ssm[g] = jnp.concatenate([sssm0_ref[0, g * SSM_RATIO + r] for r in range(SSM_RATIO)], axis=1)


def _store_states(s_ret, s_ssm, sret_ref, sssm_ref):
    sret_ref[0, 0] = s_ret[...]
    for g in range(SSM_GROUPS):
        for r in range(SSM_RATIO):
            sssm_ref[0, 0, g * SSM_RATIO + r] = s_ssm[g][:, r * SSM_HEADDIM:(r + 1) * SSM_HEADDIM]


def _ssm_log_decay(dt_raw, dtb, alog, lane):
    dt = jax.nn.softplus(dt_raw + dtb)
    return dt, jnp.where(lane < SSM_HEADS, dt * (-jnp.exp(alog)), 0.0)


def _bwd_state_body(has_init, k_ref, v_ref, xs_ref, bm_ref, dt_ref, dtb_ref, alog_ref, spread_ref, logdec_ref,
                    *rest):
    sret0_ref = sssm0_ref = None
    if has_init:
        sret0_ref, sssm0_ref = rest[0], rest[1]
        rest = rest[2:]
    sret_in_ref, sssm_in_ref, sret_fin_ref, sssm_fin_ref, s_ret, s_ssm, wcol = rest
    i = pl.program_id(1)
    n = pl.num_programs(1)
    ii = lax.broadcasted_iota(jnp.int32, (CHUNK, CHUNK), 0)
    jj = lax.broadcasted_iota(jnp.int32, (CHUNK, CHUNK), 1)
    lane = lax.broadcasted_iota(jnp.int32, (1, LANES), 1)
    left = lane < SSM_HEADDIM

    @pl.when(i == 0)
    def _():
        _load_states(s_ret, s_ssm, sret0_ref, sssm0_ref)
        for h in range(RET_HEADS):
            wcol[h] = jnp.exp(ii.astype(F32) * logdec_ref[1, h])

    cps = sret_in_ref.shape[1]
    full = jnp.full((1, LANES), float(CHUNK), F32)
    prep = []
    for c in range(cps):
        rows = slice(c * CHUNK, (c + 1) * CHUNK)
        dt, la = _ssm_log_decay(dt_ref[rows], dtb_ref[...], alog_ref[...], lane)
        rc = jnp.dot((jj >= ii).astype(F32), la, precision=HIGHEST, preferred_element_type=F32)
        last = rc[0:1]
        prep.append((_expand_heads(jnp.exp(last - rc) * dt, spread_ref), jnp.exp(last)))

    for c in reversed(range(cps)):
        rows = slice(c * CHUNK, (c + 1) * CHUNK)
        sret_in_ref[0, c] = s_ret[...].astype(BF16)
        sssm_in_ref[0, c] = s_ssm[...].astype(BF16)
        for h in range(RET_HEADS):
            kh = k_ref[rows, h * RET_DK:(h + 1) * RET_DK]
            vw = (v_ref[rows, h * RET_DV:(h + 1) * RET_DV].astype(F32) * wcol[h]).astype(BF16)
            upd = lax.dot_general(kh, vw, (((0,), (0,)), ((), ())), preferred_element_type=F32)
            s_ret[h] = s_ret[h] * jnp.exp(full * logdec_ref[1, h]) + upd
        wx, efull = prep[c]
        for g in range(SSM_GROUPS):
            vw = (xs_ref[rows, g * SSM_GW:(g + 1) * SSM_GW] * wx[:, g * SSM_GW:(g + 1) * SSM_GW]).astype(BF16)
            bg = bm_ref[rows, g * SSM_STATE:(g + 1) * SSM_STATE]
            upd = lax.dot_general(bg, vw, (((0,), (0,)), ((), ())), preferred_element_type=F32)
            s_ssm[g] = s_ssm[g] * _ssm_decay_row(left, efull, g) + upd

    @pl.when(i == n - 1)
    def _():
        _store_states(s_ret, s_ssm, sret_fin_ref, sssm_fin_ref)


def _hyb_fwd_body(has_init, qkv_ref, xs_ref, bc_ref, dt_ref, sret_in_ref, sssm_in_ref, dtb_ref, alog_ref, dsk_ref,
                  spread_ref, logdec_ref, *rest):
    sret0_ref = sssm0_ref = None
    if has_init:
        sret0_ref, sssm0_ref = rest[0], rest[1]
        rest = rest[2:]
    y_ref, sret_fin_ref, sssm_fin_ref, s_ret, s_ssm, dcomb, ein0, ein1, wcol = rest[2:]
    i = pl.program_id(1)
    n = pl.num_programs(1)
    ii = lax.broadcasted_iota(jnp.int32, (CHUNK, CHUNK), 0)
    jj = lax.broadcasted_iota(jnp.int32, (CHUNK, CHUNK), 1)
    lane = lax.broadcasted_iota(jnp.int32, (1, LANES), 1)
    left = lane < SSM_HEADDIM
    lower = jj <= ii

    @pl.when(i == 0)
    def _():
        _load_states(s_ret, s_ssm, sret0_ref, sssm0_ref)
        fi, fj = ii.astype(F32), jj.astype(F32)
        for h in range(RET_HEADS):
            la0, la1 = logdec_ref[0, h], logdec_ref[1, h]
            dec = jnp.exp(jnp.where(lower, (fi - fj) * la0, (fj - fi) * la1))
            dcomb[h] = jnp.where(ii == jj, 2.0, dec)
            ein0[h] = jnp.exp((fi + 1.0) * la0)
            ein1[h] = jnp.exp((CHUNK - fi) * la1)
            wcol[h] = jnp.exp((CHUNK - 1.0 - fi) * la0)

    for c in range(sret_in_ref.shape[1]):
        _fwd_chunk(c, qkv_ref, xs_ref, bc_ref, dt_ref, sret_in_ref, sssm_in_ref, dtb_ref, alog_ref, dsk_ref,
                   spread_ref, logdec_ref, y_ref, s_ret, s_ssm, dcomb, ein0, ein1, wcol)

    @pl.when(i == n - 1)
    def _():
        _store_states(s_ret, s_ssm, sret_fin_ref, sssm_fin_ref)


def _fwd_chunk(c, qkv_ref, xs_ref, bc_ref, dt_ref, sret_in_ref, sssm_in_ref, dtb_ref, alog_ref, dsk_ref,
               spread_ref, logdec_ref, y_ref, s_ret, s_ssm, dcomb, ein0, ein1, wcol):
    rows = pl.ds(c * CHUNK, CHUNK)
    qkv_ref, xs_ref, bc_ref, dt_ref, y_ref = (r.at[rows] for r in (qkv_ref, xs_ref, bc_ref, dt_ref, y_ref))
    ii = lax.broadcasted_iota(jnp.int32, (CHUNK, CHUNK), 0)
    jj = lax.broadcasted_iota(jnp.int32, (CHUNK, CHUNK), 1)
    lane = lax.broadcasted_iota(jnp.int32, (1, LANES), 1)
    left = lane < SSM_HEADDIM
    lower = jj <= ii

    def ret_head(h):
        return (qkv_ref[:, h * RET_DK:(h + 1) * RET_DK],
                qkv_ref[:, RET_Q + h * RET_DK:RET_Q + (h + 1) * RET_DK],
                qkv_ref[:, 2 * RET_Q + h * RET_DV:2 * RET_Q + (h + 1) * RET_DV])

    def ssm_group(g):
        return (bc_ref[:, g * SSM_STATE:(g + 1) * SSM_STATE],
                bc_ref[:, SSM_BC + g * SSM_STATE:SSM_BC + (g + 1) * SSM_STATE])

    dt_raw = dt_ref[...]
    dt0, la0 = _ssm_log_decay(dt_raw, dtb_ref[0:1], alog_ref[0:1], lane)
    dt1, la1 = _ssm_log_decay(dt_raw, dtb_ref[1:2], alog_ref[1:2], lane)
    c0 = jnp.dot(lower.astype(F32), la0, precision=HIGHEST, preferred_element_type=F32)
    c1 = jnp.dot((jj >= ii).astype(F32), la1, precision=HIGHEST, preferred_element_type=F32)

    ret_scores, ret_inter = [], []
    for h in range(RET_HEADS):
        qh, kh, _ = ret_head(h)
        ret_scores.append(lax.dot_general(qh, kh, (((1,), (1,)), ((), ())), preferred_element_type=F32))
        states = jnp.concatenate([s_ret[h].astype(BF16), sret_in_ref[0, c, h]], axis=1)
        ret_inter.append(jnp.dot(qh, states, preferred_element_type=F32))
    ssm_scores, ssm_inter = [], []
    for g in range(SSM_GROUPS):
        bg, cg = ssm_group(g)
        ssm_scores.append(lax.dot_general(cg, bg, (((1,), (1,)), ((), ())), preferred_element_type=F32))
        states = jnp.concatenate([s_ssm[g].astype(BF16), sssm_in_ref[0, c, g]], axis=1)
        ssm_inter.append(jnp.dot(cg, states, preferred_element_type=F32))

    full = jnp.full((1, LANES), float(CHUNK), F32)
    for h in range(RET_HEADS):
        _, kh, vh = ret_head(h)
        y = jnp.dot((ret_scores[h] * dcomb[h]).astype(BF16), vh, preferred_element_type=F32)
        inter = ret_inter[h]
        y_ref[:, h * RET_DV:(h + 1) * RET_DV] = y + inter[:, 0:RET_DV] * ein0[h] + inter[:, RET_DV:] * ein1[h]
        vw = (vh.astype(F32) * wcol[h]).astype(BF16)
        upd = lax.dot_general(kh, vw, (((0,), (0,)), ((), ())), preferred_element_type=F32)
        s_ret[h] = s_ret[h] * jnp.exp(full * logdec_ref[0, h]) + upd

    c0kt, c1kt = (c0 - jnp.log(dt0)).T, (c1 - jnp.log(dt1)).T
    dst = (dt0 + dt1).T
    last0 = c0[CHUNK - 1:CHUNK]
    w0x = _expand_heads(jnp.exp(last0 - c0) * dt0, spread_ref)
    efull0 = jnp.exp(last0)
    diag = jj == ii
    for g in range(SSM_GROUPS):
        bg, _ = ssm_group(g)
        s = ssm_scores[g]
        inter = ssm_inter[g]
        vws = []
        for m in range(SSM_RATIO // 2):
            ha = g * SSM_RATIO + 2 * m
            ps, c0bs, c1bs = [], [], []
            for h in (ha, ha + 1):
                c0b, c1b = _bcol(c0, h), _bcol(c1, h)
                arg = jnp.where(lower, c0b - c0kt[h:h + 1], c1b - c1kt[h:h + 1])
                ps.append((s * jnp.where(diag, dst[h:h + 1], jnp.exp(arg))).astype(BF16))
                c0bs.append(c0b)
                c1bs.append(c1b)
            col = (g * 2 + m) * LANES
            xs_pair = xs_ref[:, col:col + LANES]
            vals = jnp.concatenate([jnp.where(left, xs_pair, 0.0), jnp.where(left, 0.0, xs_pair)], axis=0)
            y = jnp.dot(jnp.concatenate(ps, axis=1), vals.astype(BF16), preferred_element_type=F32)
            y = y + inter[:, m * LANES:(m + 1) * LANES] * jnp.exp(jnp.where(left, c0bs[0], c0bs[1]))
            y = y + (inter[:, SSM_GW + m * LANES:SSM_GW + (m + 1) * LANES]
                     * jnp.exp(jnp.where(left, c1bs[0], c1bs[1])))
            y_ref[:, RET_V + col:RET_V + col + LANES] = y + dsk_ref[:, col:col + LANES] * xs_pair
            vws.append((xs_pair * w0x[:, col:col + LANES]).astype(BF16))
        upd = lax.dot_general(bg, jnp.concatenate(vws, axis=1), (((0,), (0,)), ((), ())),
                              preferred_element_type=F32)
        s_ssm[g] = s_ssm[g] * _ssm_decay_row(left, efull0, g) + upd


def _hybrid_mixer(qkv, rest, dt_bias, a_log, log_decay, d_skip, s_ret0, s_ssm0, batch, seq_len):
    m = qkv.shape[0]
    n = seq_len // CHUNK
    has_init = s_ret0 is not None
    dt_blk = DT_COL // LANES
    pad = lambda v: jnp.pad(v.astype(F32), ((0, 0), (0, LANES - v.shape[1])))
    dtb, alog = pad(dt_bias), pad(a_log)
    logdec = log_decay.astype(F32)
    spread = (lax.broadcasted_iota(jnp.int32, (LANES, SSM_D_INNER), 0)
              == lax.broadcasted_iota(jnp.int32, (LANES, SSM_D_INNER), 1) // SSM_HEADDIM).astype(BF16)
    spread_spec = pl.BlockSpec((LANES, SSM_D_INNER), lambda b, i: (0, 0))
    dsk = jnp.repeat((d_skip[0] + d_skip[1]).astype(F32), SSM_HEADDIM).reshape(1, SSM_D_INNER)
    smem = pl.BlockSpec(memory_space=pltpu.SMEM)
    ret_spec = pl.BlockSpec((1, RET_HEADS, RET_DK, RET_DV), lambda b, i: (b, 0, 0, 0))
    ssm_spec = pl.BlockSpec((1, SSM_HEADS, SSM_STATE, SSM_HEADDIM), lambda b, i: (b, 0, 0, 0))
    ret_shape = jax.ShapeDtypeStruct((batch, 2, RET_HEADS, RET_DK, RET_DV), F32)
    ssm_shape = jax.ShapeDtypeStruct((batch, 2, SSM_HEADS, SSM_STATE, SSM_HEADDIM), F32)
    fin_specs = lambda d: [pl.BlockSpec((1, 1, RET_HEADS, RET_DK, RET_DV), lambda b, i: (b, d, 0, 0, 0)),
                           pl.BlockSpec((1, 1, SSM_HEADS, SSM_STATE, SSM_HEADDIM), lambda b, i: (b, d, 0, 0, 0))]
    ret_scr = pltpu.VMEM((RET_HEADS, RET_DK, RET_DV), F32)
    ssm_scr = pltpu.VMEM((SSM_GROUPS, SSM_STATE, SSM_GW), F32)
    const_scr = pltpu.VMEM((RET_HEADS, CHUNK, CHUNK), F32)

    cps = next(c for c in (8, 4, 2, 1) if n % c == 0)
    rsteps = n // cps
    rrows = cps * CHUNK

    def rev(b, i):
        return b * rsteps + rsteps - 1 - i

    in_specs = [
        pl.BlockSpec((rrows, RET_Q), lambda b, i: (rev(b, i), 1)),
        pl.BlockSpec((rrows, RET_V), lambda b, i: (rev(b, i), 1)),
        pl.BlockSpec((rrows, SSM_D_INNER), lambda b, i: (rev(b, i), XS_COL // SSM_D_INNER)),
        pl.BlockSpec((rrows, SSM_BC), lambda b, i: (rev(b, i), HYB_QKV // SSM_BC)),
        pl.BlockSpec((rrows, LANES), lambda b, i: (rev(b, i), dt_blk)),
        pl.BlockSpec((1, LANES), lambda b, i: (0, 0)),
        pl.BlockSpec((1, LANES), lambda b, i: (0, 0)),
        spread_spec,
        smem,
    ]
    args = [qkv, qkv, rest, qkv, rest, dtb[1:2], alog[1:2], spread, logdec]
    if has_init:
        in_specs += [ret_spec, ssm_spec]
        args += [s_ret0[:, 1], s_ssm0[:, 1]]
    sret_in, sssm_in, sret_fin, sssm_fin = pl.pallas_call(
        functools.partial(_bwd_state_body, has_init),
        grid=(batch, rsteps),
        in_specs=in_specs,
        out_specs=[
            pl.BlockSpec((1, cps, RET_HEADS, RET_DK, RET_DV), lambda b, i: (b, rsteps - 1 - i, 0, 0, 0)),
            pl.BlockSpec((1, cps, SSM_GROUPS, SSM_STATE, SSM_GW), lambda b, i: (b, rsteps - 1 - i, 0, 0, 0)),
            *fin_specs(1),
        ],
        out_shape=[
            jax.ShapeDtypeStruct((batch, n, RET_HEADS, RET_DK, RET_DV), BF16),
            jax.ShapeDtypeStruct((batch, n, SSM_GROUPS, SSM_STATE, SSM_GW), BF16),
            ret_shape, ssm_shape,
        ],
        scratch_shapes=[ret_scr, ssm_scr, const_scr],
        compiler_params=_cparams(("arbitrary", "arbitrary")),
        name="hybrid_reverse_states",
    )(*args)

    fcps = next(c for c in (FWD_CHUNKS_PER_STEP, 2, 1) if n % c == 0)
    fsteps = n // fcps
    frows = fcps * CHUNK

    def fwd(b, i):
        return b * fsteps + i

    in_specs = [
        pl.BlockSpec((frows, HYB_QKV), lambda b, i: (fwd(b, i), 0)),
        pl.BlockSpec((frows, SSM_D_INNER), lambda b, i: (fwd(b, i), XS_COL // SSM_D_INNER)),
        pl.BlockSpec((frows, 2 * SSM_BC), lambda b, i: (fwd(b, i), HYB_QKV // (2 * SSM_BC))),
        pl.BlockSpec((frows, LANES), lambda b, i: (fwd(b, i), dt_blk)),
        pl.BlockSpec((1, fcps, RET_HEADS, RET_DK, RET_DV), lambda b, i: (b, i, 0, 0, 0)),
        pl.BlockSpec((1, fcps, SSM_GROUPS, SSM_STATE, SSM_GW), lambda b, i: (b, i, 0, 0, 0)),
        pl.BlockSpec((2, LANES), lambda b, i: (0, 0)),
        pl.BlockSpec((2, LANES), lambda b, i: (0, 0)),
        pl.BlockSpec((1, SSM_D_INNER), lambda b, i: (0, 0)),
        spread_spec,
        smem,
    ]
    args = [qkv, rest, qkv, rest, sret_in, sssm_in, dtb, alog, dsk, spread, logdec]
    if has_init:
        in_specs += [ret_spec, ssm_spec]
        args += [s_ret0[:, 0], s_ssm0[:, 0]]
    in_specs += [pl.BlockSpec(memory_space=pl.ANY)] * 2
    args += [sret_fin, sssm_fin]
    y, sret_fin, sssm_fin = pl.pallas_call(
        functools.partial(_hyb_fwd_body, has_init),
        grid=(batch, fsteps),
        in_specs=in_specs,
        out_specs=[pl.BlockSpec((frows, HYB_MIX), lambda b, i: (fwd(b, i), 0)), *fin_specs(0)],
        out_shape=[jax.ShapeDtypeStruct((m, HYB_MIX), F32), ret_shape, ssm_shape],
        input_output_aliases={len(args) - 2: 1, len(args) - 1: 2},
        scratch_shapes=[ret_scr, ssm_scr, const_scr, const_scr, const_scr, const_scr],
        compiler_params=_cparams(("arbitrary", "arbitrary")),
        name="hybrid_forward_mix",
    )(*args)
    return y, sret_fin, sssm_fin


def _hyb_out_body(y_ref, g_ref, z_ref, x_ref, gate_ref, gn_ref, ng_ref, w_ref, o_ref):
    y = y_ref[...]
    parts = []
    for h in range(RET_HEADS):
        yh = y[:, h * RET_DV:(h + 1) * RET_DV]
        mu = jnp.mean(yh, axis=-1, keepdims=True)
        var = jnp.mean(jnp.square(yh - mu), axis=-1, keepdims=True)
        parts.append((yh - mu) * lax.rsqrt(var + EPS))
    y_ret = jax.nn.silu(g_ref[...]) * (jnp.concatenate(parts, axis=1) * gn_ref[...])
    yz = y[:, RET_V:] * jax.nn.silu(z_ref[...])
    y_ssm = yz * lax.rsqrt(jnp.mean(yz * yz, axis=-1, keepdims=True) + EPS) * ng_ref[...]
    out = jnp.dot(y_ret.astype(BF16), w_ref[0:RET_V], preferred_element_type=F32)
    out = out + jnp.dot(y_ssm.astype(BF16), w_ref[RET_V:HYB_MIX], preferred_element_type=F32)
    o_ref[...] = x_ref[...] + gate_ref[0] * out


def _hyb_out(y, rest, x, gate, gn_gain, norm_gain, w_out, seq_len, tm):
    m = x.shape[0]
    per_batch = gate.shape[0] > 1
    tpb = seq_len // tm
    return pl.pallas_call(
        _hyb_out_body,
        grid=(m // tm,),
        in_specs=[
            pl.BlockSpec((tm, HYB_MIX), lambda i: (i, 0)),
            pl.BlockSpec((tm, RET_V), lambda i: (i, 0)),
            pl.BlockSpec((tm, SSM_D_INNER), lambda i: (i, 1)),
            pl.BlockSpec((tm, D_MODEL), lambda i: (i, 0)),
            _vec_spec(D_MODEL, tpb, per_batch),
            pl.BlockSpec((1, RET_V), lambda i: (0, 0)),
            pl.BlockSpec((1, SSM_D_INNER), lambda i: (0, 0)),
            pl.BlockSpec((HYB_MIX, D_MODEL), lambda i: (0, 0)),
        ],
        out_specs=pl.BlockSpec((tm, D_MODEL), lambda i: (i, 0)),
        out_shape=jax.ShapeDtypeStruct((m, D_MODEL), F32),
        compiler_params=_cparams(("arbitrary",)),
        name="hybrid_out_proj",
    )(y, rest, rest, x, gate, gn_gain, norm_gain, w_out)


def _head_rms(t, seg_ref):
    outs = []
    for c in range(t.shape[1] // LANES):
        tc = t[:, c * LANES:(c + 1) * LANES]
        sq = tc * tc
        hi = sq.astype(BF16)
        lo = (sq - hi.astype(F32)).astype(BF16)
        ssum = (jnp.dot(hi, seg_ref[...], preferred_element_type=F32)
                + jnp.dot(lo, seg_ref[...], preferred_element_type=F32))
        outs.append(tc * lax.rsqrt(ssum * (1.0 / ATT_HD) + EPS))
    return outs


def _rope(tc, cos, sin, lane):
    fwd = pltpu.roll(tc, LANES - ROT_FREQS, 1)
    bwd = pltpu.roll(tc, ROT_FREQS, 1)
    return tc * cos + jnp.where(lane % (2 * ROT_FREQS) < ROT_FREQS, fwd, bwd) * sin


def _qkv_body(has_rope, x_ref, gain_ref, sc_ref, sh_ref, w_ref, seg_ref, qg_ref, kg_ref, *rest):
    if has_rope:
        cos_ref, sin_ref, q_ref, k_ref, v_ref = rest
    else:
        q_ref, k_ref, v_ref, kf_ref, vf_ref = rest
    h = _norm_mod(x_ref[...], gain_ref[...], sc_ref[0], sh_ref[0]).astype(BF16)
    lane = lax.broadcasted_iota(jnp.int32, (1, LANES), 1)

    def project(col):
        t = jnp.dot(h, w_ref[:, col:col + MXU_COLS], preferred_element_type=F32)
        return [t[:, c * LANES:(c + 1) * LANES] for c in range(MXU_COLS // LANES)]

    def normed(tiles, gain_ref_):
        tiles = [t * gain_ref_[...] for t in _head_rms(jnp.concatenate(tiles, axis=1), seg_ref)]
        return tiles, ([_rope(t, cos_ref[...], sin_ref[...], lane) for t in tiles] if has_rope else tiles)

    def emit_q(col, tiles):
        _, q = normed(tiles, qg_ref)
        for c, t in enumerate(q):
            row = col + c * LANES
            q_ref[0, row:row + LANES, :] = (t * Q_SCALE).T.astype(BF16)

    def emit_k(col, tiles):
        k, kr = normed(tiles, kg_ref)
        for c in range(MXU_COLS // LANES):
            lo = col + c * LANES
            if not has_rope:
                kf_ref[:, lo:lo + LANES] = k[c]
            kc = kr[c].astype(BF16)
            for a in range(LANES // ATT_HD):
                k_ref[0, lo // ATT_HD + a] = kc[:, a * ATT_HD:(a + 1) * ATT_HD]

    def emit_v(col, tiles):
        for c in range(MXU_COLS // LANES):
            lo = col + c * LANES
            if not has_rope:
                vf_ref[:, lo:lo + LANES] = tiles[c]
            vc = tiles[c].T.astype(BF16)
            for a in range(LANES // ATT_HD):
                v_ref[0, lo // ATT_HD + a] = vc[a * ATT_HD:(a + 1) * ATT_HD]

    work = ([(emit_q, col, col) for col in range(0, ATT_Q, MXU_COLS)]
            + [(emit_k, col, ATT_Q + col) for col in range(0, ATT_KV, MXU_COLS)]
            + [(emit_v, col, ATT_Q + ATT_KV + col) for col in range(0, ATT_KV, MXU_COLS)])
    tiles = project(work[0][2])
    for n, (emit, col, _) in enumerate(work):
        ahead = project(work[n + 1][2]) if n + 1 < len(work) else None
        emit(col, tiles)
        tiles = ahead


def _qkv_proj(x, gain, sc, sh, w_qkv, q_gain, k_gain, rope, seq_len, tm):
    m = x.shape[0]
    per_batch = sc.shape[0] > 1
    tpb = seq_len // tm
    has_rope = rope is not None
    seg = (lax.broadcasted_iota(jnp.int32, (LANES, LANES), 0) // ATT_HD
           == lax.broadcasted_iota(jnp.int32, (LANES, LANES), 1) // ATT_HD).astype(BF16)
    tile2 = lambda v: jnp.tile(v.astype(F32), LANES // ATT_HD).reshape(1, LANES)
    in_specs = [
        pl.BlockSpec((tm, D_MODEL), lambda i: (i, 0)),
        pl.BlockSpec((1, D_MODEL), lambda i: (0, 0)),
        _vec_spec(D_MODEL, tpb, per_batch),
        _vec_spec(D_MODEL, tpb, per_batch),
        pl.BlockSpec((D_MODEL, ATT_Q + 2 * ATT_KV), lambda i: (0, 0)),
        pl.BlockSpec((LANES, LANES), lambda i: (0, 0)),
        pl.BlockSpec((1, LANES), lambda i: (0, 0)),
        pl.BlockSpec((1, LANES), lambda i: (0, 0)),
    ]
    args = [x, gain, sc, sh, w_qkv, seg, tile2(q_gain), tile2(k_gain)]
    batch = m // seq_len
    out_specs = [pl.BlockSpec((1, ATT_Q, tm), lambda i: (i // tpb, 0, i % tpb)),
                 pl.BlockSpec((1, ATT_KV_HEADS, tm, ATT_HD), lambda i: (i // tpb, 0, i % tpb, 0)),
                 pl.BlockSpec((1, ATT_KV_HEADS, ATT_HD, tm), lambda i: (i // tpb, 0, 0, i % tpb))]
    out_shape = [jax.ShapeDtypeStruct((batch, ATT_Q, seq_len), BF16),
                 jax.ShapeDtypeStruct((batch, ATT_KV_HEADS, seq_len, ATT_HD), BF16),
                 jax.ShapeDtypeStruct((batch, ATT_KV_HEADS, ATT_HD, seq_len), BF16)]
    if has_rope:
        in_specs += [pl.BlockSpec((tm, LANES), lambda i: (i % tpb, 0))] * 2
        args += list(rope)
    else:
        out_specs += [pl.BlockSpec((tm, ATT_KV), lambda i: (i, 0))] * 2
        out_shape += [jax.ShapeDtypeStruct((m, ATT_KV), F32)] * 2
    return pl.pallas_call(
        functools.partial(_qkv_body, has_rope),
        grid=(m // tm,),
        in_specs=in_specs,
        out_specs=out_specs,
        out_shape=out_shape,
        compiler_params=_cparams(("arbitrary",)),
        name="attn_qkv_proj",
    )(*args)


def _rope_tables(seq_len):
    rows = seq_len // GRID_W
    row = jnp.repeat(jnp.arange(rows, dtype=F32), GRID_W)
    col = jnp.tile(jnp.arange(GRID_W, dtype=F32), rows)
    inv = ROPE_BASE ** (-jnp.arange(ROT_FREQS, dtype=F32) / ROT_FREQS)
    ar, ac = row[:, None] * inv, col[:, None] * inv
    cos = jnp.concatenate([jnp.cos(ar), jnp.cos(ar), jnp.cos(ac), jnp.cos(ac)], axis=1)
    sin = jnp.concatenate([-jnp.sin(ar), jnp.sin(ar), -jnp.sin(ac), jnp.sin(ac)], axis=1)
    return jnp.tile(cos, (1, LANES // ATT_HD)), jnp.tile(sin, (1, LANES // ATT_HD))


V_ROWS = ATT_HD + 16
ATT_SLOTS = 4


def _attn_body(tq, tk, qt_ref, k_ref, vt_ref, x_ref, gate_ref, wo_ref, o_ref, acc_scr, s_scr, ot_scr):
    nkv = k_ref.shape[2] // tk
    ones_rows = jnp.ones((V_ROWS - ATT_HD, tk), BF16)
    def scores(g, t, slot):
        kt = k_ref[0, g, pl.ds(pl.multiple_of(t * tk, tk), tk), :]
        tops = []
        for r in range(ATT_RATIO):
            hd = g * ATT_RATIO + r
            s = jnp.dot(kt, qt_ref[0, hd * ATT_HD:(hd + 1) * ATT_HD, :], preferred_element_type=F32)
            s_scr[slot, r] = s
            tops.append(jnp.max(s, axis=0, keepdims=True))
        return tuple(tops)

    def softmax_pv(g, t, slot, ms, tops):
        vt = jnp.concatenate([vt_ref[0, g, :, pl.ds(pl.multiple_of(t * tk, tk), tk)], ones_rows], axis=0)
        new_ms = [jnp.maximum(ms[r], tops[r]) for r in range(ATT_RATIO)]
        ps = [jnp.exp2(s_scr[slot, r] - new_ms[r]).astype(BF16) for r in range(ATT_RATIO)]
        for r in range(ATT_RATIO):
            acc_scr[r] = (jnp.exp2(ms[r] - new_ms[r]) * acc_scr[r]
                          + jnp.dot(vt, ps[r], preferred_element_type=F32))
        return tuple(new_ms)

    first_tiles = range(min(2, nkv))
    quads = max(nkv - 2, 0) // ATT_SLOTS
    off = 0
    tops = {t: scores(0, t, t) for t in first_tiles}
    for g in range(ATT_KV_HEADS):
        acc_scr[...] = jnp.zeros_like(acc_scr)
        slot_of = lambda t, off=off: (t + off) % ATT_SLOTS

        def quad_step(i, carry, g=g, slot_of=slot_of):
            ms, top0, top1 = carry
            t = i * ATT_SLOTS
            top2 = scores(g, t + 2, slot_of(2))
            top3 = scores(g, t + 3, slot_of(3))
            ms = softmax_pv(g, t, slot_of(0), ms, top0)
            ms = softmax_pv(g, t + 1, slot_of(1), ms, top1)
            top0 = scores(g, t + 4, slot_of(0))
            top1 = scores(g, t + 5, slot_of(1))
            ms = softmax_pv(g, t + 2, slot_of(2), ms, top2)
            return softmax_pv(g, t + 3, slot_of(3), ms, top3), top0, top1

        ms = tuple(jnp.full((1, tq), NEG_BIG, F32) for _ in range(ATT_RATIO))
        if quads:
            ms, top0, top1 = lax.fori_loop(0, quads, quad_step, (ms, tops[0], tops[1]))
            tops = {quads * ATT_SLOTS: top0, quads * ATT_SLOTS + 1: top1}
        next_off, next_tops = (nkv + off) % ATT_SLOTS, {}
        for t in range(quads * ATT_SLOTS, nkv):
            if t % 2 == 0:
                for ahead in (t + 2, t + 3):
                    if ahead < nkv:
                        tops[ahead] = scores(g, ahead, slot_of(ahead))
            if t == max(nkv - 2, quads * ATT_SLOTS) and g + 1 < ATT_KV_HEADS:
                next_tops = {n: scores(g + 1, n, (n + next_off) % ATT_SLOTS) for n in first_tiles}
            ms = softmax_pv(g, t, slot_of(t), ms, tops.pop(t))
        off, tops = next_off, next_tops
        for r in range(ATT_RATIO):
            hd = g * ATT_RATIO + r
            a = acc_scr[r]
            ot_scr[hd * ATT_HD:(hd + 1) * ATT_HD, :] = (a[0:ATT_HD] / a[ATT_HD:ATT_HD + 1]).astype(BF16)
    out = lax.dot_general(ot_scr[...], wo_ref[...], (((0,), (0,)), ((), ())), preferred_element_type=F32)
    o_ref[0] = x_ref[0] + gate_ref[0] * out


def _attention(qt, k4, vt, x, gate, w_o, batch, seq_len, tq, tk):
    per_batch = gate.shape[0] > 1
    nkeys = k4.shape[2]
    x3 = x.reshape(batch, seq_len, D_MODEL)
    out = pl.pallas_call(
        functools.partial(_attn_body, tq, tk),
        grid=(batch, seq_len // tq),
        in_specs=[
            pl.BlockSpec((1, ATT_Q, tq), lambda b, i: (b, 0, i)),
            pl.BlockSpec((1, ATT_KV_HEADS, nkeys, ATT_HD), lambda b, i: (b, 0, 0, 0)),
            pl.BlockSpec((1, ATT_KV_HEADS, ATT_HD, nkeys), lambda b, i: (b, 0, 0, 0)),
            pl.BlockSpec((1, tq, D_MODEL), lambda b, i: (b, i, 0)),
            pl.BlockSpec((1, 1, D_MODEL), lambda b, i: (b if per_batch else 0, 0, 0)),
            pl.BlockSpec((ATT_Q, D_MODEL), lambda b, i: (0, 0)),
        ],
        out_specs=pl.BlockSpec((1, tq, D_MODEL), lambda b, i: (b, i, 0)),
        out_shape=jax.ShapeDtypeStruct((batch, seq_len, D_MODEL), F32),
        scratch_shapes=[pltpu.VMEM((ATT_RATIO, V_ROWS, tq), F32), pltpu.VMEM((ATT_SLOTS, ATT_RATIO, tk, tq), F32),
                        pltpu.VMEM((ATT_Q, tq), BF16)],
        compiler_params=_cparams(("arbitrary", "arbitrary")),
        name="attention",
    )(qt, k4, vt, x3, gate, w_o)
    return out.reshape(batch * seq_len, D_MODEL)


def _keys_values(k4, vt, cache_k, cache_v):
    k4 = jnp.concatenate([jnp.swapaxes(cache_k, 1, 2).astype(BF16), k4], axis=2)
    vt = jnp.concatenate([jnp.transpose(cache_v, (0, 2, 3, 1)).astype(BF16), vt], axis=3)
    return k4, vt


def _trunk(x, mods, caches, P, batch, seq_len, tm, tq, tk):
    rope = None if caches is None else _rope_tables(seq_len)
    outs = {}
    sh1, sc1, g1, sh2, sc2, g2 = mods[0]
    qkv, rest = _hyb_in_proj(x, P['norm_mix'][0:1], sc1, sh1, P['hyb_w_in'], P['ssm_conv_w'][0],
                             P['ssm_conv_b'][0], seq_len, tm, PROJ_CHUNK)
    s_ret0 = None if caches is None else caches[0][:, 0]
    s_ssm0 = None if caches is None else caches[1][:, 0]
    y, s_ret, s_ssm = _hybrid_mixer(qkv, rest, P['ssm_dt_bias'][0], P['ssm_a_log'][0], P['ret_log_decay'][0],
                                    P['ssm_d'][0], s_ret0, s_ssm0, batch, seq_len)
    outs['ret'] = s_ret[:, None]
    outs['ssm'] = s_ssm[:, None]
    x = _hyb_out(y, rest, x, g1, P['ret_gn_gain'][0:1], P['ssm_norm_gain'][0:1], P['hyb_w_out'], seq_len, tm)
    x = _conv_ffn(x, P['norm_ffn'][0:1], sc2, sh2, g2, P['ffn_w_up'][0], P['ffn_conv_w'][0], P['ffn_conv_b'][0],
                  P['ffn_w_down'][0], seq_len, tm, PROJ_CHUNK)
    sh1, sc1, g1, sh2, sc2, g2 = mods[1]
    res = _qkv_proj(x, P['norm_mix'][1:2], sc1, sh1, P['attn_w_qkv'], P['attn_q_gain'][0], P['attn_k_gain'][0],
                    rope, seq_len, min(tm, seq_len))
    if caches is None:
        qt, k4, vt, kf, vf = res
        outs['k'] = kf.reshape(batch, 1, seq_len, ATT_KV_HEADS, ATT_HD)
        outs['v'] = vf.reshape(batch, 1, seq_len, ATT_KV_HEADS, ATT_HD)
    else:
        qt, k4, vt = res
        k4, vt = _keys_values(k4, vt, caches[2][:, 0], caches[3][:, 0])
    x = _attention(qt, k4, vt, x, g1, P['attn_w_o'], batch, seq_len, tq, tk)
    x = _conv_ffn(x, P['norm_ffn'][1:2], sc2, sh2, g2, P['ffn_w_up'][1], P['ffn_conv_w'][1], P['ffn_conv_b'][1],
                  P['ffn_w_down'][1], seq_len, tm, PROJ_CHUNK)
    return x, outs


def kernel(x_prompt, x_sample, state_ret, state_ssm, cache_attn_k, cache_attn_v, c, c_ctx, w_mod, b_mod, norm_mix,
           norm_ffn, ffn_w_up, ffn_conv_w, ffn_conv_b, ffn_w_down, hyb_w_in, hyb_w_out, ret_log_decay, ret_gn_gain,
           ssm_conv_w, ssm_conv_b, ssm_a_log, ssm_dt_bias, ssm_d, ssm_norm_gain, attn_w_qkv, attn_q_gain,
           attn_k_gain, attn_w_o):
    batch, seq, _ = x_prompt.shape
    dec_batch, dec_seq, _ = x_sample.shape
    depth = w_mod.shape[0]

    rows = -(-(dec_batch + 1) // SUBLANES) * SUBLANES
    cond = jnp.concatenate([c, c_ctx[None, :], jnp.zeros((rows - dec_batch - 1, D_MODEL), F32)], axis=0)
    mod = _modulation(cond, w_mod, b_mod).reshape(depth, rows, 6, 1, D_MODEL)
    mods_sample = [[mod[l, 0:dec_batch, t] for t in range(6)] for l in range(depth)]
    mods_prompt = [[mod[l, dec_batch:dec_batch + 1, t] for t in range(6)] for l in range(depth)]

    P = {
        'norm_mix': norm_mix, 'norm_ffn': norm_ffn,
        'hyb_w_in': hyb_w_in[0].astype(BF16),
        'hyb_w_out': hyb_w_out[0].astype(BF16),
        'ret_log_decay': ret_log_decay, 'ret_gn_gain': ret_gn_gain,
        'ssm_conv_w': ssm_conv_w, 'ssm_conv_b': ssm_conv_b, 'ssm_a_log': ssm_a_log, 'ssm_dt_bias': ssm_dt_bias,
        'ssm_d': ssm_d, 'ssm_norm_gain': ssm_norm_gain,
        'attn_w_qkv': attn_w_qkv[0].astype(BF16), 'attn_q_gain': attn_q_gain, 'attn_k_gain': attn_k_gain,
        'attn_w_o': attn_w_o[0].astype(BF16),
        'ffn_w_up': ffn_w_up.astype(BF16), 'ffn_conv_w': ffn_conv_w, 'ffn_conv_b': ffn_conv_b,
        'ffn_w_down': ffn_w_down.astype(BF16),
    }

    y_prompt, outs = _trunk(x_prompt.reshape(batch * seq, D_MODEL), mods_prompt, None, P, batch, seq,
                            tm=TOKEN_TILE if (batch * seq) % TOKEN_TILE == 0 and TOKEN_TILE % seq == 0 else min(ATT_TILE, seq),
                            tq=min(ATT_TILE, seq), tk=min(ATT_TILE, seq))
    caches = (state_ret, state_ssm, cache_attn_k, cache_attn_v)
    y_sample, _ = _trunk(x_sample.reshape(dec_batch * dec_seq, D_MODEL), mods_sample, caches, P, dec_batch, dec_seq,
                         tm=TOKEN_TILE, tq=ATT_TILE, tk=ATT_TILE)
    return (y_prompt.reshape(batch, seq, D_MODEL), y_sample.reshape(dec_batch, dec_seq, D_MODEL),
            outs['ret'], outs['ssm'], outs['k'], outs['v'])
```

```python
import functools

import jax
import jax.numpy as jnp
from jax import lax
from jax.experimental import pallas as pl
from jax.experimental.pallas import tpu as pltpu

F32 = jnp.float32
BF16 = jnp.bfloat16
HIGHEST = lax.Precision.HIGHEST

LANES = 128
SUBLANES = 8
MXU_COLS = 256
VMEM_LIMIT = 56 * 1024 * 1024

D_MODEL = 1024
EPS = 1e-6
GRID_W = 64
CHUNK = 128
RET_HEADS = 8
RET_DK = 64
RET_DV = 128
RET_Q = RET_HEADS * RET_DK
RET_V = RET_HEADS * RET_DV
SSM_D_INNER = 1024
SSM_HEADDIM = 64
SSM_HEADS = 16
SSM_GROUPS = 4
SSM_RATIO = 4
SSM_STATE = 128
SSM_BC = SSM_GROUPS * SSM_STATE
SSM_XBC = SSM_D_INNER + 2 * SSM_BC
HYB_QKV = 2 * RET_Q + RET_V
HYB_BF = HYB_QKV + 2 * SSM_BC
XS_COL = 2 * RET_V
DT_COL = XS_COL + SSM_D_INNER
HYB_REST = DT_COL + LANES
HYB_MIX = RET_V + SSM_D_INNER
ATT_HEADS = 16
ATT_KV_HEADS = 4
ATT_RATIO = 4
ATT_HD = 64
ATT_Q = ATT_HEADS * ATT_HD
ATT_KV = ATT_KV_HEADS * ATT_HD
ROT_FREQS = ATT_HD // 4
ROPE_BASE = 10000.0
FFN_HIDDEN = 2816

TOKEN_TILE = 512
PROJ_CHUNK = 512
ATT_TILE = 256
NEG_BIG = -1e30
Q_SCALE = ATT_HD ** -0.5 * 1.4426950408889634


def _cparams(sem):
    return pltpu.CompilerParams(dimension_semantics=sem, vmem_limit_bytes=VMEM_LIMIT)


def _norm_mod(x, gain, sc, sh):
    ms = jnp.mean(x * x, axis=-1, keepdims=True)
    return x * lax.rsqrt(ms + EPS) * gain * (1.0 + sc) + sh


def _vec_spec(width, tiles_per_batch, per_batch):
    return pl.BlockSpec((1, 1, width), lambda i: ((i // tiles_per_batch) if per_batch else 0, 0, 0))


def _mod_body(c_ref, w_ref, b_ref, o_ref):
    cs = jax.nn.silu(c_ref[...])
    o_ref[0] = jnp.dot(cs, w_ref[0], precision=HIGHEST, preferred_element_type=F32) + b_ref[0]


def _modulation(cond, w_mod, b_mod):
    depth, _, n = w_mod.shape
    rows = cond.shape[0]
    tn = n // 4
    return pl.pallas_call(
        _mod_body,
        grid=(depth, n // tn),
        in_specs=[
            pl.BlockSpec((rows, D_MODEL), lambda l, j: (0, 0)),
            pl.BlockSpec((1, D_MODEL, tn), lambda l, j: (l, 0, j)),
            pl.BlockSpec((1, 1, tn), lambda l, j: (l, 0, j)),
        ],
        out_specs=pl.BlockSpec((1, rows, tn), lambda l, j: (l, 0, j)),
        out_shape=jax.ShapeDtypeStruct((depth, rows, n), F32),
        compiler_params=_cparams(("arbitrary", "arbitrary")),
        name="modulation",
    )(cond, w_mod, b_mod.reshape(depth, 1, n))


def _chunks(total, width):
    return [(s, min(width, total - s)) for s in range(0, total, width)]


def _resident(shape):
    return pl.BlockSpec(shape, lambda *_: (0,) * len(shape), pipeline_mode=pl.Buffered(1))


def _norm_mod_halo(seq_len, tm, x_ref, xn_ref, xp_ref, gain_ref, sc_ref, sh_ref, h_scr):
    gain, sc, sh = gain_ref[...], sc_ref[0], sh_ref[0]
    h_scr[0:tm] = _norm_mod(x_ref[...], gain, sc, sh).astype(BF16)
    if tm >= seq_len:
        h_scr[tm:tm + 2 * SUBLANES] = jnp.zeros((2 * SUBLANES, D_MODEL), BF16)
    else:
        tiles_per_seq = seq_len // tm
        pos = pl.program_id(0) % tiles_per_seq
        hn = jnp.where(pos == tiles_per_seq - 1, 0.0, _norm_mod(xn_ref[0], gain, sc, sh))
        hp = jnp.where(pos == 0, 0.0, _norm_mod(xp_ref[0], gain, sc, sh))
        h_scr[tm:tm + 2 * SUBLANES] = jnp.concatenate([hn, hp], axis=0).astype(BF16)


def _dwconv3_rows(u, cw, cb, tm, seq_len):
    rows = tm + 2 * SUBLANES
    before, after = pltpu.roll(u, 1, 0), pltpu.roll(u, rows - 1, 0)
    if tm > seq_len:
        pos = lax.broadcasted_iota(jnp.int32, u.shape, 0) % seq_len
        before = jnp.where(pos == 0, 0.0, before)
        after = jnp.where(pos == seq_len - 1, 0.0, after)
    c = before * cw[0:1] + u * cw[1:2] + after * cw[2:3] + cb
    return c[0:tm]


def _halo_specs(m, tm):
    m8 = m // SUBLANES
    r8 = tm // SUBLANES
    return [pl.BlockSpec((1, SUBLANES, D_MODEL), lambda i: (jnp.minimum((i + 1) * r8, m8 - 1), 0, 0)),
            pl.BlockSpec((1, SUBLANES, D_MODEL), lambda i: (jnp.maximum(i * r8 - 1, 0), 0, 0))]


def _hyb_in_body(seq_len, tm, tn, x_ref, xn_ref, xp_ref, gain_ref, sc_ref, sh_ref, w_ref, cw_ref, cb_ref,
                 bf_ref, rest_ref, h_scr):
    _norm_mod_halo(seq_len, tm, x_ref, xn_ref, xp_ref, gain_ref, sc_ref, sh_ref, h_scr)
    h_all = h_scr[...]
    h = h_scr[0:tm]
    w_g = HYB_QKV
    w_x = w_g + 2 * RET_V
    w_bc = w_x + SSM_D_INNER
    w_dt = w_bc + 2 * SSM_BC

    def project(wcol, width, conv_col):
        wcols = w_ref[:, wcol:wcol + width]
        if conv_col is None:
            return jnp.dot(h, wcols, preferred_element_type=F32)
        u = jnp.dot(h_all, wcols, preferred_element_type=F32)
        return jax.nn.silu(_dwconv3_rows(u, cw_ref[:, conv_col:conv_col + width],
                                         cb_ref[:, conv_col:conv_col + width], tm, seq_len))

    for col, width in _chunks(HYB_BF, tn):
        if col >= HYB_QKV:
            out = project(w_bc + col - HYB_QKV, width, SSM_D_INNER + col - HYB_QKV)
        else:
            out = project(col, width, None)
            if RET_Q <= col < 2 * RET_Q:
                out = out * (RET_DK ** -0.5)
        bf_ref[:, col:col + width] = out.astype(BF16)
    for col, width in _chunks(DT_COL, tn):
        rest_ref[:, col:col + width] = project(w_g + col, width, col - XS_COL if col >= XS_COL else None)
    dt = jnp.dot(h, w_ref[:, w_dt:w_dt + SSM_HEADS], preferred_element_type=F32)
    rest_ref[:, DT_COL:HYB_REST] = jnp.concatenate([dt, jnp.zeros((tm, HYB_REST - DT_COL - SSM_HEADS), F32)], axis=1)


def _hyb_in_proj(x, gain, sc, sh, w, conv_w, conv_b, seq_len, tm, tn):
    m = x.shape[0]
    per_batch = sc.shape[0] > 1
    tpb = seq_len // tm
    assert HYB_QKV % tn == 0 and XS_COL % tn == 0 and DT_COL % tn == 0
    x8 = x.reshape(m // SUBLANES, SUBLANES, D_MODEL)
    return pl.pallas_call(
        functools.partial(_hyb_in_body, seq_len, tm, tn),
        grid=(m // tm,),
        in_specs=[
            pl.BlockSpec((tm, D_MODEL), lambda i: (i, 0)),
            *_halo_specs(m, tm),
            pl.BlockSpec((1, D_MODEL), lambda i: (0, 0)),
            _vec_spec(D_MODEL, tpb, per_batch),
            _vec_spec(D_MODEL, tpb, per_batch),
            _resident(w.shape),
            _resident((3, SSM_XBC)),
            _resident((1, SSM_XBC)),
        ],
        out_specs=[pl.BlockSpec((tm, HYB_BF), lambda i: (i, 0)), pl.BlockSpec((tm, HYB_REST), lambda i: (i, 0))],
        out_shape=[jax.ShapeDtypeStruct((m, HYB_BF), BF16), jax.ShapeDtypeStruct((m, HYB_REST), F32)],
        scratch_shapes=[pltpu.VMEM((tm + 2 * SUBLANES, D_MODEL), BF16)],
        compiler_params=_cparams(("arbitrary",)),
        name="hybrid_in_proj",
    )(x, x8, x8, gain, sc, sh, w, conv_w, conv_b.reshape(1, SSM_XBC))


def _ffn_body(seq_len, tm, th, x_ref, xn_ref, xp_ref, gain_ref, sc_ref, sh_ref, gate_ref,
              wu_ref, cw_ref, cb_ref, wd_ref, o_ref, h_scr, act_scr):
    _norm_mod_halo(seq_len, tm, x_ref, xn_ref, xp_ref, gain_ref, sc_ref, sh_ref, h_scr)
    h = h_scr[...]

    def conv_up(col, width):
        u = jnp.dot(h, wu_ref[:, col:col + width], preferred_element_type=F32)
        return _dwconv3_rows(u, cw_ref[:, col:col + width], cb_ref[:, col:col + width], tm, seq_len)

    for col, width in _chunks(FFN_HIDDEN, th):
        act = jax.nn.silu(conv_up(col, width)) * conv_up(FFN_HIDDEN + col, width)
        act_scr[:, col:col + width] = act.astype(BF16)
    out = jnp.dot(act_scr[...], wd_ref[...], preferred_element_type=F32)
    o_ref[...] = x_ref[...] + gate_ref[0] * out


def _conv_ffn(x, gain, sc, sh, gate, w_up, conv_w, conv_b, w_down, seq_len, tm, th):
    m = x.shape[0]
    per_batch = sc.shape[0] > 1
    tpb = seq_len // tm
    x8 = x.reshape(m // SUBLANES, SUBLANES, D_MODEL)
    conv_b = conv_b.reshape(1, 2 * FFN_HIDDEN)
    vec = _vec_spec(D_MODEL, tpb, per_batch)
    return pl.pallas_call(
        functools.partial(_ffn_body, seq_len, tm, th),
        grid=(m // tm,),
        in_specs=[
            pl.BlockSpec((tm, D_MODEL), lambda i: (i, 0)),
            *_halo_specs(m, tm),
            pl.BlockSpec((1, D_MODEL), lambda i: (0, 0)),
            vec, vec, vec,
            _resident((D_MODEL, 2 * FFN_HIDDEN)),
            _resident((3, 2 * FFN_HIDDEN)),
            _resident((1, 2 * FFN_HIDDEN)),
            _resident((FFN_HIDDEN, D_MODEL)),
        ],
        out_specs=pl.BlockSpec((tm, D_MODEL), lambda i: (i, 0)),
        out_shape=jax.ShapeDtypeStruct((m, D_MODEL), F32),
        scratch_shapes=[pltpu.VMEM((tm + 2 * SUBLANES, D_MODEL), BF16), pltpu.VMEM((tm, FFN_HIDDEN), BF16)],
        compiler_params=_cparams(("arbitrary",)),
        name="conv_ffn",
    )(x, x8, x8, gain, sc, sh, gate, w_up, conv_w, conv_b, w_down)


SSM_GW = SSM_RATIO * SSM_HEADDIM
FWD_CHUNKS_PER_STEP = 4


def _bcol(x, h):
    return jnp.broadcast_to(x[:, h:h + 1], (x.shape[0], LANES))


def _pair_sel(left, a, h):
    return jnp.where(left, _bcol(a, h), _bcol(a, h + 1))


def _expand_heads(a, spread_ref):
    hi = a.astype(BF16)
    lo = (a - hi.astype(F32)).astype(BF16)
    return (jnp.dot(hi, spread_ref[...], preferred_element_type=F32)
            + jnp.dot(lo, spread_ref[...], preferred_element_type=F32))


def _ssm_decay_row(left, efull, g):
    return jnp.concatenate([_pair_sel(left, efull, g * SSM_RATIO + 2 * m) for m in range(SSM_RATIO // 2)], axis=1)


def _load_states(s_ret, s_ssm, sret0_ref, sssm0_ref):
    if sret0_ref is None:
        s_ret[...] = jnp.zeros_like(s_ret)
        s_ssm[...] = jnp.zeros_like(s_ssm)
    else:
        s_ret[...] = sret0_ref[0]
        for g in range(SSM_GROUPS):
            s_ssm[g] = jnp.concatenate([sssm0_ref[0, g * SSM_RATIO + r] for r in range(SSM_RATIO)], axis=1)


def _store_states(s_ret, s_ssm, sret_ref, sssm_ref):
    sret_ref[0, 0] = s_ret[...]
    for g in range(SSM_GROUPS):
        for r in range(SSM_RATIO):
            sssm_ref[0, 0, g * SSM_RATIO + r] = s_ssm[g][:, r * SSM_HEADDIM:(r + 1) * SSM_HEADDIM]


def _ssm_log_decay(dt_raw, dtb, alog, lane):
    dt = jax.nn.softplus(dt_raw + dtb)
    return dt, jnp.where(lane < SSM_HEADS, dt * (-jnp.exp(alog)), 0.0)


def _bwd_state_body(has_init, k_ref, v_ref, xs_ref, bm_ref, dt_ref, dtb_ref, alog_ref, spread_ref, logdec_ref,
                    *rest):
    sret0_ref = sssm0_ref = None
    if has_init:
        sret0_ref, sssm0_ref = rest[0], rest[1]
        rest = rest[2:]
    sret_in_ref, sssm_in_ref, sret_fin_ref, sssm_fin_ref, s_ret, s_ssm, wcol = rest
    i = pl.program_id(1)
    n = pl.num_programs(1)
    ii = lax.broadcasted_iota(jnp.int32, (CHUNK, CHUNK), 0)
    jj = lax.broadcasted_iota(jnp.int32, (CHUNK, CHUNK), 1)
    lane = lax.broadcasted_iota(jnp.int32, (1, LANES), 1)
    left = lane < SSM_HEADDIM

    @pl.when(i == 0)
    def _():
        _load_states(s_ret, s_ssm, sret0_ref, sssm0_ref)
        for h in range(RET_HEADS):
            wcol[h] = jnp.exp(ii.astype(F32) * logdec_ref[1, h])

    cps = sret_in_ref.shape[1]
    full = jnp.full((1, LANES), float(CHUNK), F32)
    prep = []
    for c in range(cps):
        rows = slice(c * CHUNK, (c + 1) * CHUNK)
        dt, la = _ssm_log_decay(dt_ref[rows], dtb_ref[...], alog_ref[...], lane)
        rc = jnp.dot((jj >= ii).astype(F32), la, precision=HIGHEST, preferred_element_type=F32)
        last = rc[0:1]
        prep.append((_expand_heads(jnp.exp(last - rc) * dt, spread_ref), jnp.exp(last)))

    for c in reversed(range(cps)):
        rows = slice(c * CHUNK, (c + 1) * CHUNK)
        sret_in_ref[0, c] = s_ret[...].astype(BF16)
        sssm_in_ref[0, c] = s_ssm[...].astype(BF16)
        for h in range(RET_HEADS):
            kh = k_ref[rows, h * RET_DK:(h + 1) * RET_DK]
            vw = (v_ref[rows, h * RET_DV:(h + 1) * RET_DV].astype(F32) * wcol[h]).astype(BF16)
            upd = lax.dot_general(kh, vw, (((0,), (0,)), ((), ())), preferred_element_type=F32)
            s_ret[h] = s_ret[h] * jnp.exp(full * logdec_ref[1, h]) + upd
        wx, efull = prep[c]
        for g in range(SSM_GROUPS):
            vw = (xs_ref[rows, g * SSM_GW:(g + 1) * SSM_GW] * wx[:, g * SSM_GW:(g + 1) * SSM_GW]).astype(BF16)
            bg = bm_ref[rows, g * SSM_STATE:(g + 1) * SSM_STATE]
            upd = lax.dot_general(bg, vw, (((0,), (0,)), ((), ())), preferred_element_type=F32)
            s_ssm[g] = s_ssm[g] * _ssm_decay_row(left, efull, g) + upd

    @pl.when(i == n - 1)
    def _():
        _store_states(s_ret, s_ssm, sret_fin_ref, sssm_fin_ref)


def _hyb_fwd_body(has_init, qkv_ref, xs_ref, bc_ref, dt_ref, sret_in_ref, sssm_in_ref, dtb_ref, alog_ref, dsk_ref,
                  spread_ref, logdec_ref, *rest):
    sret0_ref = sssm0_ref = None
    if has_init:
        sret0_ref, sssm0_ref = rest[0], rest[1]
        rest = rest[2:]
    y_ref, sret_fin_ref, sssm_fin_ref, s_ret, s_ssm, dcomb, ein0, ein1, wcol = rest[2:]
    i = pl.program_id(1)
    n = pl.num_programs(1)
    ii = lax.broadcasted_iota(jnp.int32, (CHUNK, CHUNK), 0)
    jj = lax.broadcasted_iota(jnp.int32, (CHUNK, CHUNK), 1)
    lane = lax.broadcasted_iota(jnp.int32, (1, LANES), 1)
    left = lane < SSM_HEADDIM
    lower = jj <= ii

    @pl.when(i == 0)
    def _():
        _load_states(s_ret, s_ssm, sret0_ref, sssm0_ref)
        fi, fj = ii.astype(F32), jj.astype(F32)
        for h in range(RET_HEADS):
            la0, la1 = logdec_ref[0, h], logdec_ref[1, h]
            dec = jnp.exp(jnp.where(lower, (fi - fj) * la0, (fj - fi) * la1))
            dcomb[h] = jnp.where(ii == jj, 2.0, dec)
            ein0[h] = jnp.exp((fi + 1.0) * la0)
            ein1[h] = jnp.exp((CHUNK - fi) * la1)
            wcol[h] = jnp.exp((CHUNK - 1.0 - fi) * la0)

    for c in range(sret_in_ref.shape[1]):
        _fwd_chunk(c, qkv_ref, xs_ref, bc_ref, dt_ref, sret_in_ref, sssm_in_ref, dtb_ref, alog_ref, dsk_ref,
                   spread_ref, logdec_ref, y_ref, s_ret, s_ssm, dcomb, ein0, ein1, wcol)

    @pl.when(i == n - 1)
    def _():
        _store_states(s_ret, s_ssm, sret_fin_ref, sssm_fin_ref)


def _fwd_chunk(c, qkv_ref, xs_ref, bc_ref, dt_ref, sret_in_ref, sssm_in_ref, dtb_ref, alog_ref, dsk_ref,
               spread_ref, logdec_ref, y_ref, s_ret, s_ssm, dcomb, ein0, ein1, wcol):
    rows = pl.ds(c * CHUNK, CHUNK)
    qkv_ref, xs_ref, bc_ref, dt_ref, y_ref = (r.at[rows] for r in (qkv_ref, xs_ref, bc_ref, dt_ref, y_ref))
    ii = lax.broadcasted_iota(jnp.int32, (CHUNK, CHUNK), 0)
    jj = lax.broadcasted_iota(jnp.int32, (CHUNK, CHUNK), 1)
    lane = lax.broadcasted_iota(jnp.int32, (1, LANES), 1)
    left = lane < SSM_HEADDIM
    lower = jj <= ii

    def ret_head(h):
        return (qkv_ref[:, h * RET_DK:(h + 1) * RET_DK],
                qkv_ref[:, RET_Q + h * RET_DK:RET_Q + (h + 1) * RET_DK],
                qkv_ref[:, 2 * RET_Q + h * RET_DV:2 * RET_Q + (h + 1) * RET_DV])

    def ssm_group(g):
        return (bc_ref[:, g * SSM_STATE:(g + 1) * SSM_STATE],
                bc_ref[:, SSM_BC + g * SSM_STATE:SSM_BC + (g + 1) * SSM_STATE])

    dt_raw = dt_ref[...]
    dt0, la0 = _ssm_log_decay(dt_raw, dtb_ref[0:1], alog_ref[0:1], lane)
    dt1, la1 = _ssm_log_decay(dt_raw, dtb_ref[1:2], alog_ref[1:2], lane)
    c0 = jnp.dot(lower.astype(F32), la0, precision=HIGHEST, preferred_element_type=F32)
    c1 = jnp.dot((jj >= ii).astype(F32), la1, precision=HIGHEST, preferred_element_type=F32)

    ret_scores, ret_inter = [], []
    for h in range(RET_HEADS):
        qh, kh, _ = ret_head(h)
        ret_scores.append(lax.dot_general(qh, kh, (((1,), (1,)), ((), ())), preferred_element_type=F32))
        states = jnp.concatenate([s_ret[h].astype(BF16), sret_in_ref[0, c, h]], axis=1)
        ret_inter.append(jnp.dot(qh, states, preferred_element_type=F32))
    ssm_scores, ssm_inter = [], []
    for g in range(SSM_GROUPS):
        bg, cg = ssm_group(g)
        ssm_scores.append(lax.dot_general(cg, bg, (((1,), (1,)), ((), ())), preferred_element_type=F32))
        states = jnp.concatenate([s_ssm[g].astype(BF16), sssm_in_ref[0, c, g]], axis=1)
        ssm_inter.append(jnp.dot(cg, states, preferred_element_type=F32))

    full = jnp.full((1, LANES), float(CHUNK), F32)
    for h in range(RET_HEADS):
        _, kh, vh = ret_head(h)
        y = jnp.dot((ret_scores[h] * dcomb[h]).astype(BF16), vh, preferred_element_type=F32)
        inter = ret_inter[h]
        y_ref[:, h * RET_DV:(h + 1) * RET_DV] = y + inter[:, 0:RET_DV] * ein0[h] + inter[:, RET_DV:] * ein1[h]
        vw = (vh.astype(F32) * wcol[h]).astype(BF16)
        upd = lax.dot_general(kh, vw, (((0,), (0,)), ((), ())), preferred_element_type=F32)
        s_ret[h] = s_ret[h] * jnp.exp(full * logdec_ref[0, h]) + upd

    c0kt, c1kt = (c0 - jnp.log(dt0)).T, (c1 - jnp.log(dt1)).T
    dst = (dt0 + dt1).T
    last0 = c0[CHUNK - 1:CHUNK]
    w0x = _expand_heads(jnp.exp(last0 - c0) * dt0, spread_ref)
    efull0 = jnp.exp(last0)
    diag = jj == ii
    for g in range(SSM_GROUPS):
        bg, _ = ssm_group(g)
        s = ssm_scores[g]
        inter = ssm_inter[g]
        vws = []
        for m in range(SSM_RATIO // 2):
            ha = g * SSM_RATIO + 2 * m
            ps, c0bs, c1bs = [], [], []
            for h in (ha, ha + 1):
                c0b, c1b = _bcol(c0, h), _bcol(c1, h)
                arg = jnp.where(lower, c0b - c0kt[h:h + 1], c1b - c1kt[h:h + 1])
                ps.append((s * jnp.where(diag, dst[h:h + 1], jnp.exp(arg))).astype(BF16))
                c0bs.append(c0b)
                c1bs.append(c1b)
            col = (g * 2 + m) * LANES
            xs_pair = xs_ref[:, col:col + LANES]
            vals = jnp.concatenate([jnp.where(left, xs_pair, 0.0), jnp.where(left, 0.0, xs_pair)], axis=0)
            y = jnp.dot(jnp.concatenate(ps, axis=1), vals.astype(BF16), preferred_element_type=F32)
            y = y + inter[:, m * LANES:(m + 1) * LANES] * jnp.exp(jnp.where(left, c0bs[0], c0bs[1]))
            y = y + (inter[:, SSM_GW + m * LANES:SSM_GW + (m + 1) * LANES]
                     * jnp.exp(jnp.where(left, c1bs[0], c1bs[1])))
            y_ref[:, RET_V + col:RET_V + col + LANES] = y + dsk_ref[:, col:col + LANES] * xs_pair
            vws.append((xs_pair * w0x[:, col:col + LANES]).astype(BF16))
        upd = lax.dot_general(bg, jnp.concatenate(vws, axis=1), (((0,), (0,)), ((), ())),
                              preferred_element_type=F32)
        s_ssm[g] = s_ssm[g] * _ssm_decay_row(left, efull0, g) + upd


def _hybrid_mixer(qkv, rest, dt_bias, a_log, log_decay, d_skip, s_ret0, s_ssm0, batch, seq_len):
    m = qkv.shape[0]
    n = seq_len // CHUNK
    has_init = s_ret0 is not None
    dt_blk = DT_COL // LANES
    pad = lambda v: jnp.pad(v.astype(F32), ((0, 0), (0, LANES - v.shape[1])))
    dtb, alog = pad(dt_bias), pad(a_log)
    logdec = log_decay.astype(F32)
    spread = (lax.broadcasted_iota(jnp.int32, (LANES, SSM_D_INNER), 0)
              == lax.broadcasted_iota(jnp.int32, (LANES, SSM_D_INNER), 1) // SSM_HEADDIM).astype(BF16)
    spread_spec = pl.BlockSpec((LANES, SSM_D_INNER), lambda b, i: (0, 0))
    dsk = jnp.repeat((d_skip[0] + d_skip[1]).astype(F32), SSM_HEADDIM).reshape(1, SSM_D_INNER)
    smem = pl.BlockSpec(memory_space=pltpu.SMEM)
    ret_spec = pl.BlockSpec((1, RET_HEADS, RET_DK, RET_DV), lambda b, i: (b, 0, 0, 0))
    ssm_spec = pl.BlockSpec((1, SSM_HEADS, SSM_STATE, SSM_HEADDIM), lambda b, i: (b, 0, 0, 0))
    ret_shape = jax.ShapeDtypeStruct((batch, 2, RET_HEADS, RET_DK, RET_DV), F32)
    ssm_shape = jax.ShapeDtypeStruct((batch, 2, SSM_HEADS, SSM_STATE, SSM_HEADDIM), F32)
    fin_specs = lambda d: [pl.BlockSpec((1, 1, RET_HEADS, RET_DK, RET_DV), lambda b, i: (b, d, 0, 0, 0)),
                           pl.BlockSpec((1, 1, SSM_HEADS, SSM_STATE, SSM_HEADDIM), lambda b, i: (b, d, 0, 0, 0))]
    ret_scr = pltpu.VMEM((RET_HEADS, RET_DK, RET_DV), F32)
    ssm_scr = pltpu.VMEM((SSM_GROUPS, SSM_STATE, SSM_GW), F32)
    const_scr = pltpu.VMEM((RET_HEADS, CHUNK, CHUNK), F32)

    cps = next(c for c in (8, 4, 2, 1) if n % c == 0)
    rsteps = n // cps
    rrows = cps * CHUNK

    def rev(b, i):
        return b * rsteps + rsteps - 1 - i

    in_specs = [
        pl.BlockSpec((rrows, RET_Q), lambda b, i: (rev(b, i), 1)),
        pl.BlockSpec((rrows, RET_V), lambda b, i: (rev(b, i), 1)),
        pl.BlockSpec((rrows, SSM_D_INNER), lambda b, i: (rev(b, i), XS_COL // SSM_D_INNER)),
        pl.BlockSpec((rrows, SSM_BC), lambda b, i: (rev(b, i), HYB_QKV // SSM_BC)),
        pl.BlockSpec((rrows, LANES), lambda b, i: (rev(b, i), dt_blk)),
        pl.BlockSpec((1, LANES), lambda b, i: (0, 0)),
        pl.BlockSpec((1, LANES), lambda b, i: (0, 0)),
        spread_spec,
        smem,
    ]
    args = [qkv, qkv, rest, qkv, rest, dtb[1:2], alog[1:2], spread, logdec]
    if has_init:
        in_specs += [ret_spec, ssm_spec]
        args += [s_ret0[:, 1], s_ssm0[:, 1]]
    sret_in, sssm_in, sret_fin, sssm_fin = pl.pallas_call(
        functools.partial(_bwd_state_body, has_init),
        grid=(batch, rsteps),
        in_specs=in_specs,
        out_specs=[
            pl.BlockSpec((1, cps, RET_HEADS, RET_DK, RET_DV), lambda b, i: (b, rsteps - 1 - i, 0, 0, 0)),
            pl.BlockSpec((1, cps, SSM_GROUPS, SSM_STATE, SSM_GW), lambda b, i: (b, rsteps - 1 - i, 0, 0, 0)),
            *fin_specs(1),
        ],
        out_shape=[
            jax.ShapeDtypeStruct((batch, n, RET_HEADS, RET_DK, RET_DV), BF16),
            jax.ShapeDtypeStruct((batch, n, SSM_GROUPS, SSM_STATE, SSM_GW), BF16),
            ret_shape, ssm_shape,
        ],
        scratch_shapes=[ret_scr, ssm_scr, const_scr],
        compiler_params=_cparams(("arbitrary", "arbitrary")),
        name="hybrid_reverse_states",
    )(*args)

    fcps = next(c for c in (FWD_CHUNKS_PER_STEP, 2, 1) if n % c == 0)
    fsteps = n // fcps
    frows = fcps * CHUNK

    def fwd(b, i):
        return b * fsteps + i

    in_specs = [
        pl.BlockSpec((frows, HYB_QKV), lambda b, i: (fwd(b, i), 0)),
        pl.BlockSpec((frows, SSM_D_INNER), lambda b, i: (fwd(b, i), XS_COL // SSM_D_INNER)),
        pl.BlockSpec((frows, 2 * SSM_BC), lambda b, i: (fwd(b, i), HYB_QKV // (2 * SSM_BC))),
        pl.BlockSpec((frows, LANES), lambda b, i: (fwd(b, i), dt_blk)),
        pl.BlockSpec((1, fcps, RET_HEADS, RET_DK, RET_DV), lambda b, i: (b, i, 0, 0, 0)),
        pl.BlockSpec((1, fcps, SSM_GROUPS, SSM_STATE, SSM_GW), lambda b, i: (b, i, 0, 0, 0)),
        pl.BlockSpec((2, LANES), lambda b, i: (0, 0)),
        pl.BlockSpec((2, LANES), lambda b, i: (0, 0)),
        pl.BlockSpec((1, SSM_D_INNER), lambda b, i: (0, 0)),
        spread_spec,
        smem,
    ]
    args = [qkv, rest, qkv, rest, sret_in, sssm_in, dtb, alog, dsk, spread, logdec]
    if has_init:
        in_specs += [ret_spec, ssm_spec]
        args += [s_ret0[:, 0], s_ssm0[:, 0]]
    in_specs += [pl.BlockSpec(memory_space=pl.ANY)] * 2
    args += [sret_fin, sssm_fin]
    y, sret_fin, sssm_fin = pl.pallas_call(
        functools.partial(_hyb_fwd_body, has_init),
        grid=(batch, fsteps),
        in_specs=in_specs,
        out_specs=[pl.BlockSpec((frows, HYB_MIX), lambda b, i: (fwd(b, i), 0)), *fin_specs(0)],
        out_shape=[jax.ShapeDtypeStruct((m, HYB_MIX), F32), ret_shape, ssm_shape],
        input_output_aliases={len(args) - 2: 1, len(args) - 1: 2},
        scratch_shapes=[ret_scr, ssm_scr, const_scr, const_scr, const_scr, const_scr],
        compiler_params=_cparams(("arbitrary", "arbitrary")),
        name="hybrid_forward_mix",
    )(*args)
    return y, sret_fin, sssm_fin


def _hyb_out_body(y_ref, g_ref, z_ref, x_ref, gate_ref, gn_ref, ng_ref, w_ref, o_ref):
    y = y_ref[...]
    parts = []
    for h in range(RET_HEADS):
        yh = y[:, h * RET_DV:(h + 1) * RET_DV]
        mu = jnp.mean(yh, axis=-1, keepdims=True)
        var = jnp.mean(jnp.square(yh - mu), axis=-1, keepdims=True)
        parts.append((yh - mu) * lax.rsqrt(var + EPS))
    y_ret = jax.nn.silu(g_ref[...]) * (jnp.concatenate(parts, axis=1) * gn_ref[...])
    yz = y[:, RET_V:] * jax.nn.silu(z_ref[...])
    y_ssm = yz * lax.rsqrt(jnp.mean(yz * yz, axis=-1, keepdims=True) + EPS) * ng_ref[...]
    out = jnp.dot(y_ret.astype(BF16), w_ref[0:RET_V], preferred_element_type=F32)
    out = out + jnp.dot(y_ssm.astype(BF16), w_ref[RET_V:HYB_MIX], preferred_element_type=F32)
    o_ref[...] = x_ref[...] + gate_ref[0] * out


def _hyb_out(y, rest, x, gate, gn_gain, norm_gain, w_out, seq_len, tm):
    m = x.shape[0]
    per_batch = gate.shape[0] > 1
    tpb = seq_len // tm
    return pl.pallas_call(
        _hyb_out_body,
        grid=(m // tm,),
        in_specs=[
            pl.BlockSpec((tm, HYB_MIX), lambda i: (i, 0)),
            pl.BlockSpec((tm, RET_V), lambda i: (i, 0)),
            pl.BlockSpec((tm, SSM_D_INNER), lambda i: (i, 1)),
            pl.BlockSpec((tm, D_MODEL), lambda i: (i, 0)),
            _vec_spec(D_MODEL, tpb, per_batch),
            pl.BlockSpec((1, RET_V), lambda i: (0, 0)),
            pl.BlockSpec((1, SSM_D_INNER), lambda i: (0, 0)),
            pl.BlockSpec((HYB_MIX, D_MODEL), lambda i: (0, 0)),
        ],
        out_specs=pl.BlockSpec((tm, D_MODEL), lambda i: (i, 0)),
        out_shape=jax.ShapeDtypeStruct((m, D_MODEL), F32),
        compiler_params=_cparams(("arbitrary",)),
        name="hybrid_out_proj",
    )(y, rest, rest, x, gate, gn_gain, norm_gain, w_out)


def _head_rms(t, seg_ref):
    outs = []
    for c in range(t.shape[1] // LANES):
        tc = t[:, c * LANES:(c + 1) * LANES]
        sq = tc * tc
        hi = sq.astype(BF16)
        lo = (sq - hi.astype(F32)).astype(BF16)
        ssum = (jnp.dot(hi, seg_ref[...], preferred_element_type=F32)
                + jnp.dot(lo, seg_ref[...], preferred_element_type=F32))
        outs.append(tc * lax.rsqrt(ssum * (1.0 / ATT_HD) + EPS))
    return outs


def _rope(tc, cos, sin, lane):
    fwd = pltpu.roll(tc, LANES - ROT_FREQS, 1)
    bwd = pltpu.roll(tc, ROT_FREQS, 1)
    return tc * cos + jnp.where(lane % (2 * ROT_FREQS) < ROT_FREQS, fwd, bwd) * sin


def _qkv_body(has_rope, x_ref, gain_ref, sc_ref, sh_ref, w_ref, seg_ref, qg_ref, kg_ref, *rest):
    if has_rope:
        cos_ref, sin_ref, q_ref, k_ref, v_ref = rest
    else:
        q_ref, k_ref, v_ref, kf_ref, vf_ref = rest
    h = _norm_mod(x_ref[...], gain_ref[...], sc_ref[0], sh_ref[0]).astype(BF16)
    lane = lax.broadcasted_iota(jnp.int32, (1, LANES), 1)

    def project(col):
        t = jnp.dot(h, w_ref[:, col:col + MXU_COLS], preferred_element_type=F32)
        return [t[:, c * LANES:(c + 1) * LANES] for c in range(MXU_COLS // LANES)]

    def normed(tiles, gain_ref_):
        tiles = [t * gain_ref_[...] for t in _head_rms(jnp.concatenate(tiles, axis=1), seg_ref)]
        return tiles, ([_rope(t, cos_ref[...], sin_ref[...], lane) for t in tiles] if has_rope else tiles)

    def emit_q(col, tiles):
        _, q = normed(tiles, qg_ref)
        for c, t in enumerate(q):
            row = col + c * LANES
            q_ref[0, row:row + LANES, :] = (t * Q_SCALE).T.astype(BF16)

    def emit_k(col, tiles):
        k, kr = normed(tiles, kg_ref)
        for c in range(MXU_COLS // LANES):
            lo = col + c * LANES
            if not has_rope:
                kf_ref[:, lo:lo + LANES] = k[c]
            kc = kr[c].astype(BF16)
            for a in range(LANES // ATT_HD):
                k_ref[0, lo // ATT_HD + a] = kc[:, a * ATT_HD:(a + 1) * ATT_HD]

    def emit_v(col, tiles):
        for c in range(MXU_COLS // LANES):
            lo = col + c * LANES
            if not has_rope:
                vf_ref[:, lo:lo + LANES] = tiles[c]
            vc = tiles[c].T.astype(BF16)
            for a in range(LANES // ATT_HD):
                v_ref[0, lo // ATT_HD + a] = vc[a * ATT_HD:(a + 1) * ATT_HD]

    work = ([(emit_q, col, col) for col in range(0, ATT_Q, MXU_COLS)]
            + [(emit_k, col, ATT_Q + col) for col in range(0, ATT_KV, MXU_COLS)]
            + [(emit_v, col, ATT_Q + ATT_KV + col) for col in range(0, ATT_KV, MXU_COLS)])
    tiles = project(work[0][2])
    for n, (emit, col, _) in enumerate(work):
        ahead = project(work[n + 1][2]) if n + 1 < len(work) else None
        emit(col, tiles)
        tiles = ahead


def _qkv_proj(x, gain, sc, sh, w_qkv, q_gain, k_gain, rope, seq_len, tm):
    m = x.shape[0]
    per_batch = sc.shape[0] > 1
    tpb = seq_len // tm
    has_rope = rope is not None
    seg = (lax.broadcasted_iota(jnp.int32, (LANES, LANES), 0) // ATT_HD
           == lax.broadcasted_iota(jnp.int32, (LANES, LANES), 1) // ATT_HD).astype(BF16)
    tile2 = lambda v: jnp.tile(v.astype(F32), LANES // ATT_HD).reshape(1, LANES)
    in_specs = [
        pl.BlockSpec((tm, D_MODEL), lambda i: (i, 0)),
        pl.BlockSpec((1, D_MODEL), lambda i: (0, 0)),
        _vec_spec(D_MODEL, tpb, per_batch),
        _vec_spec(D_MODEL, tpb, per_batch),
        pl.BlockSpec((D_MODEL, ATT_Q + 2 * ATT_KV), lambda i: (0, 0)),
        pl.BlockSpec((LANES, LANES), lambda i: (0, 0)),
        pl.BlockSpec((1, LANES), lambda i: (0, 0)),
        pl.BlockSpec((1, LANES), lambda i: (0, 0)),
    ]
    args = [x, gain, sc, sh, w_qkv, seg, tile2(q_gain), tile2(k_gain)]
    batch = m // seq_len
    out_specs = [pl.BlockSpec((1, ATT_Q, tm), lambda i: (i // tpb, 0, i % tpb)),
                 pl.BlockSpec((1, ATT_KV_HEADS, tm, ATT_HD), lambda i: (i // tpb, 0, i % tpb, 0)),
                 pl.BlockSpec((1, ATT_KV_HEADS, ATT_HD, tm), lambda i: (i // tpb, 0, 0, i % tpb))]
    out_shape = [jax.ShapeDtypeStruct((batch, ATT_Q, seq_len), BF16),
                 jax.ShapeDtypeStruct((batch, ATT_KV_HEADS, seq_len, ATT_HD), BF16),
                 jax.ShapeDtypeStruct((batch, ATT_KV_HEADS, ATT_HD, seq_len), BF16)]
    if has_rope:
        in_specs += [pl.BlockSpec((tm, LANES), lambda i: (i % tpb, 0))] * 2
        args += list(rope)
    else:
        out_specs += [pl.BlockSpec((tm, ATT_KV), lambda i: (i, 0))] * 2
        out_shape += [jax.ShapeDtypeStruct((m, ATT_KV), F32)] * 2
    return pl.pallas_call(
        functools.partial(_qkv_body, has_rope),
        grid=(m // tm,),
        in_specs=in_specs,
        out_specs=out_specs,
        out_shape=out_shape,
        compiler_params=_cparams(("arbitrary",)),
        name="attn_qkv_proj",
    )(*args)


def _rope_tables(seq_len):
    rows = seq_len // GRID_W
    row = jnp.repeat(jnp.arange(rows, dtype=F32), GRID_W)
    col = jnp.tile(jnp.arange(GRID_W, dtype=F32), rows)
    inv = ROPE_BASE ** (-jnp.arange(ROT_FREQS, dtype=F32) / ROT_FREQS)
    ar, ac = row[:, None] * inv, col[:, None] * inv
    cos = jnp.concatenate([jnp.cos(ar), jnp.cos(ar), jnp.cos(ac), jnp.cos(ac)], axis=1)
    sin = jnp.concatenate([-jnp.sin(ar), jnp.sin(ar), -jnp.sin(ac), jnp.sin(ac)], axis=1)
    return jnp.tile(cos, (1, LANES // ATT_HD)), jnp.tile(sin, (1, LANES // ATT_HD))


V_ROWS = ATT_HD + 16
ATT_SLOTS = 4


def _attn_body(tq, tk, has_ctx, qt_ref, k_ref, vt_ref, *rest):
    if has_ctx:
        ck_ref, cvt_ref = rest[0], rest[1]
        rest = rest[2:]
    x_ref, gate_ref, wo_ref, o_ref, acc_scr, s_scr, ot_scr = rest
    n_own = k_ref.shape[2] // tk
    nkv = n_own + (1 if has_ctx else 0)
    ones_rows = jnp.ones((V_ROWS - ATT_HD, tk), BF16)

    def key_tile(g, t):
        if has_ctx and isinstance(t, int) and t == n_own:
            return ck_ref[0, g]
        return k_ref[0, g, pl.ds(pl.multiple_of(t * tk, tk), tk), :]

    def value_tile(g, t):
        if has_ctx and isinstance(t, int) and t == n_own:
            return cvt_ref[0, g]
        return vt_ref[0, g, :, pl.ds(pl.multiple_of(t * tk, tk), tk)]

    def scores(g, t, slot):
        kt = key_tile(g, t)
        tops = []
        for r in range(ATT_RATIO):
            hd = g * ATT_RATIO + r
            s = jnp.dot(kt, qt_ref[0, hd * ATT_HD:(hd + 1) * ATT_HD, :], preferred_element_type=F32)
            s_scr[slot, r] = s
            tops.append(jnp.max(s, axis=0, keepdims=True))
        return tuple(tops)

    def softmax_pv(g, t, slot, ms, tops):
        vt = jnp.concatenate([value_tile(g, t), ones_rows], axis=0)
        new_ms = [jnp.maximum(ms[r], tops[r]) for r in range(ATT_RATIO)]
        ps = [jnp.exp2(s_scr[slot, r] - new_ms[r]).astype(BF16) for r in range(ATT_RATIO)]
        for r in range(ATT_RATIO):
            acc_scr[r] = (jnp.exp2(ms[r] - new_ms[r]) * acc_scr[r]
                          + jnp.dot(vt, ps[r], preferred_element_type=F32))
        return tuple(new_ms)

    first_tiles = range(min(2, nkv))
    quads = max(n_own - 2, 0) // ATT_SLOTS
    off = 0
    tops = {t: scores(0, t, t) for t in first_tiles}
    for g in range(ATT_KV_HEADS):
        acc_scr[...] = jnp.zeros_like(acc_scr)
        slot_of = lambda t, off=off: (t + off) % ATT_SLOTS

        def quad_step(i, carry, g=g, slot_of=slot_of):
            ms, top0, top1 = carry
            t = i * ATT_SLOTS
            top2 = scores(g, t + 2, slot_of(2))
            top3 = scores(g, t + 3, slot_of(3))
            ms = softmax_pv(g, t, slot_of(0), ms, top0)
            ms = softmax_pv(g, t + 1, slot_of(1), ms, top1)
            top0 = scores(g, t + 4, slot_of(0))
            top1 = scores(g, t + 5, slot_of(1))
            ms = softmax_pv(g, t + 2, slot_of(2), ms, top2)
            return softmax_pv(g, t + 3, slot_of(3), ms, top3), top0, top1

        ms = tuple(jnp.full((1, tq), NEG_BIG, F32) for _ in range(ATT_RATIO))
        if quads:
            ms, top0, top1 = lax.fori_loop(0, quads, quad_step, (ms, tops[0], tops[1]))
            tops = {quads * ATT_SLOTS: top0, quads * ATT_SLOTS + 1: top1}
        next_off, next_tops = (nkv + off) % ATT_SLOTS, {}
        for t in range(quads * ATT_SLOTS, nkv):
            if t % 2 == 0:
                for ahead in (t + 2, t + 3):
                    if ahead < nkv:
                        tops[ahead] = scores(g, ahead, slot_of(ahead))
            if t == max(nkv - 2, quads * ATT_SLOTS) and g + 1 < ATT_KV_HEADS:
                next_tops = {n: scores(g + 1, n, (n + next_off) % ATT_SLOTS) for n in first_tiles}
            ms = softmax_pv(g, t, slot_of(t), ms, tops.pop(t))
        off, tops = next_off, next_tops
        for r in range(ATT_RATIO):
            hd = g * ATT_RATIO + r
            a = acc_scr[r]
            ot_scr[hd * ATT_HD:(hd + 1) * ATT_HD, :] = (a[0:ATT_HD] / a[ATT_HD:ATT_HD + 1]).astype(BF16)
    out = lax.dot_general(ot_scr[...], wo_ref[...], (((0,), (0,)), ((), ())), preferred_element_type=F32)
    o_ref[0] = x_ref[0] + gate_ref[0] * out


def _attention(qt, k4, vt, ctx, x, gate, w_o, batch, seq_len, tq, tk):
    per_batch = gate.shape[0] > 1
    nkeys = k4.shape[2]
    x3 = x.reshape(batch, seq_len, D_MODEL)
    in_specs = [
        pl.BlockSpec((1, ATT_Q, tq), lambda b, i: (b, 0, i)),
        pl.BlockSpec((1, ATT_KV_HEADS, nkeys, ATT_HD), lambda b, i: (b, 0, 0, 0)),
        pl.BlockSpec((1, ATT_KV_HEADS, ATT_HD, nkeys), lambda b, i: (b, 0, 0, 0)),
    ]
    args = [qt, k4, vt]
    if ctx is not None:
        assert ctx[0].shape[1] == tk
        in_specs += [pl.BlockSpec((1, ATT_KV_HEADS, tk, ATT_HD), lambda b, i: (b, 0, 0, 0)),
                     pl.BlockSpec((1, ATT_KV_HEADS, ATT_HD, tk), lambda b, i: (b, 0, 0, 0))]
        args += [jnp.swapaxes(ctx[0], 1, 2).astype(BF16), jnp.transpose(ctx[1], (0, 2, 3, 1)).astype(BF16)]
    in_specs += [
        pl.BlockSpec((1, tq, D_MODEL), lambda b, i: (b, i, 0)),
        pl.BlockSpec((1, 1, D_MODEL), lambda b, i: (b if per_batch else 0, 0, 0)),
        pl.BlockSpec((ATT_Q, D_MODEL), lambda b, i: (0, 0)),
    ]
    out = pl.pallas_call(
        functools.partial(_attn_body, tq, tk, ctx is not None),
        grid=(batch, seq_len // tq),
        in_specs=in_specs,
        out_specs=pl.BlockSpec((1, tq, D_MODEL), lambda b, i: (b, i, 0)),
        out_shape=jax.ShapeDtypeStruct((batch, seq_len, D_MODEL), F32),
        scratch_shapes=[pltpu.VMEM((ATT_RATIO, V_ROWS, tq), F32), pltpu.VMEM((ATT_SLOTS, ATT_RATIO, tk, tq), F32),
                        pltpu.VMEM((ATT_Q, tq), BF16)],
        compiler_params=_cparams(("arbitrary", "arbitrary")),
        name="attention",
    )(*args, x3, gate, w_o)
    return out.reshape(batch * seq_len, D_MODEL)


def _trunk(x, mods, caches, P, batch, seq_len, tm, tq, tk):
    rope = None if caches is None else _rope_tables(seq_len)
    outs = {}
    sh1, sc1, g1, sh2, sc2, g2 = mods[0]
    qkv, rest = _hyb_in_proj(x, P['norm_mix'][0:1], sc1, sh1, P['hyb_w_in'], P['ssm_conv_w'][0],
                             P['ssm_conv_b'][0], seq_len, tm, PROJ_CHUNK)
    s_ret0 = None if caches is None else caches[0][:, 0]
    s_ssm0 = None if caches is None else caches[1][:, 0]
    y, s_ret, s_ssm = _hybrid_mixer(qkv, rest, P['ssm_dt_bias'][0], P['ssm_a_log'][0], P['ret_log_decay'][0],
                                    P['ssm_d'][0], s_ret0, s_ssm0, batch, seq_len)
    outs['ret'] = s_ret[:, None]
    outs['ssm'] = s_ssm[:, None]
    x = _hyb_out(y, rest, x, g1, P['ret_gn_gain'][0:1], P['ssm_norm_gain'][0:1], P['hyb_w_out'], seq_len, tm)
    x = _conv_ffn(x, P['norm_ffn'][0:1], sc2, sh2, g2, P['ffn_w_up'][0], P['ffn_conv_w'][0], P['ffn_conv_b'][0],
                  P['ffn_w_down'][0], seq_len, tm, PROJ_CHUNK)
    sh1, sc1, g1, sh2, sc2, g2 = mods[1]
    res = _qkv_proj(x, P['norm_mix'][1:2], sc1, sh1, P['attn_w_qkv'], P['attn_q_gain'][0], P['attn_k_gain'][0],
                    rope, seq_len, min(tm, seq_len))
    if caches is None:
        qt, k4, vt, kf, vf = res
        outs['k'] = kf.reshape(batch, 1, seq_len, ATT_KV_HEADS, ATT_HD)
        outs['v'] = vf.reshape(batch, 1, seq_len, ATT_KV_HEADS, ATT_HD)
        ctx = None
    else:
        qt, k4, vt = res
        ctx = (caches[2][:, 0], caches[3][:, 0])
    x = _attention(qt, k4, vt, ctx, x, g1, P['attn_w_o'], batch, seq_len, tq, tk)
    x = _conv_ffn(x, P['norm_ffn'][1:2], sc2, sh2, g2, P['ffn_w_up'][1], P['ffn_conv_w'][1], P['ffn_conv_b'][1],
                  P['ffn_w_down'][1], seq_len, tm, PROJ_CHUNK)
    return x, outs


def kernel(x_prompt, x_sample, state_ret, state_ssm, cache_attn_k, cache_attn_v, c, c_ctx, w_mod, b_mod, norm_mix,
           norm_ffn, ffn_w_up, ffn_conv_w, ffn_conv_b, ffn_w_down, hyb_w_in, hyb_w_out, ret_log_decay, ret_gn_gain,
           ssm_conv_w, ssm_conv_b, ssm_a_log, ssm_dt_bias, ssm_d, ssm_norm_gain, attn_w_qkv, attn_q_gain,
           attn_k_gain, attn_w_o):
    batch, seq, _ = x_prompt.shape
    dec_batch, dec_seq, _ = x_sample.shape
    depth = w_mod.shape[0]

    rows = -(-(dec_batch + 1) // SUBLANES) * SUBLANES
    cond = jnp.concatenate([c, c_ctx[None, :], jnp.zeros((rows - dec_batch - 1, D_MODEL), F32)], axis=0)
    mod = _modulation(cond, w_mod, b_mod).reshape(depth, rows, 6, 1, D_MODEL)
    mods_sample = [[mod[l, 0:dec_batch, t] for t in range(6)] for l in range(depth)]
    mods_prompt = [[mod[l, dec_batch:dec_batch + 1, t] for t in range(6)] for l in range(depth)]

    P = {
        'norm_mix': norm_mix, 'norm_ffn': norm_ffn,
        'hyb_w_in': hyb_w_in[0].astype(BF16),
        'hyb_w_out': hyb_w_out[0].astype(BF16),
        'ret_log_decay': ret_log_decay, 'ret_gn_gain': ret_gn_gain,
        'ssm_conv_w': ssm_conv_w, 'ssm_conv_b': ssm_conv_b, 'ssm_a_log': ssm_a_log, 'ssm_dt_bias': ssm_dt_bias,
        'ssm_d': ssm_d, 'ssm_norm_gain': ssm_norm_gain,
        'attn_w_qkv': attn_w_qkv[0].astype(BF16), 'attn_q_gain': attn_q_gain, 'attn_k_gain': attn_k_gain,
        'attn_w_o': attn_w_o[0].astype(BF16),
        'ffn_w_up': ffn_w_up.astype(BF16), 'ffn_conv_w': ffn_conv_w, 'ffn_conv_b': ffn_conv_b,
        'ffn_w_down': ffn_w_down.astype(BF16),
    }

    y_prompt, outs = _trunk(x_prompt.reshape(batch * seq, D_MODEL), mods_prompt, None, P, batch, seq,
                            tm=TOKEN_TILE if (batch * seq) % TOKEN_TILE == 0 and TOKEN_TILE % seq == 0 else min(ATT_TILE, seq),
                            tq=min(ATT_TILE, seq), tk=min(ATT_TILE, seq))
    caches = (state_ret, state_ssm, cache_attn_k, cache_attn_v)
    y_sample, _ = _trunk(x_sample.reshape(dec_batch * dec_seq, D_MODEL), mods_sample, caches, P, dec_batch, dec_seq,
                         tm=TOKEN_TILE, tq=ATT_TILE, tk=ATT_TILE)
    return (y_prompt.reshape(batch, seq, D_MODEL), y_sample.reshape(dec_batch, dec_seq, D_MODEL),
            outs['ret'], outs['ssm'], outs['k'], outs['v'])
```

```python
import functools

import jax
import jax.numpy as jnp
from jax import lax
from jax.experimental import pallas as pl
from jax.experimental.pallas import tpu as pltpu

F32 = jnp.float32
BF16 = jnp.bfloat16
HIGHEST = lax.Precision.HIGHEST

LANES = 128
SUBLANES = 8
MXU_COLS = 256
VMEM_LIMIT = 56 * 1024 * 1024

D_MODEL = 1024
EPS = 1e-6
GRID_W = 64
CHUNK = 128
RET_HEADS = 8
RET_DK = 64
RET_DV = 128
RET_Q = RET_HEADS * RET_DK
RET_V = RET_HEADS * RET_DV
SSM_D_INNER = 1024
SSM_HEADDIM = 64
SSM_HEADS = 16
SSM_GROUPS = 4
SSM_RATIO = 4
SSM_STATE = 128
SSM_BC = SSM_GROUPS * SSM_STATE
SSM_XBC = SSM_D_INNER + 2 * SSM_BC
HYB_QKV = 2 * RET_Q + RET_V
HYB_BF = HYB_QKV + 2 * SSM_BC
XS_COL = 2 * RET_V
DT_COL = XS_COL + SSM_D_INNER
HYB_REST = DT_COL + LANES
HYB_MIX = RET_V + SSM_D_INNER
ATT_HEADS = 16
ATT_KV_HEADS = 4
ATT_RATIO = 4
ATT_HD = 64
ATT_Q = ATT_HEADS * ATT_HD
ATT_KV = ATT_KV_HEADS * ATT_HD
ROT_FREQS = ATT_HD // 4
ROPE_BASE = 10000.0
FFN_HIDDEN = 2816

TOKEN_TILE = 512
PROJ_CHUNK = 512
ATT_TILE = 256
NEG_BIG = -1e30
Q_SCALE = ATT_HD ** -0.5 * 1.4426950408889634


def _cparams(sem):
    return pltpu.CompilerParams(dimension_semantics=sem, vmem_limit_bytes=VMEM_LIMIT)


def _norm_mod(x, gain, sc, sh):
    ms = jnp.mean(x * x, axis=-1, keepdims=True)
    return x * lax.rsqrt(ms + EPS) * gain * (1.0 + sc) + sh


def _vec_spec(width, tiles_per_batch, per_batch):
    return pl.BlockSpec((1, 1, width), lambda i: ((i // tiles_per_batch) if per_batch else 0, 0, 0))


def _mod_body(c_ref, w_ref, b_ref, o_ref):
    cs = jax.nn.silu(c_ref[...])
    o_ref[0] = jnp.dot(cs, w_ref[0], precision=HIGHEST, preferred_element_type=F32) + b_ref[0]


def _modulation(cond, w_mod, b_mod):
    depth, _, n = w_mod.shape
    rows = cond.shape[0]
    tn = n // 4
    return pl.pallas_call(
        _mod_body,
        grid=(depth, n // tn),
        in_specs=[
            pl.BlockSpec((rows, D_MODEL), lambda l, j: (0, 0)),
            pl.BlockSpec((1, D_MODEL, tn), lambda l, j: (l, 0, j)),
            pl.BlockSpec((1, 1, tn), lambda l, j: (l, 0, j)),
        ],
        out_specs=pl.BlockSpec((1, rows, tn), lambda l, j: (l, 0, j)),
        out_shape=jax.ShapeDtypeStruct((depth, rows, n), F32),
        compiler_params=_cparams(("arbitrary", "arbitrary")),
        name="modulation",
    )(cond, w_mod, b_mod.reshape(depth, 1, n))


def _chunks(total, width):
    return [(s, min(width, total - s)) for s in range(0, total, width)]


def _resident(shape):
    return pl.BlockSpec(shape, lambda *_: (0,) * len(shape), pipeline_mode=pl.Buffered(1))


def _norm_mod_halo(seq_len, tm, x_ref, xn_ref, xp_ref, gain_ref, sc_ref, sh_ref, h_scr):
    gain, sc, sh = gain_ref[...], sc_ref[0], sh_ref[0]
    h_scr[0:tm] = _norm_mod(x_ref[...], gain, sc, sh).astype(BF16)
    if tm >= seq_len:
        h_scr[tm:tm + 2 * SUBLANES] = jnp.zeros((2 * SUBLANES, D_MODEL), BF16)
    else:
        tiles_per_seq = seq_len // tm
        pos = pl.program_id(0) % tiles_per_seq
        hn = jnp.where(pos == tiles_per_seq - 1, 0.0, _norm_mod(xn_ref[0], gain, sc, sh))
        hp = jnp.where(pos == 0, 0.0, _norm_mod(xp_ref[0], gain, sc, sh))
        h_scr[tm:tm + 2 * SUBLANES] = jnp.concatenate([hn, hp], axis=0).astype(BF16)


def _dwconv3_rows(u, cw, cb, tm, seq_len):
    rows = tm + 2 * SUBLANES
    before, after = pltpu.roll(u, 1, 0), pltpu.roll(u, rows - 1, 0)
    if tm > seq_len:
        pos = lax.broadcasted_iota(jnp.int32, u.shape, 0) % seq_len
        before = jnp.where(pos == 0, 0.0, before)
        after = jnp.where(pos == seq_len - 1, 0.0, after)
    c = before * cw[0:1] + u * cw[1:2] + after * cw[2:3] + cb
    return c[0:tm]


def _halo_specs(m, tm):
    m8 = m // SUBLANES
    r8 = tm // SUBLANES
    return [pl.BlockSpec((1, SUBLANES, D_MODEL), lambda i: (jnp.minimum((i + 1) * r8, m8 - 1), 0, 0)),
            pl.BlockSpec((1, SUBLANES, D_MODEL), lambda i: (jnp.maximum(i * r8 - 1, 0), 0, 0))]


def _hyb_in_body(seq_len, tm, tn, x_ref, xn_ref, xp_ref, gain_ref, sc_ref, sh_ref, w_ref, cw_ref, cb_ref,
                 bf_ref, rest_ref, h_scr):
    _norm_mod_halo(seq_len, tm, x_ref, xn_ref, xp_ref, gain_ref, sc_ref, sh_ref, h_scr)
    h_all = h_scr[...]
    h = h_scr[0:tm]
    w_g = HYB_QKV
    w_x = w_g + 2 * RET_V
    w_bc = w_x + SSM_D_INNER
    w_dt = w_bc + 2 * SSM_BC

    def project(wcol, width, conv_col):
        wcols = w_ref[:, wcol:wcol + width]
        if conv_col is None:
            return jnp.dot(h, wcols, preferred_element_type=F32)
        u = jnp.dot(h_all, wcols, preferred_element_type=F32)
        return jax.nn.silu(_dwconv3_rows(u, cw_ref[:, conv_col:conv_col + width],
                                         cb_ref[:, conv_col:conv_col + width], tm, seq_len))

    for col, width in _chunks(HYB_BF, tn):
        if col >= HYB_QKV:
            out = project(w_bc + col - HYB_QKV, width, SSM_D_INNER + col - HYB_QKV)
        else:
            out = project(col, width, None)
            if RET_Q <= col < 2 * RET_Q:
                out = out * (RET_DK ** -0.5)
        bf_ref[:, col:col + width] = out.astype(BF16)
    for col, width in _chunks(DT_COL, tn):
        rest_ref[:, col:col + width] = project(w_g + col, width, col - XS_COL if col >= XS_COL else None)
    dt = jnp.dot(h, w_ref[:, w_dt:w_dt + SSM_HEADS], preferred_element_type=F32)
    rest_ref[:, DT_COL:HYB_REST] = jnp.concatenate([dt, jnp.zeros((tm, HYB_REST - DT_COL - SSM_HEADS), F32)], axis=1)


def _hyb_in_proj(x, gain, sc, sh, w, conv_w, conv_b, seq_len, tm, tn):
    m = x.shape[0]
    per_batch = sc.shape[0] > 1
    tpb = seq_len // tm
    assert HYB_QKV % tn == 0 and XS_COL % tn == 0 and DT_COL % tn == 0
    x8 = x.reshape(m // SUBLANES, SUBLANES, D_MODEL)
    return pl.pallas_call(
        functools.partial(_hyb_in_body, seq_len, tm, tn),
        grid=(m // tm,),
        in_specs=[
            pl.BlockSpec((tm, D_MODEL), lambda i: (i, 0)),
            *_halo_specs(m, tm),
            pl.BlockSpec((1, D_MODEL), lambda i: (0, 0)),
            _vec_spec(D_MODEL, tpb, per_batch),
            _vec_spec(D_MODEL, tpb, per_batch),
            _resident(w.shape),
            _resident((3, SSM_XBC)),
            _resident((1, SSM_XBC)),
        ],
        out_specs=[pl.BlockSpec((tm, HYB_BF), lambda i: (i, 0)), pl.BlockSpec((tm, HYB_REST), lambda i: (i, 0))],
        out_shape=[jax.ShapeDtypeStruct((m, HYB_BF), BF16), jax.ShapeDtypeStruct((m, HYB_REST), F32)],
        scratch_shapes=[pltpu.VMEM((tm + 2 * SUBLANES, D_MODEL), BF16)],
        compiler_params=_cparams(("arbitrary",)),
        name="hybrid_in_proj",
    )(x, x8, x8, gain, sc, sh, w, conv_w, conv_b.reshape(1, SSM_XBC))


def _ffn_body(seq_len, tm, th, x_ref, xn_ref, xp_ref, gain_ref, sc_ref, sh_ref, gate_ref,
              wu_ref, cw_ref, cb_ref, wd_ref, o_ref, h_scr, act_scr):
    _norm_mod_halo(seq_len, tm, x_ref, xn_ref, xp_ref, gain_ref, sc_ref, sh_ref, h_scr)
    h = h_scr[...]

    def conv_up(col, width):
        u = jnp.dot(h, wu_ref[:, col:col + width], preferred_element_type=F32)
        return _dwconv3_rows(u, cw_ref[:, col:col + width], cb_ref[:, col:col + width], tm, seq_len)

    for col, width in _chunks(FFN_HIDDEN, th):
        act = jax.nn.silu(conv_up(col, width)) * conv_up(FFN_HIDDEN + col, width)
        act_scr[:, col:col + width] = act.astype(BF16)
    out = jnp.dot(act_scr[...], wd_ref[...], preferred_element_type=F32)
    o_ref[...] = x_ref[...] + gate_ref[0] * out


def _conv_ffn(x, gain, sc, sh, gate, w_up, conv_w, conv_b, w_down, seq_len, tm, th):
    m = x.shape[0]
    per_batch = sc.shape[0] > 1
    tpb = seq_len // tm
    x8 = x.reshape(m // SUBLANES, SUBLANES, D_MODEL)
    conv_b = conv_b.reshape(1, 2 * FFN_HIDDEN)
    vec = _vec_spec(D_MODEL, tpb, per_batch)
    return pl.pallas_call(
        functools.partial(_ffn_body, seq_len, tm, th),
        grid=(m // tm,),
        in_specs=[
            pl.BlockSpec((tm, D_MODEL), lambda i: (i, 0)),
            *_halo_specs(m, tm),
            pl.BlockSpec((1, D_MODEL), lambda i: (0, 0)),
            vec, vec, vec,
            _resident((D_MODEL, 2 * FFN_HIDDEN)),
            _resident((3, 2 * FFN_HIDDEN)),
            _resident((1, 2 * FFN_HIDDEN)),
            _resident((FFN_HIDDEN, D_MODEL)),
        ],
        out_specs=pl.BlockSpec((tm, D_MODEL), lambda i: (i, 0)),
        out_shape=jax.ShapeDtypeStruct((m, D_MODEL), F32),
        scratch_shapes=[pltpu.VMEM((tm + 2 * SUBLANES, D_MODEL), BF16), pltpu.VMEM((tm, FFN_HIDDEN), BF16)],
        compiler_params=_cparams(("arbitrary",)),
        name="conv_ffn",
    )(x, x8, x8, gain, sc, sh, gate, w_up, conv_w, conv_b, w_down)


SSM_GW = SSM_RATIO * SSM_HEADDIM
FWD_CHUNKS_PER_STEP = 4


def _bcol(x, h):
    return jnp.broadcast_to(x[:, h:h + 1], (x.shape[0], LANES))


def _pair_sel(left, a, h):
    return jnp.where(left, _bcol(a, h), _bcol(a, h + 1))


def _expand_heads(a, spread_ref):
    hi = a.astype(BF16)
    lo = (a - hi.astype(F32)).astype(BF16)
    return (jnp.dot(hi, spread_ref[...], preferred_element_type=F32)
            + jnp.dot(lo, spread_ref[...], preferred_element_type=F32))


def _ssm_decay_row(left, efull, g):
    return jnp.concatenate([_pair_sel(left, efull, g * SSM_RATIO + 2 * m) for m in range(SSM_RATIO // 2)], axis=1)


def _load_states(s_ret, s_ssm, sret0_ref, sssm0_ref):
    if sret0_ref is None:
        s_ret[...] = jnp.zeros_like(s_ret)
        s_ssm[...] = jnp.zeros_like(s_ssm)
    else:
        s_ret[...] = sret0_ref[0]
        for g in range(SSM_GROUPS):
            s_ssm[g] = jnp.concatenate([sssm0_ref[0, g * SSM_RATIO + r] for r in range(SSM_RATIO)], axis=1)


def _store_states(s_ret, s_ssm, sret_ref, sssm_ref):
    sret_ref[0, 0] = s_ret[...]
    for g in range(SSM_GROUPS):
        for r in range(SSM_RATIO):
            sssm_ref[0, 0, g * SSM_RATIO + r] = s_ssm[g][:, r * SSM_HEADDIM:(r + 1) * SSM_HEADDIM]


def _ssm_log_decay(dt_raw, dtb, alog, lane):
    dt = jax.nn.softplus(dt_raw + dtb)
    return dt, jnp.where(lane < SSM_HEADS, dt * (-jnp.exp(alog)), 0.0)


def _bwd_state_body(has_init, k_ref, v_ref, xs_ref, bm_ref, dt_ref, dtb_ref, alog_ref, spread_ref, logdec_ref,
                    *rest):
    sret0_ref = sssm0_ref = None
    if has_init:
        sret0_ref, sssm0_ref = rest[0], rest[1]
        rest = rest[2:]
    sret_in_ref, sssm_in_ref, sret_fin_ref, sssm_fin_ref, s_ret, s_ssm, wcol = rest
    i = pl.program_id(1)
    n = pl.num_programs(1)
    ii = lax.broadcasted_iota(jnp.int32, (CHUNK, CHUNK), 0)
    jj = lax.broadcasted_iota(jnp.int32, (CHUNK, CHUNK), 1)
    lane = lax.broadcasted_iota(jnp.int32, (1, LANES), 1)
    left = lane < SSM_HEADDIM

    @pl.when(i == 0)
    def _():
        _load_states(s_ret, s_ssm, sret0_ref, sssm0_ref)
        for h in range(RET_HEADS):
            wcol[h] = jnp.exp(ii.astype(F32) * logdec_ref[1, h])

    cps = sret_in_ref.shape[1]
    full = jnp.full((1, LANES), float(CHUNK), F32)
    prep = []
    for c in range(cps):
        rows = slice(c * CHUNK, (c + 1) * CHUNK)
        dt, la = _ssm_log_decay(dt_ref[rows], dtb_ref[...], alog_ref[...], lane)
        rc = jnp.dot((jj >= ii).astype(F32), la, precision=HIGHEST, preferred_element_type=F32)
        last = rc[0:1]
        prep.append((_expand_heads(jnp.exp(last - rc) * dt, spread_ref), jnp.exp(last)))

    for c in reversed(range(cps)):
        rows = slice(c * CHUNK, (c + 1) * CHUNK)
        sret_in_ref[0, c] = s_ret[...].astype(BF16)
        sssm_in_ref[0, c] = s_ssm[...].astype(BF16)
        for h in range(RET_HEADS):
            kh = k_ref[rows, h * RET_DK:(h + 1) * RET_DK]
            vw = (v_ref[rows, h * RET_DV:(h + 1) * RET_DV].astype(F32) * wcol[h]).astype(BF16)
            upd = lax.dot_general(kh, vw, (((0,), (0,)), ((), ())), preferred_element_type=F32)
            s_ret[h] = s_ret[h] * jnp.exp(full * logdec_ref[1, h]) + upd
        wx, efull = prep[c]
        for g in range(SSM_GROUPS):
            vw = (xs_ref[rows, g * SSM_GW:(g + 1) * SSM_GW] * wx[:, g * SSM_GW:(g + 1) * SSM_GW]).astype(BF16)
            bg = bm_ref[rows, g * SSM_STATE:(g + 1) * SSM_STATE]
            upd = lax.dot_general(bg, vw, (((0,), (0,)), ((), ())), preferred_element_type=F32)
            s_ssm[g] = s_ssm[g] * _ssm_decay_row(left, efull, g) + upd

    @pl.when(i == n - 1)
    def _():
        _store_states(s_ret, s_ssm, sret_fin_ref, sssm_fin_ref)


def _hyb_fwd_body(has_init, qkv_ref, xs_ref, bc_ref, dt_ref, sret_in_ref, sssm_in_ref, dtb_ref, alog_ref, dsk_ref,
                  spread_ref, logdec_ref, *rest):
    sret0_ref = sssm0_ref = None
    if has_init:
        sret0_ref, sssm0_ref = rest[0], rest[1]
        rest = rest[2:]
    y_ref, sret_fin_ref, sssm_fin_ref, s_ret, s_ssm, dcomb, ein0, ein1, wcol = rest[2:]
    i = pl.program_id(1)
    n = pl.num_programs(1)
    ii = lax.broadcasted_iota(jnp.int32, (CHUNK, CHUNK), 0)
    jj = lax.broadcasted_iota(jnp.int32, (CHUNK, CHUNK), 1)
    lane = lax.broadcasted_iota(jnp.int32, (1, LANES), 1)
    left = lane < SSM_HEADDIM
    lower = jj <= ii

    @pl.when(i == 0)
    def _():
        _load_states(s_ret, s_ssm, sret0_ref, sssm0_ref)
        fi, fj = ii.astype(F32), jj.astype(F32)
        for h in range(RET_HEADS):
            la0, la1 = logdec_ref[0, h], logdec_ref[1, h]
            dec = jnp.exp(jnp.where(lower, (fi - fj) * la0, (fj - fi) * la1))
            dcomb[h] = jnp.where(ii == jj, 2.0, dec)
            ein0[h] = jnp.exp((fi + 1.0) * la0)
            ein1[h] = jnp.exp((CHUNK - fi) * la1)
            wcol[h] = jnp.exp((CHUNK - 1.0 - fi) * la0)

    for c in range(sret_in_ref.shape[1]):
        _fwd_chunk(c, qkv_ref, xs_ref, bc_ref, dt_ref, sret_in_ref, sssm_in_ref, dtb_ref, alog_ref, dsk_ref,
                   spread_ref, logdec_ref, y_ref, s_ret, s_ssm, dcomb, ein0, ein1, wcol)

    @pl.when(i == n - 1)
    def _():
        _store_states(s_ret, s_ssm, sret_fin_ref, sssm_fin_ref)


def _fwd_chunk(c, qkv_ref, xs_ref, bc_ref, dt_ref, sret_in_ref, sssm_in_ref, dtb_ref, alog_ref, dsk_ref,
               spread_ref, logdec_ref, y_ref, s_ret, s_ssm, dcomb, ein0, ein1, wcol):
    rows = pl.ds(c * CHUNK, CHUNK)
    qkv_ref, xs_ref, bc_ref, dt_ref, y_ref = (r.at[rows] for r in (qkv_ref, xs_ref, bc_ref, dt_ref, y_ref))
    ii = lax.broadcasted_iota(jnp.int32, (CHUNK, CHUNK), 0)
    jj = lax.broadcasted_iota(jnp.int32, (CHUNK, CHUNK), 1)
    lane = lax.broadcasted_iota(jnp.int32, (1, LANES), 1)
    left = lane < SSM_HEADDIM
    lower = jj <= ii

    def ret_head(h):
        return (qkv_ref[:, h * RET_DK:(h + 1) * RET_DK],
                qkv_ref[:, RET_Q + h * RET_DK:RET_Q + (h + 1) * RET_DK],
                qkv_ref[:, 2 * RET_Q + h * RET_DV:2 * RET_Q + (h + 1) * RET_DV])

    def ssm_group(g):
        return (bc_ref[:, g * SSM_STATE:(g + 1) * SSM_STATE],
                bc_ref[:, SSM_BC + g * SSM_STATE:SSM_BC + (g + 1) * SSM_STATE])

    dt_raw = dt_ref[...]
    dt0, la0 = _ssm_log_decay(dt_raw, dtb_ref[0:1], alog_ref[0:1], lane)
    dt1, la1 = _ssm_log_decay(dt_raw, dtb_ref[1:2], alog_ref[1:2], lane)
    c0 = jnp.dot(lower.astype(F32), la0, precision=HIGHEST, preferred_element_type=F32)
    c1 = jnp.dot((jj >= ii).astype(F32), la1, precision=HIGHEST, preferred_element_type=F32)

    ret_scores, ret_inter = [], []
    for h in range(RET_HEADS):
        qh, kh, _ = ret_head(h)
        ret_scores.append(lax.dot_general(qh, kh, (((1,), (1,)), ((), ())), preferred_element_type=F32))
        states = jnp.concatenate([s_ret[h].astype(BF16), sret_in_ref[0, c, h]], axis=1)
        ret_inter.append(jnp.dot(qh, states, preferred_element_type=F32))
    ssm_scores, ssm_inter = [], []
    for g in range(SSM_GROUPS):
        bg, cg = ssm_group(g)
        ssm_scores.append(lax.dot_general(cg, bg, (((1,), (1,)), ((), ())), preferred_element_type=F32))
        states = jnp.concatenate([s_ssm[g].astype(BF16), sssm_in_ref[0, c, g]], axis=1)
        ssm_inter.append(jnp.dot(cg, states, preferred_element_type=F32))

    full = jnp.full((1, LANES), float(CHUNK), F32)
    for h in range(RET_HEADS):
        _, kh, vh = ret_head(h)
        y = jnp.dot((ret_scores[h] * dcomb[h]).astype(BF16), vh, preferred_element_type=F32)
        inter = ret_inter[h]
        y_ref[:, h * RET_DV:(h + 1) * RET_DV] = y + inter[:, 0:RET_DV] * ein0[h] + inter[:, RET_DV:] * ein1[h]
        vw = (vh.astype(F32) * wcol[h]).astype(BF16)
        upd = lax.dot_general(kh, vw, (((0,), (0,)), ((), ())), preferred_element_type=F32)
        s_ret[h] = s_ret[h] * jnp.exp(full * logdec_ref[0, h]) + upd

    c0kt, c1kt = (c0 - jnp.log(dt0)).T, (c1 - jnp.log(dt1)).T
    dst = (dt0 + dt1).T
    last0 = c0[CHUNK - 1:CHUNK]
    w0x = _expand_heads(jnp.exp(last0 - c0) * dt0, spread_ref)
    efull0 = jnp.exp(last0)
    diag = jj == ii
    for g in range(SSM_GROUPS):
        bg, _ = ssm_group(g)
        s = ssm_scores[g]
        inter = ssm_inter[g]
        vws = []
        for m in range(SSM_RATIO // 2):
            ha = g * SSM_RATIO + 2 * m
            ps, c0bs, c1bs = [], [], []
            for h in (ha, ha + 1):
                c0b, c1b = _bcol(c0, h), _bcol(c1, h)
                arg = jnp.where(lower, c0b - c0kt[h:h + 1], c1b - c1kt[h:h + 1])
                ps.append((s * jnp.where(diag, dst[h:h + 1], jnp.exp(arg))).astype(BF16))
                c0bs.append(c0b)
                c1bs.append(c1b)
            col = (g * 2 + m) * LANES
            xs_pair = xs_ref[:, col:col + LANES]
            vals = jnp.concatenate([jnp.where(left, xs_pair, 0.0), jnp.where(left, 0.0, xs_pair)], axis=0)
            y = jnp.dot(jnp.concatenate(ps, axis=1), vals.astype(BF16), preferred_element_type=F32)
            y = y + inter[:, m * LANES:(m + 1) * LANES] * jnp.exp(jnp.where(left, c0bs[0], c0bs[1]))
            y = y + (inter[:, SSM_GW + m * LANES:SSM_GW + (m + 1) * LANES]
                     * jnp.exp(jnp.where(left, c1bs[0], c1bs[1])))
            y_ref[:, RET_V + col:RET_V + col + LANES] = y + dsk_ref[:, col:col + LANES] * xs_pair
            vws.append((xs_pair * w0x[:, col:col + LANES]).astype(BF16))
        upd = lax.dot_general(bg, jnp.concatenate(vws, axis=1), (((0,), (0,)), ((), ())),
                              preferred_element_type=F32)
        s_ssm[g] = s_ssm[g] * _ssm_decay_row(left, efull0, g) + upd


def _hybrid_mixer(qkv, rest, dt_bias, a_log, log_decay, d_skip, s_ret0, s_ssm0, batch, seq_len):
    m = qkv.shape[0]
    n = seq_len // CHUNK
    has_init = s_ret0 is not None
    dt_blk = DT_COL // LANES
    pad = lambda v: jnp.pad(v.astype(F32), ((0, 0), (0, LANES - v.shape[1])))
    dtb, alog = pad(dt_bias), pad(a_log)
    logdec = log_decay.astype(F32)
    spread = (lax.broadcasted_iota(jnp.int32, (LANES, SSM_D_INNER), 0)
              == lax.broadcasted_iota(jnp.int32, (LANES, SSM_D_INNER), 1) // SSM_HEADDIM).astype(BF16)
    spread_spec = pl.BlockSpec((LANES, SSM_D_INNER), lambda b, i: (0, 0))
    dsk = jnp.repeat((d_skip[0] + d_skip[1]).astype(F32), SSM_HEADDIM).reshape(1, SSM_D_INNER)
    smem = pl.BlockSpec(memory_space=pltpu.SMEM)
    ret_spec = pl.BlockSpec((1, RET_HEADS, RET_DK, RET_DV), lambda b, i: (b, 0, 0, 0))
    ssm_spec = pl.BlockSpec((1, SSM_HEADS, SSM_STATE, SSM_HEADDIM), lambda b, i: (b, 0, 0, 0))
    ret_shape = jax.ShapeDtypeStruct((batch, 2, RET_HEADS, RET_DK, RET_DV), F32)
    ssm_shape = jax.ShapeDtypeStruct((batch, 2, SSM_HEADS, SSM_STATE, SSM_HEADDIM), F32)
    fin_specs = lambda d: [pl.BlockSpec((1, 1, RET_HEADS, RET_DK, RET_DV), lambda b, i: (b, d, 0, 0, 0)),
                           pl.BlockSpec((1, 1, SSM_HEADS, SSM_STATE, SSM_HEADDIM), lambda b, i: (b, d, 0, 0, 0))]
    ret_scr = pltpu.VMEM((RET_HEADS, RET_DK, RET_DV), F32)
    ssm_scr = pltpu.VMEM((SSM_GROUPS, SSM_STATE, SSM_GW), F32)
    const_scr = pltpu.VMEM((RET_HEADS, CHUNK, CHUNK), F32)

    cps = next(c for c in (8, 4, 2, 1) if n % c == 0)
    rsteps = n // cps
    rrows = cps * CHUNK

    def rev(b, i):
        return b * rsteps + rsteps - 1 - i

    in_specs = [
        pl.BlockSpec((rrows, RET_Q), lambda b, i: (rev(b, i), 1)),
        pl.BlockSpec((rrows, RET_V), lambda b, i: (rev(b, i), 1)),
        pl.BlockSpec((rrows, SSM_D_INNER), lambda b, i: (rev(b, i), XS_COL // SSM_D_INNER)),
        pl.BlockSpec((rrows, SSM_BC), lambda b, i: (rev(b, i), HYB_QKV // SSM_BC)),
        pl.BlockSpec((rrows, LANES), lambda b, i: (rev(b, i), dt_blk)),
        pl.BlockSpec((1, LANES), lambda b, i: (0, 0)),
        pl.BlockSpec((1, LANES), lambda b, i: (0, 0)),
        spread_spec,
        smem,
    ]
    args = [qkv, qkv, rest, qkv, rest, dtb[1:2], alog[1:2], spread, logdec]
    if has_init:
        in_specs += [ret_spec, ssm_spec]
        args += [s_ret0[:, 1], s_ssm0[:, 1]]
    sret_in, sssm_in, sret_fin, sssm_fin = pl.pallas_call(
        functools.partial(_bwd_state_body, has_init),
        grid=(batch, rsteps),
        in_specs=in_specs,
        out_specs=[
            pl.BlockSpec((1, cps, RET_HEADS, RET_DK, RET_DV), lambda b, i: (b, rsteps - 1 - i, 0, 0, 0)),
            pl.BlockSpec((1, cps, SSM_GROUPS, SSM_STATE, SSM_GW), lambda b, i: (b, rsteps - 1 - i, 0, 0, 0)),
            *fin_specs(1),
        ],
        out_shape=[
            jax.ShapeDtypeStruct((batch, n, RET_HEADS, RET_DK, RET_DV), BF16),
            jax.ShapeDtypeStruct((batch, n, SSM_GROUPS, SSM_STATE, SSM_GW), BF16),
            ret_shape, ssm_shape,
        ],
        scratch_shapes=[ret_scr, ssm_scr, const_scr],
        compiler_params=_cparams(("arbitrary", "arbitrary")),
        name="hybrid_reverse_states",
    )(*args)

    fcps = next(c for c in (FWD_CHUNKS_PER_STEP, 2, 1) if n % c == 0)
    fsteps = n // fcps
    frows = fcps * CHUNK

    def fwd(b, i):
        return b * fsteps + i

    in_specs = [
        pl.BlockSpec((frows, HYB_QKV), lambda b, i: (fwd(b, i), 0)),
        pl.BlockSpec((frows, SSM_D_INNER), lambda b, i: (fwd(b, i), XS_COL // SSM_D_INNER)),
        pl.BlockSpec((frows, 2 * SSM_BC), lambda b, i: (fwd(b, i), HYB_QKV // (2 * SSM_BC))),
        pl.BlockSpec((frows, LANES), lambda b, i: (fwd(b, i), dt_blk)),
        pl.BlockSpec((1, fcps, RET_HEADS, RET_DK, RET_DV), lambda b, i: (b, i, 0, 0, 0)),
        pl.BlockSpec((1, fcps, SSM_GROUPS, SSM_STATE, SSM_GW), lambda b, i: (b, i, 0, 0, 0)),
        pl.BlockSpec((2, LANES), lambda b, i: (0, 0)),
        pl.BlockSpec((2, LANES), lambda b, i: (0, 0)),
        pl.BlockSpec((1, SSM_D_INNER), lambda b, i: (0, 0)),
        spread_spec,
        smem,
    ]
    args = [qkv, rest, qkv, rest, sret_in, sssm_in, dtb, alog, dsk, spread, logdec]
    if has_init:
        in_specs += [ret_spec, ssm_spec]
        args += [s_ret0[:, 0], s_ssm0[:, 0]]
    in_specs += [pl.BlockSpec(memory_space=pl.ANY)] * 2
    args += [sret_fin, sssm_fin]
    y, sret_fin, sssm_fin = pl.pallas_call(
        functools.partial(_hyb_fwd_body, has_init),
        grid=(batch, fsteps),
        in_specs=in_specs,
        out_specs=[pl.BlockSpec((frows, HYB_MIX), lambda b, i: (fwd(b, i), 0)), *fin_specs(0)],
        out_shape=[jax.ShapeDtypeStruct((m, HYB_MIX), F32), ret_shape, ssm_shape],
        input_output_aliases={len(args) - 2: 1, len(args) - 1: 2},
        scratch_shapes=[ret_scr, ssm_scr, const_scr, const_scr, const_scr, const_scr],
        compiler_params=_cparams(("arbitrary", "arbitrary")),
        name="hybrid_forward_mix",
    )(*args)
    return y, sret_fin, sssm_fin


def _hyb_out_body(y_ref, g_ref, z_ref, x_ref, gate_ref, gn_ref, ng_ref, w_ref, o_ref):
    y = y_ref[...]
    parts = []
    for h in range(RET_HEADS):
        yh = y[:, h * RET_DV:(h + 1) * RET_DV]
        mu = jnp.mean(yh, axis=-1, keepdims=True)
        var = jnp.mean(jnp.square(yh - mu), axis=-1, keepdims=True)
        parts.append((yh - mu) * lax.rsqrt(var + EPS))
    y_ret = jax.nn.silu(g_ref[...]) * (jnp.concatenate(parts, axis=1) * gn_ref[...])
    yz = y[:, RET_V:] * jax.nn.silu(z_ref[...])
    y_ssm = yz * lax.rsqrt(jnp.mean(yz * yz, axis=-1, keepdims=True) + EPS) * ng_ref[...]
    out = jnp.dot(y_ret.astype(BF16), w_ref[0:RET_V], preferred_element_type=F32)
    out = out + jnp.dot(y_ssm.astype(BF16), w_ref[RET_V:HYB_MIX], preferred_element_type=F32)
    o_ref[...] = x_ref[...] + gate_ref[0] * out


def _hyb_out(y, rest, x, gate, gn_gain, norm_gain, w_out, seq_len, tm):
    m = x.shape[0]
    per_batch = gate.shape[0] > 1
    tpb = seq_len // tm
    return pl.pallas_call(
        _hyb_out_body,
        grid=(m // tm,),
        in_specs=[
            pl.BlockSpec((tm, HYB_MIX), lambda i: (i, 0)),
            pl.BlockSpec((tm, RET_V), lambda i: (i, 0)),
            pl.BlockSpec((tm, SSM_D_INNER), lambda i: (i, 1)),
            pl.BlockSpec((tm, D_MODEL), lambda i: (i, 0)),
            _vec_spec(D_MODEL, tpb, per_batch),
            pl.BlockSpec((1, RET_V), lambda i: (0, 0)),
            pl.BlockSpec((1, SSM_D_INNER), lambda i: (0, 0)),
            pl.BlockSpec((HYB_MIX, D_MODEL), lambda i: (0, 0)),
        ],
        out_specs=pl.BlockSpec((tm, D_MODEL), lambda i: (i, 0)),
        out_shape=jax.ShapeDtypeStruct((m, D_MODEL), F32),
        compiler_params=_cparams(("arbitrary",)),
        name="hybrid_out_proj",
    )(y, rest, rest, x, gate, gn_gain, norm_gain, w_out)


def _head_rms(t, seg_ref):
    outs = []
    for c in range(t.shape[1] // LANES):
        tc = t[:, c * LANES:(c + 1) * LANES]
        sq = tc * tc
        hi = sq.astype(BF16)
        lo = (sq - hi.astype(F32)).astype(BF16)
        ssum = (jnp.dot(hi, seg_ref[...], preferred_element_type=F32)
                + jnp.dot(lo, seg_ref[...], preferred_element_type=F32))
        outs.append(tc * lax.rsqrt(ssum * (1.0 / ATT_HD) + EPS))
    return outs


def _rope(tc, cos, sin, lane):
    fwd = pltpu.roll(tc, LANES - ROT_FREQS, 1)
    bwd = pltpu.roll(tc, ROT_FREQS, 1)
    return tc * cos + jnp.where(lane % (2 * ROT_FREQS) < ROT_FREQS, fwd, bwd) * sin


def _qkv_body(has_rope, x_ref, gain_ref, sc_ref, sh_ref, w_ref, seg_ref, qg_ref, kg_ref, *rest):
    if has_rope:
        cos_ref, sin_ref, q_ref, k_ref, v_ref = rest
    else:
        q_ref, k_ref, v_ref, kf_ref, vf_ref = rest
    h = _norm_mod(x_ref[...], gain_ref[...], sc_ref[0], sh_ref[0]).astype(BF16)
    lane = lax.broadcasted_iota(jnp.int32, (1, LANES), 1)

    def project(col):
        t = jnp.dot(h, w_ref[:, col:col + MXU_COLS], preferred_element_type=F32)
        return [t[:, c * LANES:(c + 1) * LANES] for c in range(MXU_COLS // LANES)]

    def normed(tiles, gain_ref_):
        tiles = [t * gain_ref_[...] for t in _head_rms(jnp.concatenate(tiles, axis=1), seg_ref)]
        return tiles, ([_rope(t, cos_ref[...], sin_ref[...], lane) for t in tiles] if has_rope else tiles)

    def emit_q(col, tiles):
        _, q = normed(tiles, qg_ref)
        for c, t in enumerate(q):
            row = col + c * LANES
            q_ref[0, row:row + LANES, :] = (t * Q_SCALE).T.astype(BF16)

    def emit_k(col, tiles):
        k, kr = normed(tiles, kg_ref)
        for c in range(MXU_COLS // LANES):
            lo = col + c * LANES
            if not has_rope:
                kf_ref[:, lo:lo + LANES] = k[c]
            kc = kr[c].astype(BF16)
            for a in range(LANES // ATT_HD):
                k_ref[0, lo // ATT_HD + a] = kc[:, a * ATT_HD:(a + 1) * ATT_HD]

    def emit_v(col, tiles):
        for c in range(MXU_COLS // LANES):
            lo = col + c * LANES
            if not has_rope:
                vf_ref[:, lo:lo + LANES] = tiles[c]
            vc = tiles[c].T.astype(BF16)
            for a in range(LANES // ATT_HD):
                v_ref[0, lo // ATT_HD + a] = vc[a * ATT_HD:(a + 1) * ATT_HD]

    work = ([(emit_q, col, col) for col in range(0, ATT_Q, MXU_COLS)]
            + [(emit_k, col, ATT_Q + col) for col in range(0, ATT_KV, MXU_COLS)]
            + [(emit_v, col, ATT_Q + ATT_KV + col) for col in range(0, ATT_KV, MXU_COLS)])
    tiles = project(work[0][2])
    for n, (emit, col, _) in enumerate(work):
        ahead = project(work[n + 1][2]) if n + 1 < len(work) else None
        emit(col, tiles)
        tiles = ahead


def _qkv_proj(x, gain, sc, sh, w_qkv, q_gain, k_gain, rope, seq_len, tm):
    m = x.shape[0]
    per_batch = sc.shape[0] > 1
    tpb = seq_len // tm
    has_rope = rope is not None
    seg = (lax.broadcasted_iota(jnp.int32, (LANES, LANES), 0) // ATT_HD
           == lax.broadcasted_iota(jnp.int32, (LANES, LANES), 1) // ATT_HD).astype(BF16)
    tile2 = lambda v: jnp.tile(v.astype(F32), LANES // ATT_HD).reshape(1, LANES)
    in_specs = [
        pl.BlockSpec((tm, D_MODEL), lambda i: (i, 0)),
        pl.BlockSpec((1, D_MODEL), lambda i: (0, 0)),
        _vec_spec(D_MODEL, tpb, per_batch),
        _vec_spec(D_MODEL, tpb, per_batch),
        pl.BlockSpec((D_MODEL, ATT_Q + 2 * ATT_KV), lambda i: (0, 0)),
        pl.BlockSpec((LANES, LANES), lambda i: (0, 0)),
        pl.BlockSpec((1, LANES), lambda i: (0, 0)),
        pl.BlockSpec((1, LANES), lambda i: (0, 0)),
    ]
    args = [x, gain, sc, sh, w_qkv, seg, tile2(q_gain), tile2(k_gain)]
    batch = m // seq_len
    out_specs = [pl.BlockSpec((1, ATT_Q, tm), lambda i: (i // tpb, 0, i % tpb)),
                 pl.BlockSpec((1, ATT_KV_HEADS, tm, ATT_HD), lambda i: (i // tpb, 0, i % tpb, 0)),
                 pl.BlockSpec((1, ATT_KV_HEADS, ATT_HD, tm), lambda i: (i // tpb, 0, 0, i % tpb))]
    out_shape = [jax.ShapeDtypeStruct((batch, ATT_Q, seq_len), BF16),
                 jax.ShapeDtypeStruct((batch, ATT_KV_HEADS, seq_len, ATT_HD), BF16),
                 jax.ShapeDtypeStruct((batch, ATT_KV_HEADS, ATT_HD, seq_len), BF16)]
    if has_rope:
        in_specs += [pl.BlockSpec((tm, LANES), lambda i: (i % tpb, 0))] * 2
        args += list(rope)
    else:
        out_specs += [pl.BlockSpec((tm, ATT_KV), lambda i: (i, 0))] * 2
        out_shape += [jax.ShapeDtypeStruct((m, ATT_KV), F32)] * 2
    return pl.pallas_call(
        functools.partial(_qkv_body, has_rope),
        grid=(m // tm,),
        in_specs=in_specs,
        out_specs=out_specs,
        out_shape=out_shape,
        compiler_params=_cparams(("arbitrary",)),
        name="attn_qkv_proj",
    )(*args)


def _rope_tables(seq_len):
    rows = seq_len // GRID_W
    row = jnp.repeat(jnp.arange(rows, dtype=F32), GRID_W)
    col = jnp.tile(jnp.arange(GRID_W, dtype=F32), rows)
    inv = ROPE_BASE ** (-jnp.arange(ROT_FREQS, dtype=F32) / ROT_FREQS)
    ar, ac = row[:, None] * inv, col[:, None] * inv
    cos = jnp.concatenate([jnp.cos(ar), jnp.cos(ar), jnp.cos(ac), jnp.cos(ac)], axis=1)
    sin = jnp.concatenate([-jnp.sin(ar), jnp.sin(ar), -jnp.sin(ac), jnp.sin(ac)], axis=1)
    return jnp.tile(cos, (1, LANES // ATT_HD)), jnp.tile(sin, (1, LANES // ATT_HD))


V_ROWS = ATT_HD + 16
ATT_SLOTS = 4


def _attn_body(tq, tk, has_ctx, qt_ref, k_ref, vt_ref, *rest):
    if has_ctx:
        ck_ref, cvt_ref = rest[0], rest[1]
        rest = rest[2:]
    x_ref, gate_ref, wo_ref, o_ref, acc_scr, s_scr, ot_scr = rest
    n_own = k_ref.shape[2] // tk
    nkv = n_own + (1 if has_ctx else 0)
    ones_rows = jnp.ones((V_ROWS - ATT_HD, tk), BF16)

    def key_tile(g, t):
        if has_ctx and isinstance(t, int) and t == n_own:
            return ck_ref[0, g]
        return k_ref[0, g, pl.ds(pl.multiple_of(t * tk, tk), tk), :]

    def value_tile(g, t):
        if has_ctx and isinstance(t, int) and t == n_own:
            return cvt_ref[0, g]
        return vt_ref[0, g, :, pl.ds(pl.multiple_of(t * tk, tk), tk)]

    def scores(g, t, slot):
        kt = key_tile(g, t)
        tops = []
        for r in range(ATT_RATIO):
            hd = g * ATT_RATIO + r
            s = jnp.dot(kt, qt_ref[0, hd * ATT_HD:(hd + 1) * ATT_HD, :], preferred_element_type=F32)
            s_scr[slot, r] = s
            tops.append(jnp.max(s, axis=0, keepdims=True))
        return tuple(tops)

    def softmax_pv(g, t, slot, ms, tops):
        vt = jnp.concatenate([value_tile(g, t), ones_rows], axis=0)
        new_ms = [jnp.maximum(ms[r], tops[r]) for r in range(ATT_RATIO)]
        ps = [jnp.exp2(s_scr[slot, r] - new_ms[r]).astype(BF16) for r in range(ATT_RATIO)]
        for r in range(ATT_RATIO):
            acc_scr[r] = (jnp.exp2(ms[r] - new_ms[r]) * acc_scr[r]
                          + jnp.dot(vt, ps[r], preferred_element_type=F32))
        return tuple(new_ms)

    first_tiles = range(min(2, nkv))
    quads = max(n_own - 2, 0) // ATT_SLOTS
    off = 0
    tops = {t: scores(0, t, t) for t in first_tiles}
    for g in range(ATT_KV_HEADS):
        acc_scr[...] = jnp.zeros_like(acc_scr)
        slot_of = lambda t, off=off: (t + off) % ATT_SLOTS

        def quad_step(i, carry, g=g, slot_of=slot_of):
            ms, top0, top1 = carry
            t = i * ATT_SLOTS
            top2 = scores(g, t + 2, slot_of(2))
            top3 = scores(g, t + 3, slot_of(3))
            ms = softmax_pv(g, t, slot_of(0), ms, top0)
            ms = softmax_pv(g, t + 1, slot_of(1), ms, top1)
            top0 = scores(g, t + 4, slot_of(0))
            top1 = scores(g, t + 5, slot_of(1))
            ms = softmax_pv(g, t + 2, slot_of(2), ms, top2)
            return softmax_pv(g, t + 3, slot_of(3), ms, top3), top0, top1

        ms = tuple(jnp.full((1, tq), NEG_BIG, F32) for _ in range(ATT_RATIO))
        if quads:
            ms, top0, top1 = lax.fori_loop(0, quads, quad_step, (ms, tops[0], tops[1]))
            tops = {quads * ATT_SLOTS: top0, quads * ATT_SLOTS + 1: top1}
        next_off, next_tops = (nkv + off) % ATT_SLOTS, {}
        for t in range(quads * ATT_SLOTS, nkv):
            if t % 2 == 0:
                for ahead in (t + 2, t + 3):
                    if ahead < nkv:
                        tops[ahead] = scores(g, ahead, slot_of(ahead))
            if t == max(nkv - 2, quads * ATT_SLOTS) and g + 1 < ATT_KV_HEADS:
                next_tops = {n: scores(g + 1, n, (n + next_off) % ATT_SLOTS) for n in first_tiles}
            ms = softmax_pv(g, t, slot_of(t), ms, tops.pop(t))
        off, tops = next_off, next_tops
        for r in range(ATT_RATIO):
            hd = g * ATT_RATIO + r
            a = acc_scr[r]
            ot_scr[hd * ATT_HD:(hd + 1) * ATT_HD, :] = (a[0:ATT_HD] / a[ATT_HD:ATT_HD + 1]).astype(BF16)
    out = lax.dot_general(ot_scr[...], wo_ref[...], (((0,), (0,)), ((), ())), preferred_element_type=F32)
    o_ref[0] = x_ref[0] + gate_ref[0] * out


def _attention(qt, k4, vt, ctx, x, gate, w_o, batch, seq_len, tq, tk):
    per_batch = gate.shape[0] > 1
    nkeys = k4.shape[2]
    x3 = x.reshape(batch, seq_len, D_MODEL)
    in_specs = [
        pl.BlockSpec((1, ATT_Q, tq), lambda b, i: (b, 0, i)),
        pl.BlockSpec((1, ATT_KV_HEADS, nkeys, ATT_HD), lambda b, i: (b, 0, 0, 0)),
        pl.BlockSpec((1, ATT_KV_HEADS, ATT_HD, nkeys), lambda b, i: (b, 0, 0, 0)),
    ]
    args = [qt, k4, vt]
    if ctx is not None:
        assert ctx[0].shape[1] == tk
        in_specs += [pl.BlockSpec((1, ATT_KV_HEADS, tk, ATT_HD), lambda b, i: (b, 0, 0, 0)),
                     pl.BlockSpec((1, ATT_KV_HEADS, ATT_HD, tk), lambda b, i: (b, 0, 0, 0))]
        args += [jnp.swapaxes(ctx[0], 1, 2).astype(BF16), jnp.transpose(ctx[1], (0, 2, 3, 1)).astype(BF16)]
    in_specs += [
        pl.BlockSpec((1, tq, D_MODEL), lambda b, i: (b, i, 0)),
        pl.BlockSpec((1, 1, D_MODEL), lambda b, i: (b if per_batch else 0, 0, 0)),
        pl.BlockSpec((ATT_Q, D_MODEL), lambda b, i: (0, 0)),
    ]
    out = pl.pallas_call(
        functools.partial(_attn_body, tq, tk, ctx is not None),
        grid=(batch, seq_len // tq),
        in_specs=in_specs,
        out_specs=pl.BlockSpec((1, tq, D_MODEL), lambda b, i: (b, i, 0)),
        out_shape=jax.ShapeDtypeStruct((batch, seq_len, D_MODEL), F32),
        scratch_shapes=[pltpu.VMEM((ATT_RATIO, V_ROWS, tq), F32), pltpu.VMEM((ATT_SLOTS, ATT_RATIO, tk, tq), F32),
                        pltpu.VMEM((ATT_Q, tq), BF16)],
        compiler_params=_cparams(("arbitrary", "arbitrary")),
        name="attention",
    )(*args, x3, gate, w_o)
    return out.reshape(batch * seq_len, D_MODEL)


def _trunk(x, mods, caches, P, batch, seq_len, tm, tq, tk):
    rope = None if caches is None else _rope_tables(seq_len)
    outs = {}
    sh1, sc1, g1, sh2, sc2, g2 = mods[0]
    qkv, rest = _hyb_in_proj(x, P['norm_mix'][0:1], sc1, sh1, P['hyb_w_in'], P['ssm_conv_w'][0],
                             P['ssm_conv_b'][0], seq_len, tm, PROJ_CHUNK)
    s_ret0 = None if caches is None else caches[0][:, 0]
    s_ssm0 = None if caches is None else caches[1][:, 0]
    y, s_ret, s_ssm = _hybrid_mixer(qkv, rest, P['ssm_dt_bias'][0], P['ssm_a_log'][0], P['ret_log_decay'][0],
                                    P['ssm_d'][0], s_ret0, s_ssm0, batch, seq_len)
    outs['ret'] = s_ret[:, None]
    outs['ssm'] = s_ssm[:, None]
    x = _hyb_out(y, rest, x, g1, P['ret_gn_gain'][0:1], P['ssm_norm_gain'][0:1], P['hyb_w_out'], seq_len, tm)
    x = _conv_ffn(x, P['norm_ffn'][0:1], sc2, sh2, g2, P['ffn_w_up'][0], P['ffn_conv_w'][0], P['ffn_conv_b'][0],
                  P['ffn_w_down'][0], seq_len, tm, PROJ_CHUNK)
    sh1, sc1, g1, sh2, sc2, g2 = mods[1]
    res = _qkv_proj(x, P['norm_mix'][1:2], sc1, sh1, P['attn_w_qkv'], P['attn_q_gain'][0], P['attn_k_gain'][0],
                    rope, seq_len, min(tm, seq_len))
    if caches is None:
        qt, k4, vt, kf, vf = res
        outs['k'] = kf.reshape(batch, 1, seq_len, ATT_KV_HEADS, ATT_HD)
        outs['v'] = vf.reshape(batch, 1, seq_len, ATT_KV_HEADS, ATT_HD)
        ctx = None
    else:
        qt, k4, vt = res
        ctx = (caches[2][:, 0], caches[3][:, 0])
    x = _attention(qt, k4, vt, ctx, x, g1, P['attn_w_o'], batch, seq_len, tq, tk)
    x = _conv_ffn(x, P['norm_ffn'][1:2], sc2, sh2, g2, P['ffn_w_up'][1], P['ffn_conv_w'][1], P['ffn_conv_b'][1],
                  P['ffn_w_down'][1], seq_len, tm, PROJ_CHUNK)
    return x, outs


def kernel(x_prompt, x_sample, state_ret, state_ssm, cache_attn_k, cache_attn_v, c, c_ctx, w_mod, b_mod, norm_mix,
           norm_ffn, ffn_w_up, ffn_conv_w, ffn_conv_b, ffn_w_down, hyb_w_in, hyb_w_out, ret_log_decay, ret_gn_gain,
           ssm_conv_w, ssm_conv_b, ssm_a_log, ssm_dt_bias, ssm_d, ssm_norm_gain, attn_w_qkv, attn_q_gain,
           attn_k_gain, attn_w_o):
    batch, seq, _ = x_prompt.shape
    dec_batch, dec_seq, _ = x_sample.shape
    depth = w_mod.shape[0]

    rows = -(-(dec_batch + 1) // SUBLANES) * SUBLANES
    cond = jnp.concatenate([c, c_ctx[None, :], jnp.zeros((rows - dec_batch - 1, D_MODEL), F32)], axis=0)
    mod = _modulation(cond, w_mod, b_mod).reshape(depth, rows, 6, 1, D_MODEL)
    mods_sample = [[mod[l, 0:dec_batch, t] for t in range(6)] for l in range(depth)]
    mods_prompt = [[mod[l, dec_batch:dec_batch + 1, t] for t in range(6)] for l in range(depth)]

    P = {
        'norm_mix': norm_mix, 'norm_ffn': norm_ffn,
        'hyb_w_in': hyb_w_in[0].astype(BF16),
        'hyb_w_out': hyb_w_out[0].astype(BF16),
        'ret_log_decay': ret_log_decay, 'ret_gn_gain': ret_gn_gain,
        'ssm_conv_w': ssm_conv_w, 'ssm_conv_b': ssm_conv_b, 'ssm_a_log': ssm_a_log, 'ssm_dt_bias': ssm_dt_bias,
        'ssm_d': ssm_d, 'ssm_norm_gain': ssm_norm_gain,
        'attn_w_qkv': attn_w_qkv[0].astype(BF16), 'attn_q_gain': attn_q_gain, 'attn_k_gain': attn_k_gain,
        'attn_w_o': attn_w_o[0].astype(BF16),
        'ffn_w_up': [ffn_w_up[l].astype(BF16) for l in range(depth)],
        'ffn_conv_w': ffn_conv_w, 'ffn_conv_b': ffn_conv_b,
        'ffn_w_down': [ffn_w_down[l].astype(BF16) for l in range(depth)],
    }

    y_prompt, outs = _trunk(x_prompt.reshape(batch * seq, D_MODEL), mods_prompt, None, P, batch, seq,
                            tm=TOKEN_TILE if (batch * seq) % TOKEN_TILE == 0 and TOKEN_TILE % seq == 0 else min(ATT_TILE, seq),
                            tq=min(ATT_TILE, seq), tk=min(ATT_TILE, seq))
    caches = (state_ret, state_ssm, cache_attn_k, cache_attn_v)
    y_sample, _ = _trunk(x_sample.reshape(dec_batch * dec_seq, D_MODEL), mods_sample, caches, P, dec_batch, dec_seq,
                         tm=TOKEN_TILE, tq=ATT_TILE, tk=ATT_TILE)
    return (y_prompt.reshape(batch, seq, D_MODEL), y_sample.reshape(dec_batch, dec_seq, D_MODEL),
            outs['ret'], outs['ssm'], outs['k'], outs['v'])
```

```python
import functools

import jax
import jax.numpy as jnp
from jax import lax
from jax.experimental import pallas as pl
from jax.experimental.pallas import tpu as pltpu

F32 = jnp.float32
BF16 = jnp.bfloat16
HIGHEST = lax.Precision.HIGHEST

LANES = 128
SUBLANES = 8
MXU_COLS = 256
VMEM_LIMIT = 56 * 1024 * 1024

D_MODEL = 1024
EPS = 1e-6
GRID_W = 64
CHUNK = 128
RET_HEADS = 8
RET_DK = 64
RET_DV = 128
RET_Q = RET_HEADS * RET_DK
RET_V = RET_HEADS * RET_DV
SSM_D_INNER = 1024
SSM_HEADDIM = 64
SSM_HEADS = 16
SSM_GROUPS = 4
SSM_RATIO = 4
SSM_STATE = 128
SSM_BC = SSM_GROUPS * SSM_STATE
SSM_XBC = SSM_D_INNER + 2 * SSM_BC
HYB_QKV = 2 * RET_Q + RET_V
HYB_BF = HYB_QKV + 2 * SSM_BC
XS_COL = 2 * RET_V
DT_COL = XS_COL + SSM_D_INNER
HYB_REST = DT_COL + LANES
HYB_MIX = RET_V + SSM_D_INNER
ATT_HEADS = 16
ATT_KV_HEADS = 4
ATT_RATIO = 4
ATT_HD = 64
ATT_Q = ATT_HEADS * ATT_HD
ATT_KV = ATT_KV_HEADS * ATT_HD
ROT_FREQS = ATT_HD // 4
ROPE_BASE = 10000.0
FFN_HIDDEN = 2816

TOKEN_TILE = 512
PROJ_CHUNK = 512
ATT_TILE = 256
NEG_BIG = -1e30
Q_SCALE = ATT_HD ** -0.5 * 1.4426950408889634


def _cparams(sem):
    return pltpu.CompilerParams(dimension_semantics=sem, vmem_limit_bytes=VMEM_LIMIT)


def _norm_mod(x, gain, sc, sh):
    ms = jnp.mean(x * x, axis=-1, keepdims=True)
    return x * lax.rsqrt(ms + EPS) * gain * (1.0 + sc) + sh


def _vec_spec(width, tiles_per_batch, per_batch):
    return pl.BlockSpec((1, 1, width), lambda i: ((i // tiles_per_batch) if per_batch else 0, 0, 0))


def _mod_body(c_ref, w_ref, b_ref, o_ref):
    cs = jax.nn.silu(c_ref[...])
    o_ref[0] = jnp.dot(cs, w_ref[0], precision=HIGHEST, preferred_element_type=F32) + b_ref[0]


def _modulation(cond, w_mod, b_mod):
    depth, _, n = w_mod.shape
    rows = cond.shape[0]
    tn = n // 4
    return pl.pallas_call(
        _mod_body,
        grid=(depth, n // tn),
        in_specs=[
            pl.BlockSpec((rows, D_MODEL), lambda l, j: (0, 0)),
            pl.BlockSpec((1, D_MODEL, tn), lambda l, j: (l, 0, j)),
            pl.BlockSpec((1, 1, tn), lambda l, j: (l, 0, j)),
        ],
        out_specs=pl.BlockSpec((1, rows, tn), lambda l, j: (l, 0, j)),
        out_shape=jax.ShapeDtypeStruct((depth, rows, n), F32),
        compiler_params=_cparams(("arbitrary", "arbitrary")),
        name="modulation",
    )(cond, w_mod, b_mod.reshape(depth, 1, n))


def _chunks(total, width):
    return [(s, min(width, total - s)) for s in range(0, total, width)]


def _resident(shape):
    return pl.BlockSpec(shape, lambda *_: (0,) * len(shape), pipeline_mode=pl.Buffered(1))


def _norm_mod_halo(seq_len, tm, x_ref, xn_ref, xp_ref, gain_ref, sc_ref, sh_ref, h_scr):
    gain, sc, sh = gain_ref[...], sc_ref[0], sh_ref[0]
    h_scr[0:tm] = _norm_mod(x_ref[...], gain, sc, sh).astype(BF16)
    if tm >= seq_len:
        h_scr[tm:tm + 2 * SUBLANES] = jnp.zeros((2 * SUBLANES, D_MODEL), BF16)
    else:
        tiles_per_seq = seq_len // tm
        pos = pl.program_id(0) % tiles_per_seq
        hn = jnp.where(pos == tiles_per_seq - 1, 0.0, _norm_mod(xn_ref[0], gain, sc, sh))
        hp = jnp.where(pos == 0, 0.0, _norm_mod(xp_ref[0], gain, sc, sh))
        h_scr[tm:tm + 2 * SUBLANES] = jnp.concatenate([hn, hp], axis=0).astype(BF16)


def _dwconv3_rows(u, cw, cb, tm, seq_len):
    rows = tm + 2 * SUBLANES
    before, after = pltpu.roll(u, 1, 0), pltpu.roll(u, rows - 1, 0)
    if tm > seq_len:
        pos = lax.broadcasted_iota(jnp.int32, u.shape, 0) % seq_len
        before = jnp.where(pos == 0, 0.0, before)
        after = jnp.where(pos == seq_len - 1, 0.0, after)
    c = before * cw[0:1] + u * cw[1:2] + after * cw[2:3] + cb
    return c[0:tm]


def _halo_specs(m, tm):
    m8 = m // SUBLANES
    r8 = tm // SUBLANES
    return [pl.BlockSpec((1, SUBLANES, D_MODEL), lambda i: (jnp.minimum((i + 1) * r8, m8 - 1), 0, 0)),
            pl.BlockSpec((1, SUBLANES, D_MODEL), lambda i: (jnp.maximum(i * r8 - 1, 0), 0, 0))]


def _hyb_in_body(seq_len, tm, tn, x_ref, xn_ref, xp_ref, gain_ref, sc_ref, sh_ref, w_ref, cw_ref, cb_ref,
                 bf_ref, rest_ref, h_scr):
    _norm_mod_halo(seq_len, tm, x_ref, xn_ref, xp_ref, gain_ref, sc_ref, sh_ref, h_scr)
    h_all = h_scr[...]
    h = h_scr[0:tm]
    w_g = HYB_QKV
    w_x = w_g + 2 * RET_V
    w_bc = w_x + SSM_D_INNER
    w_dt = w_bc + 2 * SSM_BC

    def project(wcol, width, conv_col):
        wcols = w_ref[:, wcol:wcol + width]
        if conv_col is None:
            return jnp.dot(h, wcols, preferred_element_type=F32)
        u = jnp.dot(h_all, wcols, preferred_element_type=F32)
        return jax.nn.silu(_dwconv3_rows(u, cw_ref[:, conv_col:conv_col + width],
                                         cb_ref[:, conv_col:conv_col + width], tm, seq_len))

    for col, width in _chunks(HYB_BF, tn):
        if col >= HYB_QKV:
            out = project(w_bc + col - HYB_QKV, width, SSM_D_INNER + col - HYB_QKV)
        else:
            out = project(col, width, None)
            if RET_Q <= col < 2 * RET_Q:
                out = out * (RET_DK ** -0.5)
        bf_ref[:, col:col + width] = out.astype(BF16)
    for col, width in _chunks(DT_COL, tn):
        rest_ref[:, col:col + width] = project(w_g + col, width, col - XS_COL if col >= XS_COL else None)
    dt = jnp.dot(h, w_ref[:, w_dt:w_dt + SSM_HEADS], preferred_element_type=F32)
    rest_ref[:, DT_COL:HYB_REST] = jnp.concatenate([dt, jnp.zeros((tm, HYB_REST - DT_COL - SSM_HEADS), F32)], axis=1)


def _hyb_in_proj(x, gain, sc, sh, w, conv_w, conv_b, seq_len, tm, tn):
    m = x.shape[0]
    per_batch = sc.shape[0] > 1
    tpb = seq_len // tm
    assert HYB_QKV % tn == 0 and XS_COL % tn == 0 and DT_COL % tn == 0
    x8 = x.reshape(m // SUBLANES, SUBLANES, D_MODEL)
    return pl.pallas_call(
        functools.partial(_hyb_in_body, seq_len, tm, tn),
        grid=(m // tm,),
        in_specs=[
            pl.BlockSpec((tm, D_MODEL), lambda i: (i, 0)),
            *_halo_specs(m, tm),
            pl.BlockSpec((1, D_MODEL), lambda i: (0, 0)),
            _vec_spec(D_MODEL, tpb, per_batch),
            _vec_spec(D_MODEL, tpb, per_batch),
            _resident(w.shape),
            _resident((3, SSM_XBC)),
            _resident((1, SSM_XBC)),
        ],
        out_specs=[pl.BlockSpec((tm, HYB_BF), lambda i: (i, 0)), pl.BlockSpec((tm, HYB_REST), lambda i: (i, 0))],
        out_shape=[jax.ShapeDtypeStruct((m, HYB_BF), BF16), jax.ShapeDtypeStruct((m, HYB_REST), F32)],
        scratch_shapes=[pltpu.VMEM((tm + 2 * SUBLANES, D_MODEL), BF16)],
        compiler_params=_cparams(("arbitrary",)),
        name="hybrid_in_proj",
    )(x, x8, x8, gain, sc, sh, w, conv_w, conv_b.reshape(1, SSM_XBC))


def _ffn_body(seq_len, tm, th, x_ref, xn_ref, xp_ref, gain_ref, sc_ref, sh_ref, gate_ref,
              wu_ref, cw_ref, cb_ref, wd_ref, o_ref, h_scr, act_scr):
    _norm_mod_halo(seq_len, tm, x_ref, xn_ref, xp_ref, gain_ref, sc_ref, sh_ref, h_scr)
    h = h_scr[...]

    def conv_up(col, width):
        u = jnp.dot(h, wu_ref[:, col:col + width], preferred_element_type=F32)
        return _dwconv3_rows(u, cw_ref[:, col:col + width], cb_ref[:, col:col + width], tm, seq_len)

    for col, width in _chunks(FFN_HIDDEN, th):
        act = jax.nn.silu(conv_up(col, width)) * conv_up(FFN_HIDDEN + col, width)
        act_scr[:, col:col + width] = act.astype(BF16)
    out = jnp.dot(act_scr[...], wd_ref[...], preferred_element_type=F32)
    o_ref[...] = x_ref[...] + gate_ref[0] * out


def _conv_ffn(x, gain, sc, sh, gate, w_up, conv_w, conv_b, w_down, seq_len, tm, th):
    m = x.shape[0]
    per_batch = sc.shape[0] > 1
    tpb = seq_len // tm
    x8 = x.reshape(m // SUBLANES, SUBLANES, D_MODEL)
    conv_b = conv_b.reshape(1, 2 * FFN_HIDDEN)
    vec = _vec_spec(D_MODEL, tpb, per_batch)
    return pl.pallas_call(
        functools.partial(_ffn_body, seq_len, tm, th),
        grid=(m // tm,),
        in_specs=[
            pl.BlockSpec((tm, D_MODEL), lambda i: (i, 0)),
            *_halo_specs(m, tm),
            pl.BlockSpec((1, D_MODEL), lambda i: (0, 0)),
            vec, vec, vec,
            _resident((D_MODEL, 2 * FFN_HIDDEN)),
            _resident((3, 2 * FFN_HIDDEN)),
            _resident((1, 2 * FFN_HIDDEN)),
            _resident((FFN_HIDDEN, D_MODEL)),
        ],
        out_specs=pl.BlockSpec((tm, D_MODEL), lambda i: (i, 0)),
        out_shape=jax.ShapeDtypeStruct((m, D_MODEL), F32),
        scratch_shapes=[pltpu.VMEM((tm + 2 * SUBLANES, D_MODEL), BF16), pltpu.VMEM((tm, FFN_HIDDEN), BF16)],
        compiler_params=_cparams(("arbitrary",)),
        name="conv_ffn",
    )(x, x8, x8, gain, sc, sh, gate, w_up, conv_w, conv_b, w_down)


SSM_GW = SSM_RATIO * SSM_HEADDIM
FWD_CHUNKS_PER_STEP = 4


def _bcol(x, h):
    return jnp.broadcast_to(x[:, h:h + 1], (x.shape[0], LANES))


def _pair_sel(left, a, h):
    return jnp.where(left, _bcol(a, h), _bcol(a, h + 1))


def _expand_heads(a, spread_ref):
    hi = a.astype(BF16)
    lo = (a - hi.astype(F32)).astype(BF16)
    return (jnp.dot(hi, spread_ref[...], preferred_element_type=F32)
            + jnp.dot(lo, spread_ref[...], preferred_element_type=F32))


def _ssm_decay_row(left, efull, g):
    return jnp.concatenate([_pair_sel(left, efull, g * SSM_RATIO + 2 * m) for m in range(SSM_RATIO // 2)], axis=1)


def _load_states(s_ret, s_ssm, sret0_ref, sssm0_ref):
    if sret0_ref is None:
        s_ret[...] = jnp.zeros_like(s_ret)
        s_ssm[...] = jnp.zeros_like(s_ssm)
    else:
        s_ret[...] = sret0_ref[0]
        for g in range(SSM_GROUPS):
            s_ssm[g] = jnp.concatenate([sssm0_ref[0, g * SSM_RATIO + r] for r in range(SSM_RATIO)], axis=1)


def _store_states(s_ret, s_ssm, sret_ref, sssm_ref):
    sret_ref[0, 0] = s_ret[...]
    for g in range(SSM_GROUPS):
        for r in range(SSM_RATIO):
            sssm_ref[0, 0, g * SSM_RATIO + r] = s_ssm[g][:, r * SSM_HEADDIM:(r + 1) * SSM_HEADDIM]


def _ssm_log_decay(dt_raw, dtb, alog, lane):
    dt = jax.nn.softplus(dt_raw + dtb)
    return dt, jnp.where(lane < SSM_HEADS, dt * (-jnp.exp(alog)), 0.0)


def _bwd_state_body(has_init, k_ref, v_ref, xs_ref, bm_ref, dt_ref, dtb_ref, alog_ref, spread_ref, logdec_ref,
                    *rest):
    sret0_ref = sssm0_ref = None
    if has_init:
        sret0_ref, sssm0_ref = rest[0], rest[1]
        rest = rest[2:]
    sret_in_ref, sssm_in_ref, sret_fin_ref, sssm_fin_ref, s_ret, s_ssm, wcol = rest
    i = pl.program_id(1)
    n = pl.num_programs(1)
    ii = lax.broadcasted_iota(jnp.int32, (CHUNK, CHUNK), 0)
    jj = lax.broadcasted_iota(jnp.int32, (CHUNK, CHUNK), 1)
    lane = lax.broadcasted_iota(jnp.int32, (1, LANES), 1)
    left = lane < SSM_HEADDIM

    @pl.when(i == 0)
    def _():
        _load_states(s_ret, s_ssm, sret0_ref, sssm0_ref)
        for h in range(RET_HEADS):
            wcol[h] = jnp.exp(ii.astype(F32) * logdec_ref[1, h])

    cps = sret_in_ref.shape[1]
    full = jnp.full((1, LANES), float(CHUNK), F32)
    prep = []
    for c in range(cps):
        rows = slice(c * CHUNK, (c + 1) * CHUNK)
        dt, la = _ssm_log_decay(dt_ref[rows], dtb_ref[...], alog_ref[...], lane)
        rc = jnp.dot((jj >= ii).astype(F32), la, precision=HIGHEST, preferred_element_type=F32)
        last = rc[0:1]
        prep.append((_expand_heads(jnp.exp(last - rc) * dt, spread_ref), jnp.exp(last)))

    for c in reversed(range(cps)):
        rows = slice(c * CHUNK, (c + 1) * CHUNK)
        sret_in_ref[0, c] = s_ret[...].astype(BF16)
        sssm_in_ref[0, c] = s_ssm[...].astype(BF16)
        for h in range(RET_HEADS):
            kh = k_ref[rows, h * RET_DK:(h + 1) * RET_DK]
            vw = (v_ref[rows, h * RET_DV:(h + 1) * RET_DV].astype(F32) * wcol[h]).astype(BF16)
            upd = lax.dot_general(kh, vw, (((0,), (0,)), ((), ())), preferred_element_type=F32)
            s_ret[h] = s_ret[h] * jnp.exp(full * logdec_ref[1, h]) + upd
        wx, efull = prep[c]
        for g in range(SSM_GROUPS):
            vw = (xs_ref[rows, g * SSM_GW:(g + 1) * SSM_GW] * wx[:, g * SSM_GW:(g + 1) * SSM_GW]).astype(BF16)
            bg = bm_ref[rows, g * SSM_STATE:(g + 1) * SSM_STATE]
            upd = lax.dot_general(bg, vw, (((0,), (0,)), ((), ())), preferred_element_type=F32)
            s_ssm[g] = s_ssm[g] * _ssm_decay_row(left, efull, g) + upd

    @pl.when(i == n - 1)
    def _():
        _store_states(s_ret, s_ssm, sret_fin_ref, sssm_fin_ref)


def _hyb_fwd_body(has_init, qkv_ref, xs_ref, bc_ref, dt_ref, sret_in_ref, sssm_in_ref, dtb_ref, alog_ref, dsk_ref,
                  spread_ref, logdec_ref, *rest):
    sret0_ref = sssm0_ref = None
    if has_init:
        sret0_ref, sssm0_ref = rest[0], rest[1]
        rest = rest[2:]
    y_ref, sret_fin_ref, sssm_fin_ref, s_ret, s_ssm, dcomb, ein0, ein1, wcol = rest[2:]
    i = pl.program_id(1)
    n = pl.num_programs(1)
    ii = lax.broadcasted_iota(jnp.int32, (CHUNK, CHUNK), 0)
    jj = lax.broadcasted_iota(jnp.int32, (CHUNK, CHUNK), 1)
    lane = lax.broadcasted_iota(jnp.int32, (1, LANES), 1)
    left = lane < SSM_HEADDIM
    lower = jj <= ii

    @pl.when(i == 0)
    def _():
        _load_states(s_ret, s_ssm, sret0_ref, sssm0_ref)
        fi, fj = ii.astype(F32), jj.astype(F32)
        for h in range(RET_HEADS):
            la0, la1 = logdec_ref[0, h], logdec_ref[1, h]
            dec = jnp.exp(jnp.where(lower, (fi - fj) * la0, (fj - fi) * la1))
            dcomb[h] = jnp.where(ii == jj, 2.0, dec)
            ein0[h] = jnp.exp((fi + 1.0) * la0)
            ein1[h] = jnp.exp((CHUNK - fi) * la1)
            wcol[h] = jnp.exp((CHUNK - 1.0 - fi) * la0)

    for c in range(sret_in_ref.shape[1]):
        _fwd_chunk(c, qkv_ref, xs_ref, bc_ref, dt_ref, sret_in_ref, sssm_in_ref, dtb_ref, alog_ref, dsk_ref,
                   spread_ref, logdec_ref, y_ref, s_ret, s_ssm, dcomb, ein0, ein1, wcol)

    @pl.when(i == n - 1)
    def _():
        _store_states(s_ret, s_ssm, sret_fin_ref, sssm_fin_ref)


def _fwd_chunk(c, qkv_ref, xs_ref, bc_ref, dt_ref, sret_in_ref, sssm_in_ref, dtb_ref, alog_ref, dsk_ref,
               spread_ref, logdec_ref, y_ref, s_ret, s_ssm, dcomb, ein0, ein1, wcol):
    rows = pl.ds(c * CHUNK, CHUNK)
    qkv_ref, xs_ref, bc_ref, dt_ref, y_ref = (r.at[rows] for r in (qkv_ref, xs_ref, bc_ref, dt_ref, y_ref))
    ii = lax.broadcasted_iota(jnp.int32, (CHUNK, CHUNK), 0)
    jj = lax.broadcasted_iota(jnp.int32, (CHUNK, CHUNK), 1)
    lane = lax.broadcasted_iota(jnp.int32, (1, LANES), 1)
    left = lane < SSM_HEADDIM
    lower = jj <= ii

    def ret_head(h):
        return (qkv_ref[:, h * RET_DK:(h + 1) * RET_DK],
                qkv_ref[:, RET_Q + h * RET_DK:RET_Q + (h + 1) * RET_DK],
                qkv_ref[:, 2 * RET_Q + h * RET_DV:2 * RET_Q + (h + 1) * RET_DV])

    def ssm_group(g):
        return (bc_ref[:, g * SSM_STATE:(g + 1) * SSM_STATE],
                bc_ref[:, SSM_BC + g * SSM_STATE:SSM_BC + (g + 1) * SSM_STATE])

    dt_raw = dt_ref[...]
    dt0, la0 = _ssm_log_decay(dt_raw, dtb_ref[0:1], alog_ref[0:1], lane)
    dt1, la1 = _ssm_log_decay(dt_raw, dtb_ref[1:2], alog_ref[1:2], lane)
    c0 = jnp.dot(lower.astype(F32), la0, precision=HIGHEST, preferred_element_type=F32)
    c1 = jnp.dot((jj >= ii).astype(F32), la1, precision=HIGHEST, preferred_element_type=F32)

    full = jnp.full((1, LANES), float(CHUNK), F32)

    def ret_ready(heads):
        ready = {}
        for h in heads:
            qh, kh, _ = ret_head(h)
            sc = lax.dot_general(qh, kh, (((1,), (1,)), ((), ())), preferred_element_type=F32)
            states = jnp.concatenate([s_ret[h].astype(BF16), sret_in_ref[0, c, h]], axis=1)
            ready[h] = (sc, jnp.dot(qh, states, preferred_element_type=F32))
        return ready

    def ret_finish(ready):
        for h, (sc, inter) in ready.items():
            _, kh, vh = ret_head(h)
            y = jnp.dot((sc * dcomb[h]).astype(BF16), vh, preferred_element_type=F32)
            y_ref[:, h * RET_DV:(h + 1) * RET_DV] = y + inter[:, 0:RET_DV] * ein0[h] + inter[:, RET_DV:] * ein1[h]
            vw = (vh.astype(F32) * wcol[h]).astype(BF16)
            upd = lax.dot_general(kh, vw, (((0,), (0,)), ((), ())), preferred_element_type=F32)
            s_ret[h] = s_ret[h] * jnp.exp(full * logdec_ref[0, h]) + upd

    ret_first = ret_ready(range(0, RET_HEADS // 2))
    ret_second = ret_ready(range(RET_HEADS // 2, RET_HEADS))
    ssm_scores, ssm_inter = [], []
    for g in range(SSM_GROUPS):
        bg, cg = ssm_group(g)
        ssm_scores.append(lax.dot_general(cg, bg, (((1,), (1,)), ((), ())), preferred_element_type=F32))
        states = jnp.concatenate([s_ssm[g].astype(BF16), sssm_in_ref[0, c, g]], axis=1)
        ssm_inter.append(jnp.dot(cg, states, preferred_element_type=F32))

    ret_finish(ret_first)

    c0kt, c1kt = (c0 - jnp.log(dt0)).T, (c1 - jnp.log(dt1)).T
    dst = (dt0 + dt1).T
    last0 = c0[CHUNK - 1:CHUNK]
    w0x = _expand_heads(jnp.exp(last0 - c0) * dt0, spread_ref)
    efull0 = jnp.exp(last0)
    diag = jj == ii

    def ssm_group_out(g):
        bg, _ = ssm_group(g)
        s = ssm_scores[g]
        inter = ssm_inter[g]
        vws = []
        for m in range(SSM_RATIO // 2):
            ha = g * SSM_RATIO + 2 * m
            ps, c0bs, c1bs = [], [], []
            for h in (ha, ha + 1):
                c0b, c1b = _bcol(c0, h), _bcol(c1, h)
                arg = jnp.where(lower, c0b - c0kt[h:h + 1], c1b - c1kt[h:h + 1])
                ps.append((s * jnp.where(diag, dst[h:h + 1], jnp.exp(arg))).astype(BF16))
                c0bs.append(c0b)
                c1bs.append(c1b)
            col = (g * 2 + m) * LANES
            xs_pair = xs_ref[:, col:col + LANES]
            vals = jnp.concatenate([jnp.where(left, xs_pair, 0.0), jnp.where(left, 0.0, xs_pair)], axis=0)
            y = jnp.dot(jnp.concatenate(ps, axis=1), vals.astype(BF16), preferred_element_type=F32)
            y = y + inter[:, m * LANES:(m + 1) * LANES] * jnp.exp(jnp.where(left, c0bs[0], c0bs[1]))
            y = y + (inter[:, SSM_GW + m * LANES:SSM_GW + (m + 1) * LANES]
                     * jnp.exp(jnp.where(left, c1bs[0], c1bs[1])))
            y_ref[:, RET_V + col:RET_V + col + LANES] = y + dsk_ref[:, col:col + LANES] * xs_pair
            vws.append((xs_pair * w0x[:, col:col + LANES]).astype(BF16))
        upd = lax.dot_general(bg, jnp.concatenate(vws, axis=1), (((0,), (0,)), ((), ())),
                              preferred_element_type=F32)
        s_ssm[g] = s_ssm[g] * _ssm_decay_row(left, efull0, g) + upd

    for g in range(SSM_GROUPS // 2):
        ssm_group_out(g)
    ret_finish(ret_second)
    for g in range(SSM_GROUPS // 2, SSM_GROUPS):
        ssm_group_out(g)


def _hybrid_mixer(qkv, rest, dt_bias, a_log, log_decay, d_skip, s_ret0, s_ssm0, batch, seq_len):
    m = qkv.shape[0]
    n = seq_len // CHUNK
    has_init = s_ret0 is not None
    dt_blk = DT_COL // LANES
    pad = lambda v: jnp.pad(v.astype(F32), ((0, 0), (0, LANES - v.shape[1])))
    dtb, alog = pad(dt_bias), pad(a_log)
    logdec = log_decay.astype(F32)
    spread = (lax.broadcasted_iota(jnp.int32, (LANES, SSM_D_INNER), 0)
              == lax.broadcasted_iota(jnp.int32, (LANES, SSM_D_INNER), 1) // SSM_HEADDIM).astype(BF16)
    spread_spec = pl.BlockSpec((LANES, SSM_D_INNER), lambda b, i: (0, 0))
    dsk = jnp.repeat((d_skip[0] + d_skip[1]).astype(F32), SSM_HEADDIM).reshape(1, SSM_D_INNER)
    smem = pl.BlockSpec(memory_space=pltpu.SMEM)
    ret_spec = pl.BlockSpec((1, RET_HEADS, RET_DK, RET_DV), lambda b, i: (b, 0, 0, 0))
    ssm_spec = pl.BlockSpec((1, SSM_HEADS, SSM_STATE, SSM_HEADDIM), lambda b, i: (b, 0, 0, 0))
    ret_shape = jax.ShapeDtypeStruct((batch, 2, RET_HEADS, RET_DK, RET_DV), F32)
    ssm_shape = jax.ShapeDtypeStruct((batch, 2, SSM_HEADS, SSM_STATE, SSM_HEADDIM), F32)
    fin_specs = lambda d: [pl.BlockSpec((1, 1, RET_HEADS, RET_DK, RET_DV), lambda b, i: (b, d, 0, 0, 0)),
                           pl.BlockSpec((1, 1, SSM_HEADS, SSM_STATE, SSM_HEADDIM), lambda b, i: (b, d, 0, 0, 0))]
    ret_scr = pltpu.VMEM((RET_HEADS, RET_DK, RET_DV), F32)
    ssm_scr = pltpu.VMEM((SSM_GROUPS, SSM_STATE, SSM_GW), F32)
    const_scr = pltpu.VMEM((RET_HEADS, CHUNK, CHUNK), F32)

    cps = next(c for c in (8, 4, 2, 1) if n % c == 0)
    rsteps = n // cps
    rrows = cps * CHUNK

    def rev(b, i):
        return b * rsteps + rsteps - 1 - i

    in_specs = [
        pl.BlockSpec((rrows, RET_Q), lambda b, i: (rev(b, i), 1)),
        pl.BlockSpec((rrows, RET_V), lambda b, i: (rev(b, i), 1)),
        pl.BlockSpec((rrows, SSM_D_INNER), lambda b, i: (rev(b, i), XS_COL // SSM_D_INNER)),
        pl.BlockSpec((rrows, SSM_BC), lambda b, i: (rev(b, i), HYB_QKV // SSM_BC)),
        pl.BlockSpec((rrows, LANES), lambda b, i: (rev(b, i), dt_blk)),
        pl.BlockSpec((1, LANES), lambda b, i: (0, 0)),
        pl.BlockSpec((1, LANES), lambda b, i: (0, 0)),
        spread_spec,
        smem,
    ]
    args = [qkv, qkv, rest, qkv, rest, dtb[1:2], alog[1:2], spread, logdec]
    if has_init:
        in_specs += [ret_spec, ssm_spec]
        args += [s_ret0[:, 1], s_ssm0[:, 1]]
    sret_in, sssm_in, sret_fin, sssm_fin = pl.pallas_call(
        functools.partial(_bwd_state_body, has_init),
        grid=(batch, rsteps),
        in_specs=in_specs,
        out_specs=[
            pl.BlockSpec((1, cps, RET_HEADS, RET_DK, RET_DV), lambda b, i: (b, rsteps - 1 - i, 0, 0, 0)),
            pl.BlockSpec((1, cps, SSM_GROUPS, SSM_STATE, SSM_GW), lambda b, i: (b, rsteps - 1 - i, 0, 0, 0)),
            *fin_specs(1),
        ],
        out_shape=[
            jax.ShapeDtypeStruct((batch, n, RET_HEADS, RET_DK, RET_DV), BF16),
            jax.ShapeDtypeStruct((batch, n, SSM_GROUPS, SSM_STATE, SSM_GW), BF16),
            ret_shape, ssm_shape,
        ],
        scratch_shapes=[ret_scr, ssm_scr, const_scr],
        compiler_params=_cparams(("arbitrary", "arbitrary")),
        name="hybrid_reverse_states",
    )(*args)

    fcps = next(c for c in (FWD_CHUNKS_PER_STEP, 2, 1) if n % c == 0)
    fsteps = n // fcps
    frows = fcps * CHUNK

    def fwd(b, i):
        return b * fsteps + i

    in_specs = [
        pl.BlockSpec((frows, HYB_QKV), lambda b, i: (fwd(b, i), 0)),
        pl.BlockSpec((frows, SSM_D_INNER), lambda b, i: (fwd(b, i), XS_COL // SSM_D_INNER)),
        pl.BlockSpec((frows, 2 * SSM_BC), lambda b, i: (fwd(b, i), HYB_QKV // (2 * SSM_BC))),
        pl.BlockSpec((frows, LANES), lambda b, i: (fwd(b, i), dt_blk)),
        pl.BlockSpec((1, fcps, RET_HEADS, RET_DK, RET_DV), lambda b, i: (b, i, 0, 0, 0)),
        pl.BlockSpec((1, fcps, SSM_GROUPS, SSM_STATE, SSM_GW), lambda b, i: (b, i, 0, 0, 0)),
        pl.BlockSpec((2, LANES), lambda b, i: (0, 0)),
        pl.BlockSpec((2, LANES), lambda b, i: (0, 0)),
        pl.BlockSpec((1, SSM_D_INNER), lambda b, i: (0, 0)),
        spread_spec,
        smem,
    ]
    args = [qkv, rest, qkv, rest, sret_in, sssm_in, dtb, alog, dsk, spread, logdec]
    if has_init:
        in_specs += [ret_spec, ssm_spec]
        args += [s_ret0[:, 0], s_ssm0[:, 0]]
    in_specs += [pl.BlockSpec(memory_space=pl.ANY)] * 2
    args += [sret_fin, sssm_fin]
    y, sret_fin, sssm_fin = pl.pallas_call(
        functools.partial(_hyb_fwd_body, has_init),
        grid=(batch, fsteps),
        in_specs=in_specs,
        out_specs=[pl.BlockSpec((frows, HYB_MIX), lambda b, i: (fwd(b, i), 0)), *fin_specs(0)],
        out_shape=[jax.ShapeDtypeStruct((m, HYB_MIX), F32), ret_shape, ssm_shape],
        input_output_aliases={len(args) - 2: 1, len(args) - 1: 2},
        scratch_shapes=[ret_scr, ssm_scr, const_scr, const_scr, const_scr, const_scr],
        compiler_params=_cparams(("arbitrary", "arbitrary")),
        name="hybrid_forward_mix",
    )(*args)
    return y, sret_fin, sssm_fin


def _hyb_out_body(y_ref, g_ref, z_ref, x_ref, gate_ref, gn_ref, ng_ref, w_ref, o_ref):
    y = y_ref[...]
    parts = []
    for h in range(RET_HEADS):
        yh = y[:, h * RET_DV:(h + 1) * RET_DV]
        mu = jnp.mean(yh, axis=-1, keepdims=True)
        var = jnp.mean(jnp.square(yh - mu), axis=-1, keepdims=True)
        parts.append((yh - mu) * lax.rsqrt(var + EPS))
    y_ret = jax.nn.silu(g_ref[...]) * (jnp.concatenate(parts, axis=1) * gn_ref[...])
    yz = y[:, RET_V:] * jax.nn.silu(z_ref[...])
    y_ssm = yz * lax.rsqrt(jnp.mean(yz * yz, axis=-1, keepdims=True) + EPS) * ng_ref[...]
    out = jnp.dot(y_ret.astype(BF16), w_ref[0:RET_V], preferred_element_type=F32)
    out = out + jnp.dot(y_ssm.astype(BF16), w_ref[RET_V:HYB_MIX], preferred_element_type=F32)
    o_ref[...] = x_ref[...] + gate_ref[0] * out


def _hyb_out(y, rest, x, gate, gn_gain, norm_gain, w_out, seq_len, tm):
    m = x.shape[0]
    per_batch = gate.shape[0] > 1
    tpb = seq_len // tm
    return pl.pallas_call(
        _hyb_out_body,
        grid=(m // tm,),
        in_specs=[
            pl.BlockSpec((tm, HYB_MIX), lambda i: (i, 0)),
            pl.BlockSpec((tm, RET_V), lambda i: (i, 0)),
            pl.BlockSpec((tm, SSM_D_INNER), lambda i: (i, 1)),
            pl.BlockSpec((tm, D_MODEL), lambda i: (i, 0)),
            _vec_spec(D_MODEL, tpb, per_batch),
            pl.BlockSpec((1, RET_V), lambda i: (0, 0)),
            pl.BlockSpec((1, SSM_D_INNER), lambda i: (0, 0)),
            pl.BlockSpec((HYB_MIX, D_MODEL), lambda i: (0, 0)),
        ],
        out_specs=pl.BlockSpec((tm, D_MODEL), lambda i: (i, 0)),
        out_shape=jax.ShapeDtypeStruct((m, D_MODEL), F32),
        compiler_params=_cparams(("arbitrary",)),
        name="hybrid_out_proj",
    )(y, rest, rest, x, gate, gn_gain, norm_gain, w_out)


def _head_rms(t, seg_ref):
    outs = []
    for c in range(t.shape[1] // LANES):
        tc = t[:, c * LANES:(c + 1) * LANES]
        sq = tc * tc
        hi = sq.astype(BF16)
        lo = (sq - hi.astype(F32)).astype(BF16)
        ssum = (jnp.dot(hi, seg_ref[...], preferred_element_type=F32)
                + jnp.dot(lo, seg_ref[...], preferred_element_type=F32))
        outs.append(tc * lax.rsqrt(ssum * (1.0 / ATT_HD) + EPS))
    return outs


def _rope(tc, cos, sin, lane):
    fwd = pltpu.roll(tc, LANES - ROT_FREQS, 1)
    bwd = pltpu.roll(tc, ROT_FREQS, 1)
    return tc * cos + jnp.where(lane % (2 * ROT_FREQS) < ROT_FREQS, fwd, bwd) * sin


def _qkv_body(has_rope, x_ref, gain_ref, sc_ref, sh_ref, w_ref, seg_ref, qg_ref, kg_ref, *rest):
    if has_rope:
        cos_ref, sin_ref, q_ref, k_ref, v_ref = rest
    else:
        q_ref, k_ref, v_ref, kf_ref, vf_ref = rest
    h = _norm_mod(x_ref[...], gain_ref[...], sc_ref[0], sh_ref[0]).astype(BF16)
    lane = lax.broadcasted_iota(jnp.int32, (1, LANES), 1)

    def project(col):
        t = jnp.dot(h, w_ref[:, col:col + MXU_COLS], preferred_element_type=F32)
        return [t[:, c * LANES:(c + 1) * LANES] for c in range(MXU_COLS // LANES)]

    def normed(tiles, gain_ref_):
        tiles = [t * gain_ref_[...] for t in _head_rms(jnp.concatenate(tiles, axis=1), seg_ref)]
        return tiles, ([_rope(t, cos_ref[...], sin_ref[...], lane) for t in tiles] if has_rope else tiles)

    def emit_q(col, tiles):
        _, q = normed(tiles, qg_ref)
        for c, t in enumerate(q):
            row = col + c * LANES
            q_ref[0, row:row + LANES, :] = (t * Q_SCALE).T.astype(BF16)

    def emit_k(col, tiles):
        k, kr = normed(tiles, kg_ref)
        for c in range(MXU_COLS // LANES):
            lo = col + c * LANES
            if not has_rope:
                kf_ref[:, lo:lo + LANES] = k[c]
            kc = kr[c].astype(BF16)
            for a in range(LANES // ATT_HD):
                k_ref[0, lo // ATT_HD + a] = kc[:, a * ATT_HD:(a + 1) * ATT_HD]

    def emit_v(col, tiles):
        for c in range(MXU_COLS // LANES):
            lo = col + c * LANES
            if not has_rope:
                vf_ref[:, lo:lo + LANES] = tiles[c]
            vc = tiles[c].T.astype(BF16)
            for a in range(LANES // ATT_HD):
                v_ref[0, lo // ATT_HD + a] = vc[a * ATT_HD:(a + 1) * ATT_HD]

    work = ([(emit_q, col, col) for col in range(0, ATT_Q, MXU_COLS)]
            + [(emit_k, col, ATT_Q + col) for col in range(0, ATT_KV, MXU_COLS)]
            + [(emit_v, col, ATT_Q + ATT_KV + col) for col in range(0, ATT_KV, MXU_COLS)])
    tiles = project(work[0][2])
    for n, (emit, col, _) in enumerate(work):
        ahead = project(work[n + 1][2]) if n + 1 < len(work) else None
        emit(col, tiles)
        tiles = ahead


def _qkv_proj(x, gain, sc, sh, w_qkv, q_gain, k_gain, rope, seq_len, tm):
    m = x.shape[0]
    per_batch = sc.shape[0] > 1
    tpb = seq_len // tm
    has_rope = rope is not None
    seg = (lax.broadcasted_iota(jnp.int32, (LANES, LANES), 0) // ATT_HD
           == lax.broadcasted_iota(jnp.int32, (LANES, LANES), 1) // ATT_HD).astype(BF16)
    tile2 = lambda v: jnp.tile(v.astype(F32), LANES // ATT_HD).reshape(1, LANES)
    in_specs = [
        pl.BlockSpec((tm, D_MODEL), lambda i: (i, 0)),
        pl.BlockSpec((1, D_MODEL), lambda i: (0, 0)),
        _vec_spec(D_MODEL, tpb, per_batch),
        _vec_spec(D_MODEL, tpb, per_batch),
        pl.BlockSpec((D_MODEL, ATT_Q + 2 * ATT_KV), lambda i: (0, 0)),
        pl.BlockSpec((LANES, LANES), lambda i: (0, 0)),
        pl.BlockSpec((1, LANES), lambda i: (0, 0)),
        pl.BlockSpec((1, LANES), lambda i: (0, 0)),
    ]
    args = [x, gain, sc, sh, w_qkv, seg, tile2(q_gain), tile2(k_gain)]
    batch = m // seq_len
    out_specs = [pl.BlockSpec((1, ATT_Q, tm), lambda i: (i // tpb, 0, i % tpb)),
                 pl.BlockSpec((1, ATT_KV_HEADS, tm, ATT_HD), lambda i: (i // tpb, 0, i % tpb, 0)),
                 pl.BlockSpec((1, ATT_KV_HEADS, ATT_HD, tm), lambda i: (i // tpb, 0, 0, i % tpb))]
    out_shape = [jax.ShapeDtypeStruct((batch, ATT_Q, seq_len), BF16),
                 jax.ShapeDtypeStruct((batch, ATT_KV_HEADS, seq_len, ATT_HD), BF16),
                 jax.ShapeDtypeStruct((batch, ATT_KV_HEADS, ATT_HD, seq_len), BF16)]
    if has_rope:
        in_specs += [pl.BlockSpec((tm, LANES), lambda i: (i % tpb, 0))] * 2
        args += list(rope)
    else:
        out_specs += [pl.BlockSpec((tm, ATT_KV), lambda i: (i, 0))] * 2
        out_shape += [jax.ShapeDtypeStruct((m, ATT_KV), F32)] * 2
    return pl.pallas_call(
        functools.partial(_qkv_body, has_rope),
        grid=(m // tm,),
        in_specs=in_specs,
        out_specs=out_specs,
        out_shape=out_shape,
        compiler_params=_cparams(("arbitrary",)),
        name="attn_qkv_proj",
    )(*args)


def _rope_tables(seq_len):
    rows = seq_len // GRID_W
    row = jnp.repeat(jnp.arange(rows, dtype=F32), GRID_W)
    col = jnp.tile(jnp.arange(GRID_W, dtype=F32), rows)
    inv = ROPE_BASE ** (-jnp.arange(ROT_FREQS, dtype=F32) / ROT_FREQS)
    ar, ac = row[:, None] * inv, col[:, None] * inv
    cos = jnp.concatenate([jnp.cos(ar), jnp.cos(ar), jnp.cos(ac), jnp.cos(ac)], axis=1)
    sin = jnp.concatenate([-jnp.sin(ar), jnp.sin(ar), -jnp.sin(ac), jnp.sin(ac)], axis=1)
    return jnp.tile(cos, (1, LANES // ATT_HD)), jnp.tile(sin, (1, LANES // ATT_HD))


V_ROWS = ATT_HD + 16
ATT_SLOTS = 4


def _attn_body(tq, tk, has_ctx, qt_ref, k_ref, vt_ref, *rest):
    if has_ctx:
        ck_ref, cvt_ref = rest[0], rest[1]
        rest = rest[2:]
    x_ref, gate_ref, wo_ref, o_ref, acc_scr, s_scr, ot_scr = rest
    n_own = k_ref.shape[2] // tk
    nkv = n_own + (1 if has_ctx else 0)
    ones_rows = jnp.ones((V_ROWS - ATT_HD, tk), BF16)

    def key_tile(g, t):
        if has_ctx and isinstance(t, int) and t == n_own:
            return ck_ref[0, g]
        return k_ref[0, g, pl.ds(pl.multiple_of(t * tk, tk), tk), :]

    def value_tile(g, t):
        if has_ctx and isinstance(t, int) and t == n_own:
            return cvt_ref[0, g]
        return vt_ref[0, g, :, pl.ds(pl.multiple_of(t * tk, tk), tk)]

    def scores(g, t, slot):
        kt = key_tile(g, t)
        tops = []
        for r in range(ATT_RATIO):
            hd = g * ATT_RATIO + r
            s = jnp.dot(kt, qt_ref[0, hd * ATT_HD:(hd + 1) * ATT_HD, :], preferred_element_type=F32)
            s_scr[slot, r] = s
            tops.append(jnp.max(s, axis=0, keepdims=True))
        return tuple(tops)

    def softmax_pv(g, t, slot, ms, tops):
        vt = jnp.concatenate([value_tile(g, t), ones_rows], axis=0)
        new_ms = [jnp.maximum(ms[r], tops[r]) for r in range(ATT_RATIO)]
        ps = [jnp.exp2(s_scr[slot, r] - new_ms[r]).astype(BF16) for r in range(ATT_RATIO)]
        for r in range(ATT_RATIO):
            acc_scr[r] = (jnp.exp2(ms[r] - new_ms[r]) * acc_scr[r]
                          + jnp.dot(vt, ps[r], preferred_element_type=F32))
        return tuple(new_ms)

    first_tiles = range(min(2, nkv))
    quads = max(n_own - 2, 0) // ATT_SLOTS
    off = 0
    tops = {t: scores(0, t, t) for t in first_tiles}
    for g in range(ATT_KV_HEADS):
        acc_scr[...] = jnp.zeros_like(acc_scr)
        slot_of = lambda t, off=off: (t + off) % ATT_SLOTS

        def quad_step(i, carry, g=g, slot_of=slot_of):
            ms, top0, top1 = carry
            t = i * ATT_SLOTS
            top2 = scores(g, t + 2, slot_of(2))
            top3 = scores(g, t + 3, slot_of(3))
            ms = softmax_pv(g, t, slot_of(0), ms, top0)
            ms = softmax_pv(g, t + 1, slot_of(1), ms, top1)
            top0 = scores(g, t + 4, slot_of(0))
            top1 = scores(g, t + 5, slot_of(1))
            ms = softmax_pv(g, t + 2, slot_of(2), ms, top2)
            return softmax_pv(g, t + 3, slot_of(3), ms, top3), top0, top1

        ms = tuple(jnp.full((1, tq), NEG_BIG, F32) for _ in range(ATT_RATIO))
        if quads:
            ms, top0, top1 = lax.fori_loop(0, quads, quad_step, (ms, tops[0], tops[1]))
            tops = {quads * ATT_SLOTS: top0, quads * ATT_SLOTS + 1: top1}
        next_off, next_tops = (nkv + off) % ATT_SLOTS, {}
        for t in range(quads * ATT_SLOTS, nkv):
            if t % 2 == 0:
                for ahead in (t + 2, t + 3):
                    if ahead < nkv:
                        tops[ahead] = scores(g, ahead, slot_of(ahead))
            if t == max(nkv - 2, quads * ATT_SLOTS) and g + 1 < ATT_KV_HEADS:
                next_tops = {n: scores(g + 1, n, (n + next_off) % ATT_SLOTS) for n in first_tiles}
            ms = softmax_pv(g, t, slot_of(t), ms, tops.pop(t))
        off, tops = next_off, next_tops
        for r in range(ATT_RATIO):
            hd = g * ATT_RATIO + r
            a = acc_scr[r]
            ot_scr[hd * ATT_HD:(hd + 1) * ATT_HD, :] = (a[0:ATT_HD] / a[ATT_HD:ATT_HD + 1]).astype(BF16)
    out = lax.dot_general(ot_scr[...], wo_ref[...], (((0,), (0,)), ((), ())), preferred_element_type=F32)
    o_ref[0] = x_ref[0] + gate_ref[0] * out


def _attention(qt, k4, vt, ctx, x, gate, w_o, batch, seq_len, tq, tk):
    per_batch = gate.shape[0] > 1
    nkeys = k4.shape[2]
    x3 = x.reshape(batch, seq_len, D_MODEL)
    in_specs = [
        pl.BlockSpec((1, ATT_Q, tq), lambda b, i: (b, 0, i)),
        pl.BlockSpec((1, ATT_KV_HEADS, nkeys, ATT_HD), lambda b, i: (b, 0, 0, 0)),
        pl.BlockSpec((1, ATT_KV_HEADS, ATT_HD, nkeys), lambda b, i: (b, 0, 0, 0)),
    ]
    args = [qt, k4, vt]
    if ctx is not None:
        assert ctx[0].shape[1] == tk
        in_specs += [pl.BlockSpec((1, ATT_KV_HEADS, tk, ATT_HD), lambda b, i: (b, 0, 0, 0)),
                     pl.BlockSpec((1, ATT_KV_HEADS, ATT_HD, tk), lambda b, i: (b, 0, 0, 0))]
        args += [jnp.swapaxes(ctx[0], 1, 2).astype(BF16), jnp.transpose(ctx[1], (0, 2, 3, 1)).astype(BF16)]
    in_specs += [
        pl.BlockSpec((1, tq, D_MODEL), lambda b, i: (b, i, 0)),
        pl.BlockSpec((1, 1, D_MODEL), lambda b, i: (b if per_batch else 0, 0, 0)),
        pl.BlockSpec((ATT_Q, D_MODEL), lambda b, i: (0, 0)),
    ]
    out = pl.pallas_call(
        functools.partial(_attn_body, tq, tk, ctx is not None),
        grid=(batch, seq_len // tq),
        in_specs=in_specs,
        out_specs=pl.BlockSpec((1, tq, D_MODEL), lambda b, i: (b, i, 0)),
        out_shape=jax.ShapeDtypeStruct((batch, seq_len, D_MODEL), F32),
        scratch_shapes=[pltpu.VMEM((ATT_RATIO, V_ROWS, tq), F32), pltpu.VMEM((ATT_SLOTS, ATT_RATIO, tk, tq), F32),
                        pltpu.VMEM((ATT_Q, tq), BF16)],
        compiler_params=_cparams(("arbitrary", "arbitrary")),
        name="attention",
    )(*args, x3, gate, w_o)
    return out.reshape(batch * seq_len, D_MODEL)


def _trunk(x, mods, caches, P, batch, seq_len, tm, tq, tk):
    rope = None if caches is None else _rope_tables(seq_len)
    outs = {}
    sh1, sc1, g1, sh2, sc2, g2 = mods[0]
    qkv, rest = _hyb_in_proj(x, P['norm_mix'][0:1], sc1, sh1, P['hyb_w_in'], P['ssm_conv_w'][0],
                             P['ssm_conv_b'][0], seq_len, tm, PROJ_CHUNK)
    s_ret0 = None if caches is None else caches[0][:, 0]
    s_ssm0 = None if caches is None else caches[1][:, 0]
    y, s_ret, s_ssm = _hybrid_mixer(qkv, rest, P['ssm_dt_bias'][0], P['ssm_a_log'][0], P['ret_log_decay'][0],
                                    P['ssm_d'][0], s_ret0, s_ssm0, batch, seq_len)
    outs['ret'] = s_ret[:, None]
    outs['ssm'] = s_ssm[:, None]
    x = _hyb_out(y, rest, x, g1, P['ret_gn_gain'][0:1], P['ssm_norm_gain'][0:1], P['hyb_w_out'], seq_len, tm)
    x = _conv_ffn(x, P['norm_ffn'][0:1], sc2, sh2, g2, P['ffn_w_up'][0], P['ffn_conv_w'][0], P['ffn_conv_b'][0],
                  P['ffn_w_down'][0], seq_len, tm, PROJ_CHUNK)
    sh1, sc1, g1, sh2, sc2, g2 = mods[1]
    res = _qkv_proj(x, P['norm_mix'][1:2], sc1, sh1, P['attn_w_qkv'], P['attn_q_gain'][0], P['attn_k_gain'][0],
                    rope, seq_len, min(tm, seq_len))
    if caches is None:
        qt, k4, vt, kf, vf = res
        outs['k'] = kf.reshape(batch, 1, seq_len, ATT_KV_HEADS, ATT_HD)
        outs['v'] = vf.reshape(batch, 1, seq_len, ATT_KV_HEADS, ATT_HD)
        ctx = None
    else:
        qt, k4, vt = res
        ctx = (caches[2][:, 0], caches[3][:, 0])
    x = _attention(qt, k4, vt, ctx, x, g1, P['attn_w_o'], batch, seq_len, tq, tk)
    x = _conv_ffn(x, P['norm_ffn'][1:2], sc2, sh2, g2, P['ffn_w_up'][1], P['ffn_conv_w'][1], P['ffn_conv_b'][1],
                  P['ffn_w_down'][1], seq_len, tm, PROJ_CHUNK)
    return x, outs


def kernel(x_prompt, x_sample, state_ret, state_ssm, cache_attn_k, cache_attn_v, c, c_ctx, w_mod, b_mod, norm_mix,
           norm_ffn, ffn_w_up, ffn_conv_w, ffn_conv_b, ffn_w_down, hyb_w_in, hyb_w_out, ret_log_decay, ret_gn_gain,
           ssm_conv_w, ssm_conv_b, ssm_a_log, ssm_dt_bias, ssm_d, ssm_norm_gain, attn_w_qkv, attn_q_gain,
           attn_k_gain, attn_w_o):
    batch, seq, _ = x_prompt.shape
    dec_batch, dec_seq, _ = x_sample.shape
    depth = w_mod.shape[0]

    rows = -(-(dec_batch + 1) // SUBLANES) * SUBLANES
    cond = jnp.concatenate([c, c_ctx[None, :], jnp.zeros((rows - dec_batch - 1, D_MODEL), F32)], axis=0)
    mod = _modulation(cond, w_mod, b_mod).reshape(depth, rows, 6, 1, D_MODEL)
    mods_sample = [[mod[l, 0:dec_batch, t] for t in range(6)] for l in range(depth)]
    mods_prompt = [[mod[l, dec_batch:dec_batch + 1, t] for t in range(6)] for l in range(depth)]

    P = {
        'norm_mix': norm_mix, 'norm_ffn': norm_ffn,
        'hyb_w_in': hyb_w_in[0].astype(BF16),
        'hyb_w_out': hyb_w_out[0].astype(BF16),
        'ret_log_decay': ret_log_decay, 'ret_gn_gain': ret_gn_gain,
        'ssm_conv_w': ssm_conv_w, 'ssm_conv_b': ssm_conv_b, 'ssm_a_log': ssm_a_log, 'ssm_dt_bias': ssm_dt_bias,
        'ssm_d': ssm_d, 'ssm_norm_gain': ssm_norm_gain,
        'attn_w_qkv': attn_w_qkv[0].astype(BF16), 'attn_q_gain': attn_q_gain, 'attn_k_gain': attn_k_gain,
        'attn_w_o': attn_w_o[0].astype(BF16),
        'ffn_w_up': ffn_w_up.astype(BF16), 'ffn_conv_w': ffn_conv_w, 'ffn_conv_b': ffn_conv_b,
        'ffn_w_down': ffn_w_down.astype(BF16),
    }

    y_prompt, outs = _trunk(x_prompt.reshape(batch * seq, D_MODEL), mods_prompt, None, P, batch, seq,
                            tm=TOKEN_TILE if (batch * seq) % TOKEN_TILE == 0 and TOKEN_TILE % seq == 0 else min(ATT_TILE, seq),
                            tq=min(ATT_TILE, seq), tk=min(ATT_TILE, seq))
    caches = (state_ret, state_ssm, cache_attn_k, cache_attn_v)
    y_sample, _ = _trunk(x_sample.reshape(dec_batch * dec_seq, D_MODEL), mods_sample, caches, P, dec_batch, dec_seq,
                         tm=TOKEN_TILE, tq=ATT_TILE, tk=ATT_TILE)
    return (y_prompt.reshape(batch, seq, D_MODEL), y_sample.reshape(dec_batch, dec_seq, D_MODEL),
            outs['ret'], outs['ssm'], outs['k'], outs['v'])
```
